```python
import math
import jax
import jax.numpy as jnp
from jax import lax
import numpy as np

D_MODEL = 1024
BATCH = 16
SEQ = 4096
DEPTH = 2

MEM_LEN = 256
N_EVEN = (DEPTH + 1) // 2
N_ODD = DEPTH // 2
NORM_EPS = 1e-6

RW_HEAD_DIM = 64
RW_DIM = D_MODEL // 2
RW_HEADS = RW_DIM // RW_HEAD_DIM
RW_DECAY_RANK = 64
RW_A_RANK = 64
RW_GATE_RANK = 128
RW_GN_EPS = 64e-5
RW_PROJ = 3 * RW_DIM + RW_DECAY_RANK + RW_A_RANK + RW_GATE_RANK

SSD_HEAD_DIM = 64
SSD_DIM = D_MODEL // 2
SSD_HEADS = SSD_DIM // SSD_HEAD_DIM
SSD_GROUPS = 2
SSD_STATE = 128
SSD_CONV = 3
SSD_CHUNK = 128
SSD_CONV_DIM = SSD_DIM + 2 * SSD_GROUPS * SSD_STATE
SSD_PROJ = SSD_DIM + SSD_CONV_DIM + SSD_HEADS

AB_PROJ = RW_PROJ + SSD_PROJ
AB_OUT = RW_DIM + SSD_DIM

HG_HEADS = 8
HG_KEY_DIM = 128
HG_VAL_DIM = D_MODEL // HG_HEADS
HG_KEY_WIDTH = HG_HEADS * HG_KEY_DIM
HG_VAL_WIDTH = HG_HEADS * HG_VAL_DIM
HG_CHUNK = 64
HG_PROJ = 3 * HG_KEY_WIDTH + 2 * HG_VAL_WIDTH

XA_HEADS = 4
XA_HEAD_DIM = D_MODEL // XA_HEADS
FFN_DIM = ((8 * D_MODEL + 3 * 256 - 1) // (3 * 256)) * 256

kernel_name = "bidir_rwkv7_mamba2_hgrn2_hybrid_block"


def rms_norm(x, gain):
    xf = x.astype(jnp.float32)
    y = xf * lax.rsqrt(jnp.mean(xf * xf, axis=-1, keepdims=True) + NORM_EPS)
    return (y * gain.astype(jnp.float32)).astype(x.dtype)


def flip_time(z):
    return jnp.flip(z, axis=1)


def centred_shift(z):
    prev = jnp.pad(z[:, :-1], ((0, 0), (1, 0), (0, 0)))
    nxt = jnp.pad(z[:, 1:], ((0, 0), (0, 1), (0, 0)))
    return 0.5 * (prev + nxt)


def centred_depthwise_conv(u, w, b):
    out = lax.conv_general_dilated(
        u, w[:, None, :].astype(u.dtype), window_strides=(1,),
        padding=[(SSD_CONV // 2, SSD_CONV // 2)],
        dimension_numbers=('NWC', 'WIO', 'NWC'), feature_group_count=u.shape[-1])
    return out + b.astype(u.dtype)


def rwkv7_scan(r, w, k, v, kk, a):
    nb, _, nh, n = r.shape

    def step(state, inp):
        r_t, w_t, k_t, v_t, kk_t, a_t = inp
        sa = jnp.einsum('bhvk,bhk->bhv', state, -kk_t)
        state = (state * w_t[:, :, None, :]
                 + sa[..., None] * (kk_t * a_t)[:, :, None, :]
                 + v_t[..., None] * k_t[:, :, None, :])
        return state, jnp.einsum('bhvk,bhk->bhv', state, r_t)

    xs = tuple(jnp.moveaxis(z, 1, 0) for z in (r, w, k, v, kk, a))
    _, out = lax.scan(step, jnp.zeros((nb, nh, n, n), jnp.float32), xs)
    return jnp.moveaxis(out, 0, 1)


def rwkv7_branch(p, w0, w2, a0, a2, g2, k_k, k_a, r_k, gn_w, gn_b):
    nb, t, _ = p.shape
    o1, o2, o3 = RW_DIM, 2 * RW_DIM, 3 * RW_DIM
    o4 = o3 + RW_DECAY_RANK
    o5 = o4 + RW_A_RANK
    r, k, v = p[..., :o1], p[..., o1:o2], p[..., o2:o3]
    wd, ad, gd = p[..., o3:o4], p[..., o4:o5], p[..., o5:]

    def heads(z):
        return z.reshape(nb, t, RW_HEADS, RW_HEAD_DIM)

    a = jax.nn.sigmoid(a0 + ad @ a2)
    g = jax.nn.sigmoid(gd) @ g2
    kk = heads(k * k_k)
    kk = kk / jnp.maximum(jnp.linalg.norm(kk, axis=-1, keepdims=True), 1e-12)
    k = k * (1.0 + (a - 1.0) * k_a)
    w_lora = jnp.tanh(wd)

    def decay(d):
        w = -jax.nn.softplus(-(w0[d] + w_lora @ w2[d])) - 0.5
        return heads(jnp.exp(-jnp.exp(w)))

    rh, kh, vh, ah = heads(r), heads(k), heads(v), heads(a)
    o_fwd = rwkv7_scan(rh, decay(0), kh, vh, kk, ah)
    o_bwd = flip_time(rwkv7_scan(flip_time(rh), flip_time(decay(1)), flip_time(kh),
                                 flip_time(vh), flip_time(kk), flip_time(ah)))
    o = o_fwd + o_bwd
    mu = jnp.mean(o, axis=-1, keepdims=True)
    var = jnp.mean(jnp.square(o - mu), axis=-1, keepdims=True)
    o = (o - mu) * lax.rsqrt(var + RW_GN_EPS)
    o = o * gn_w.reshape(RW_HEADS, RW_HEAD_DIM) + gn_b.reshape(RW_HEADS, RW_HEAD_DIM)
    o = o + jnp.sum(rh * kh * r_k, axis=-1, keepdims=True) * vh
    return o.reshape(nb, t, RW_DIM) * g


def ssd_chunked(xs, dt, a, bm, cm):
    nb, t = xs.shape[0], xs.shape[1]
    nc, L = t // SSD_CHUNK, SSD_CHUNK
    J = SSD_HEADS // SSD_GROUPS
    xdt = (xs * dt[..., None]).reshape(nb, nc, L, SSD_GROUPS, J, SSD_HEAD_DIM)
    la = (dt * a).reshape(nb, nc, L, SSD_GROUPS, J).transpose(0, 3, 4, 1, 2)
    bc = bm.reshape(nb, nc, L, SSD_GROUPS, SSD_STATE)
    cc = cm.reshape(nb, nc, L, SSD_GROUPS, SSD_STATE)
    cum = jnp.cumsum(la, axis=-1)
    tri = jnp.tril(jnp.ones((L, L), bool))
    decay = jnp.exp(jnp.where(tri, cum[..., :, None] - cum[..., None, :], -jnp.inf))
    scores = jnp.einsum('bclgn,bcsgn->bgcls', cc, bc)
    y_diag = jnp.einsum('bgjcls,bcsgjp->bclgjp', scores[:, :, None] * decay, xdt)
    states = jnp.einsum('bclgn,bgjcl,bclgjp->bcgjpn', bc, jnp.exp(cum[..., -1:] - cum), xdt)
    tot = jnp.pad(cum[..., -1], ((0, 0), (0, 0), (0, 0), (1, 0)))
    ctot = jnp.cumsum(tot, axis=-1)
    tri_c = jnp.tril(jnp.ones((nc + 1, nc + 1), bool))
    decay_c = jnp.exp(jnp.where(tri_c, ctot[..., :, None] - ctot[..., None, :], -jnp.inf))
    states = jnp.concatenate([jnp.zeros_like(states[:, :1]), states], axis=1)
    carried = jnp.einsum('bgjzc,bcgjpn->bzgjpn', decay_c, states)[:, :-1]
    y_off = jnp.einsum('bclgn,bcgjpn,bgjcl->bclgjp', cc, carried, jnp.exp(cum))
    return (y_diag + y_off).reshape(nb, t, SSD_HEADS, SSD_HEAD_DIM)


def ssd_branch(p, conv_w, conv_b, dt_bias, a_log, d_skip, norm_w):
    nb, t, _ = p.shape
    gn = SSD_GROUPS * SSD_STATE
    z = p[..., :SSD_DIM].astype(jnp.float32)
    xbc = p[..., SSD_DIM:SSD_DIM + SSD_CONV_DIM]
    dt_raw = p[..., SSD_DIM + SSD_CONV_DIM:].astype(jnp.float32)
    xbc = jax.nn.silu(centred_depthwise_conv(xbc, conv_w, conv_b)).astype(jnp.float32)
    xs = xbc[..., :SSD_DIM].reshape(nb, t, SSD_HEADS, SSD_HEAD_DIM)
    bm = xbc[..., SSD_DIM:SSD_DIM + gn].reshape(nb, t, SSD_GROUPS, SSD_STATE)
    cm = xbc[..., SSD_DIM + gn:].reshape(nb, t, SSD_GROUPS, SSD_STATE)
    dt_f = jax.nn.softplus(dt_raw + dt_bias[0])
    dt_b = jax.nn.softplus(dt_raw + dt_bias[1])
    y_f = ssd_chunked(xs, dt_f, -jnp.exp(a_log[0]), bm, cm)
    y_b = flip_time(ssd_chunked(flip_time(xs), flip_time(dt_b), -jnp.exp(a_log[1]),
                                flip_time(bm), flip_time(cm)))
    y = y_f + y_b + d_skip[:, None] * xs
    y = (y.reshape(nb, t, SSD_DIM) * jax.nn.silu(z)).reshape(nb, t, SSD_GROUPS, SSD_DIM // SSD_GROUPS)
    y = y * lax.rsqrt(jnp.mean(y * y, axis=-1, keepdims=True) + NORM_EPS)
    return y.reshape(nb, t, SSD_DIM) * norm_w


def rwkv_ssd_mixer(h, w_in, w_out, rw_mu, rw_w0, rw_w2, rw_a0, rw_a2, rw_g2, rw_k_k, rw_k_a,
                   rw_r_k, rw_gn_w, rw_gn_b, ssd_conv_w, ssd_conv_b, ssd_dt_bias, ssd_a_log,
                   ssd_d, ssd_norm_w):
    p = h @ w_in
    p_rw = p[..., :RW_PROJ].astype(jnp.float32)
    p_rw = p_rw + rw_mu * (centred_shift(p_rw) - p_rw)
    o_rw = rwkv7_branch(p_rw, rw_w0, rw_w2, rw_a0, rw_a2, rw_g2, rw_k_k, rw_k_a, rw_r_k,
                        rw_gn_w, rw_gn_b)
    o_ssd = ssd_branch(p[..., RW_PROJ:], ssd_conv_w, ssd_conv_b, ssd_dt_bias, ssd_a_log,
                       ssd_d, ssd_norm_w)
    o = jnp.concatenate([o_rw, o_ssd], axis=-1).astype(h.dtype)
    return o @ w_out


def gla_chunked(q, k, v, log_f):
    nb, t, nh, dk = q.shape
    dv = v.shape[-1]
    nc, L = t // HG_CHUNK, HG_CHUNK

    def to_chunks(z):
        return z.reshape(nb, nc, L, nh, z.shape[-1]).transpose(1, 0, 3, 2, 4)

    tri = jnp.tril(jnp.ones((L, L), bool))[..., None]

    def step(state, inp):
        qc, kc, vc, gc = inp
        bcum = jnp.cumsum(gc, axis=2)
        decay = jnp.exp(jnp.where(tri, bcum[:, :, :, None, :] - bcum[:, :, None, :, :], -jnp.inf))
        attn = jnp.einsum('bhld,bhlsd,bhsd->bhls', qc, decay, kc)
        out = (jnp.einsum('bhls,bhsv->bhlv', attn, vc)
               + jnp.einsum('bhld,bhdv->bhlv', qc * jnp.exp(bcum), state))
        last = bcum[:, :, -1:, :]
        state = (state * jnp.exp(bcum[:, :, -1, :])[..., None]
                 + jnp.einsum('bhsd,bhsv->bhdv', kc * jnp.exp(last - bcum), vc))
        return state, out

    xs = (to_chunks(q), to_chunks(k), to_chunks(v), to_chunks(log_f))
    _, out = lax.scan(step, jnp.zeros((nb, nh, dk, dv), jnp.float32), xs)
    return out.transpose(1, 0, 3, 2, 4).reshape(nb, t, nh, dv)


def hgrn2_mixer(h, w_in, w_out, norm_w, lb):
    nb, t, _ = h.shape
    kw, vw = HG_KEY_WIDTH, HG_VAL_WIDTH
    p = (h @ w_in).astype(jnp.float32)
    q, f_fwd, f_bwd = p[..., :kw], p[..., kw:2 * kw], p[..., 2 * kw:3 * kw]
    i, g = p[..., 3 * kw:3 * kw + vw], p[..., 3 * kw + vw:]

    def kheads(z):
        return z.reshape(nb, t, HG_HEADS, HG_KEY_DIM)

    def gates(f_raw):
        f = lb + (1.0 - lb) * jax.nn.sigmoid(f_raw)
        return kheads(1.0 - f), kheads(jnp.log(f))

    k_f, lf_f = gates(f_fwd)
    k_b, lf_b = gates(f_bwd)
    qh = kheads(q)
    ih = i.reshape(nb, t, HG_HEADS, HG_VAL_DIM)
    o = (gla_chunked(qh, k_f, ih, lf_f)
         + flip_time(gla_chunked(flip_time(qh), flip_time(k_b), flip_time(ih), flip_time(lf_b))))
    o = o * lax.rsqrt(jnp.mean(o * o, axis=-1, keepdims=True) + NORM_EPS)
    o = o * norm_w.reshape(HG_HEADS, HG_VAL_DIM)
    o = o.reshape(nb, t, vw) * jax.nn.silu(g)
    return o.astype(h.dtype) @ w_out


def memory_cross_attention(h, m, wq, wkv, wo):
    nb, t, _ = h.shape
    nm = m.shape[1]
    q = (h @ wq).reshape(nb, t, XA_HEADS, XA_HEAD_DIM)
    kv = m @ wkv
    k = kv[..., :D_MODEL].reshape(nb, nm, XA_HEADS, XA_HEAD_DIM)
    v = kv[..., D_MODEL:].reshape(nb, nm, XA_HEADS, XA_HEAD_DIM)
    s = jnp.einsum('bthd,bmhd->bhtm', q, k).astype(jnp.float32) * (XA_HEAD_DIM ** -0.5)
    pr = jax.nn.softmax(s, axis=-1).astype(v.dtype)
    o = jnp.einsum('bhtm,bmhd->bthd', pr, v).reshape(nb, t, D_MODEL)
    return o @ wo


def swiglu_ffn(h, w_in, w_out):
    gu = h @ w_in
    return (jax.nn.silu(gu[..., :FFN_DIM]) * gu[..., FFN_DIM:]) @ w_out


def setup_inputs(seed: int = 0) -> dict:
    key = jax.random.key(seed)
    ks = jax.random.split(key, 40)
    f32 = jnp.float32

    def nrm(k, shape, scale):
        return scale * jax.random.normal(k, shape, f32)

    def gain(k, shape):
        return 1.0 + 0.01 * jax.random.normal(k, shape, f32)

    dt0 = jnp.exp(jax.random.uniform(ks[17], (N_EVEN, 2, SSD_HEADS), f32,
                                     minval=math.log(1e-3), maxval=math.log(1e-1)))
    return {
        "x": nrm(ks[0], (BATCH, SEQ, D_MODEL), 1.0),
        "mem": nrm(ks[1], (BATCH, MEM_LEN, D_MODEL), 1.0),
        "mix_norm": gain(ks[2], (DEPTH, D_MODEL)),
        "ab_w_in": nrm(ks[3], (N_EVEN, D_MODEL, AB_PROJ), D_MODEL ** -0.5),
        "ab_w_out": nrm(ks[4], (N_EVEN, AB_OUT, D_MODEL), AB_OUT ** -0.5),
        "rw_mu": jax.random.uniform(ks[5], (N_EVEN, RW_PROJ), f32),
        "rw_w0": jax.random.uniform(ks[6], (N_EVEN, 2, RW_DIM), f32, minval=-6.5, maxval=-1.5),
        "rw_w2": nrm(ks[7], (N_EVEN, 2, RW_DECAY_RANK, RW_DIM), 0.1 * RW_DECAY_RANK ** -0.5),
        "rw_a0": nrm(ks[8], (N_EVEN, RW_DIM), 0.1),
        "rw_a2": nrm(ks[9], (N_EVEN, RW_A_RANK, RW_DIM), RW_A_RANK ** -0.5),
        "rw_g2": nrm(ks[10], (N_EVEN, RW_GATE_RANK, RW_DIM), RW_GATE_RANK ** -0.5),
        "rw_k_k": 0.85 + nrm(ks[11], (N_EVEN, RW_DIM), 0.05),
        "rw_k_a": 1.0 + nrm(ks[12], (N_EVEN, RW_DIM), 0.05),
        "rw_r_k": nrm(ks[13], (N_EVEN, RW_HEADS, RW_HEAD_DIM), 0.1),
        "rw_gn_w": gain(ks[14], (N_EVEN, RW_DIM)),
        "rw_gn_b": nrm(ks[15], (N_EVEN, RW_DIM), 0.01),
        "ssd_conv_w": nrm(ks[16], (N_EVEN, SSD_CONV, SSD_CONV_DIM), SSD_CONV ** -0.5),
        "ssd_conv_b": nrm(ks[18], (N_EVEN, SSD_CONV_DIM), 0.01),
        "ssd_dt_bias": dt0 + jnp.log(-jnp.expm1(-dt0)),
        "ssd_a_log": jnp.log(jax.random.uniform(ks[19], (N_EVEN, 2, SSD_HEADS), f32, minval=1.0, maxval=16.0)),
        "ssd_d": 1.0 + nrm(ks[20], (N_EVEN, SSD_HEADS), 0.1),
        "ssd_norm_w": gain(ks[21], (N_EVEN, SSD_DIM)),
        "hg_w_in": nrm(ks[22], (N_ODD, D_MODEL, HG_PROJ), D_MODEL ** -0.5),
        "hg_w_out": nrm(ks[23], (N_ODD, HG_VAL_WIDTH, D_MODEL), HG_VAL_WIDTH ** -0.5),
        "hg_norm_w": gain(ks[24], (N_ODD, HG_VAL_WIDTH)),
        "hg_lb": nrm(ks[25], (DEPTH, HG_KEY_WIDTH), 0.1),
        "xa_norm": gain(ks[26], (DEPTH, D_MODEL)),
        "mem_norm": gain(ks[27], (DEPTH, D_MODEL)),
        "xa_wq": nrm(ks[28], (DEPTH, D_MODEL, D_MODEL), D_MODEL ** -0.5),
        "xa_wkv": nrm(ks[29], (DEPTH, D_MODEL, 2 * D_MODEL), D_MODEL ** -0.5),
        "xa_wo": nrm(ks[30], (DEPTH, D_MODEL, D_MODEL), D_MODEL ** -0.5),
        "ffn_norm": gain(ks[31], (DEPTH, D_MODEL)),
        "ffn_w_in": nrm(ks[32], (DEPTH, D_MODEL, 2 * FFN_DIM), D_MODEL ** -0.5),
        "ffn_w_out": nrm(ks[33], (DEPTH, FFN_DIM, D_MODEL), FFN_DIM ** -0.5),
        "final_norm": gain(ks[34], (D_MODEL,)),
    }


def reference(x, mem, mix_norm, ab_w_in, ab_w_out, rw_mu, rw_w0, rw_w2, rw_a0, rw_a2, rw_g2,
              rw_k_k, rw_k_a, rw_r_k, rw_gn_w, rw_gn_b, ssd_conv_w, ssd_conv_b, ssd_dt_bias,
              ssd_a_log, ssd_d, ssd_norm_w, hg_w_in, hg_w_out, hg_norm_w, hg_lb, xa_norm,
              mem_norm, xa_wq, xa_wkv, xa_wo, ffn_norm, ffn_w_in, ffn_w_out, final_norm):
    lb_all = jax.nn.softmax(hg_lb.astype(jnp.float32), axis=0)
    lb_all = jnp.cumsum(lb_all, axis=0) - lb_all[0]
    for layer in range(DEPTH):
        h = rms_norm(x, mix_norm[layer])
        if layer % 2 == 0:
            e = layer // 2
            mixed = rwkv_ssd_mixer(h, ab_w_in[e], ab_w_out[e], rw_mu[e], rw_w0[e], rw_w2[e],
                                   rw_a0[e], rw_a2[e], rw_g2[e], rw_k_k[e], rw_k_a[e], rw_r_k[e],
                                   rw_gn_w[e], rw_gn_b[e], ssd_conv_w[e], ssd_conv_b[e],
                                   ssd_dt_bias[e], ssd_a_log[e], ssd_d[e], ssd_norm_w[e])
        else:
            o = layer // 2
            mixed = hgrn2_mixer(h, hg_w_in[o], hg_w_out[o], hg_norm_w[o], lb_all[layer])
        x = x + mixed
        x = x + memory_cross_attention(rms_norm(x, xa_norm[layer]), rms_norm(mem, mem_norm[layer]),
                                       xa_wq[layer], xa_wkv[layer], xa_wo[layer])
        x = x + swiglu_ffn(rms_norm(x, ffn_norm[layer]), ffn_w_in[layer], ffn_w_out[layer])
    return rms_norm(x, final_norm)
```

```python
import functools
import math

import jax
import jax.numpy as jnp
from jax import lax
from jax.experimental import pallas as pl
from jax.experimental.pallas import tpu as pltpu

F32 = jnp.float32
BF16 = jnp.bfloat16

D_MODEL = 1024
NORM_EPS = 1e-6

RW_HEAD_DIM = 64
RW_DIM = D_MODEL // 2
RW_HEADS = RW_DIM // RW_HEAD_DIM
RW_DECAY_RANK = 64
RW_A_RANK = 64
RW_GATE_RANK = 128
RW_GN_EPS = 64e-5
RW_LORA = RW_DECAY_RANK + RW_A_RANK + RW_GATE_RANK
RW_PROJ = 3 * RW_DIM + RW_LORA
RW_CHUNK = 64

SSD_HEAD_DIM = 64
SSD_DIM = D_MODEL // 2
SSD_HEADS = SSD_DIM // SSD_HEAD_DIM
SSD_GROUPS = 2
SSD_STATE = 128
SSD_CONV_DIM = SSD_DIM + 2 * SSD_GROUPS * SSD_STATE
SSD_CHUNK = 128
SSD_GROUP_HEADS = SSD_HEADS // SSD_GROUPS

HG_HEADS = 8
HG_KEY_DIM = 128
HG_VAL_DIM = D_MODEL // HG_HEADS
HG_KEY_WIDTH = HG_HEADS * HG_KEY_DIM
HG_VAL_WIDTH = HG_HEADS * HG_VAL_DIM
HG_CHUNK = 64
HG_SUB = 16

XA_HEADS = 4
XA_HEAD_DIM = D_MODEL // XA_HEADS
FFN_DIM = ((8 * D_MODEL + 3 * 256 - 1) // (3 * 256)) * 256

SUBLANES = 8
LANES = 128
VMEM_LIMIT_BYTES = 56 * 1024 * 1024
NEG_BIG = -1e30


def _params(*semantics):
    return pltpu.CompilerParams(dimension_semantics=semantics, vmem_limit_bytes=VMEM_LIMIT_BYTES)


def _tile(n, pref):
    t = min(n, pref)
    while n % t or t % SUBLANES:
        t -= 1
    return t


def _mm(a, b):
    return jnp.dot(a.astype(BF16), b.astype(BF16), preferred_element_type=F32)


def _mm_nt(a, b):
    return lax.dot_general(a.astype(BF16), b.astype(BF16), (((1,), (1,)), ((), ())),
                           preferred_element_type=F32)


def _mm_tn(a, b):
    return lax.dot_general(a.astype(BF16), b.astype(BF16), (((0,), (0,)), ((), ())),
                           preferred_element_type=F32)


def _split3(x):
    hi = x.astype(BF16)
    r1 = x - hi.astype(F32)
    mid = r1.astype(BF16)
    lo = (r1 - mid.astype(F32)).astype(BF16)
    return hi, mid, lo


def _mm_exact_lhs(m_bf16, x):
    hi, mid, lo = _split3(x)
    dot = functools.partial(jnp.dot, preferred_element_type=F32)
    return dot(m_bf16, hi) + dot(m_bf16, mid) + dot(m_bf16, lo)


def _mm_exact_rhs(x, m_bf16):
    hi, mid, lo = _split3(x)
    dot = functools.partial(jnp.dot, preferred_element_type=F32)
    return dot(hi, m_bf16) + dot(mid, m_bf16) + dot(lo, m_bf16)


def _rms_normed(x, gain):
    ms = jnp.mean(x * x, axis=-1, keepdims=True)
    return x * lax.rsqrt(ms + NORM_EPS) * gain


def _sigmoid(x):
    return 1.0 / (1.0 + jnp.exp(-x))


def _silu(x):
    return x * _sigmoid(x)


def _softplus(x):
    return jnp.maximum(x, 0.0) + jnp.log(1.0 + jnp.exp(-jnp.abs(x)))


def _order_masks(n, reverse):
    row = lax.broadcasted_iota(jnp.int32, (n, n), 0)
    col = lax.broadcasted_iota(jnp.int32, (n, n), 1)
    if reverse:
        return col >= row, col > row
    return col <= row, col < row


def _norm_matmul_kernel(x_ref, g_ref, *refs, n_out):
    xn = _rms_normed(x_ref[...], g_ref[...]).astype(BF16)
    for w_ref, o_ref in zip(refs[:n_out], refs[n_out:]):
        o_ref[...] = jnp.dot(xn, w_ref[...], preferred_element_type=F32).astype(o_ref.dtype)


def _norm_matmul(x2d, gain, weights, out_dtypes, tm_pref=512):
    n, d = x2d.shape
    tm = _tile(n, tm_pref)
    in_specs = [pl.BlockSpec((tm, d), lambda i: (i, 0)), pl.BlockSpec((1, d), lambda i: (0, 0))]
    in_specs += [pl.BlockSpec(w.shape, lambda i: (0, 0)) for w in weights]
    out_specs = [pl.BlockSpec((tm, w.shape[1]), lambda i: (i, 0)) for w in weights]
    out_shape = [jax.ShapeDtypeStruct((n, w.shape[1]), dt) for w, dt in zip(weights, out_dtypes)]
    return pl.pallas_call(
        functools.partial(_norm_matmul_kernel, n_out=len(weights)),
        grid=(n // tm,), in_specs=in_specs, out_specs=out_specs, out_shape=out_shape,
        compiler_params=_params("parallel"), name="norm_matmul",
    )(x2d, gain.reshape(1, d), *weights)


def _halo_specs(width, tt, t_len, n_rows):
    per_seq = t_len // tt
    blocks_per_tile = tt // SUBLANES
    n_blocks = n_rows // SUBLANES

    def cur(b, j):
        return (b * per_seq + j, 0)

    def prev(b, j):
        return (jnp.maximum((b * per_seq + j) * blocks_per_tile - 1, 0), 0)

    def nxt(b, j):
        return (jnp.minimum((b * per_seq + j + 1) * blocks_per_tile, n_blocks - 1), 0)

    return [pl.BlockSpec((tt, width), cur), pl.BlockSpec((SUBLANES, width), prev),
            pl.BlockSpec((SUBLANES, width), nxt)]


def _neighbours(p, prev_blk, next_blk, first, last):
    tt = p.shape[0]
    ridx = lax.broadcasted_iota(jnp.int32, p.shape, 0)
    prev_row = jnp.where(first, 0.0, prev_blk[SUBLANES - 1:SUBLANES, :])
    next_row = jnp.where(last, 0.0, next_blk[0:1, :])
    prev = jnp.where(ridx == 0, prev_row, pltpu.roll(p, 1, 0))
    nxt = jnp.where(ridx == tt - 1, next_row, pltpu.roll(p, tt - 1, 0))
    return prev, nxt


def _head_sum(x, blockdiag_bf16):
    return _mm_exact_rhs(x, blockdiag_bf16)


def _rwkv_prep_kernel(p_ref, pp_ref, pn_ref, l_ref, lp_ref, ln_ref, mu_ref, mul_ref, a0_ref, a2_ref, g2_ref,
                      kk_ref, ka_ref, w0_ref, w2_ref, rk_ref, bd_ref,
                      r_out, k_out, v_out, kkn_out, kka_out, lwf_out, lwb_out, g_out, bonus_out):
    j = pl.program_id(1)
    first = j == 0
    last = j == pl.num_programs(1) - 1

    def token_shift(x_ref, xp_ref, xn_ref, m_ref):
        x = x_ref[...]
        prev, nxt = _neighbours(x, xp_ref[...], xn_ref[...], first, last)
        return x + m_ref[...] * (0.5 * (prev + nxt) - x)

    p = token_shift(p_ref, pp_ref, pn_ref, mu_ref)
    lo = token_shift(l_ref, lp_ref, ln_ref, mul_ref)
    r = p[:, :RW_DIM]
    k = p[:, RW_DIM:2 * RW_DIM]
    v = p[:, 2 * RW_DIM:]
    wd = lo[:, :RW_DECAY_RANK]
    ad = lo[:, RW_DECAY_RANK:RW_DECAY_RANK + RW_A_RANK]
    gd = lo[:, RW_DECAY_RANK + RW_A_RANK:]
    bd = bd_ref[...]

    a = _sigmoid(a0_ref[...] + _mm(ad, a2_ref[...]))
    g = _mm(_sigmoid(gd), g2_ref[...])
    kk = k * kk_ref[...]
    kk_norm = jnp.sqrt(_head_sum(kk * kk, bd))
    kk = kk / jnp.maximum(kk_norm, 1e-12)
    k = k * (1.0 + (a - 1.0) * ka_ref[...])
    w_lora = jnp.tanh(wd)

    def log_decay(d):
        wl = w0_ref[d:d + 1, :] + _mm(w_lora, w2_ref[d])
        return -jnp.exp(-_softplus(-wl) - 0.5)

    r_out[...] = r
    k_out[...] = k
    v_out[...] = v
    kkn_out[...] = kk
    kka_out[...] = kk * a
    lwf_out[...] = log_decay(0)
    lwb_out[...] = log_decay(1)
    g_out[...] = g
    bonus_out[...] = _head_sum(r * k * rk_ref[...], bd) * v


def _rwkv_prep(p_rkv, p_lora, nb, t_len, mu_rkv, mu_lora, a0, a2, g2, k_k, k_a, w0, w2, r_k, blockdiag):
    n = p_rkv.shape[0]
    tt = _tile(t_len, 256)
    c3, cl = p_rkv.shape[1], p_lora.shape[1]
    halo_specs = _halo_specs(c3, tt, t_len, n) + _halo_specs(cl, tt, t_len, n)

    def full(a):
        nd = a.ndim
        return pl.BlockSpec(a.shape, lambda b, j: (0,) * nd)

    consts = (mu_rkv, mu_lora, a0, a2, g2, k_k, k_a, w0, w2, r_k, blockdiag)
    out_spec = pl.BlockSpec((tt, RW_DIM), lambda b, j: (b * (t_len // tt) + j, 0))
    return pl.pallas_call(
        _rwkv_prep_kernel,
        grid=(nb, t_len // tt),
        in_specs=halo_specs + [full(a) for a in consts],
        out_specs=[out_spec] * 9,
        out_shape=[jax.ShapeDtypeStruct((n, RW_DIM), F32)] * 9,
        compiler_params=_params("parallel", "parallel"), name="rwkv_prep",
    )(p_rkv, p_rkv, p_rkv, p_lora, p_lora, p_lora, *consts)


def _unit_lower_inverse(n_strict, eye):
    size = n_strict.shape[0]
    acc = eye + n_strict
    power = n_strict
    for _ in range(int(math.log2(size)) - 1):
        power = _mm(power, power)
        acc = acc + _mm(acc, power)
    return acc


def _rwkv_scan_kernel(r_ref, k_ref, v_ref, kk_ref, kka_ref, lw_ref, o_ref, h_ref, *, reverse):
    @pl.when(pl.program_id(1) == 0)
    def _():
        h_ref[...] = jnp.zeros_like(h_ref)

    chunk = r_ref.shape[0]
    incl, strict = _order_masks(chunk, reverse)
    eye = (lax.broadcasted_iota(jnp.int32, (chunk, chunk), 0)
           == lax.broadcasted_iota(jnp.int32, (chunk, chunk), 1)).astype(F32)
    eye_k = (lax.broadcasted_iota(jnp.int32, (RW_HEAD_DIM, RW_HEAD_DIM), 0)
             == lax.broadcasted_iota(jnp.int32, (RW_HEAD_DIM, RW_HEAD_DIM), 1)).astype(F32)
    end = 0 if reverse else chunk - 1

    lw = lw_ref[...]
    cum = _mm_exact_lhs(incl.astype(BF16), lw)
    cum_end = cum[end:end + 1, :]
    e_excl = jnp.exp(cum - lw)
    e_neg = jnp.exp(-cum)
    e_pos = jnp.exp(cum)
    e_tail = jnp.exp(cum_end - cum)
    p_end = jnp.exp(cum_end)

    r = r_ref[...]
    k = k_ref[...]
    v = v_ref[...]
    kka = kka_ref[...]
    at_all = -kk_ref[...] * e_excl
    bt_all = kka * e_neg
    kt_all = k * e_neg
    rt_all = r * e_pos
    bh_all = kka * e_tail
    kh_all = k * e_tail

    for h in range(RW_HEADS):
        sl = slice(h * RW_HEAD_DIM, (h + 1) * RW_HEAD_DIM)
        at, bt, kt, rt, bh, kh, vh = (at_all[:, sl], bt_all[:, sl], kt_all[:, sl], rt_all[:, sl],
                                      bh_all[:, sl], kh_all[:, sl], v[:, sl])
        a_ab = jnp.where(strict, _mm_nt(at, bt), 0.0)
        a_ak = jnp.where(strict, _mm_nt(at, kt), 0.0)
        a_rb = jnp.where(incl, _mm_nt(rt, bt), 0.0)
        a_rk = jnp.where(incl, _mm_nt(rt, kt), 0.0)
        t_inv = _unit_lower_inverse(a_ab, eye)
        w_u = _mm(t_inv, at)
        u0 = _mm(t_inv, _mm(a_ak, vh))
        w_o = rt + _mm(a_rb, w_u)
        o0 = _mm(a_rb, u0) + _mm(a_rk, vh)
        g_mat = eye_k * p_end[:, sl] + _mm_tn(bh, w_u)
        h0 = _mm_tn(bh, u0) + _mm_tn(kh, vh)
        h_in = h_ref[h]
        o_ref[:, sl] = _mm(w_o, h_in) + o0
        h_ref[h] = _mm(g_mat, h_in) + h0


def _rwkv_scan(r, k, v, kk, kka, lw, nb, t_len, reverse):
    n = r.shape[0]
    nc = t_len // RW_CHUNK

    def idx(b, c):
        return (b * nc + (nc - 1 - c if reverse else c), 0)

    spec = pl.BlockSpec((RW_CHUNK, RW_DIM), idx)
    return pl.pallas_call(
        functools.partial(_rwkv_scan_kernel, reverse=reverse),
        grid=(nb, nc), in_specs=[spec] * 6, out_specs=spec,
        out_shape=jax.ShapeDtypeStruct((n, RW_DIM), F32),
        scratch_shapes=[pltpu.VMEM((RW_HEADS, RW_HEAD_DIM, RW_HEAD_DIM), F32)],
        compiler_params=_params("parallel", "arbitrary"), name="rwkv_scan_bwd" if reverse else "rwkv_scan_fwd",
    )(r, k, v, kk, kka, lw)


def _ssd_prep_kernel(u_ref, up_ref, un_ref, dt_ref, cw_ref, cb_ref, dtb_ref, alog_ref,
                     xbc_out, dtf_out, dtb_out, laf_out, lab_out):
    j = pl.program_id(1)
    u = u_ref[...]
    prev, nxt = _neighbours(u, up_ref[...], un_ref[...], j == 0, j == pl.num_programs(1) - 1)
    conv = prev * cw_ref[0:1, :] + u * cw_ref[1:2, :] + nxt * cw_ref[2:3, :] + cb_ref[...]
    xbc_out[...] = _silu(conv)
    dt_raw = dt_ref[...]
    dt_f = _softplus(dt_raw + dtb_ref[0:1, :])
    dt_b = _softplus(dt_raw + dtb_ref[1:2, :])
    dtf_out[...] = dt_f
    dtb_out[...] = dt_b
    laf_out[...] = dt_f * (-jnp.exp(alog_ref[0:1, :]))
    lab_out[...] = dt_b * (-jnp.exp(alog_ref[1:2, :]))


def _ssd_prep(p_xbc, p_dt, nb, t_len, conv_w, conv_b, dt_bias, a_log):
    n = p_xbc.shape[0]
    tt = _tile(t_len, 512)
    halo_specs = _halo_specs(SSD_CONV_DIM, tt, t_len, n)

    def row(b, j):
        return (b * (t_len // tt) + j, 0)

    def full(a):
        return pl.BlockSpec(a.shape, lambda b, j: (0, 0))

    consts = (conv_w, conv_b, dt_bias, a_log)
    small = pl.BlockSpec((tt, LANES), row)
    return pl.pallas_call(
        _ssd_prep_kernel,
        grid=(nb, t_len // tt),
        in_specs=halo_specs + [small] + [full(a) for a in consts],
        out_specs=[pl.BlockSpec((tt, SSD_CONV_DIM), row)] + [small] * 4,
        out_shape=[jax.ShapeDtypeStruct((n, SSD_CONV_DIM), F32)] + [jax.ShapeDtypeStruct((n, LANES), F32)] * 4,
        compiler_params=_params("parallel", "parallel"), name="ssd_prep",
    )(p_xbc, p_xbc, p_xbc, p_dt, *consts)


def _ssd_scan_kernel(xbc_ref, dt_ref, la_ref, y_ref, s_ref, *, reverse):
    @pl.when(pl.program_id(1) == 0)
    def _():
        s_ref[...] = jnp.zeros_like(s_ref)

    chunk = xbc_ref.shape[0]
    incl, _ = _order_masks(chunk, reverse)
    end = 0 if reverse else chunk - 1
    gn = SSD_GROUPS * SSD_STATE

    la = la_ref[...]
    dt = dt_ref[...]
    cum = _mm_exact_lhs(incl.astype(BF16), la)
    cum_rows = _mm_exact_rhs(la.T, jnp.transpose(incl).astype(BF16))
    cum_end = cum[end:end + 1, :]
    e_cum = jnp.exp(cum)
    e_tail = jnp.exp(cum_end - cum)
    e_end = jnp.exp(cum_end)

    for g in range(SSD_GROUPS):
        bm = xbc_ref[:, SSD_DIM + g * SSD_STATE:SSD_DIM + (g + 1) * SSD_STATE]
        cm = xbc_ref[:, SSD_DIM + gn + g * SSD_STATE:SSD_DIM + gn + (g + 1) * SSD_STATE]
        scores = _mm_nt(cm, bm)
        carried = s_ref[g]
        y_off = _mm(cm, carried)
        xdt_tail = []
        for jh in range(SSD_GROUP_HEADS):
            h = g * SSD_GROUP_HEADS + jh
            xs = xbc_ref[:, h * SSD_HEAD_DIM:(h + 1) * SSD_HEAD_DIM]
            xdt = xs * dt[:, h:h + 1]
            diff = cum[:, h:h + 1] - cum_rows[h:h + 1, :]
            decay = jnp.exp(jnp.where(incl, diff, NEG_BIG))
            y_h = _mm(scores * decay, xdt) + y_off[:, jh * SSD_HEAD_DIM:(jh + 1) * SSD_HEAD_DIM] * e_cum[:, h:h + 1]
            y_ref[:, h * SSD_HEAD_DIM:(h + 1) * SSD_HEAD_DIM] = y_h
            xdt_tail.append(xdt * e_tail[:, h:h + 1])
            s_ref[g, :, jh * SSD_HEAD_DIM:(jh + 1) * SSD_HEAD_DIM] = (
                carried[:, jh * SSD_HEAD_DIM:(jh + 1) * SSD_HEAD_DIM] * e_end[:, h:h + 1]
                + _mm_tn(bm, xdt_tail[-1]))


def _ssd_scan(xbc, dt, la, nb, t_len, reverse):
    n = xbc.shape[0]
    nc = t_len // SSD_CHUNK

    def idx(b, c):
        return (b * nc + (nc - 1 - c if reverse else c), 0)

    return pl.pallas_call(
        functools.partial(_ssd_scan_kernel, reverse=reverse),
        grid=(nb, nc),
        in_specs=[pl.BlockSpec((SSD_CHUNK, SSD_CONV_DIM), idx), pl.BlockSpec((SSD_CHUNK, LANES), idx),
                  pl.BlockSpec((SSD_CHUNK, LANES), idx)],
        out_specs=pl.BlockSpec((SSD_CHUNK, SSD_DIM), idx),
        out_shape=jax.ShapeDtypeStruct((n, SSD_DIM), F32),
        scratch_shapes=[pltpu.VMEM((SSD_GROUPS, SSD_STATE, SSD_GROUP_HEADS * SSD_HEAD_DIM), F32)],
        compiler_params=_params("parallel", "arbitrary"), name="ssd_scan_bwd" if reverse else "ssd_scan_fwd",
    )(xbc, dt, la)


def _mix0_out_kernel(x_ref, of_ref, ob_ref, bonus_ref, g_ref, yf_ref, yb_ref, xbc_ref, z_ref,
                     gnw_ref, gnb_ref, bd_ref, dsk_ref, nw_ref, wo_rw_ref, wo_ssd_ref, out_ref):
    bd = bd_ref[...]
    o = of_ref[...] + ob_ref[...]
    inv_n = 1.0 / RW_HEAD_DIM
    mu = _head_sum(o, bd) * inv_n
    oc = o - mu
    var = _head_sum(oc * oc, bd) * inv_n
    o = oc * lax.rsqrt(var + RW_GN_EPS) * gnw_ref[...] + gnb_ref[...]
    o_rw = (o + bonus_ref[...]) * g_ref[...]

    y = yf_ref[...] + yb_ref[...] + dsk_ref[...] * xbc_ref[...]
    y = y * _silu(z_ref[...])
    gw = SSD_DIM // SSD_GROUPS
    parts = []
    for g in range(SSD_GROUPS):
        yg = y[:, g * gw:(g + 1) * gw]
        parts.append(yg * lax.rsqrt(jnp.mean(yg * yg, axis=-1, keepdims=True) + NORM_EPS))
    o_ssd = jnp.concatenate(parts, axis=-1) * nw_ref[...]
    out_ref[...] = x_ref[...] + _mm(o_rw, wo_rw_ref[...]) + _mm(o_ssd, wo_ssd_ref[...])


def _mix0_out(x2d, o_f, o_b, bonus, g, y_f, y_b, xbc, z, gn_w, gn_b, blockdiag, d_skip, norm_w, wo_rw, wo_ssd):
    n = x2d.shape[0]
    tm = _tile(n, 512)

    def row(width):
        return pl.BlockSpec((tm, width), lambda i: (i, 0))

    def full(a):
        return pl.BlockSpec(a.shape, lambda i: (0, 0))

    consts = (gn_w, gn_b, blockdiag, d_skip, norm_w, wo_rw, wo_ssd)
    return pl.pallas_call(
        _mix0_out_kernel,
        grid=(n // tm,),
        in_specs=[row(D_MODEL)] + [row(RW_DIM)] * 6 + [row(SSD_DIM), row(SSD_DIM)] + [full(a) for a in consts],
        out_specs=row(D_MODEL),
        out_shape=jax.ShapeDtypeStruct((n, D_MODEL), F32),
        compiler_params=_params("parallel"), name="mix0_out",
    )(x2d, o_f, o_b, bonus, g, y_f, y_b, xbc, z, *consts)


def _hgrn_lower_bound(lb_ref, layer):
    x = lb_ref[...]
    m = jnp.max(x, axis=0, keepdims=True)
    e = jnp.exp(x - m)
    s = e / jnp.sum(e, axis=0, keepdims=True)
    lb = jnp.zeros_like(s[0:1, :])
    for i in range(1, layer + 1):
        lb = lb + s[i:i + 1, :]
    return lb


def _hgrn_chunk(q, f_raw, vv, lb, state_t, reverse):
    chunk = q.shape[0]
    n_sub = chunk // HG_SUB
    incl, _ = _order_masks(chunk, reverse)
    end = 0 if reverse else chunk - 1
    s_idx = lax.broadcasted_iota(jnp.int32, (HG_SUB, 1), 0)

    f = lb + (1.0 - lb) * _sigmoid(f_raw)
    kg = 1.0 - f
    bcum = _mm_exact_lhs(incl.astype(BF16), jnp.log(f))
    b_end = bcum[end:end + 1, :]
    carried = _mm_nt(q * jnp.exp(bcum), state_t)

    order = list(range(n_sub - 1, -1, -1)) if reverse else list(range(n_sub))
    outs = [None] * n_sub
    for pos, sb in enumerate(order):
        rows = slice(sb * HG_SUB, (sb + 1) * HG_SUB)
        q_s, k_s, b_s, v_s = q[rows], kg[rows], bcum[rows], vv[rows]
        acc = carried[rows]
        if pos > 0:
            prev_sb = order[pos - 1]
            edge = prev_sb * HG_SUB if reverse else prev_sb * HG_SUB + HG_SUB - 1
            b_edge = bcum[edge:edge + 1, :]
            src = slice((sb + 1) * HG_SUB, chunk) if reverse else slice(0, sb * HG_SUB)
            q_hat = q_s * jnp.exp(b_s - b_edge)
            k_hat = kg[src] * jnp.exp(b_edge - bcum[src])
            acc = acc + _mm(_mm_nt(q_hat, k_hat), vv[src])
        rows_out = []
        for l in range(HG_SUB):
            w = jnp.exp(jnp.minimum(b_s[l:l + 1, :] - b_s, 0.0)) * k_s * q_s[l:l + 1, :]
            att = jnp.sum(w, axis=-1, keepdims=True)
            att = jnp.where((s_idx >= l) if reverse else (s_idx <= l), att, 0.0)
            rows_out.append(jnp.sum(att * v_s, axis=0, keepdims=True))
        outs[sb] = acc + jnp.concatenate(rows_out, axis=0)

    k_tail = kg * jnp.exp(b_end - bcum)
    new_state_t = state_t * jnp.exp(b_end) + _mm_tn(vv, k_tail)
    return jnp.concatenate(outs, axis=0), new_state_t


def _hgrn_scan_kernel(q_ref, f_ref, i_ref, lb_ref, o_ref, s_ref, *, reverse, layer):
    @pl.when(pl.program_id(2) == 0)
    def _():
        s_ref[...] = jnp.zeros_like(s_ref)

    n_chunks = q_ref.shape[0] // HG_CHUNK
    lb = _hgrn_lower_bound(lb_ref, layer)

    def body(i, carry):
        ci = (n_chunks - 1 - i) if reverse else i
        rows = pl.ds(pl.multiple_of(ci * HG_CHUNK, HG_CHUNK), HG_CHUNK)
        out, new_state = _hgrn_chunk(q_ref[rows, :], f_ref[rows, :], i_ref[rows, :], lb, s_ref[...], reverse)
        o_ref[rows, :] = out
        s_ref[...] = new_state
        return carry

    lax.fori_loop(0, n_chunks, body, 0)


def _hgrn_scan(q, f_raw, i_val, hg_lb, nb, t_len, reverse, layer):
    n = q.shape[0]
    tb = _tile(t_len, 512)
    while tb % HG_CHUNK:
        tb -= SUBLANES
    n_outer = t_len // tb

    def idx(b, h, c):
        return (b * n_outer + (n_outer - 1 - c if reverse else c), h)

    kspec = pl.BlockSpec((tb, HG_KEY_DIM), idx)
    vspec = pl.BlockSpec((tb, HG_VAL_DIM), idx)
    return pl.pallas_call(
        functools.partial(_hgrn_scan_kernel, reverse=reverse, layer=layer),
        grid=(nb, HG_HEADS, n_outer),
        in_specs=[kspec, kspec, vspec, pl.BlockSpec((hg_lb.shape[0], HG_KEY_DIM), lambda b, h, c: (0, h))],
        out_specs=vspec,
        out_shape=jax.ShapeDtypeStruct((n, HG_VAL_WIDTH), F32),
        scratch_shapes=[pltpu.VMEM((HG_VAL_DIM, HG_KEY_DIM), F32)],
        compiler_params=_params("parallel", "parallel", "arbitrary"),
        name="hgrn_scan_bwd" if reverse else "hgrn_scan_fwd",
    )(q, f_raw, i_val, hg_lb)


def _hgrn_out_kernel(x_ref, of_ref, ob_ref, g_ref, nw_ref, wo_ref, out_ref):
    o = of_ref[...] + ob_ref[...]
    parts = []
    for h in range(HG_HEADS):
        oh = o[:, h * HG_VAL_DIM:(h + 1) * HG_VAL_DIM]
        parts.append(oh * lax.rsqrt(jnp.mean(oh * oh, axis=-1, keepdims=True) + NORM_EPS))
    o = jnp.concatenate(parts, axis=-1) * nw_ref[...] * _silu(g_ref[...])
    out_ref[...] = x_ref[...] + _mm(o, wo_ref[...])


def _hgrn_out(x2d, o_f, o_b, g, norm_w, w_out):
    n = x2d.shape[0]
    tm = _tile(n, 512)
    row = pl.BlockSpec((tm, D_MODEL), lambda i: (i, 0))
    return pl.pallas_call(
        _hgrn_out_kernel,
        grid=(n // tm,),
        in_specs=[row] * 4 + [pl.BlockSpec((1, HG_VAL_WIDTH), lambda i: (0, 0)),
                              pl.BlockSpec(w_out.shape, lambda i: (0, 0))],
        out_specs=row,
        out_shape=jax.ShapeDtypeStruct((n, D_MODEL), F32),
        compiler_params=_params("parallel"), name="hgrn_out",
    )(x2d, o_f, o_b, g, norm_w, w_out)


def _xattn_kernel(x_ref, g_ref, wq_ref, kv_ref, wo_ref, out_ref):
    x = x_ref[...]
    q = _mm(_rms_normed(x, g_ref[...]), wq_ref[...])
    scale = XA_HEAD_DIM ** -0.5
    heads = []
    for h in range(XA_HEADS):
        sl = slice(h * XA_HEAD_DIM, (h + 1) * XA_HEAD_DIM)
        k = kv_ref[:, sl]
        v = kv_ref[:, D_MODEL + h * XA_HEAD_DIM:D_MODEL + (h + 1) * XA_HEAD_DIM]
        s = _mm_nt(q[:, sl], k) * scale
        s = s - jnp.max(s, axis=-1, keepdims=True)
        p = jnp.exp(s)
        p = p / jnp.sum(p, axis=-1, keepdims=True)
        heads.append(_mm(p, v))
    out_ref[...] = x + _mm(jnp.concatenate(heads, axis=-1), wo_ref[...])


def _xattn(x2d, gain, wq, kv, wo, nb, t_len, mem_len):
    n = x2d.shape[0]
    tm = _tile(t_len, 512)
    per_seq = t_len // tm
    row = pl.BlockSpec((tm, D_MODEL), lambda b, j: (b * per_seq + j, 0))
    return pl.pallas_call(
        _xattn_kernel,
        grid=(nb, per_seq),
        in_specs=[row, pl.BlockSpec((1, D_MODEL), lambda b, j: (0, 0)),
                  pl.BlockSpec(wq.shape, lambda b, j: (0, 0)),
                  pl.BlockSpec((mem_len, 2 * D_MODEL), lambda b, j: (b, 0)),
                  pl.BlockSpec(wo.shape, lambda b, j: (0, 0))],
        out_specs=row,
        out_shape=jax.ShapeDtypeStruct((n, D_MODEL), F32),
        compiler_params=_params("parallel", "parallel"), name="xattn",
    )(x2d, gain.reshape(1, D_MODEL), wq, kv, wo)


def _ffn_kernel(x_ref, g_ref, wi_ref, wo_ref, fg_ref, out_ref, *, final_norm):
    x = x_ref[...]
    gu = _mm(_rms_normed(x, g_ref[...]), wi_ref[...])
    act = _silu(gu[:, :FFN_DIM]) * gu[:, FFN_DIM:]
    y = x + _mm(act, wo_ref[...])
    if final_norm:
        y = _rms_normed(y, fg_ref[...])
    out_ref[...] = y


def _ffn(x2d, gain, w_in, w_out, final_gain, final_norm):
    n = x2d.shape[0]
    tm = _tile(n, 256)
    row = pl.BlockSpec((tm, D_MODEL), lambda i: (i, 0))
    vec = pl.BlockSpec((1, D_MODEL), lambda i: (0, 0))
    return pl.pallas_call(
        functools.partial(_ffn_kernel, final_norm=final_norm),
        grid=(n // tm,),
        in_specs=[row, vec, pl.BlockSpec(w_in.shape, lambda i: (0, 0)),
                  pl.BlockSpec(w_out.shape, lambda i: (0, 0)), vec],
        out_specs=row,
        out_shape=jax.ShapeDtypeStruct((n, D_MODEL), F32),
        compiler_params=_params("parallel"), name="ffn",
    )(x2d, gain.reshape(1, D_MODEL), w_in, w_out, final_gain.reshape(1, D_MODEL))


def _pad_lanes(a, width=LANES):
    return jnp.pad(a, [(0, 0)] * (a.ndim - 1) + [(0, width - a.shape[-1])])


def _rwkv_ssd_layer(x2d, nb, t_len, norm_g, w_in, w_out, rw_mu, rw_w0, rw_w2, rw_a0, rw_a2, rw_g2, rw_k_k, rw_k_a,
                    rw_r_k, rw_gn_w, rw_gn_b, conv_w, conv_b, dt_bias, a_log, d_skip, ssd_norm_w):
    o_lora = 3 * RW_DIM
    o_z = RW_PROJ
    o_xbc = o_z + SSD_DIM
    o_dt = o_xbc + SSD_CONV_DIM
    wb = w_in.astype(BF16)
    weights = [wb[:, :o_lora], wb[:, o_lora:o_z], wb[:, o_z:o_xbc], wb[:, o_xbc:o_dt], _pad_lanes(wb[:, o_dt:])]
    p_rkv, p_lora, p_z, p_xbc, p_dt = _norm_matmul(x2d, norm_g, weights, [F32] * 5)

    head_id = jnp.arange(RW_DIM) // RW_HEAD_DIM
    blockdiag = (head_id[:, None] == head_id[None, :]).astype(BF16)
    row = lambda a: a.reshape(1, -1)
    r, k, v, kk, kka, lw_f, lw_b, g, bonus = _rwkv_prep(
        p_rkv, p_lora, nb, t_len, row(rw_mu[:o_lora]), row(rw_mu[o_lora:]), row(rw_a0), rw_a2.astype(BF16),
        rw_g2.astype(BF16), row(rw_k_k), row(rw_k_a), rw_w0, rw_w2.astype(BF16), row(rw_r_k), blockdiag)
    o_f = _rwkv_scan(r, k, v, kk, kka, lw_f, nb, t_len, reverse=False)
    o_b = _rwkv_scan(r, k, v, kk, kka, lw_b, nb, t_len, reverse=True)

    xbc, dt_f, dt_b, la_f, la_b = _ssd_prep(p_xbc, p_dt, nb, t_len, conv_w, row(conv_b),
                                            _pad_lanes(dt_bias), _pad_lanes(a_log))
    y_f = _ssd_scan(xbc, dt_f, la_f, nb, t_len, reverse=False)
    y_b = _ssd_scan(xbc, dt_b, la_b, nb, t_len, reverse=True)

    wo = w_out.astype(BF16)
    d_skip_lanes = jnp.repeat(d_skip, SSD_HEAD_DIM).reshape(1, SSD_DIM)
    return _mix0_out(x2d, o_f, o_b, bonus, g, y_f, y_b, xbc, p_z, row(rw_gn_w), row(rw_gn_b), blockdiag,
                     d_skip_lanes, row(ssd_norm_w), wo[:RW_DIM], wo[RW_DIM:])


def _hgrn_layer(x2d, nb, t_len, norm_g, w_in, w_out, norm_w, hg_lb, layer):
    kw, vw = HG_KEY_WIDTH, HG_VAL_WIDTH
    wb = w_in.astype(BF16)
    weights = [wb[:, :kw], wb[:, kw:2 * kw], wb[:, 2 * kw:3 * kw], wb[:, 3 * kw:3 * kw + vw], wb[:, 3 * kw + vw:]]
    q, f_fwd, f_bwd, i_val, g = _norm_matmul(x2d, norm_g, weights, [F32] * 5)
    o_f = _hgrn_scan(q, f_fwd, i_val, hg_lb, nb, t_len, reverse=False, layer=layer)
    o_b = _hgrn_scan(q, f_bwd, i_val, hg_lb, nb, t_len, reverse=True, layer=layer)
    return _hgrn_out(x2d, o_f, o_b, g, norm_w.reshape(1, vw), w_out.astype(BF16))


def kernel(x, mem, mix_norm, ab_w_in, ab_w_out, rw_mu, rw_w0, rw_w2, rw_a0, rw_a2, rw_g2, rw_k_k, rw_k_a, rw_r_k, rw_gn_w, rw_gn_b, ssd_conv_w, ssd_conv_b, ssd_dt_bias, ssd_a_log, ssd_d, ssd_norm_w, hg_w_in, hg_w_out, hg_norm_w, hg_lb, xa_norm, mem_norm, xa_wq, xa_wkv, xa_wo, ffn_norm, ffn_w_in, ffn_w_out, final_norm):
    nb, t_len, d = x.shape
    mem_len = mem.shape[1]
    depth = mix_norm.shape[0]
    x2d = x.reshape(nb * t_len, d)
    mem2d = mem.reshape(nb * mem_len, d)
    for layer in range(depth):
        if layer % 2 == 0:
            e = layer // 2
            x2d = _rwkv_ssd_layer(x2d, nb, t_len, mix_norm[layer], ab_w_in[e], ab_w_out[e], rw_mu[e], rw_w0[e],
                                  rw_w2[e], rw_a0[e], rw_a2[e], rw_g2[e], rw_k_k[e], rw_k_a[e],
                                  rw_r_k[e].reshape(-1), rw_gn_w[e], rw_gn_b[e], ssd_conv_w[e], ssd_conv_b[e],
                                  ssd_dt_bias[e], ssd_a_log[e], ssd_d[e], ssd_norm_w[e])
        else:
            o = layer // 2
            x2d = _hgrn_layer(x2d, nb, t_len, mix_norm[layer], hg_w_in[o], hg_w_out[o], hg_norm_w[o], hg_lb, layer)
        (kv,) = _norm_matmul(mem2d, mem_norm[layer], [xa_wkv[layer].astype(BF16)], [BF16])
        x2d = _xattn(x2d, xa_norm[layer], xa_wq[layer].astype(BF16), kv, xa_wo[layer].astype(BF16),
                     nb, t_len, mem_len)
        x2d = _ffn(x2d, ffn_norm[layer], ffn_w_in[layer].astype(BF16), ffn_w_out[layer].astype(BF16),
                   final_norm, final_norm=(layer == depth - 1))
    return x2d.reshape(nb, t_len, d)
```

```python
import functools
import math

import jax
import jax.numpy as jnp
from jax import lax
from jax.experimental import pallas as pl
from jax.experimental.pallas import tpu as pltpu

F32 = jnp.float32
BF16 = jnp.bfloat16

D_MODEL = 1024
NORM_EPS = 1e-6

RW_HEAD_DIM = 64
RW_DIM = D_MODEL // 2
RW_HEADS = RW_DIM // RW_HEAD_DIM
RW_DECAY_RANK = 64
RW_A_RANK = 64
RW_GATE_RANK = 128
RW_GN_EPS = 64e-5
RW_LORA = RW_DECAY_RANK + RW_A_RANK + RW_GATE_RANK
RW_PROJ = 3 * RW_DIM + RW_LORA
RW_CHUNK = 64
RW_BLOCK_CHUNKS = 2

SSD_HEAD_DIM = 64
SSD_DIM = D_MODEL // 2
SSD_HEADS = SSD_DIM // SSD_HEAD_DIM
SSD_GROUPS = 2
SSD_STATE = 128
SSD_CONV_DIM = SSD_DIM + 2 * SSD_GROUPS * SSD_STATE
SSD_CHUNK = 128
SSD_GROUP_HEADS = SSD_HEADS // SSD_GROUPS

HG_HEADS = 8
HG_KEY_DIM = 128
HG_VAL_DIM = D_MODEL // HG_HEADS
HG_KEY_WIDTH = HG_HEADS * HG_KEY_DIM
HG_VAL_WIDTH = HG_HEADS * HG_VAL_DIM
HG_CHUNK = 64
HG_SUB = 16
HG_BLOCK_CHUNKS = 2
HG_SAFE_LOG_RANGE = 64.0
assert HG_KEY_DIM == HG_VAL_DIM

XA_HEADS = 4
XA_HEAD_DIM = D_MODEL // XA_HEADS
FFN_DIM = ((8 * D_MODEL + 3 * 256 - 1) // (3 * 256)) * 256

SUBLANES = 8
LANES = 128
VMEM_LIMIT_BYTES = 56 * 1024 * 1024
NEG_BIG = -1e30


def _params(*semantics):
    return pltpu.CompilerParams(dimension_semantics=semantics, vmem_limit_bytes=VMEM_LIMIT_BYTES)


def _tile(n, pref):
    t = min(n, pref)
    while n % t or t % SUBLANES:
        t -= 1
    return t


def _mm(a, b):
    return jnp.dot(a.astype(BF16), b.astype(BF16), preferred_element_type=F32)


def _mm_nt(a, b):
    return lax.dot_general(a.astype(BF16), b.astype(BF16), (((1,), (1,)), ((), ())),
                           preferred_element_type=F32)


def _mm_tn(a, b):
    return lax.dot_general(a.astype(BF16), b.astype(BF16), (((0,), (0,)), ((), ())),
                           preferred_element_type=F32)


def _split3(x):
    hi = x.astype(BF16)
    r1 = x - hi.astype(F32)
    mid = r1.astype(BF16)
    lo = (r1 - mid.astype(F32)).astype(BF16)
    return hi, mid, lo


def _mm_exact_lhs(m_bf16, x):
    hi, mid, lo = _split3(x)
    dot = functools.partial(jnp.dot, preferred_element_type=F32)
    return dot(m_bf16, hi) + dot(m_bf16, mid) + dot(m_bf16, lo)


def _mm_exact_rhs(x, m_bf16):
    hi, mid, lo = _split3(x)
    dot = functools.partial(jnp.dot, preferred_element_type=F32)
    return dot(hi, m_bf16) + dot(mid, m_bf16) + dot(lo, m_bf16)


def _rms_normed(x, gain):
    ms = jnp.mean(x * x, axis=-1, keepdims=True)
    return x * lax.rsqrt(ms + NORM_EPS) * gain


def _sigmoid(x):
    return 1.0 / (1.0 + jnp.exp(-x))


def _silu(x):
    return x * _sigmoid(x)


def _softplus(x):
    return jnp.maximum(x, 0.0) + jnp.log(1.0 + jnp.exp(-jnp.abs(x)))


def _order_masks(n, reverse):
    row = lax.broadcasted_iota(jnp.int32, (n, n), 0)
    col = lax.broadcasted_iota(jnp.int32, (n, n), 1)
    if reverse:
        return col >= row, col > row
    return col <= row, col < row


def _norm_matmul_kernel(x_ref, g_ref, *refs, n_out):
    xn = _rms_normed(x_ref[...], g_ref[...]).astype(BF16)
    for w_ref, o_ref in zip(refs[:n_out], refs[n_out:]):
        o_ref[...] = jnp.dot(xn, w_ref[...], preferred_element_type=F32).astype(o_ref.dtype)


def _norm_matmul(x2d, gain, weights, out_dtypes, tm_pref=512):
    n, d = x2d.shape
    tm = _tile(n, tm_pref)
    in_specs = [pl.BlockSpec((tm, d), lambda i: (i, 0)), pl.BlockSpec((1, d), lambda i: (0, 0))]
    in_specs += [pl.BlockSpec(w.shape, lambda i: (0, 0)) for w in weights]
    out_specs = [pl.BlockSpec((tm, w.shape[1]), lambda i: (i, 0)) for w in weights]
    out_shape = [jax.ShapeDtypeStruct((n, w.shape[1]), dt) for w, dt in zip(weights, out_dtypes)]
    return pl.pallas_call(
        functools.partial(_norm_matmul_kernel, n_out=len(weights)),
        grid=(n // tm,), in_specs=in_specs, out_specs=out_specs, out_shape=out_shape,
        compiler_params=_params("parallel"), name="norm_matmul",
    )(x2d, gain.reshape(1, d), *weights)


def _halo_specs(width, tt, t_len, n_rows):
    per_seq = t_len // tt
    blocks_per_tile = tt // SUBLANES
    n_blocks = n_rows // SUBLANES

    def cur(b, j):
        return (b * per_seq + j, 0)

    def prev(b, j):
        return (jnp.maximum((b * per_seq + j) * blocks_per_tile - 1, 0), 0)

    def nxt(b, j):
        return (jnp.minimum((b * per_seq + j + 1) * blocks_per_tile, n_blocks - 1), 0)

    return [pl.BlockSpec((tt, width), cur), pl.BlockSpec((SUBLANES, width), prev),
            pl.BlockSpec((SUBLANES, width), nxt)]


def _neighbours(p, prev_blk, next_blk, first, last):
    tt = p.shape[0]
    ridx = lax.broadcasted_iota(jnp.int32, p.shape, 0)
    prev_row = jnp.where(first, 0.0, prev_blk[SUBLANES - 1:SUBLANES, :])
    next_row = jnp.where(last, 0.0, next_blk[0:1, :])
    prev = jnp.where(ridx == 0, prev_row, pltpu.roll(p, 1, 0))
    nxt = jnp.where(ridx == tt - 1, next_row, pltpu.roll(p, tt - 1, 0))
    return prev, nxt


def _head_sum(x, blockdiag_bf16):
    return _mm_exact_rhs(x, blockdiag_bf16)


def _rwkv_prep_kernel(p_ref, pp_ref, pn_ref, l_ref, lp_ref, ln_ref, mu_ref, mul_ref, a0_ref, a2_ref, g2_ref,
                      kk_ref, ka_ref, w0_ref, w2_ref, rk_ref, bd_ref,
                      r_out, k_out, v_out, kkn_out, kka_out, lwf_out, lwb_out, g_out, bonus_out):
    j = pl.program_id(1)
    first = j == 0
    last = j == pl.num_programs(1) - 1

    def token_shift(x_ref, xp_ref, xn_ref, m_ref):
        x = x_ref[...]
        prev, nxt = _neighbours(x, xp_ref[...], xn_ref[...], first, last)
        return x + m_ref[...] * (0.5 * (prev + nxt) - x)

    p = token_shift(p_ref, pp_ref, pn_ref, mu_ref)
    lo = token_shift(l_ref, lp_ref, ln_ref, mul_ref)
    r = p[:, :RW_DIM]
    k = p[:, RW_DIM:2 * RW_DIM]
    v = p[:, 2 * RW_DIM:]
    wd = lo[:, :RW_DECAY_RANK]
    ad = lo[:, RW_DECAY_RANK:RW_DECAY_RANK + RW_A_RANK]
    gd = lo[:, RW_DECAY_RANK + RW_A_RANK:]
    bd = bd_ref[...]

    a = _sigmoid(a0_ref[...] + _mm(ad, a2_ref[...]))
    g = _mm(_sigmoid(gd), g2_ref[...])
    kk = k * kk_ref[...]
    kk_norm = jnp.sqrt(_head_sum(kk * kk, bd))
    kk = kk / jnp.maximum(kk_norm, 1e-12)
    k = k * (1.0 + (a - 1.0) * ka_ref[...])
    w_lora = jnp.tanh(wd)

    def log_decay(d):
        wl = w0_ref[d:d + 1, :] + _mm(w_lora, w2_ref[d])
        return -jnp.exp(-_softplus(-wl) - 0.5)

    r_out[...] = r
    k_out[...] = k
    v_out[...] = v
    kkn_out[...] = kk
    kka_out[...] = kk * a
    lwf_out[...] = log_decay(0)
    lwb_out[...] = log_decay(1)
    g_out[...] = g
    bonus_out[...] = _head_sum(r * k * rk_ref[...], bd) * v


def _rwkv_prep(p_rkv, p_lora, nb, t_len, mu_rkv, mu_lora, a0, a2, g2, k_k, k_a, w0, w2, r_k, blockdiag):
    n = p_rkv.shape[0]
    tt = _tile(t_len, 256)
    c3, cl = p_rkv.shape[1], p_lora.shape[1]
    halo_specs = _halo_specs(c3, tt, t_len, n) + _halo_specs(cl, tt, t_len, n)

    def full(a):
        nd = a.ndim
        return pl.BlockSpec(a.shape, lambda b, j: (0,) * nd)

    consts = (mu_rkv, mu_lora, a0, a2, g2, k_k, k_a, w0, w2, r_k, blockdiag)
    out_spec = pl.BlockSpec((tt, RW_DIM), lambda b, j: (b * (t_len // tt) + j, 0))
    return pl.pallas_call(
        _rwkv_prep_kernel,
        grid=(nb, t_len // tt),
        in_specs=halo_specs + [full(a) for a in consts],
        out_specs=[out_spec] * 9,
        out_shape=[jax.ShapeDtypeStruct((n, RW_DIM), F32)] * 9,
        compiler_params=_params("parallel", "parallel"), name="rwkv_prep",
    )(p_rkv, p_rkv, p_rkv, p_lora, p_lora, p_lora, *consts)


def _unit_lower_inverses(n_strict_list, eye):
    size = n_strict_list[0].shape[0]
    accs = [eye + n for n in n_strict_list]
    powers = list(n_strict_list)
    for _ in range(int(math.log2(size)) - 1):
        powers = [_mm(p, p) for p in powers]
        accs = [a + _mm(a, p) for a, p in zip(accs, powers)]
    return accs


def _rwkv_scan_kernel(r_ref, k_ref, v_ref, kk_ref, kka_ref, lw_ref, o_ref, h_ref, *, reverse):
    @pl.when(pl.program_id(1) == 0)
    def _():
        h_ref[...] = jnp.zeros_like(h_ref)

    chunk = RW_CHUNK
    n_chunks = r_ref.shape[0] // chunk
    incl, strict = _order_masks(chunk, reverse)
    incl_bf = incl.astype(BF16)
    eye = (lax.broadcasted_iota(jnp.int32, (chunk, chunk), 0)
           == lax.broadcasted_iota(jnp.int32, (chunk, chunk), 1)).astype(F32)
    eye_k = (lax.broadcasted_iota(jnp.int32, (RW_HEAD_DIM, RW_HEAD_DIM), 0)
             == lax.broadcasted_iota(jnp.int32, (RW_HEAD_DIM, RW_HEAD_DIM), 1)).astype(F32)
    end = 0 if reverse else chunk - 1
    chunk_order = list(range(n_chunks - 1, -1, -1)) if reverse else list(range(n_chunks))

    items = []
    for ci in chunk_order:
        rows = slice(ci * chunk, (ci + 1) * chunk)
        lw = lw_ref[rows, :]
        cum = _mm_exact_lhs(incl_bf, lw)
        cum_end = cum[end:end + 1, :]
        e_neg = jnp.exp(-cum)
        e_tail = jnp.exp(cum_end - cum)
        p_end = jnp.exp(cum_end)
        k = k_ref[rows, :]
        kka = kka_ref[rows, :]
        at_all = -kk_ref[rows, :] * jnp.exp(cum - lw)
        bt_all = kka * e_neg
        kt_all = k * e_neg
        rt_all = r_ref[rows, :] * jnp.exp(cum)
        bh_all = kka * e_tail
        kh_all = k * e_tail
        v = v_ref[rows, :]
        for h in range(RW_HEADS):
            sl = slice(h * RW_HEAD_DIM, (h + 1) * RW_HEAD_DIM)
            items.append(dict(ci=ci, h=h, at=at_all[:, sl], bt=bt_all[:, sl], kt=kt_all[:, sl], rt=rt_all[:, sl],
                              bh=bh_all[:, sl], kh=kh_all[:, sl], v=v[:, sl], p_end=p_end[:, sl]))

    for it in items:
        lhs = jnp.concatenate([it["at"], it["rt"]], axis=0)
        ab_rb = _mm_nt(lhs, it["bt"])
        ak_rk = _mm_nt(lhs, it["kt"])
        it["a_ab"] = jnp.where(strict, ab_rb[:chunk], 0.0)
        it["a_rb"] = jnp.where(incl, ab_rb[chunk:], 0.0)
        it["a_ak"] = jnp.where(strict, ak_rk[:chunk], 0.0)
        it["a_rk"] = jnp.where(incl, ak_rk[chunk:], 0.0)
    t_invs = _unit_lower_inverses([it["a_ab"] for it in items], eye)
    for it in items:
        it["x1"] = _mm(it["a_ak"], it["v"])
    for it, t_inv in zip(items, t_invs):
        it["w_u"] = _mm(t_inv, it["at"])
        it["u0"] = _mm(t_inv, it["x1"])
    for it in items:
        w_o = it["rt"] + _mm(it["a_rb"], it["w_u"])
        g_mat = eye_k * it["p_end"] + _mm_tn(it["bh"], it["w_u"])
        it["wo_g"] = jnp.concatenate([w_o, g_mat], axis=0)
        it["o0"] = _mm(it["a_rb"], it["u0"]) + _mm(it["a_rk"], it["v"])
        it["h0"] = _mm_tn(it["bh"], it["u0"]) + _mm_tn(it["kh"], it["v"])

    states = [h_ref[h] for h in range(RW_HEADS)]
    for it in items:
        h = it["h"]
        res = _mm(it["wo_g"], states[h])
        rows = slice(it["ci"] * chunk, (it["ci"] + 1) * chunk)
        o_ref[rows, h * RW_HEAD_DIM:(h + 1) * RW_HEAD_DIM] = res[:chunk] + it["o0"]
        states[h] = res[chunk:] + it["h0"]
    for h in range(RW_HEADS):
        h_ref[h] = states[h]


def _rwkv_scan(r, k, v, kk, kka, lw, nb, t_len, reverse):
    n = r.shape[0]
    tb = _tile(t_len, RW_BLOCK_CHUNKS * RW_CHUNK)
    assert tb % RW_CHUNK == 0
    nc = t_len // tb

    def idx(b, c):
        return (b * nc + (nc - 1 - c if reverse else c), 0)

    spec = pl.BlockSpec((tb, RW_DIM), idx)
    return pl.pallas_call(
        functools.partial(_rwkv_scan_kernel, reverse=reverse),
        grid=(nb, nc), in_specs=[spec] * 6, out_specs=spec,
        out_shape=jax.ShapeDtypeStruct((n, RW_DIM), F32),
        scratch_shapes=[pltpu.VMEM((RW_HEADS, RW_HEAD_DIM, RW_HEAD_DIM), F32)],
        compiler_params=_params("parallel", "arbitrary"), name="rwkv_scan_bwd" if reverse else "rwkv_scan_fwd",
    )(r, k, v, kk, kka, lw)


def _ssd_prep_kernel(u_ref, up_ref, un_ref, dt_ref, cw_ref, cb_ref, dtb_ref, alog_ref,
                     xbc_out, dtf_out, dtb_out, laf_out, lab_out):
    j = pl.program_id(1)
    u = u_ref[...]
    prev, nxt = _neighbours(u, up_ref[...], un_ref[...], j == 0, j == pl.num_programs(1) - 1)
    conv = prev * cw_ref[0:1, :] + u * cw_ref[1:2, :] + nxt * cw_ref[2:3, :] + cb_ref[...]
    xbc_out[...] = _silu(conv)
    dt_raw = dt_ref[...]
    dt_f = _softplus(dt_raw + dtb_ref[0:1, :])
    dt_b = _softplus(dt_raw + dtb_ref[1:2, :])
    dtf_out[...] = dt_f
    dtb_out[...] = dt_b
    laf_out[...] = dt_f * (-jnp.exp(alog_ref[0:1, :]))
    lab_out[...] = dt_b * (-jnp.exp(alog_ref[1:2, :]))


def _ssd_prep(p_xbc, p_dt, nb, t_len, conv_w, conv_b, dt_bias, a_log):
    n = p_xbc.shape[0]
    tt = _tile(t_len, 512)
    halo_specs = _halo_specs(SSD_CONV_DIM, tt, t_len, n)

    def row(b, j):
        return (b * (t_len // tt) + j, 0)

    def full(a):
        return pl.BlockSpec(a.shape, lambda b, j: (0, 0))

    consts = (conv_w, conv_b, dt_bias, a_log)
    small = pl.BlockSpec((tt, LANES), row)
    return pl.pallas_call(
        _ssd_prep_kernel,
        grid=(nb, t_len // tt),
        in_specs=halo_specs + [small] + [full(a) for a in consts],
        out_specs=[pl.BlockSpec((tt, SSD_CONV_DIM), row)] + [small] * 4,
        out_shape=[jax.ShapeDtypeStruct((n, SSD_CONV_DIM), F32)] + [jax.ShapeDtypeStruct((n, LANES), F32)] * 4,
        compiler_params=_params("parallel", "parallel"), name="ssd_prep",
    )(p_xbc, p_xbc, p_xbc, p_dt, *consts)


def _ssd_scan_kernel(xbc_ref, dt_ref, la_ref, y_ref, s_ref, *, reverse):
    @pl.when(pl.program_id(1) == 0)
    def _():
        s_ref[...] = jnp.zeros_like(s_ref)

    chunk = xbc_ref.shape[0]
    incl, _ = _order_masks(chunk, reverse)
    end = 0 if reverse else chunk - 1
    gn = SSD_GROUPS * SSD_STATE

    la = la_ref[...]
    dt = dt_ref[...]
    cum = _mm_exact_lhs(incl.astype(BF16), la)
    cum_rows = _mm_exact_rhs(la.T, jnp.transpose(incl).astype(BF16))
    cum_end = cum[end:end + 1, :]
    e_cum = jnp.exp(cum)
    e_tail = jnp.exp(cum_end - cum)
    e_end = jnp.exp(cum_end)

    for g in range(SSD_GROUPS):
        bm = xbc_ref[:, SSD_DIM + g * SSD_STATE:SSD_DIM + (g + 1) * SSD_STATE]
        cm = xbc_ref[:, SSD_DIM + gn + g * SSD_STATE:SSD_DIM + gn + (g + 1) * SSD_STATE]
        scores = _mm_nt(cm, bm)
        carried = s_ref[g]
        y_off = _mm(cm, carried)
        xdt_tail = []
        for jh in range(SSD_GROUP_HEADS):
            h = g * SSD_GROUP_HEADS + jh
            xs = xbc_ref[:, h * SSD_HEAD_DIM:(h + 1) * SSD_HEAD_DIM]
            xdt = xs * dt[:, h:h + 1]
            diff = cum[:, h:h + 1] - cum_rows[h:h + 1, :]
            decay = jnp.exp(jnp.where(incl, diff, NEG_BIG))
            y_h = _mm(scores * decay, xdt) + y_off[:, jh * SSD_HEAD_DIM:(jh + 1) * SSD_HEAD_DIM] * e_cum[:, h:h + 1]
            y_ref[:, h * SSD_HEAD_DIM:(h + 1) * SSD_HEAD_DIM] = y_h
            xdt_tail.append(xdt * e_tail[:, h:h + 1])
            s_ref[g, :, jh * SSD_HEAD_DIM:(jh + 1) * SSD_HEAD_DIM] = (
                carried[:, jh * SSD_HEAD_DIM:(jh + 1) * SSD_HEAD_DIM] * e_end[:, h:h + 1]
                + _mm_tn(bm, xdt_tail[-1]))


def _ssd_scan(xbc, dt, la, nb, t_len, reverse):
    n = xbc.shape[0]
    nc = t_len // SSD_CHUNK

    def idx(b, c):
        return (b * nc + (nc - 1 - c if reverse else c), 0)

    return pl.pallas_call(
        functools.partial(_ssd_scan_kernel, reverse=reverse),
        grid=(nb, nc),
        in_specs=[pl.BlockSpec((SSD_CHUNK, SSD_CONV_DIM), idx), pl.BlockSpec((SSD_CHUNK, LANES), idx),
                  pl.BlockSpec((SSD_CHUNK, LANES), idx)],
        out_specs=pl.BlockSpec((SSD_CHUNK, SSD_DIM), idx),
        out_shape=jax.ShapeDtypeStruct((n, SSD_DIM), F32),
        scratch_shapes=[pltpu.VMEM((SSD_GROUPS, SSD_STATE, SSD_GROUP_HEADS * SSD_HEAD_DIM), F32)],
        compiler_params=_params("parallel", "arbitrary"), name="ssd_scan_bwd" if reverse else "ssd_scan_fwd",
    )(xbc, dt, la)


def _mix0_out_kernel(x_ref, of_ref, ob_ref, bonus_ref, g_ref, yf_ref, yb_ref, xbc_ref, z_ref,
                     gnw_ref, gnb_ref, bd_ref, dsk_ref, nw_ref, wo_rw_ref, wo_ssd_ref, out_ref):
    bd = bd_ref[...]
    o = of_ref[...] + ob_ref[...]
    inv_n = 1.0 / RW_HEAD_DIM
    mu = _head_sum(o, bd) * inv_n
    oc = o - mu
    var = _head_sum(oc * oc, bd) * inv_n
    o = oc * lax.rsqrt(var + RW_GN_EPS) * gnw_ref[...] + gnb_ref[...]
    o_rw = (o + bonus_ref[...]) * g_ref[...]

    y = yf_ref[...] + yb_ref[...] + dsk_ref[...] * xbc_ref[...]
    y = y * _silu(z_ref[...])
    gw = SSD_DIM // SSD_GROUPS
    parts = []
    for g in range(SSD_GROUPS):
        yg = y[:, g * gw:(g + 1) * gw]
        parts.append(yg * lax.rsqrt(jnp.mean(yg * yg, axis=-1, keepdims=True) + NORM_EPS))
    o_ssd = jnp.concatenate(parts, axis=-1) * nw_ref[...]
    out_ref[...] = x_ref[...] + _mm(o_rw, wo_rw_ref[...]) + _mm(o_ssd, wo_ssd_ref[...])


def _mix0_out(x2d, o_f, o_b, bonus, g, y_f, y_b, xbc, z, gn_w, gn_b, blockdiag, d_skip, norm_w, wo_rw, wo_ssd):
    n = x2d.shape[0]
    tm = _tile(n, 512)

    def row(width):
        return pl.BlockSpec((tm, width), lambda i: (i, 0))

    def full(a):
        return pl.BlockSpec(a.shape, lambda i: (0, 0))

    consts = (gn_w, gn_b, blockdiag, d_skip, norm_w, wo_rw, wo_ssd)
    return pl.pallas_call(
        _mix0_out_kernel,
        grid=(n // tm,),
        in_specs=[row(D_MODEL)] + [row(RW_DIM)] * 6 + [row(SSD_DIM), row(SSD_DIM)] + [full(a) for a in consts],
        out_specs=row(D_MODEL),
        out_shape=jax.ShapeDtypeStruct((n, D_MODEL), F32),
        compiler_params=_params("parallel"), name="mix0_out",
    )(x2d, o_f, o_b, bonus, g, y_f, y_b, xbc, z, *consts)


def _hgrn_lower_bound(lb_ref, layer):
    x = lb_ref[...]
    m = jnp.max(x, axis=0, keepdims=True)
    e = jnp.exp(x - m)
    s = e / jnp.sum(e, axis=0, keepdims=True)
    lb = jnp.zeros_like(s[0:1, :])
    for i in range(1, layer + 1):
        lb = lb + s[i:i + 1, :]
    return lb


def _hgrn_chunk(q, f_raw, vv, lb, state_t, reverse):
    chunk = q.shape[0]
    n_sub = chunk // HG_SUB
    incl, _ = _order_masks(chunk, reverse)
    end = 0 if reverse else chunk - 1
    s_idx = lax.broadcasted_iota(jnp.int32, (HG_SUB, 1), 0)

    f = lb + (1.0 - lb) * _sigmoid(f_raw)
    kg = 1.0 - f
    bcum = _mm_exact_lhs(incl.astype(BF16), jnp.log(f))
    b_end = bcum[end:end + 1, :]
    carried = _mm_nt(q * jnp.exp(bcum), state_t)

    order = list(range(n_sub - 1, -1, -1)) if reverse else list(range(n_sub))
    outs = [None] * n_sub
    for pos, sb in enumerate(order):
        rows = slice(sb * HG_SUB, (sb + 1) * HG_SUB)
        q_s, k_s, b_s, v_s = q[rows], kg[rows], bcum[rows], vv[rows]
        acc = carried[rows]
        if pos > 0:
            prev_sb = order[pos - 1]
            edge = prev_sb * HG_SUB if reverse else prev_sb * HG_SUB + HG_SUB - 1
            b_edge = bcum[edge:edge + 1, :]
            src = slice((sb + 1) * HG_SUB, chunk) if reverse else slice(0, sb * HG_SUB)
            q_hat = q_s * jnp.exp(b_s - b_edge)
            k_hat = kg[src] * jnp.exp(b_edge - bcum[src])
            acc = acc + _mm(_mm_nt(q_hat, k_hat), vv[src])
        rows_out = []
        for l in range(HG_SUB):
            w = jnp.exp(jnp.minimum(b_s[l:l + 1, :] - b_s, 0.0)) * k_s * q_s[l:l + 1, :]
            att = jnp.sum(w, axis=-1, keepdims=True)
            att = jnp.where((s_idx >= l) if reverse else (s_idx <= l), att, 0.0)
            rows_out.append(jnp.sum(att * v_s, axis=0, keepdims=True))
        outs[sb] = acc + jnp.concatenate(rows_out, axis=0)

    k_tail = kg * jnp.exp(b_end - bcum)
    new_state_t = state_t * jnp.exp(b_end) + _mm_tn(vv, k_tail)
    return jnp.concatenate(outs, axis=0), new_state_t


def _hgrn_scan_kernel(q_ref, f_ref, i_ref, lb_ref, o_ref, s_ref, hq_ref, hf_ref, hv_ref, ho_ref, *, reverse, layer):
    @pl.when(pl.program_id(1) == 0)
    def _():
        s_ref[...] = jnp.zeros_like(s_ref)

    chunk = HG_CHUNK
    n_chunks = q_ref.shape[0] // chunk
    incl, _ = _order_masks(chunk, reverse)
    incl_bf = incl.astype(BF16)
    end = 0 if reverse else chunk - 1
    chunk_order = list(range(n_chunks - 1, -1, -1)) if reverse else list(range(n_chunks))
    lb = _hgrn_lower_bound(lb_ref, layer)

    def head_cols(h):
        return slice(h * HG_KEY_DIM, (h + 1) * HG_KEY_DIM)

    pre = []
    worst = None
    for ci in chunk_order:
        rows = slice(ci * chunk, (ci + 1) * chunk)
        f = lb + (1.0 - lb) * _sigmoid(f_ref[rows, :])
        bcum = _mm_exact_lhs(incl_bf, jnp.log(f))
        b_end = bcum[end:end + 1, :]
        worst = b_end if worst is None else jnp.minimum(worst, b_end)
        pre.append((ci, rows, f, bcum, b_end))
    safe = jnp.min(worst) >= -HG_SAFE_LOG_RANGE

    @pl.when(safe)
    def _():
        items = []
        for ci, rows, f, bcum, b_end in pre:
            kg = 1.0 - f
            qt_all = q_ref[rows, :] * jnp.exp(bcum)
            kt_all = kg * jnp.exp(-bcum)
            ktail_all = kg * jnp.exp(b_end - bcum)
            e_end = jnp.exp(b_end)
            v_all = i_ref[rows, :]
            for h in range(HG_HEADS):
                c = head_cols(h)
                items.append(dict(rows=rows, h=h, qt=qt_all[:, c], kt=kt_all[:, c], ktail=ktail_all[:, c],
                                  v=v_all[:, c], e_end=e_end[:, c]))
        for it in items:
            it["attn"] = jnp.where(incl, _mm_nt(it["qt"], it["kt"]), 0.0)
        for it in items:
            it["local"] = _mm(it["attn"], it["v"])
            it["inject"] = _mm_tn(it["v"], it["ktail"])
        states = [s_ref[h] for h in range(HG_HEADS)]
        for it in items:
            h = it["h"]
            o_ref[it["rows"], head_cols(h)] = it["local"] + _mm_nt(it["qt"], states[h])
            states[h] = states[h] * it["e_end"] + it["inject"]
        for h in range(HG_HEADS):
            s_ref[h] = states[h]

    @pl.when(jnp.logical_not(safe))
    def _():
        for ci, rows, _, _, _ in pre:
            for h in range(HG_HEADS):
                hq_ref[h] = q_ref[rows, head_cols(h)]
                hf_ref[h] = f_ref[rows, head_cols(h)]
                hv_ref[h] = i_ref[rows, head_cols(h)]

            def head_body(h, carry):
                lb_h = jnp.sum(jnp.where(lax.broadcasted_iota(jnp.int32, (HG_HEADS, 1), 0) == h, lb_heads, 0.0),
                               axis=0, keepdims=True)
                out, new_state = _hgrn_chunk(hq_ref[h], hf_ref[h], hv_ref[h], lb_h, s_ref[h], reverse)
                ho_ref[h] = out
                s_ref[h] = new_state
                return carry

            lb_heads = jnp.concatenate([lb[:, head_cols(h)] for h in range(HG_HEADS)], axis=0)
            lax.fori_loop(0, HG_HEADS, head_body, 0)
            for h in range(HG_HEADS):
                o_ref[rows, head_cols(h)] = ho_ref[h]


def _hgrn_scan(q, f_raw, i_val, hg_lb, nb, t_len, reverse, layer):
    n = q.shape[0]
    tb = _tile(t_len, HG_BLOCK_CHUNKS * HG_CHUNK)
    assert tb % HG_CHUNK == 0
    n_outer = t_len // tb

    def idx(b, c):
        return (b * n_outer + (n_outer - 1 - c if reverse else c), 0)

    spec = pl.BlockSpec((tb, HG_KEY_WIDTH), idx)
    head_buf = pltpu.VMEM((HG_HEADS, HG_CHUNK, HG_KEY_DIM), F32)
    return pl.pallas_call(
        functools.partial(_hgrn_scan_kernel, reverse=reverse, layer=layer),
        grid=(nb, n_outer),
        in_specs=[spec, spec, spec, pl.BlockSpec(hg_lb.shape, lambda b, c: (0, 0))],
        out_specs=spec,
        out_shape=jax.ShapeDtypeStruct((n, HG_VAL_WIDTH), F32),
        scratch_shapes=[pltpu.VMEM((HG_HEADS, HG_VAL_DIM, HG_KEY_DIM), F32), head_buf, head_buf, head_buf, head_buf],
        compiler_params=_params("parallel", "arbitrary"),
        name="hgrn_scan_bwd" if reverse else "hgrn_scan_fwd",
    )(q, f_raw, i_val, hg_lb)


def _hgrn_out_kernel(x_ref, of_ref, ob_ref, g_ref, nw_ref, wo_ref, out_ref):
    o = of_ref[...] + ob_ref[...]
    parts = []
    for h in range(HG_HEADS):
        oh = o[:, h * HG_VAL_DIM:(h + 1) * HG_VAL_DIM]
        parts.append(oh * lax.rsqrt(jnp.mean(oh * oh, axis=-1, keepdims=True) + NORM_EPS))
    o = jnp.concatenate(parts, axis=-1) * nw_ref[...] * _silu(g_ref[...])
    out_ref[...] = x_ref[...] + _mm(o, wo_ref[...])


def _hgrn_out(x2d, o_f, o_b, g, norm_w, w_out):
    n = x2d.shape[0]
    tm = _tile(n, 512)
    row = pl.BlockSpec((tm, D_MODEL), lambda i: (i, 0))
    return pl.pallas_call(
        _hgrn_out_kernel,
        grid=(n // tm,),
        in_specs=[row] * 4 + [pl.BlockSpec((1, HG_VAL_WIDTH), lambda i: (0, 0)),
                              pl.BlockSpec(w_out.shape, lambda i: (0, 0))],
        out_specs=row,
        out_shape=jax.ShapeDtypeStruct((n, D_MODEL), F32),
        compiler_params=_params("parallel"), name="hgrn_out",
    )(x2d, o_f, o_b, g, norm_w, w_out)


def _xattn_kernel(x_ref, g_ref, wq_ref, kv_ref, wo_ref, out_ref):
    x = x_ref[...]
    q = _mm(_rms_normed(x, g_ref[...]), wq_ref[...])
    scale = XA_HEAD_DIM ** -0.5
    heads = []
    for h in range(XA_HEADS):
        sl = slice(h * XA_HEAD_DIM, (h + 1) * XA_HEAD_DIM)
        k = kv_ref[:, sl]
        v = kv_ref[:, D_MODEL + h * XA_HEAD_DIM:D_MODEL + (h + 1) * XA_HEAD_DIM]
        s = _mm_nt(q[:, sl], k) * scale
        s = s - jnp.max(s, axis=-1, keepdims=True)
        p = jnp.exp(s)
        p = p / jnp.sum(p, axis=-1, keepdims=True)
        heads.append(_mm(p, v))
    out_ref[...] = x + _mm(jnp.concatenate(heads, axis=-1), wo_ref[...])


def _xattn(x2d, gain, wq, kv, wo, nb, t_len, mem_len):
    n = x2d.shape[0]
    tm = _tile(t_len, 512)
    per_seq = t_len // tm
    row = pl.BlockSpec((tm, D_MODEL), lambda b, j: (b * per_seq + j, 0))
    return pl.pallas_call(
        _xattn_kernel,
        grid=(nb, per_seq),
        in_specs=[row, pl.BlockSpec((1, D_MODEL), lambda b, j: (0, 0)),
                  pl.BlockSpec(wq.shape, lambda b, j: (0, 0)),
                  pl.BlockSpec((mem_len, 2 * D_MODEL), lambda b, j: (b, 0)),
                  pl.BlockSpec(wo.shape, lambda b, j: (0, 0))],
        out_specs=row,
        out_shape=jax.ShapeDtypeStruct((n, D_MODEL), F32),
        compiler_params=_params("parallel", "parallel"), name="xattn",
    )(x2d, gain.reshape(1, D_MODEL), wq, kv, wo)


def _ffn_kernel(x_ref, g_ref, wi_ref, wo_ref, fg_ref, out_ref, *, final_norm):
    x = x_ref[...]
    gu = _mm(_rms_normed(x, g_ref[...]), wi_ref[...])
    act = _silu(gu[:, :FFN_DIM]) * gu[:, FFN_DIM:]
    y = x + _mm(act, wo_ref[...])
    if final_norm:
        y = _rms_normed(y, fg_ref[...])
    out_ref[...] = y


def _ffn(x2d, gain, w_in, w_out, final_gain, final_norm):
    n = x2d.shape[0]
    tm = _tile(n, 256)
    row = pl.BlockSpec((tm, D_MODEL), lambda i: (i, 0))
    vec = pl.BlockSpec((1, D_MODEL), lambda i: (0, 0))
    return pl.pallas_call(
        functools.partial(_ffn_kernel, final_norm=final_norm),
        grid=(n // tm,),
        in_specs=[row, vec, pl.BlockSpec(w_in.shape, lambda i: (0, 0)),
                  pl.BlockSpec(w_out.shape, lambda i: (0, 0)), vec],
        out_specs=row,
        out_shape=jax.ShapeDtypeStruct((n, D_MODEL), F32),
        compiler_params=_params("parallel"), name="ffn",
    )(x2d, gain.reshape(1, D_MODEL), w_in, w_out, final_gain.reshape(1, D_MODEL))


def _pad_lanes(a, width=LANES):
    return jnp.pad(a, [(0, 0)] * (a.ndim - 1) + [(0, width - a.shape[-1])])


def _rwkv_ssd_layer(x2d, nb, t_len, norm_g, w_in, w_out, rw_mu, rw_w0, rw_w2, rw_a0, rw_a2, rw_g2, rw_k_k, rw_k_a,
                    rw_r_k, rw_gn_w, rw_gn_b, conv_w, conv_b, dt_bias, a_log, d_skip, ssd_norm_w):
    o_lora = 3 * RW_DIM
    o_z = RW_PROJ
    o_xbc = o_z + SSD_DIM
    o_dt = o_xbc + SSD_CONV_DIM
    wb = w_in.astype(BF16)
    weights = [wb[:, :o_lora], wb[:, o_lora:o_z], wb[:, o_z:o_xbc], wb[:, o_xbc:o_dt], _pad_lanes(wb[:, o_dt:])]
    p_rkv, p_lora, p_z, p_xbc, p_dt = _norm_matmul(x2d, norm_g, weights, [F32] * 5)

    head_id = jnp.arange(RW_DIM) // RW_HEAD_DIM
    blockdiag = (head_id[:, None] == head_id[None, :]).astype(BF16)
    row = lambda a: a.reshape(1, -1)
    r, k, v, kk, kka, lw_f, lw_b, g, bonus = _rwkv_prep(
        p_rkv, p_lora, nb, t_len, row(rw_mu[:o_lora]), row(rw_mu[o_lora:]), row(rw_a0), rw_a2.astype(BF16),
        rw_g2.astype(BF16), row(rw_k_k), row(rw_k_a), rw_w0, rw_w2.astype(BF16), row(rw_r_k), blockdiag)
    o_f = _rwkv_scan(r, k, v, kk, kka, lw_f, nb, t_len, reverse=False)
    o_b = _rwkv_scan(r, k, v, kk, kka, lw_b, nb, t_len, reverse=True)

    xbc, dt_f, dt_b, la_f, la_b = _ssd_prep(p_xbc, p_dt, nb, t_len, conv_w, row(conv_b),
                                            _pad_lanes(dt_bias), _pad_lanes(a_log))
    y_f = _ssd_scan(xbc, dt_f, la_f, nb, t_len, reverse=False)
    y_b = _ssd_scan(xbc, dt_b, la_b, nb, t_len, reverse=True)

    wo = w_out.astype(BF16)
    d_skip_lanes = jnp.repeat(d_skip, SSD_HEAD_DIM).reshape(1, SSD_DIM)
    return _mix0_out(x2d, o_f, o_b, bonus, g, y_f, y_b, xbc, p_z, row(rw_gn_w), row(rw_gn_b), blockdiag,
                     d_skip_lanes, row(ssd_norm_w), wo[:RW_DIM], wo[RW_DIM:])


def _hgrn_layer(x2d, nb, t_len, norm_g, w_in, w_out, norm_w, hg_lb, layer):
    kw, vw = HG_KEY_WIDTH, HG_VAL_WIDTH
    wb = w_in.astype(BF16)
    weights = [wb[:, :kw], wb[:, kw:2 * kw], wb[:, 2 * kw:3 * kw], wb[:, 3 * kw:3 * kw + vw], wb[:, 3 * kw + vw:]]
    q, f_fwd, f_bwd, i_val, g = _norm_matmul(x2d, norm_g, weights, [F32] * 5)
    o_f = _hgrn_scan(q, f_fwd, i_val, hg_lb, nb, t_len, reverse=False, layer=layer)
    o_b = _hgrn_scan(q, f_bwd, i_val, hg_lb, nb, t_len, reverse=True, layer=layer)
    return _hgrn_out(x2d, o_f, o_b, g, norm_w.reshape(1, vw), w_out.astype(BF16))


def kernel(x, mem, mix_norm, ab_w_in, ab_w_out, rw_mu, rw_w0, rw_w2, rw_a0, rw_a2, rw_g2, rw_k_k, rw_k_a, rw_r_k, rw_gn_w, rw_gn_b, ssd_conv_w, ssd_conv_b, ssd_dt_bias, ssd_a_log, ssd_d, ssd_norm_w, hg_w_in, hg_w_out, hg_norm_w, hg_lb, xa_norm, mem_norm, xa_wq, xa_wkv, xa_wo, ffn_norm, ffn_w_in, ffn_w_out, final_norm):
    nb, t_len, d = x.shape
    mem_len = mem.shape[1]
    depth = mix_norm.shape[0]
    x2d = x.reshape(nb * t_len, d)
    mem2d = mem.reshape(nb * mem_len, d)
    for layer in range(depth):
        if layer % 2 == 0:
            e = layer // 2
            x2d = _rwkv_ssd_layer(x2d, nb, t_len, mix_norm[layer], ab_w_in[e], ab_w_out[e], rw_mu[e], rw_w0[e],
                                  rw_w2[e], rw_a0[e], rw_a2[e], rw_g2[e], rw_k_k[e], rw_k_a[e],
                                  rw_r_k[e].reshape(-1), rw_gn_w[e], rw_gn_b[e], ssd_conv_w[e], ssd_conv_b[e],
                                  ssd_dt_bias[e], ssd_a_log[e], ssd_d[e], ssd_norm_w[e])
        else:
            o = layer // 2
            x2d = _hgrn_layer(x2d, nb, t_len, mix_norm[layer], hg_w_in[o], hg_w_out[o], hg_norm_w[o], hg_lb, layer)
        (kv,) = _norm_matmul(mem2d, mem_norm[layer], [xa_wkv[layer].astype(BF16)], [BF16])
        x2d = _xattn(x2d, xa_norm[layer], xa_wq[layer].astype(BF16), kv, xa_wo[layer].astype(BF16),
                     nb, t_len, mem_len)
        x2d = _ffn(x2d, ffn_norm[layer], ffn_w_in[layer].astype(BF16), ffn_w_out[layer].astype(BF16),
                   final_norm, final_norm=(layer == depth - 1))
    return x2d.reshape(nb, t_len, d)
```

```python
import functools
import math

import jax
import jax.numpy as jnp
from jax import lax
from jax.experimental import pallas as pl
from jax.experimental.pallas import tpu as pltpu

F32 = jnp.float32
BF16 = jnp.bfloat16
ACT = BF16

D_MODEL = 1024
NORM_EPS = 1e-6

RW_HEAD_DIM = 64
RW_DIM = D_MODEL // 2
RW_HEADS = RW_DIM // RW_HEAD_DIM
RW_DECAY_RANK = 64
RW_A_RANK = 64
RW_GATE_RANK = 128
RW_GN_EPS = 64e-5
RW_LORA = RW_DECAY_RANK + RW_A_RANK + RW_GATE_RANK
RW_PROJ = 3 * RW_DIM + RW_LORA
RW_CHUNK = 64
RW_BLOCK_CHUNKS = 4
RW_HEAD_PAIRS = RW_HEADS // 2
assert RW_CHUNK == RW_HEAD_DIM and 2 * RW_HEAD_DIM == 128

SSD_HEAD_DIM = 64
SSD_DIM = D_MODEL // 2
SSD_HEADS = SSD_DIM // SSD_HEAD_DIM
SSD_GROUPS = 2
SSD_STATE = 128
SSD_CONV_DIM = SSD_DIM + 2 * SSD_GROUPS * SSD_STATE
SSD_CHUNK = 128
SSD_BLOCK_CHUNKS = 2
SSD_GROUP_HEADS = SSD_HEADS // SSD_GROUPS

HG_HEADS = 8
HG_KEY_DIM = 128
HG_VAL_DIM = D_MODEL // HG_HEADS
HG_KEY_WIDTH = HG_HEADS * HG_KEY_DIM
HG_VAL_WIDTH = HG_HEADS * HG_VAL_DIM
HG_CHUNK = 64
HG_SUB = 16
HG_BLOCK_CHUNKS = 4
HG_SAFE_LOG_RANGE = 64.0
assert HG_KEY_DIM == HG_VAL_DIM

XA_HEADS = 4
XA_HEAD_DIM = D_MODEL // XA_HEADS
FFN_DIM = ((8 * D_MODEL + 3 * 256 - 1) // (3 * 256)) * 256

SUBLANES = 8
HALO_ROWS = 16
LANES = 128
VMEM_LIMIT_BYTES = 56 * 1024 * 1024
NEG_BIG = -1e30


def _params(*semantics):
    return pltpu.CompilerParams(dimension_semantics=semantics, vmem_limit_bytes=VMEM_LIMIT_BYTES)


def _tile(n, pref):
    t = min(n, pref)
    while n % t or t % SUBLANES:
        t -= 1
    return t


def _mm(a, b):
    return jnp.dot(a.astype(BF16), b.astype(BF16), preferred_element_type=F32)


def _mm_nt(a, b):
    return lax.dot_general(a.astype(BF16), b.astype(BF16), (((1,), (1,)), ((), ())),
                           preferred_element_type=F32)


def _mm_tn(a, b):
    return lax.dot_general(a.astype(BF16), b.astype(BF16), (((0,), (0,)), ((), ())),
                           preferred_element_type=F32)


def _split3(x):
    hi = x.astype(BF16)
    r1 = x - hi.astype(F32)
    mid = r1.astype(BF16)
    lo = (r1 - mid.astype(F32)).astype(BF16)
    return hi, mid, lo


def _mm_exact_lhs(m_bf16, x):
    hi, mid, lo = _split3(x)
    dot = functools.partial(jnp.dot, preferred_element_type=F32)
    return dot(m_bf16, hi) + dot(m_bf16, mid) + dot(m_bf16, lo)


def _mm_exact_rhs(x, m_bf16, terms=3):
    dot = functools.partial(jnp.dot, preferred_element_type=F32)
    pieces = _split3(x)[:terms]
    out = dot(pieces[0], m_bf16)
    for piece in pieces[1:]:
        out = out + dot(piece, m_bf16)
    return out


def _rms_normed(x, gain):
    ms = jnp.mean(x * x, axis=-1, keepdims=True)
    return x * lax.rsqrt(ms + NORM_EPS) * gain


def _sigmoid(x):
    return 0.5 * jnp.tanh(0.5 * x) + 0.5


def _silu(x):
    return x * _sigmoid(x)


def _softplus(x):
    return jnp.maximum(x, 0.0) + jnp.log(1.0 + jnp.exp(-jnp.abs(x)))


def _order_masks(n, reverse):
    row = lax.broadcasted_iota(jnp.int32, (n, n), 0)
    col = lax.broadcasted_iota(jnp.int32, (n, n), 1)
    if reverse:
        return col >= row, col > row
    return col <= row, col < row


def _norm_matmul_kernel(x_ref, g_ref, *refs, n_out):
    xn = _rms_normed(x_ref[...], g_ref[...]).astype(BF16)
    for w_ref, o_ref in zip(refs[:n_out], refs[n_out:]):
        o_ref[...] = jnp.dot(xn, w_ref[...], preferred_element_type=F32).astype(o_ref.dtype)


def _norm_matmul(x2d, gain, weights, out_dtypes, tm_pref=512):
    n, d = x2d.shape
    tm = _tile(n, tm_pref)
    in_specs = [pl.BlockSpec((tm, d), lambda i: (i, 0)), pl.BlockSpec((1, d), lambda i: (0, 0))]
    in_specs += [pl.BlockSpec(w.shape, lambda i: (0, 0)) for w in weights]
    out_specs = [pl.BlockSpec((tm, w.shape[1]), lambda i: (i, 0)) for w in weights]
    out_shape = [jax.ShapeDtypeStruct((n, w.shape[1]), dt) for w, dt in zip(weights, out_dtypes)]
    return pl.pallas_call(
        functools.partial(_norm_matmul_kernel, n_out=len(weights)),
        grid=(n // tm,), in_specs=in_specs, out_specs=out_specs, out_shape=out_shape,
        compiler_params=_params("parallel"), name="norm_matmul",
    )(x2d, gain.reshape(1, d), *weights)


def _halo_specs(width, tt, t_len, n_rows):
    assert tt % HALO_ROWS == 0
    per_seq = t_len // tt
    blocks_per_tile = tt // HALO_ROWS
    n_blocks = n_rows // HALO_ROWS

    def cur(b, j):
        return (b * per_seq + j, 0)

    def prev(b, j):
        return (jnp.maximum((b * per_seq + j) * blocks_per_tile - 1, 0), 0)

    def nxt(b, j):
        return (jnp.minimum((b * per_seq + j + 1) * blocks_per_tile, n_blocks - 1), 0)

    return [pl.BlockSpec((tt, width), cur), pl.BlockSpec((HALO_ROWS, width), prev),
            pl.BlockSpec((HALO_ROWS, width), nxt)]


def _neighbours(p, prev_blk, next_blk, first, last):
    tt = p.shape[0]
    ridx = lax.broadcasted_iota(jnp.int32, p.shape, 0)
    prev_row = jnp.where(first, 0.0, prev_blk[HALO_ROWS - 1:HALO_ROWS, :].astype(F32))
    next_row = jnp.where(last, 0.0, next_blk[0:1, :].astype(F32))
    prev = jnp.where(ridx == 0, prev_row, pltpu.roll(p, 1, 0))
    nxt = jnp.where(ridx == tt - 1, next_row, pltpu.roll(p, tt - 1, 0))
    return prev, nxt


def _with_halo(x_ref, prev_ref, next_ref, first, last):
    assert x_ref.dtype == BF16
    prev_blk = jnp.where(first, jnp.zeros_like(prev_ref[...]), prev_ref[...])
    next_blk = jnp.where(last, jnp.zeros_like(next_ref[...]), next_ref[...])
    return jnp.concatenate([prev_blk, x_ref[...], next_blk], axis=0)


def _shift_matrix(tt, offsets, weight):
    t = jnp.arange(tt)[:, None]
    c = jnp.arange(tt + 2 * HALO_ROWS)[None, :]
    hit = functools.reduce(jnp.logical_or, [c == HALO_ROWS + t + d for d in offsets])
    return jnp.where(hit, weight, 0.0).astype(BF16)


def _head_sum(x, blockdiag_bf16):
    return _mm_exact_rhs(x, blockdiag_bf16, terms=2)


def _rwkv_prep_kernel(p_ref, pp_ref, pn_ref, l_ref, lp_ref, ln_ref, savg_ref, mu_ref, mul_ref, a0_ref, a2_ref,
                      g2_ref, kk_ref, ka_ref, w0_ref, w2_ref, rk_ref, bd_ref,
                      r_out, k_out, v_out, kkn_out, kka_out, lwf_out, lwb_out, g_out, bonus_out):
    j = pl.program_id(1)
    first = j == 0
    last = j == pl.num_programs(1) - 1

    def token_shift(x_ref, xp_ref, xn_ref, m_ref):
        x = x_ref[...].astype(F32)
        avg = jnp.dot(savg_ref[...], _with_halo(x_ref, xp_ref, xn_ref, first, last), preferred_element_type=F32)
        return x + m_ref[...] * (avg - x)

    p = token_shift(p_ref, pp_ref, pn_ref, mu_ref)
    lo = token_shift(l_ref, lp_ref, ln_ref, mul_ref)
    r = p[:, :RW_DIM]
    k = p[:, RW_DIM:2 * RW_DIM]
    v = p[:, 2 * RW_DIM:]
    wd = lo[:, :RW_DECAY_RANK]
    ad = lo[:, RW_DECAY_RANK:RW_DECAY_RANK + RW_A_RANK]
    gd = lo[:, RW_DECAY_RANK + RW_A_RANK:]
    bd = bd_ref[...]

    a = _sigmoid(a0_ref[...] + _mm(ad, a2_ref[...]))
    g = _mm(_sigmoid(gd), g2_ref[...])
    kk = k * kk_ref[...]
    kk_norm = jnp.sqrt(_head_sum(kk * kk, bd))
    kk = kk / jnp.maximum(kk_norm, 1e-12)
    k = k * (1.0 + (a - 1.0) * ka_ref[...])
    w_lora = jnp.tanh(wd)

    def log_decay(d):
        wl = w0_ref[d:d + 1, :] + _mm(w_lora, w2_ref[d])
        return -jnp.exp(-_softplus(-wl) - 0.5)

    r_out[...] = r.astype(r_out.dtype)
    k_out[...] = k.astype(k_out.dtype)
    v_out[...] = v.astype(v_out.dtype)
    kkn_out[...] = kk.astype(kkn_out.dtype)
    kka_out[...] = (kk * a).astype(kka_out.dtype)
    lwf_out[...] = log_decay(0)
    lwb_out[...] = log_decay(1)
    g_out[...] = g.astype(g_out.dtype)
    bonus_out[...] = (_head_sum(r * k * rk_ref[...], bd) * v).astype(bonus_out.dtype)


def _rwkv_prep(p_rkv, p_lora, nb, t_len, mu_rkv, mu_lora, a0, a2, g2, k_k, k_a, w0, w2, r_k, blockdiag):
    n = p_rkv.shape[0]
    tt = _tile(t_len, 256)
    c3, cl = p_rkv.shape[1], p_lora.shape[1]
    halo_specs = _halo_specs(c3, tt, t_len, n) + _halo_specs(cl, tt, t_len, n)

    def full(a):
        nd = a.ndim
        return pl.BlockSpec(a.shape, lambda b, j: (0,) * nd)

    consts = (_shift_matrix(tt, (-1, 1), 0.5), mu_rkv, mu_lora, a0, a2, g2, k_k, k_a, w0, w2, r_k, blockdiag)
    out_spec = pl.BlockSpec((tt, RW_DIM), lambda b, j: (b * (t_len // tt) + j, 0))
    return pl.pallas_call(
        _rwkv_prep_kernel,
        grid=(nb, t_len // tt),
        in_specs=halo_specs + [full(a) for a in consts],
        out_specs=[out_spec] * 9,
        out_shape=[jax.ShapeDtypeStruct((n, RW_DIM), dt) for dt in (ACT, ACT, ACT, ACT, ACT, F32, F32, ACT, ACT)],
        compiler_params=_params("parallel", "parallel"), name="rwkv_prep",
    )(p_rkv, p_rkv, p_rkv, p_lora, p_lora, p_lora, *consts)


def _pair_blockdiag(y):
    lane = lax.broadcasted_iota(jnp.int32, y.shape, 1)
    top = jnp.where(lane < RW_HEAD_DIM, y, 0.0).astype(BF16)
    bottom = jnp.where(lane >= RW_HEAD_DIM, y, 0.0).astype(BF16)
    return jnp.concatenate([top, bottom], axis=0)


def _pair_diag_blocks(full):
    lane = lax.broadcasted_iota(jnp.int32, (RW_HEAD_DIM, LANES), 1)
    return jnp.where(lane < RW_HEAD_DIM, full[:RW_HEAD_DIM], full[RW_HEAD_DIM:])


def _unit_lower_inverses(n_strict_list, eye):
    size = n_strict_list[0].shape[0]
    levels = int(math.log2(size)) - 1
    accs = [eye + n for n in n_strict_list]
    powers = [_mm(n, _pair_blockdiag(n)) for n in n_strict_list]
    for level in range(levels):
        power_bds = [_pair_blockdiag(p) for p in powers]
        if level == levels - 1:
            return [a + _mm(a, b) for a, b in zip(accs, power_bds)]
        both = [_mm(jnp.concatenate([a, p], axis=0), b) for a, p, b in zip(accs, powers, power_bds)]
        accs = [a + ab[:size] for a, ab in zip(accs, both)]
        powers = [ab[size:] for ab in both]


def _rwkv_scan_kernel(r_ref, k_ref, v_ref, kk_ref, kka_ref, lw_ref, o_ref, h_ref, *, reverse):
    @pl.when(pl.program_id(1) == 0)
    def _():
        h_ref[...] = jnp.zeros_like(h_ref)

    chunk = RW_CHUNK
    n_chunks = r_ref.shape[0] // chunk
    incl_bf = _order_masks(chunk, reverse)[0].astype(BF16)
    row = lax.broadcasted_iota(jnp.int32, (chunk, LANES), 0)
    pos = lax.broadcasted_iota(jnp.int32, (chunk, LANES), 1) % RW_HEAD_DIM
    incl, strict = (pos >= row, pos > row) if reverse else (pos <= row, pos < row)
    eye = (pos == row).astype(F32)
    end = 0 if reverse else chunk - 1
    chunk_order = list(range(n_chunks - 1, -1, -1)) if reverse else list(range(n_chunks))

    items = []
    for ci in chunk_order:
        rows = slice(ci * chunk, (ci + 1) * chunk)
        lw = lw_ref[rows, :]
        cum = _mm_exact_lhs(incl_bf, lw)
        cum_end = cum[end:end + 1, :]
        e_neg = jnp.exp(-cum)
        e_tail = jnp.exp(cum_end - cum)
        p_end = jnp.exp(cum_end)
        k = k_ref[rows, :].astype(F32)
        kka = kka_ref[rows, :].astype(F32)
        at_all = -kk_ref[rows, :].astype(F32) * jnp.exp(cum - lw)
        bt_all = kka * e_neg
        kt_all = k * e_neg
        rt_all = r_ref[rows, :].astype(F32) * jnp.exp(cum)
        bh_all = kka * e_tail
        kh_all = k * e_tail
        v = v_ref[rows, :].astype(F32)
        for j in range(RW_HEAD_PAIRS):
            sl = slice(j * LANES, (j + 1) * LANES)
            items.append(dict(ci=ci, j=j, at=at_all[:, sl], bt=bt_all[:, sl], kt=kt_all[:, sl], rt=rt_all[:, sl],
                              bh=bh_all[:, sl], kh=kh_all[:, sl], v=v[:, sl], p_end=p_end[:, sl]))

    for it in items:
        lhs = jnp.concatenate([it["at"], it["rt"]], axis=0)
        ab_rb = _mm_nt(lhs, _pair_blockdiag(it["bt"]))
        ak_rk = _mm_nt(lhs, _pair_blockdiag(it["kt"]))
        it["a_ab"] = jnp.where(strict, ab_rb[:chunk], 0.0)
        it["a_rb"] = jnp.where(incl, ab_rb[chunk:], 0.0)
        it["a_ak"] = jnp.where(strict, ak_rk[:chunk], 0.0)
        it["a_rk"] = jnp.where(incl, ak_rk[chunk:], 0.0)
    t_invs = _unit_lower_inverses([it["a_ab"] for it in items], eye)
    for it in items:
        akv_rkv = _mm(jnp.concatenate([it["a_ak"], it["a_rk"]], axis=0), _pair_blockdiag(it["v"]))
        it["x1"] = akv_rkv[:chunk]
        it["rk_v"] = akv_rkv[chunk:]
    for it, t_inv in zip(items, t_invs):
        it["w_u"] = _mm(t_inv, _pair_blockdiag(it["at"]))
        it["u0"] = _mm(t_inv, _pair_blockdiag(it["x1"]))
    for it in items:
        w_o = it["rt"] + _mm(it["a_rb"], _pair_blockdiag(it["w_u"]))
        g_mat = eye * it["p_end"] + _pair_diag_blocks(_mm_tn(it["bh"], it["w_u"]))
        it["wo_g"] = jnp.concatenate([w_o, g_mat], axis=0)
        it["o0"] = _mm(it["a_rb"], _pair_blockdiag(it["u0"])) + it["rk_v"]
        it["h0"] = _pair_diag_blocks(_mm_tn(it["bh"], it["u0"]) + _mm_tn(it["kh"], it["v"]))

    states = [h_ref[j] for j in range(RW_HEAD_PAIRS)]
    for it in items:
        j = it["j"]
        res = _mm(it["wo_g"], _pair_blockdiag(states[j]))
        rows = slice(it["ci"] * chunk, (it["ci"] + 1) * chunk)
        o_ref[rows, j * LANES:(j + 1) * LANES] = (res[:chunk] + it["o0"]).astype(o_ref.dtype)
        states[j] = res[chunk:] + it["h0"]
    for j in range(RW_HEAD_PAIRS):
        h_ref[j] = states[j]


def _rwkv_scan(r, k, v, kk, kka, lw, nb, t_len, reverse):
    n = r.shape[0]
    tb = _tile(t_len, RW_BLOCK_CHUNKS * RW_CHUNK)
    assert tb % RW_CHUNK == 0
    nc = t_len // tb

    def idx(b, c):
        return (b * nc + (nc - 1 - c if reverse else c), 0)

    spec = pl.BlockSpec((tb, RW_DIM), idx)
    return pl.pallas_call(
        functools.partial(_rwkv_scan_kernel, reverse=reverse),
        grid=(nb, nc), in_specs=[spec] * 6, out_specs=spec,
        out_shape=jax.ShapeDtypeStruct((n, RW_DIM), ACT),
        scratch_shapes=[pltpu.VMEM((RW_HEAD_PAIRS, RW_HEAD_DIM, LANES), F32)],
        compiler_params=_params("parallel", "arbitrary"), name="rwkv_scan_bwd" if reverse else "rwkv_scan_fwd",
    )(r, k, v, kk, kka, lw)


def _ssd_prep_kernel(u_ref, up_ref, un_ref, dt_ref, cw_ref, cb_ref, dtb_ref, alog_ref,
                     xbc_out, dtf_out, dtb_out, laf_out, lab_out):
    j = pl.program_id(1)
    u = u_ref[...].astype(F32)
    prev, nxt = _neighbours(u, up_ref[...], un_ref[...], j == 0, j == pl.num_programs(1) - 1)
    conv = prev * cw_ref[0:1, :] + u * cw_ref[1:2, :] + nxt * cw_ref[2:3, :] + cb_ref[...]
    xbc_out[...] = _silu(conv).astype(xbc_out.dtype)
    dt_raw = dt_ref[...]
    dt_f = _softplus(dt_raw + dtb_ref[0:1, :])
    dt_b = _softplus(dt_raw + dtb_ref[1:2, :])
    dtf_out[...] = dt_f
    dtb_out[...] = dt_b
    laf_out[...] = dt_f * (-jnp.exp(alog_ref[0:1, :]))
    lab_out[...] = dt_b * (-jnp.exp(alog_ref[1:2, :]))


def _ssd_prep(p_xbc, p_dt, nb, t_len, conv_w, conv_b, dt_bias, a_log):
    n = p_xbc.shape[0]
    tt = _tile(t_len, 512)
    halo_specs = _halo_specs(SSD_CONV_DIM, tt, t_len, n)

    def row(b, j):
        return (b * (t_len // tt) + j, 0)

    def full(a):
        return pl.BlockSpec(a.shape, lambda b, j: (0, 0))

    consts = (conv_w, conv_b, dt_bias, a_log)
    small = pl.BlockSpec((tt, LANES), row)
    return pl.pallas_call(
        _ssd_prep_kernel,
        grid=(nb, t_len // tt),
        in_specs=halo_specs + [small] + [full(a) for a in consts],
        out_specs=[pl.BlockSpec((tt, SSD_CONV_DIM), row)] + [small] * 4,
        out_shape=[jax.ShapeDtypeStruct((n, SSD_CONV_DIM), ACT)] + [jax.ShapeDtypeStruct((n, LANES), F32)] * 4,
        compiler_params=_params("parallel", "parallel"), name="ssd_prep",
    )(p_xbc, p_xbc, p_xbc, p_dt, *consts)


def _ssd_scan_kernel(xbc_ref, dt_ref, la_ref, ex64_ref, ex128_ref, y_ref, s_ref, *, reverse):
    @pl.when(pl.program_id(1) == 0)
    def _():
        s_ref[...] = jnp.zeros_like(s_ref)

    chunk = SSD_CHUNK
    n_chunks = xbc_ref.shape[0] // chunk
    incl, _ = _order_masks(chunk, reverse)
    incl_bf = incl.astype(BF16)
    incl_t_bf = _order_masks(chunk, not reverse)[0].astype(BF16)
    end = 0 if reverse else chunk - 1
    gn = SSD_GROUPS * SSD_STATE
    gw = SSD_GROUP_HEADS * SSD_HEAD_DIM
    chunk_order = list(range(n_chunks - 1, -1, -1)) if reverse else list(range(n_chunks))
    ex64 = ex64_ref[...]
    ex128 = ex128_ref[...]

    pre = []
    for ci in chunk_order:
        rows = slice(ci * chunk, (ci + 1) * chunk)
        la = la_ref[rows, :]
        cum = _mm_exact_lhs(incl_bf, la)
        cum_rows = _mm_exact_rhs(la.T, incl_t_bf)
        cum_end = cum[end:end + 1, :]
        ecum64 = _mm_exact_rhs(jnp.exp(cum), ex64, terms=2)
        etail64 = _mm_exact_rhs(jnp.exp(cum_end - cum), ex64, terms=2)
        dt64 = _mm_exact_rhs(dt_ref[rows, :], ex64, terms=2)
        cumcol = _mm_exact_rhs(cum, ex128)
        xdt = xbc_ref[rows, :SSD_DIM].astype(F32) * dt64
        pre.append(dict(rows=rows, cum_rows=cum_rows, ecum64=ecum64, eend64=ecum64[end:end + 1, :],
                        cumcol=cumcol, xdt=xdt, xtail=xdt * etail64,
                        bm=[xbc_ref[rows, SSD_DIM + g * SSD_STATE:SSD_DIM + (g + 1) * SSD_STATE]
                            for g in range(SSD_GROUPS)],
                        cm=[xbc_ref[rows, SSD_DIM + gn + g * SSD_STATE:SSD_DIM + gn + (g + 1) * SSD_STATE]
                            for g in range(SSD_GROUPS)]))
    for p in pre:
        p["scores"] = [_mm_nt(p["cm"][g], p["bm"][g]) for g in range(SSD_GROUPS)]
        p["inject"] = [_mm_tn(p["bm"][g], p["xtail"][:, g * gw:(g + 1) * gw]) for g in range(SSD_GROUPS)]
    for p in pre:
        p["ydiag"] = []
        for h in range(SSD_HEADS):
            diff = p["cumcol"][:, h * LANES:(h + 1) * LANES] - p["cum_rows"][h:h + 1, :]
            decay = jnp.exp(jnp.where(incl, diff, NEG_BIG))
            w = p["scores"][h // SSD_GROUP_HEADS] * decay
            p["ydiag"].append(_mm(w, p["xdt"][:, h * SSD_HEAD_DIM:(h + 1) * SSD_HEAD_DIM]))

    states = [s_ref[g] for g in range(SSD_GROUPS)]
    for p in pre:
        for g in range(SSD_GROUPS):
            y_off = _mm(p["cm"][g], states[g]) * p["ecum64"][:, g * gw:(g + 1) * gw]
            for jh in range(SSD_GROUP_HEADS):
                h = g * SSD_GROUP_HEADS + jh
                y_ref[p["rows"], h * SSD_HEAD_DIM:(h + 1) * SSD_HEAD_DIM] = (
                    p["ydiag"][h] + y_off[:, jh * SSD_HEAD_DIM:(jh + 1) * SSD_HEAD_DIM]).astype(y_ref.dtype)
            states[g] = states[g] * p["eend64"][:, g * gw:(g + 1) * gw] + p["inject"][g]
    for g in range(SSD_GROUPS):
        s_ref[g] = states[g]


def _ssd_scan(xbc, dt, la, nb, t_len, reverse):
    n = xbc.shape[0]
    tb = _tile(t_len, SSD_BLOCK_CHUNKS * SSD_CHUNK)
    assert tb % SSD_CHUNK == 0 and SSD_CHUNK == LANES
    nc = t_len // tb
    lane_head = jnp.arange(LANES)[:, None]
    ex64 = (lane_head == jnp.arange(SSD_DIM)[None, :] // SSD_HEAD_DIM).astype(BF16)
    ex128 = (lane_head == jnp.arange(SSD_HEADS * LANES)[None, :] // LANES).astype(BF16)

    def idx(b, c):
        return (b * nc + (nc - 1 - c if reverse else c), 0)

    def full(a):
        return pl.BlockSpec(a.shape, lambda b, c: (0, 0))

    return pl.pallas_call(
        functools.partial(_ssd_scan_kernel, reverse=reverse),
        grid=(nb, nc),
        in_specs=[pl.BlockSpec((tb, SSD_CONV_DIM), idx), pl.BlockSpec((tb, LANES), idx),
                  pl.BlockSpec((tb, LANES), idx), full(ex64), full(ex128)],
        out_specs=pl.BlockSpec((tb, SSD_DIM), idx),
        out_shape=jax.ShapeDtypeStruct((n, SSD_DIM), ACT),
        scratch_shapes=[pltpu.VMEM((SSD_GROUPS, SSD_STATE, SSD_GROUP_HEADS * SSD_HEAD_DIM), F32)],
        compiler_params=_params("parallel", "arbitrary"), name="ssd_scan_bwd" if reverse else "ssd_scan_fwd",
    )(xbc, dt, la, ex64, ex128)


def _mix0_out_kernel(x_ref, of_ref, ob_ref, bonus_ref, g_ref, yf_ref, yb_ref, xbc_ref, z_ref,
                     gnw_ref, gnb_ref, bd_ref, dsk_ref, nw_ref, wo_rw_ref, wo_ssd_ref, out_ref):
    bd = bd_ref[...]
    o = of_ref[...].astype(F32) + ob_ref[...].astype(F32)
    inv_n = 1.0 / RW_HEAD_DIM
    mu = _head_sum(o, bd) * inv_n
    oc = o - mu
    var = _head_sum(oc * oc, bd) * inv_n
    o = oc * lax.rsqrt(var + RW_GN_EPS) * gnw_ref[...] + gnb_ref[...]
    o_rw = (o + bonus_ref[...].astype(F32)) * g_ref[...].astype(F32)

    y = yf_ref[...].astype(F32) + yb_ref[...].astype(F32) + dsk_ref[...] * xbc_ref[...].astype(F32)
    y = y * _silu(z_ref[...].astype(F32))
    gw = SSD_DIM // SSD_GROUPS
    parts = []
    for g in range(SSD_GROUPS):
        yg = y[:, g * gw:(g + 1) * gw]
        parts.append(yg * lax.rsqrt(jnp.mean(yg * yg, axis=-1, keepdims=True) + NORM_EPS))
    o_ssd = jnp.concatenate(parts, axis=-1) * nw_ref[...]
    out_ref[...] = x_ref[...] + _mm(o_rw, wo_rw_ref[...]) + _mm(o_ssd, wo_ssd_ref[...])


def _mix0_out(x2d, o_f, o_b, bonus, g, y_f, y_b, xbc, z, gn_w, gn_b, blockdiag, d_skip, norm_w, wo_rw, wo_ssd):
    n = x2d.shape[0]
    tm = _tile(n, 512)

    def row(width):
        return pl.BlockSpec((tm, width), lambda i: (i, 0))

    def full(a):
        return pl.BlockSpec(a.shape, lambda i: (0, 0))

    consts = (gn_w, gn_b, blockdiag, d_skip, norm_w, wo_rw, wo_ssd)
    return pl.pallas_call(
        _mix0_out_kernel,
        grid=(n // tm,),
        in_specs=[row(D_MODEL)] + [row(RW_DIM)] * 6 + [row(SSD_DIM), row(SSD_DIM)] + [full(a) for a in consts],
        out_specs=row(D_MODEL),
        out_shape=jax.ShapeDtypeStruct((n, D_MODEL), F32),
        compiler_params=_params("parallel"), name="mix0_out",
    )(x2d, o_f, o_b, bonus, g, y_f, y_b, xbc, z, *consts)


def _hgrn_lower_bound(lb_ref, layer):
    x = lb_ref[...]
    m = jnp.max(x, axis=0, keepdims=True)
    e = jnp.exp(x - m)
    s = e / jnp.sum(e, axis=0, keepdims=True)
    lb = jnp.zeros_like(s[0:1, :])
    for i in range(1, layer + 1):
        lb = lb + s[i:i + 1, :]
    return lb


def _hgrn_chunk(q, f_raw, vv, lb, state_t, reverse):
    chunk = q.shape[0]
    n_sub = chunk // HG_SUB
    incl, _ = _order_masks(chunk, reverse)
    end = 0 if reverse else chunk - 1
    s_idx = lax.broadcasted_iota(jnp.int32, (HG_SUB, 1), 0)

    f = lb + (1.0 - lb) * _sigmoid(f_raw)
    kg = 1.0 - f
    bcum = _mm_exact_lhs(incl.astype(BF16), jnp.log(f))
    b_end = bcum[end:end + 1, :]
    carried = _mm_nt(q * jnp.exp(bcum), state_t)

    order = list(range(n_sub - 1, -1, -1)) if reverse else list(range(n_sub))
    outs = [None] * n_sub
    for pos, sb in enumerate(order):
        rows = slice(sb * HG_SUB, (sb + 1) * HG_SUB)
        q_s, k_s, b_s, v_s = q[rows], kg[rows], bcum[rows], vv[rows]
        acc = carried[rows]
        if pos > 0:
            prev_sb = order[pos - 1]
            edge = prev_sb * HG_SUB if reverse else prev_sb * HG_SUB + HG_SUB - 1
            b_edge = bcum[edge:edge + 1, :]
            src = slice((sb + 1) * HG_SUB, chunk) if reverse else slice(0, sb * HG_SUB)
            q_hat = q_s * jnp.exp(b_s - b_edge)
            k_hat = kg[src] * jnp.exp(b_edge - bcum[src])
            acc = acc + _mm(_mm_nt(q_hat, k_hat), vv[src])
        rows_out = []
        for l in range(HG_SUB):
            w = jnp.exp(jnp.minimum(b_s[l:l + 1, :] - b_s, 0.0)) * k_s * q_s[l:l + 1, :]
            att = jnp.sum(w, axis=-1, keepdims=True)
            att = jnp.where((s_idx >= l) if reverse else (s_idx <= l), att, 0.0)
            rows_out.append(jnp.sum(att * v_s, axis=0, keepdims=True))
        outs[sb] = acc + jnp.concatenate(rows_out, axis=0)

    k_tail = kg * jnp.exp(b_end - bcum)
    new_state_t = state_t * jnp.exp(b_end) + _mm_tn(vv, k_tail)
    return jnp.concatenate(outs, axis=0), new_state_t


def _hgrn_scan_kernel(q_ref, f_ref, i_ref, lb_ref, o_ref, s_ref, hq_ref, hf_ref, hv_ref, ho_ref, *, reverse, layer):
    @pl.when(pl.program_id(1) == 0)
    def _():
        s_ref[...] = jnp.zeros_like(s_ref)

    chunk = HG_CHUNK
    n_chunks = q_ref.shape[0] // chunk
    incl, _ = _order_masks(chunk, reverse)
    incl_bf = incl.astype(BF16)
    end = 0 if reverse else chunk - 1
    chunk_order = list(range(n_chunks - 1, -1, -1)) if reverse else list(range(n_chunks))
    lb = _hgrn_lower_bound(lb_ref, layer)

    def head_cols(h):
        return slice(h * HG_KEY_DIM, (h + 1) * HG_KEY_DIM)

    pre = []
    worst = None
    for ci in chunk_order:
        rows = slice(ci * chunk, (ci + 1) * chunk)
        f = lb + (1.0 - lb) * _sigmoid(f_ref[rows, :])
        bcum = _mm_exact_lhs(incl_bf, jnp.log(f))
        b_end = bcum[end:end + 1, :]
        worst = b_end if worst is None else jnp.minimum(worst, b_end)
        pre.append((ci, rows, f, bcum, b_end))
    safe = jnp.min(worst) >= -HG_SAFE_LOG_RANGE

    @pl.when(safe)
    def _():
        items = []
        for ci, rows, f, bcum, b_end in pre:
            kg = 1.0 - f
            qt_all = q_ref[rows, :].astype(F32) * jnp.exp(bcum)
            kt_all = kg * jnp.exp(-bcum)
            ktail_all = kg * jnp.exp(b_end - bcum)
            e_end = jnp.exp(b_end)
            v_all = i_ref[rows, :].astype(F32)
            for h in range(HG_HEADS):
                c = head_cols(h)
                items.append(dict(rows=rows, h=h, qt=qt_all[:, c], kt=kt_all[:, c], ktail=ktail_all[:, c],
                                  v=v_all[:, c], e_end=e_end[:, c]))
        for it in items:
            it["attn"] = jnp.where(incl, _mm_nt(it["qt"], it["kt"]), 0.0)
        for it in items:
            it["local"] = _mm(it["attn"], it["v"])
            it["inject"] = _mm_tn(it["v"], it["ktail"])
        states = [s_ref[h] for h in range(HG_HEADS)]
        for it in items:
            h = it["h"]
            o_ref[it["rows"], head_cols(h)] = (it["local"] + _mm_nt(it["qt"], states[h])).astype(o_ref.dtype)
            states[h] = states[h] * it["e_end"] + it["inject"]
        for h in range(HG_HEADS):
            s_ref[h] = states[h]

    @pl.when(jnp.logical_not(safe))
    def _():
        lb_heads = jnp.concatenate([lb[:, head_cols(h)] for h in range(HG_HEADS)], axis=0)

        def head_body(h, carry):
            lb_h = jnp.sum(jnp.where(lax.broadcasted_iota(jnp.int32, (HG_HEADS, 1), 0) == h, lb_heads, 0.0),
                           axis=0, keepdims=True)
            out, new_state = _hgrn_chunk(hq_ref[h], hf_ref[h], hv_ref[h], lb_h, s_ref[h], reverse)
            ho_ref[h] = out
            s_ref[h] = new_state
            return carry

        def chunk_body(i, carry):
            ci = (n_chunks - 1 - i) if reverse else i
            rows = pl.ds(pl.multiple_of(ci * chunk, chunk), chunk)
            for h in range(HG_HEADS):
                hq_ref[h] = q_ref[rows, head_cols(h)].astype(F32)
                hf_ref[h] = f_ref[rows, head_cols(h)]
                hv_ref[h] = i_ref[rows, head_cols(h)].astype(F32)
            lax.fori_loop(0, HG_HEADS, head_body, 0)
            for h in range(HG_HEADS):
                o_ref[rows, head_cols(h)] = ho_ref[h].astype(o_ref.dtype)
            return carry

        lax.fori_loop(0, n_chunks, chunk_body, 0)


def _hgrn_scan(q, f_raw, i_val, hg_lb, nb, t_len, reverse, layer):
    n = q.shape[0]
    tb = _tile(t_len, HG_BLOCK_CHUNKS * HG_CHUNK)
    assert tb % HG_CHUNK == 0
    n_outer = t_len // tb

    def idx(b, c):
        return (b * n_outer + (n_outer - 1 - c if reverse else c), 0)

    spec = pl.BlockSpec((tb, HG_KEY_WIDTH), idx)
    head_buf = pltpu.VMEM((HG_HEADS, HG_CHUNK, HG_KEY_DIM), F32)
    return pl.pallas_call(
        functools.partial(_hgrn_scan_kernel, reverse=reverse, layer=layer),
        grid=(nb, n_outer),
        in_specs=[spec, spec, spec, pl.BlockSpec(hg_lb.shape, lambda b, c: (0, 0))],
        out_specs=spec,
        out_shape=jax.ShapeDtypeStruct((n, HG_VAL_WIDTH), ACT),
        scratch_shapes=[pltpu.VMEM((HG_HEADS, HG_VAL_DIM, HG_KEY_DIM), F32), head_buf, head_buf, head_buf, head_buf],
        compiler_params=_params("parallel", "arbitrary"),
        name="hgrn_scan_bwd" if reverse else "hgrn_scan_fwd",
    )(q, f_raw, i_val, hg_lb)


def _hgrn_out_kernel(x_ref, of_ref, ob_ref, g_ref, nw_ref, wo_ref, out_ref):
    o = of_ref[...].astype(F32) + ob_ref[...].astype(F32)
    parts = []
    for h in range(HG_HEADS):
        oh = o[:, h * HG_VAL_DIM:(h + 1) * HG_VAL_DIM]
        parts.append(oh * lax.rsqrt(jnp.mean(oh * oh, axis=-1, keepdims=True) + NORM_EPS))
    o = jnp.concatenate(parts, axis=-1) * nw_ref[...] * _silu(g_ref[...].astype(F32))
    out_ref[...] = x_ref[...] + _mm(o, wo_ref[...])


def _hgrn_out(x2d, o_f, o_b, g, norm_w, w_out):
    n = x2d.shape[0]
    tm = _tile(n, 512)
    row = pl.BlockSpec((tm, D_MODEL), lambda i: (i, 0))
    return pl.pallas_call(
        _hgrn_out_kernel,
        grid=(n // tm,),
        in_specs=[row] * 4 + [pl.BlockSpec((1, HG_VAL_WIDTH), lambda i: (0, 0)),
                              pl.BlockSpec(w_out.shape, lambda i: (0, 0))],
        out_specs=row,
        out_shape=jax.ShapeDtypeStruct((n, D_MODEL), F32),
        compiler_params=_params("parallel"), name="hgrn_out",
    )(x2d, o_f, o_b, g, norm_w, w_out)


def _xattn_kernel(x_ref, g_ref, wq_ref, kv_ref, wo_ref, out_ref):
    x = x_ref[...]
    q = _mm(_rms_normed(x, g_ref[...]), wq_ref[...])
    scale = XA_HEAD_DIM ** -0.5
    heads = []
    for h in range(XA_HEADS):
        sl = slice(h * XA_HEAD_DIM, (h + 1) * XA_HEAD_DIM)
        k = kv_ref[:, sl]
        v = kv_ref[:, D_MODEL + h * XA_HEAD_DIM:D_MODEL + (h + 1) * XA_HEAD_DIM]
        s = _mm_nt(q[:, sl], k) * scale
        s = s - jnp.max(s, axis=-1, keepdims=True)
        p = jnp.exp(s)
        p = p / jnp.sum(p, axis=-1, keepdims=True)
        heads.append(_mm(p, v))
    out_ref[...] = x + _mm(jnp.concatenate(heads, axis=-1), wo_ref[...])


def _xattn(x2d, gain, wq, kv, wo, nb, t_len, mem_len):
    n = x2d.shape[0]
    tm = _tile(t_len, 512)
    per_seq = t_len // tm
    row = pl.BlockSpec((tm, D_MODEL), lambda b, j: (b * per_seq + j, 0))
    return pl.pallas_call(
        _xattn_kernel,
        grid=(nb, per_seq),
        in_specs=[row, pl.BlockSpec((1, D_MODEL), lambda b, j: (0, 0)),
                  pl.BlockSpec(wq.shape, lambda b, j: (0, 0)),
                  pl.BlockSpec((mem_len, 2 * D_MODEL), lambda b, j: (b, 0)),
                  pl.BlockSpec(wo.shape, lambda b, j: (0, 0))],
        out_specs=row,
        out_shape=jax.ShapeDtypeStruct((n, D_MODEL), F32),
        compiler_params=_params("parallel", "parallel"), name="xattn",
    )(x2d, gain.reshape(1, D_MODEL), wq, kv, wo)


def _ffn_kernel(x_ref, g_ref, wi_ref, wo_ref, fg_ref, out_ref, *, final_norm):
    x = x_ref[...]
    gu = _mm(_rms_normed(x, g_ref[...]), wi_ref[...])
    act = _silu(gu[:, :FFN_DIM]) * gu[:, FFN_DIM:]
    y = x + _mm(act, wo_ref[...])
    if final_norm:
        y = _rms_normed(y, fg_ref[...])
    out_ref[...] = y


def _ffn(x2d, gain, w_in, w_out, final_gain, final_norm):
    n = x2d.shape[0]
    tm = _tile(n, 256)
    row = pl.BlockSpec((tm, D_MODEL), lambda i: (i, 0))
    vec = pl.BlockSpec((1, D_MODEL), lambda i: (0, 0))
    return pl.pallas_call(
        functools.partial(_ffn_kernel, final_norm=final_norm),
        grid=(n // tm,),
        in_specs=[row, vec, pl.BlockSpec(w_in.shape, lambda i: (0, 0)),
                  pl.BlockSpec(w_out.shape, lambda i: (0, 0)), vec],
        out_specs=row,
        out_shape=jax.ShapeDtypeStruct((n, D_MODEL), F32),
        compiler_params=_params("parallel"), name="ffn",
    )(x2d, gain.reshape(1, D_MODEL), w_in, w_out, final_gain.reshape(1, D_MODEL))


def _pad_lanes(a, width=LANES):
    return jnp.pad(a, [(0, 0)] * (a.ndim - 1) + [(0, width - a.shape[-1])])


def _rwkv_ssd_layer(x2d, nb, t_len, norm_g, w_in, w_out, rw_mu, rw_w0, rw_w2, rw_a0, rw_a2, rw_g2, rw_k_k, rw_k_a,
                    rw_r_k, rw_gn_w, rw_gn_b, conv_w, conv_b, dt_bias, a_log, d_skip, ssd_norm_w):
    o_lora = 3 * RW_DIM
    o_z = RW_PROJ
    o_xbc = o_z + SSD_DIM
    o_dt = o_xbc + SSD_CONV_DIM
    wb = w_in.astype(BF16)
    weights = [wb[:, :o_lora], wb[:, o_lora:o_z], wb[:, o_z:o_xbc], wb[:, o_xbc:o_dt], _pad_lanes(wb[:, o_dt:])]
    p_rkv, p_lora, p_z, p_xbc, p_dt = _norm_matmul(x2d, norm_g, weights, [ACT, ACT, ACT, ACT, F32])

    head_id = jnp.arange(RW_DIM) // RW_HEAD_DIM
    blockdiag = (head_id[:, None] == head_id[None, :]).astype(BF16)
    row = lambda a: a.reshape(1, -1)
    r, k, v, kk, kka, lw_f, lw_b, g, bonus = _rwkv_prep(
        p_rkv, p_lora, nb, t_len, row(rw_mu[:o_lora]), row(rw_mu[o_lora:]), row(rw_a0), rw_a2.astype(BF16),
        rw_g2.astype(BF16), row(rw_k_k), row(rw_k_a), rw_w0, rw_w2.astype(BF16), row(rw_r_k), blockdiag)
    o_f = _rwkv_scan(r, k, v, kk, kka, lw_f, nb, t_len, reverse=False)
    o_b = _rwkv_scan(r, k, v, kk, kka, lw_b, nb, t_len, reverse=True)

    xbc, dt_f, dt_b, la_f, la_b = _ssd_prep(p_xbc, p_dt, nb, t_len, conv_w, row(conv_b),
                                            _pad_lanes(dt_bias), _pad_lanes(a_log))
    y_f = _ssd_scan(xbc, dt_f, la_f, nb, t_len, reverse=False)
    y_b = _ssd_scan(xbc, dt_b, la_b, nb, t_len, reverse=True)

    wo = w_out.astype(BF16)
    d_skip_lanes = jnp.repeat(d_skip, SSD_HEAD_DIM).reshape(1, SSD_DIM)
    return _mix0_out(x2d, o_f, o_b, bonus, g, y_f, y_b, xbc, p_z, row(rw_gn_w), row(rw_gn_b), blockdiag,
                     d_skip_lanes, row(ssd_norm_w), wo[:RW_DIM], wo[RW_DIM:])


def _hgrn_layer(x2d, nb, t_len, norm_g, w_in, w_out, norm_w, hg_lb, layer):
    kw, vw = HG_KEY_WIDTH, HG_VAL_WIDTH
    wb = w_in.astype(BF16)
    weights = [wb[:, :kw], wb[:, kw:2 * kw], wb[:, 2 * kw:3 * kw], wb[:, 3 * kw:3 * kw + vw], wb[:, 3 * kw + vw:]]
    q, f_fwd, f_bwd, i_val, g = _norm_matmul(x2d, norm_g, weights, [ACT, F32, F32, ACT, ACT])
    o_f = _hgrn_scan(q, f_fwd, i_val, hg_lb, nb, t_len, reverse=False, layer=layer)
    o_b = _hgrn_scan(q, f_bwd, i_val, hg_lb, nb, t_len, reverse=True, layer=layer)
    return _hgrn_out(x2d, o_f, o_b, g, norm_w.reshape(1, vw), w_out.astype(BF16))


def kernel(x, mem, mix_norm, ab_w_in, ab_w_out, rw_mu, rw_w0, rw_w2, rw_a0, rw_a2, rw_g2, rw_k_k, rw_k_a, rw_r_k, rw_gn_w, rw_gn_b, ssd_conv_w, ssd_conv_b, ssd_dt_bias, ssd_a_log, ssd_d, ssd_norm_w, hg_w_in, hg_w_out, hg_norm_w, hg_lb, xa_norm, mem_norm, xa_wq, xa_wkv, xa_wo, ffn_norm, ffn_w_in, ffn_w_out, final_norm):
    nb, t_len, d = x.shape
    mem_len = mem.shape[1]
    depth = mix_norm.shape[0]
    x2d = x.reshape(nb * t_len, d)
    mem2d = mem.reshape(nb * mem_len, d)
    for layer in range(depth):
        if layer % 2 == 0:
            e = layer // 2
            x2d = _rwkv_ssd_layer(x2d, nb, t_len, mix_norm[layer], ab_w_in[e], ab_w_out[e], rw_mu[e], rw_w0[e],
                                  rw_w2[e], rw_a0[e], rw_a2[e], rw_g2[e], rw_k_k[e], rw_k_a[e],
                                  rw_r_k[e].reshape(-1), rw_gn_w[e], rw_gn_b[e], ssd_conv_w[e], ssd_conv_b[e],
                                  ssd_dt_bias[e], ssd_a_log[e], ssd_d[e], ssd_norm_w[e])
        else:
            o = layer // 2
            x2d = _hgrn_layer(x2d, nb, t_len, mix_norm[layer], hg_w_in[o], hg_w_out[o], hg_norm_w[o], hg_lb, layer)
        (kv,) = _norm_matmul(mem2d, mem_norm[layer], [xa_wkv[layer].astype(BF16)], [BF16])
        x2d = _xattn(x2d, xa_norm[layer], xa_wq[layer].astype(BF16), kv, xa_wo[layer].astype(BF16),
                     nb, t_len, mem_len)
        x2d = _ffn(x2d, ffn_norm[layer], ffn_w_in[layer].astype(BF16), ffn_w_out[layer].astype(BF16),
                   final_norm, final_norm=(layer == depth - 1))
    return x2d.reshape(nb, t_len, d)
```

```python
import functools
import math

import jax
import jax.numpy as jnp
from jax import lax
from jax.experimental import pallas as pl
from jax.experimental.pallas import tpu as pltpu

F32 = jnp.float32
BF16 = jnp.bfloat16
ACT = BF16

D_MODEL = 1024
NORM_EPS = 1e-6

RW_HEAD_DIM = 64
RW_DIM = D_MODEL // 2
RW_HEADS = RW_DIM // RW_HEAD_DIM
RW_DECAY_RANK = 64
RW_A_RANK = 64
RW_GATE_RANK = 128
RW_GN_EPS = 64e-5
RW_LORA = RW_DECAY_RANK + RW_A_RANK + RW_GATE_RANK
RW_PROJ = 3 * RW_DIM + RW_LORA
RW_CHUNK = 64
RW_BLOCK_CHUNKS = 4
RW_HEAD_PAIRS = RW_HEADS // 2
assert RW_CHUNK == RW_HEAD_DIM and 2 * RW_HEAD_DIM == 128

SSD_HEAD_DIM = 64
SSD_DIM = D_MODEL // 2
SSD_HEADS = SSD_DIM // SSD_HEAD_DIM
SSD_GROUPS = 2
SSD_STATE = 128
SSD_CONV_DIM = SSD_DIM + 2 * SSD_GROUPS * SSD_STATE
SSD_CHUNK = 128
SSD_BLOCK_CHUNKS = 2
SSD_GROUP_HEADS = SSD_HEADS // SSD_GROUPS

HG_HEADS = 8
HG_KEY_DIM = 128
HG_VAL_DIM = D_MODEL // HG_HEADS
HG_KEY_WIDTH = HG_HEADS * HG_KEY_DIM
HG_VAL_WIDTH = HG_HEADS * HG_VAL_DIM
HG_CHUNK = 64
HG_SUB = 16
HG_BLOCK_CHUNKS = 4
HG_SAFE_LOG_RANGE = 64.0
assert HG_KEY_DIM == HG_VAL_DIM

XA_HEADS = 4
XA_HEAD_DIM = D_MODEL // XA_HEADS
FFN_DIM = ((8 * D_MODEL + 3 * 256 - 1) // (3 * 256)) * 256

SUBLANES = 8
HALO_ROWS = 16
LANES = 128
VMEM_LIMIT_BYTES = 56 * 1024 * 1024
NEG_BIG = -1e30


def _params(*semantics):
    return pltpu.CompilerParams(dimension_semantics=semantics, vmem_limit_bytes=VMEM_LIMIT_BYTES)


def _tile(n, pref):
    t = min(n, pref)
    while n % t or t % SUBLANES:
        t -= 1
    return t


def _mm(a, b):
    return jnp.dot(a.astype(BF16), b.astype(BF16), preferred_element_type=F32)


def _mm_nt(a, b):
    return lax.dot_general(a.astype(BF16), b.astype(BF16), (((1,), (1,)), ((), ())),
                           preferred_element_type=F32)


def _mm_tn(a, b):
    return lax.dot_general(a.astype(BF16), b.astype(BF16), (((0,), (0,)), ((), ())),
                           preferred_element_type=F32)


def _split3(x):
    hi = x.astype(BF16)
    r1 = x - hi.astype(F32)
    mid = r1.astype(BF16)
    lo = (r1 - mid.astype(F32)).astype(BF16)
    return hi, mid, lo


def _mm_exact_lhs(m_bf16, x):
    hi, mid, lo = _split3(x)
    dot = functools.partial(jnp.dot, preferred_element_type=F32)
    return dot(m_bf16, hi) + dot(m_bf16, mid) + dot(m_bf16, lo)


def _mm_exact_rhs(x, m_bf16, terms=3):
    dot = functools.partial(jnp.dot, preferred_element_type=F32)
    pieces = _split3(x)[:terms]
    out = dot(pieces[0], m_bf16)
    for piece in pieces[1:]:
        out = out + dot(piece, m_bf16)
    return out


def _rms_normed(x, gain):
    ms = jnp.mean(x * x, axis=-1, keepdims=True)
    return x * lax.rsqrt(ms + NORM_EPS) * gain


def _sigmoid(x):
    return 0.5 * jnp.tanh(0.5 * x) + 0.5


def _silu(x):
    return x * _sigmoid(x)


def _softplus(x):
    return jnp.maximum(x, 0.0) + jnp.log(1.0 + jnp.exp(-jnp.abs(x)))


def _order_masks(n, reverse):
    row = lax.broadcasted_iota(jnp.int32, (n, n), 0)
    col = lax.broadcasted_iota(jnp.int32, (n, n), 1)
    if reverse:
        return col >= row, col > row
    return col <= row, col < row


def _norm_matmul_kernel(x_ref, g_ref, *refs, n_out):
    xn = _rms_normed(x_ref[...], g_ref[...]).astype(BF16)
    for w_ref, o_ref in zip(refs[:n_out], refs[n_out:]):
        o_ref[...] = jnp.dot(xn, w_ref[...], preferred_element_type=F32).astype(o_ref.dtype)


def _norm_matmul(x2d, gain, weights, out_dtypes, tm_pref=512):
    n, d = x2d.shape
    tm = _tile(n, tm_pref)
    in_specs = [pl.BlockSpec((tm, d), lambda i: (i, 0)), pl.BlockSpec((1, d), lambda i: (0, 0))]
    in_specs += [pl.BlockSpec(w.shape, lambda i: (0, 0)) for w in weights]
    out_specs = [pl.BlockSpec((tm, w.shape[1]), lambda i: (i, 0)) for w in weights]
    out_shape = [jax.ShapeDtypeStruct((n, w.shape[1]), dt) for w, dt in zip(weights, out_dtypes)]
    return pl.pallas_call(
        functools.partial(_norm_matmul_kernel, n_out=len(weights)),
        grid=(n // tm,), in_specs=in_specs, out_specs=out_specs, out_shape=out_shape,
        compiler_params=_params("parallel"), name="norm_matmul",
    )(x2d, gain.reshape(1, d), *weights)


def _halo_specs(width, tt, t_len, n_rows):
    assert tt % HALO_ROWS == 0
    per_seq = t_len // tt
    blocks_per_tile = tt // HALO_ROWS
    n_blocks = n_rows // HALO_ROWS

    def cur(b, j):
        return (b * per_seq + j, 0)

    def prev(b, j):
        return (jnp.maximum((b * per_seq + j) * blocks_per_tile - 1, 0), 0)

    def nxt(b, j):
        return (jnp.minimum((b * per_seq + j + 1) * blocks_per_tile, n_blocks - 1), 0)

    return [pl.BlockSpec((tt, width), cur), pl.BlockSpec((HALO_ROWS, width), prev),
            pl.BlockSpec((HALO_ROWS, width), nxt)]


def _neighbours(p, prev_blk, next_blk, first, last):
    tt = p.shape[0]
    ridx = lax.broadcasted_iota(jnp.int32, p.shape, 0)
    prev_row = jnp.where(first, 0.0, prev_blk[HALO_ROWS - 1:HALO_ROWS, :].astype(F32))
    next_row = jnp.where(last, 0.0, next_blk[0:1, :].astype(F32))
    prev = jnp.where(ridx == 0, prev_row, pltpu.roll(p, 1, 0))
    nxt = jnp.where(ridx == tt - 1, next_row, pltpu.roll(p, tt - 1, 0))
    return prev, nxt


def _shift_matrix(tt, offsets, weight):
    t = jnp.arange(tt)[:, None]
    c = jnp.arange(tt + 2 * HALO_ROWS)[None, :]
    hit = functools.reduce(jnp.logical_or, [c == HALO_ROWS + t + d for d in offsets])
    return jnp.where(hit, weight, 0.0).astype(BF16)


def _head_sum(x, blockdiag_bf16):
    return _mm_exact_rhs(x, blockdiag_bf16, terms=2)


def _mix0_in_kernel(x_ref, xp_ref, xn_ref, gain_ref, wrkv_ref, wlora_ref, wz_ref, wxbc_ref, wdt_ref,
                    savg_ref, mu_ref, mul_ref, a0_ref, a2_ref, g2_ref, kk_ref, ka_ref, w0_ref, w2_ref, rk_ref, bd_ref,
                    cw_ref, cb_ref, dtb_ref, alog_ref,
                    r_out, k_out, v_out, kkn_out, kka_out, lwf_out, lwb_out, g_out, bonus_out,
                    z_out, xbc_out, dtf_out, dtb_out, laf_out, lab_out):
    j = pl.program_id(1)
    tt = x_ref.shape[0]
    dot = functools.partial(jnp.dot, preferred_element_type=F32)
    first = j == 0
    last = j == pl.num_programs(1) - 1

    x_ext = jnp.concatenate([jnp.where(first, 0.0, xp_ref[...]), x_ref[...], jnp.where(last, 0.0, xn_ref[...])], axis=0)
    xn_ext = _rms_normed(x_ext, gain_ref[...]).astype(BF16)
    xn = xn_ext[HALO_ROWS:HALO_ROWS + tt]

    def token_shift(w_ref, m_ref):
        p_ext = dot(xn_ext, w_ref[...]).astype(ACT)
        x = p_ext[HALO_ROWS:HALO_ROWS + tt].astype(F32)
        avg = dot(savg_ref[...], p_ext)
        return x + m_ref[...] * (avg - x)

    p = token_shift(wrkv_ref, mu_ref)
    lo = token_shift(wlora_ref, mul_ref)
    r = p[:, :RW_DIM]
    k = p[:, RW_DIM:2 * RW_DIM]
    v = p[:, 2 * RW_DIM:]
    wd = lo[:, :RW_DECAY_RANK]
    ad = lo[:, RW_DECAY_RANK:RW_DECAY_RANK + RW_A_RANK]
    gd = lo[:, RW_DECAY_RANK + RW_A_RANK:]
    bd = bd_ref[...]

    a = _sigmoid(a0_ref[...] + _mm(ad, a2_ref[...]))
    g = _mm(_sigmoid(gd), g2_ref[...])
    kk = k * kk_ref[...]
    kk_norm = jnp.sqrt(_head_sum(kk * kk, bd))
    kk = kk / jnp.maximum(kk_norm, 1e-12)
    k = k * (1.0 + (a - 1.0) * ka_ref[...])
    w_lora = jnp.tanh(wd)

    def log_decay(d):
        wl = w0_ref[d:d + 1, :] + _mm(w_lora, w2_ref[d])
        return -jnp.exp(-_softplus(-wl) - 0.5)

    r_out[...] = r.astype(r_out.dtype)
    k_out[...] = k.astype(k_out.dtype)
    v_out[...] = v.astype(v_out.dtype)
    kkn_out[...] = kk.astype(kkn_out.dtype)
    kka_out[...] = (kk * a).astype(kka_out.dtype)
    lwf_out[...] = log_decay(0)
    lwb_out[...] = log_decay(1)
    g_out[...] = g.astype(g_out.dtype)
    bonus_out[...] = (_head_sum(r * k * rk_ref[...], bd) * v).astype(bonus_out.dtype)

    z_out[...] = dot(xn, wz_ref[...]).astype(z_out.dtype)
    u_ext = dot(xn_ext, wxbc_ref[...]).astype(ACT)
    u = u_ext[HALO_ROWS:HALO_ROWS + tt].astype(F32)
    prev, nxt = _neighbours(u, u_ext[:HALO_ROWS], u_ext[HALO_ROWS + tt:], False, False)
    conv = prev * cw_ref[0:1, :] + u * cw_ref[1:2, :] + nxt * cw_ref[2:3, :] + cb_ref[...]
    xbc_out[...] = _silu(conv).astype(xbc_out.dtype)
    dt_raw = dot(xn, wdt_ref[...])
    dt_f = _softplus(dt_raw + dtb_ref[0:1, :])
    dt_b = _softplus(dt_raw + dtb_ref[1:2, :])
    dtf_out[...] = dt_f
    dtb_out[...] = dt_b
    laf_out[...] = dt_f * (-jnp.exp(alog_ref[0:1, :]))
    lab_out[...] = dt_b * (-jnp.exp(alog_ref[1:2, :]))


def _mix0_in(x2d, nb, t_len, gain, weights, rwkv_consts, ssd_consts):
    n, d = x2d.shape
    tt = _tile(t_len, 256)
    per_seq = t_len // tt

    def row(b, j):
        return (b * per_seq + j, 0)

    def full(a):
        nd = a.ndim
        return pl.BlockSpec(a.shape, lambda b, j: (0,) * nd)

    consts = (gain.reshape(1, d),) + tuple(weights) + (_shift_matrix(tt, (-1, 1), 0.5),) + tuple(rwkv_consts) \
        + tuple(ssd_consts)
    wide = [(RW_DIM, ACT)] * 5 + [(RW_DIM, F32)] * 2 + [(RW_DIM, ACT)] * 2 + [(SSD_DIM, ACT), (SSD_CONV_DIM, ACT)] \
        + [(LANES, F32)] * 4
    return pl.pallas_call(
        _mix0_in_kernel,
        grid=(nb, per_seq),
        in_specs=_halo_specs(d, tt, t_len, n) + [full(a) for a in consts],
        out_specs=[pl.BlockSpec((tt, w), row) for w, _ in wide],
        out_shape=[jax.ShapeDtypeStruct((n, w), dt) for w, dt in wide],
        compiler_params=_params("parallel", "parallel"), name="mix0_in",
    )(x2d, x2d, x2d, *consts)


def _pair_blockdiag(y):
    lane = lax.broadcasted_iota(jnp.int32, y.shape, 1)
    top = jnp.where(lane < RW_HEAD_DIM, y, 0.0).astype(BF16)
    bottom = jnp.where(lane >= RW_HEAD_DIM, y, 0.0).astype(BF16)
    return jnp.concatenate([top, bottom], axis=0)


def _pair_diag_blocks(full):
    lane = lax.broadcasted_iota(jnp.int32, (RW_HEAD_DIM, LANES), 1)
    return jnp.where(lane < RW_HEAD_DIM, full[:RW_HEAD_DIM], full[RW_HEAD_DIM:])


def _unit_lower_inverses(n_strict_list, eye):
    size = n_strict_list[0].shape[0]
    levels = int(math.log2(size)) - 1
    accs = [eye + n for n in n_strict_list]
    powers = [_mm(n, _pair_blockdiag(n)) for n in n_strict_list]
    for level in range(levels):
        power_bds = [_pair_blockdiag(p) for p in powers]
        if level == levels - 1:
            return [a + _mm(a, b) for a, b in zip(accs, power_bds)]
        both = [_mm(jnp.concatenate([a, p], axis=0), b) for a, p, b in zip(accs, powers, power_bds)]
        accs = [a + ab[:size] for a, ab in zip(accs, both)]
        powers = [ab[size:] for ab in both]


def _rwkv_scan_kernel(r_ref, k_ref, v_ref, kk_ref, kka_ref, lw_ref, o_ref, h_ref, *, reverse):
    @pl.when(pl.program_id(1) == 0)
    def _():
        h_ref[...] = jnp.zeros_like(h_ref)

    chunk = RW_CHUNK
    n_chunks = r_ref.shape[0] // chunk
    incl_bf = _order_masks(chunk, reverse)[0].astype(BF16)
    row = lax.broadcasted_iota(jnp.int32, (chunk, LANES), 0)
    pos = lax.broadcasted_iota(jnp.int32, (chunk, LANES), 1) % RW_HEAD_DIM
    incl, strict = (pos >= row, pos > row) if reverse else (pos <= row, pos < row)
    eye = (pos == row).astype(F32)
    end = 0 if reverse else chunk - 1
    chunk_order = list(range(n_chunks - 1, -1, -1)) if reverse else list(range(n_chunks))

    items = []
    for ci in chunk_order:
        rows = slice(ci * chunk, (ci + 1) * chunk)
        lw = lw_ref[rows, :]
        cum = _mm_exact_lhs(incl_bf, lw)
        cum_end = cum[end:end + 1, :]
        e_neg = jnp.exp(-cum)
        e_tail = jnp.exp(cum_end - cum)
        p_end = jnp.exp(cum_end)
        k = k_ref[rows, :].astype(F32)
        kka = kka_ref[rows, :].astype(F32)
        at_all = -kk_ref[rows, :].astype(F32) * jnp.exp(cum - lw)
        bt_all = kka * e_neg
        kt_all = k * e_neg
        rt_all = r_ref[rows, :].astype(F32) * jnp.exp(cum)
        bh_all = kka * e_tail
        kh_all = k * e_tail
        v = v_ref[rows, :].astype(F32)
        for j in range(RW_HEAD_PAIRS):
            sl = slice(j * LANES, (j + 1) * LANES)
            items.append(dict(ci=ci, j=j, at=at_all[:, sl], bt=bt_all[:, sl], kt=kt_all[:, sl], rt=rt_all[:, sl],
                              bh=bh_all[:, sl], kh=kh_all[:, sl], v=v[:, sl], p_end=p_end[:, sl]))

    for it in items:
        lhs = jnp.concatenate([it["at"], it["rt"]], axis=0)
        ab_rb = _mm_nt(lhs, _pair_blockdiag(it["bt"]))
        ak_rk = _mm_nt(lhs, _pair_blockdiag(it["kt"]))
        it["a_ab"] = jnp.where(strict, ab_rb[:chunk], 0.0)
        it["a_rb"] = jnp.where(incl, ab_rb[chunk:], 0.0)
        it["a_ak"] = jnp.where(strict, ak_rk[:chunk], 0.0)
        it["a_rk"] = jnp.where(incl, ak_rk[chunk:], 0.0)
    t_invs = _unit_lower_inverses([it["a_ab"] for it in items], eye)
    for it in items:
        akv_rkv = _mm(jnp.concatenate([it["a_ak"], it["a_rk"]], axis=0), _pair_blockdiag(it["v"]))
        it["x1"] = akv_rkv[:chunk]
        it["rk_v"] = akv_rkv[chunk:]
    for it, t_inv in zip(items, t_invs):
        it["w_u"] = _mm(t_inv, _pair_blockdiag(it["at"]))
        it["u0"] = _mm(t_inv, _pair_blockdiag(it["x1"]))
    for it in items:
        w_o = it["rt"] + _mm(it["a_rb"], _pair_blockdiag(it["w_u"]))
        g_mat = eye * it["p_end"] + _pair_diag_blocks(_mm_tn(it["bh"], it["w_u"]))
        it["wo_g"] = jnp.concatenate([w_o, g_mat], axis=0)
        it["o0"] = _mm(it["a_rb"], _pair_blockdiag(it["u0"])) + it["rk_v"]
        it["h0"] = _pair_diag_blocks(_mm_tn(it["bh"], it["u0"]) + _mm_tn(it["kh"], it["v"]))

    states = [h_ref[j] for j in range(RW_HEAD_PAIRS)]
    for it in items:
        j = it["j"]
        res = _mm(it["wo_g"], _pair_blockdiag(states[j]))
        rows = slice(it["ci"] * chunk, (it["ci"] + 1) * chunk)
        o_ref[rows, j * LANES:(j + 1) * LANES] = (res[:chunk] + it["o0"]).astype(o_ref.dtype)
        states[j] = res[chunk:] + it["h0"]
    for j in range(RW_HEAD_PAIRS):
        h_ref[j] = states[j]


def _rwkv_scan(r, k, v, kk, kka, lw, nb, t_len, reverse):
    n = r.shape[0]
    tb = _tile(t_len, RW_BLOCK_CHUNKS * RW_CHUNK)
    assert tb % RW_CHUNK == 0
    nc = t_len // tb

    def idx(b, c):
        return (b * nc + (nc - 1 - c if reverse else c), 0)

    spec = pl.BlockSpec((tb, RW_DIM), idx)
    return pl.pallas_call(
        functools.partial(_rwkv_scan_kernel, reverse=reverse),
        grid=(nb, nc), in_specs=[spec] * 6, out_specs=spec,
        out_shape=jax.ShapeDtypeStruct((n, RW_DIM), ACT),
        scratch_shapes=[pltpu.VMEM((RW_HEAD_PAIRS, RW_HEAD_DIM, LANES), F32)],
        compiler_params=_params("parallel", "arbitrary"), name="rwkv_scan_bwd" if reverse else "rwkv_scan_fwd",
    )(r, k, v, kk, kka, lw)


def _ssd_scan_kernel(xbc_ref, dt_ref, la_ref, ex64_ref, ex128_ref, y_ref, s_ref, *, reverse):
    @pl.when(pl.program_id(1) == 0)
    def _():
        s_ref[...] = jnp.zeros_like(s_ref)

    chunk = SSD_CHUNK
    n_chunks = xbc_ref.shape[0] // chunk
    incl, _ = _order_masks(chunk, reverse)
    incl_bf = incl.astype(BF16)
    incl_t_bf = _order_masks(chunk, not reverse)[0].astype(BF16)
    end = 0 if reverse else chunk - 1
    gn = SSD_GROUPS * SSD_STATE
    gw = SSD_GROUP_HEADS * SSD_HEAD_DIM
    chunk_order = list(range(n_chunks - 1, -1, -1)) if reverse else list(range(n_chunks))
    ex64 = ex64_ref[...]
    ex128 = ex128_ref[...]

    pre = []
    for ci in chunk_order:
        rows = slice(ci * chunk, (ci + 1) * chunk)
        la = la_ref[rows, :]
        cum = _mm_exact_lhs(incl_bf, la)
        cum_rows = _mm_exact_rhs(la.T, incl_t_bf)
        cum_end = cum[end:end + 1, :]
        ecum64 = _mm_exact_rhs(jnp.exp(cum), ex64, terms=2)
        etail64 = _mm_exact_rhs(jnp.exp(cum_end - cum), ex64, terms=2)
        dt64 = _mm_exact_rhs(dt_ref[rows, :], ex64, terms=2)
        cumcol = _mm_exact_rhs(cum, ex128)
        xdt = xbc_ref[rows, :SSD_DIM].astype(F32) * dt64
        pre.append(dict(rows=rows, cum_rows=cum_rows, ecum64=ecum64, eend64=ecum64[end:end + 1, :],
                        cumcol=cumcol, xdt=xdt, xtail=xdt * etail64,
                        bm=[xbc_ref[rows, SSD_DIM + g * SSD_STATE:SSD_DIM + (g + 1) * SSD_STATE]
                            for g in range(SSD_GROUPS)],
                        cm=[xbc_ref[rows, SSD_DIM + gn + g * SSD_STATE:SSD_DIM + gn + (g + 1) * SSD_STATE]
                            for g in range(SSD_GROUPS)]))
    for p in pre:
        p["scores"] = [_mm_nt(p["cm"][g], p["bm"][g]) for g in range(SSD_GROUPS)]
        p["inject"] = [_mm_tn(p["bm"][g], p["xtail"][:, g * gw:(g + 1) * gw]) for g in range(SSD_GROUPS)]
    for p in pre:
        p["ydiag"] = []
        for h in range(SSD_HEADS):
            diff = p["cumcol"][:, h * LANES:(h + 1) * LANES] - p["cum_rows"][h:h + 1, :]
            decay = jnp.exp(jnp.where(incl, diff, NEG_BIG))
            w = p["scores"][h // SSD_GROUP_HEADS] * decay
            p["ydiag"].append(_mm(w, p["xdt"][:, h * SSD_HEAD_DIM:(h + 1) * SSD_HEAD_DIM]))

    states = [s_ref[g] for g in range(SSD_GROUPS)]
    for p in pre:
        for g in range(SSD_GROUPS):
            y_off = _mm(p["cm"][g], states[g]) * p["ecum64"][:, g * gw:(g + 1) * gw]
            for jh in range(SSD_GROUP_HEADS):
                h = g * SSD_GROUP_HEADS + jh
                y_ref[p["rows"], h * SSD_HEAD_DIM:(h + 1) * SSD_HEAD_DIM] = (
                    p["ydiag"][h] + y_off[:, jh * SSD_HEAD_DIM:(jh + 1) * SSD_HEAD_DIM]).astype(y_ref.dtype)
            states[g] = states[g] * p["eend64"][:, g * gw:(g + 1) * gw] + p["inject"][g]
    for g in range(SSD_GROUPS):
        s_ref[g] = states[g]


def _ssd_scan(xbc, dt, la, nb, t_len, reverse):
    n = xbc.shape[0]
    tb = _tile(t_len, SSD_BLOCK_CHUNKS * SSD_CHUNK)
    assert tb % SSD_CHUNK == 0 and SSD_CHUNK == LANES
    nc = t_len // tb
    lane_head = jnp.arange(LANES)[:, None]
    ex64 = (lane_head == jnp.arange(SSD_DIM)[None, :] // SSD_HEAD_DIM).astype(BF16)
    ex128 = (lane_head == jnp.arange(SSD_HEADS * LANES)[None, :] // LANES).astype(BF16)

    def idx(b, c):
        return (b * nc + (nc - 1 - c if reverse else c), 0)

    def full(a):
        return pl.BlockSpec(a.shape, lambda b, c: (0, 0))

    return pl.pallas_call(
        functools.partial(_ssd_scan_kernel, reverse=reverse),
        grid=(nb, nc),
        in_specs=[pl.BlockSpec((tb, SSD_CONV_DIM), idx), pl.BlockSpec((tb, LANES), idx),
                  pl.BlockSpec((tb, LANES), idx), full(ex64), full(ex128)],
        out_specs=pl.BlockSpec((tb, SSD_DIM), idx),
        out_shape=jax.ShapeDtypeStruct((n, SSD_DIM), ACT),
        scratch_shapes=[pltpu.VMEM((SSD_GROUPS, SSD_STATE, SSD_GROUP_HEADS * SSD_HEAD_DIM), F32)],
        compiler_params=_params("parallel", "arbitrary"), name="ssd_scan_bwd" if reverse else "ssd_scan_fwd",
    )(xbc, dt, la, ex64, ex128)


def _mix0_out_kernel(x_ref, of_ref, ob_ref, bonus_ref, g_ref, yf_ref, yb_ref, xbc_ref, z_ref,
                     gnw_ref, gnb_ref, bd_ref, dsk_ref, nw_ref, wo_rw_ref, wo_ssd_ref, out_ref):
    bd = bd_ref[...]
    o = of_ref[...].astype(F32) + ob_ref[...].astype(F32)
    inv_n = 1.0 / RW_HEAD_DIM
    mu = _head_sum(o, bd) * inv_n
    oc = o - mu
    var = _head_sum(oc * oc, bd) * inv_n
    o = oc * lax.rsqrt(var + RW_GN_EPS) * gnw_ref[...] + gnb_ref[...]
    o_rw = (o + bonus_ref[...].astype(F32)) * g_ref[...].astype(F32)

    y = yf_ref[...].astype(F32) + yb_ref[...].astype(F32) + dsk_ref[...] * xbc_ref[...].astype(F32)
    y = y * _silu(z_ref[...].astype(F32))
    gw = SSD_DIM // SSD_GROUPS
    parts = []
    for g in range(SSD_GROUPS):
        yg = y[:, g * gw:(g + 1) * gw]
        parts.append(yg * lax.rsqrt(jnp.mean(yg * yg, axis=-1, keepdims=True) + NORM_EPS))
    o_ssd = jnp.concatenate(parts, axis=-1) * nw_ref[...]
    out_ref[...] = x_ref[...] + _mm(o_rw, wo_rw_ref[...]) + _mm(o_ssd, wo_ssd_ref[...])


def _mix0_out(x2d, o_f, o_b, bonus, g, y_f, y_b, xbc, z, gn_w, gn_b, blockdiag, d_skip, norm_w, wo_rw, wo_ssd):
    n = x2d.shape[0]
    tm = _tile(n, 512)

    def row(width):
        return pl.BlockSpec((tm, width), lambda i: (i, 0))

    def full(a):
        return pl.BlockSpec(a.shape, lambda i: (0, 0))

    consts = (gn_w, gn_b, blockdiag, d_skip, norm_w, wo_rw, wo_ssd)
    return pl.pallas_call(
        _mix0_out_kernel,
        grid=(n // tm,),
        in_specs=[row(D_MODEL)] + [row(RW_DIM)] * 6 + [row(SSD_DIM), row(SSD_DIM)] + [full(a) for a in consts],
        out_specs=row(D_MODEL),
        out_shape=jax.ShapeDtypeStruct((n, D_MODEL), F32),
        compiler_params=_params("parallel"), name="mix0_out",
    )(x2d, o_f, o_b, bonus, g, y_f, y_b, xbc, z, *consts)


def _hgrn_lower_bound(lb_ref, layer):
    x = lb_ref[...]
    m = jnp.max(x, axis=0, keepdims=True)
    e = jnp.exp(x - m)
    s = e / jnp.sum(e, axis=0, keepdims=True)
    lb = jnp.zeros_like(s[0:1, :])
    for i in range(1, layer + 1):
        lb = lb + s[i:i + 1, :]
    return lb


def _hgrn_in_kernel(x_ref, g_ref, lb_ref, wq_ref, wff_ref, wfb_ref, wi_ref, wg_ref,
                    q_out, lff_out, lfb_out, kgf_out, kgb_out, i_out, g_out, *, layer):
    xn = _rms_normed(x_ref[...], g_ref[...]).astype(BF16)
    lb = _hgrn_lower_bound(lb_ref, layer)
    dot = functools.partial(jnp.dot, preferred_element_type=F32)
    q_out[...] = dot(xn, wq_ref[...]).astype(q_out.dtype)
    for w_ref, lf_out, kg_out in ((wff_ref, lff_out, kgf_out), (wfb_ref, lfb_out, kgb_out)):
        f = lb + (1.0 - lb) * _sigmoid(dot(xn, w_ref[...]))
        lf_out[...] = jnp.log(f)
        kg_out[...] = (1.0 - f).astype(kg_out.dtype)
    i_out[...] = dot(xn, wi_ref[...]).astype(i_out.dtype)
    g_out[...] = dot(xn, wg_ref[...]).astype(g_out.dtype)


def _hgrn_in(x2d, gain, hg_lb, weights, layer):
    n, d = x2d.shape
    tm = _tile(n, 512)
    row = pl.BlockSpec((tm, d), lambda i: (i, 0))
    in_specs = [row, pl.BlockSpec((1, d), lambda i: (0, 0)), pl.BlockSpec(hg_lb.shape, lambda i: (0, 0))]
    in_specs += [pl.BlockSpec(w.shape, lambda i: (0, 0)) for w in weights]
    dtypes = (ACT, F32, F32, ACT, ACT, ACT, ACT)
    return pl.pallas_call(
        functools.partial(_hgrn_in_kernel, layer=layer),
        grid=(n // tm,), in_specs=in_specs, out_specs=[row] * len(dtypes),
        out_shape=[jax.ShapeDtypeStruct((n, d), dt) for dt in dtypes],
        compiler_params=_params("parallel"), name="hgrn_in",
    )(x2d, gain.reshape(1, d), hg_lb, *weights)


def _hgrn_chunk(q, lf, kg, vv, state_t, reverse):
    chunk = q.shape[0]
    n_sub = chunk // HG_SUB
    incl, _ = _order_masks(chunk, reverse)
    end = 0 if reverse else chunk - 1
    s_idx = lax.broadcasted_iota(jnp.int32, (HG_SUB, 1), 0)

    bcum = _mm_exact_lhs(incl.astype(BF16), lf)
    b_end = bcum[end:end + 1, :]
    carried = _mm_nt(q * jnp.exp(bcum), state_t)

    order = list(range(n_sub - 1, -1, -1)) if reverse else list(range(n_sub))
    outs = [None] * n_sub
    for pos, sb in enumerate(order):
        rows = slice(sb * HG_SUB, (sb + 1) * HG_SUB)
        q_s, k_s, b_s, v_s = q[rows], kg[rows], bcum[rows], vv[rows]
        acc = carried[rows]
        if pos > 0:
            prev_sb = order[pos - 1]
            edge = prev_sb * HG_SUB if reverse else prev_sb * HG_SUB + HG_SUB - 1
            b_edge = bcum[edge:edge + 1, :]
            src = slice((sb + 1) * HG_SUB, chunk) if reverse else slice(0, sb * HG_SUB)
            q_hat = q_s * jnp.exp(b_s - b_edge)
            k_hat = kg[src] * jnp.exp(b_edge - bcum[src])
            acc = acc + _mm(_mm_nt(q_hat, k_hat), vv[src])
        rows_out = []
        for l in range(HG_SUB):
            w = jnp.exp(jnp.minimum(b_s[l:l + 1, :] - b_s, 0.0)) * k_s * q_s[l:l + 1, :]
            att = jnp.sum(w, axis=-1, keepdims=True)
            att = jnp.where((s_idx >= l) if reverse else (s_idx <= l), att, 0.0)
            rows_out.append(jnp.sum(att * v_s, axis=0, keepdims=True))
        outs[sb] = acc + jnp.concatenate(rows_out, axis=0)

    k_tail = kg * jnp.exp(b_end - bcum)
    new_state_t = state_t * jnp.exp(b_end) + _mm_tn(vv, k_tail)
    return jnp.concatenate(outs, axis=0), new_state_t


def _hgrn_scan_kernel(q_ref, lf_ref, kg_ref, i_ref, lb_ref, o_ref, s_ref, hq_ref, hf_ref, hk_ref, hv_ref, ho_ref, *,
                      reverse, layer):
    @pl.when(pl.program_id(1) == 0)
    def _():
        s_ref[...] = jnp.zeros_like(s_ref)

    chunk = HG_CHUNK
    n_chunks = q_ref.shape[0] // chunk
    incl, _ = _order_masks(chunk, reverse)
    incl_bf = incl.astype(BF16)
    end = 0 if reverse else chunk - 1
    chunk_order = list(range(n_chunks - 1, -1, -1)) if reverse else list(range(n_chunks))

    def head_cols(h):
        return slice(h * HG_KEY_DIM, (h + 1) * HG_KEY_DIM)

    pre = []
    worst = None
    for ci in chunk_order:
        rows = slice(ci * chunk, (ci + 1) * chunk)
        bcum = _mm_exact_lhs(incl_bf, lf_ref[rows, :])
        b_end = bcum[end:end + 1, :]
        worst = b_end if worst is None else jnp.minimum(worst, b_end)
        pre.append((ci, rows, bcum, b_end))
    lb_min = jnp.min(_hgrn_lower_bound(lb_ref, layer))
    flag = lax.cond(lb_min >= math.exp(-HG_SAFE_LOG_RANGE / HG_CHUNK),
                    lambda: jnp.int32(1),
                    lambda: (jnp.min(worst) >= -HG_SAFE_LOG_RANGE).astype(jnp.int32))
    safe = flag == 1

    @pl.when(safe)
    def _():
        items = []
        for ci, rows, bcum, b_end in pre:
            kg = kg_ref[rows, :].astype(F32)
            e_neg = jnp.exp(-bcum)
            e_end = jnp.exp(b_end)
            qt_all = q_ref[rows, :].astype(F32) * jnp.exp(bcum)
            kt_all = kg * e_neg
            ktail_all = kt_all * e_end
            v_all = i_ref[rows, :].astype(F32)
            for h in range(HG_HEADS):
                c = head_cols(h)
                items.append(dict(rows=rows, h=h, qt=qt_all[:, c], kt=kt_all[:, c], ktail=ktail_all[:, c],
                                  v=v_all[:, c], e_end=e_end[:, c]))
        for it in items:
            it["attn"] = jnp.where(incl, _mm_nt(it["qt"], it["kt"]), 0.0)
        for it in items:
            it["local"] = _mm(it["attn"], it["v"])
            it["inject"] = _mm_tn(it["v"], it["ktail"])
        states = [s_ref[h] for h in range(HG_HEADS)]
        for it in items:
            h = it["h"]
            o_ref[it["rows"], head_cols(h)] = (it["local"] + _mm_nt(it["qt"], states[h])).astype(o_ref.dtype)
            states[h] = states[h] * it["e_end"] + it["inject"]
        for h in range(HG_HEADS):
            s_ref[h] = states[h]

    @pl.when(jnp.logical_not(safe))
    def _():
        def head_body(h, carry):
            out, new_state = _hgrn_chunk(hq_ref[h], hf_ref[h], hk_ref[h], hv_ref[h], s_ref[h], reverse)
            ho_ref[h] = out
            s_ref[h] = new_state
            return carry

        def chunk_body(i, carry):
            ci = (n_chunks - 1 - i) if reverse else i
            rows = pl.ds(pl.multiple_of(ci * chunk, chunk), chunk)
            for h in range(HG_HEADS):
                hq_ref[h] = q_ref[rows, head_cols(h)].astype(F32)
                hf_ref[h] = lf_ref[rows, head_cols(h)]
                hk_ref[h] = kg_ref[rows, head_cols(h)].astype(F32)
                hv_ref[h] = i_ref[rows, head_cols(h)].astype(F32)
            lax.fori_loop(0, HG_HEADS, head_body, 0)
            for h in range(HG_HEADS):
                o_ref[rows, head_cols(h)] = ho_ref[h].astype(o_ref.dtype)
            return carry

        lax.fori_loop(0, n_chunks, chunk_body, 0)


def _hgrn_scan(q, lf, kg, i_val, hg_lb, nb, t_len, reverse, layer):
    n = q.shape[0]
    tb = _tile(t_len, HG_BLOCK_CHUNKS * HG_CHUNK)
    assert tb % HG_CHUNK == 0
    n_outer = t_len // tb

    def idx(b, c):
        return (b * n_outer + (n_outer - 1 - c if reverse else c), 0)

    spec = pl.BlockSpec((tb, HG_KEY_WIDTH), idx)
    head_buf = pltpu.VMEM((HG_HEADS, HG_CHUNK, HG_KEY_DIM), F32)
    return pl.pallas_call(
        functools.partial(_hgrn_scan_kernel, reverse=reverse, layer=layer),
        grid=(nb, n_outer),
        in_specs=[spec, spec, spec, spec, pl.BlockSpec(hg_lb.shape, lambda b, c: (0, 0))],
        out_specs=spec,
        out_shape=jax.ShapeDtypeStruct((n, HG_VAL_WIDTH), ACT),
        scratch_shapes=[pltpu.VMEM((HG_HEADS, HG_VAL_DIM, HG_KEY_DIM), F32)] + [head_buf] * 5,
        compiler_params=_params("parallel", "arbitrary"),
        name="hgrn_scan_bwd" if reverse else "hgrn_scan_fwd",
    )(q, lf, kg, i_val, hg_lb)


def _hgrn_out_kernel(x_ref, of_ref, ob_ref, g_ref, nw_ref, wo_ref, out_ref):
    o = of_ref[...].astype(F32) + ob_ref[...].astype(F32)
    parts = []
    for h in range(HG_HEADS):
        oh = o[:, h * HG_VAL_DIM:(h + 1) * HG_VAL_DIM]
        parts.append(oh * lax.rsqrt(jnp.mean(oh * oh, axis=-1, keepdims=True) + NORM_EPS))
    o = jnp.concatenate(parts, axis=-1) * nw_ref[...] * _silu(g_ref[...].astype(F32))
    out_ref[...] = x_ref[...] + _mm(o, wo_ref[...])


def _hgrn_out(x2d, o_f, o_b, g, norm_w, w_out):
    n = x2d.shape[0]
    tm = _tile(n, 512)
    row = pl.BlockSpec((tm, D_MODEL), lambda i: (i, 0))
    return pl.pallas_call(
        _hgrn_out_kernel,
        grid=(n // tm,),
        in_specs=[row] * 4 + [pl.BlockSpec((1, HG_VAL_WIDTH), lambda i: (0, 0)),
                              pl.BlockSpec(w_out.shape, lambda i: (0, 0))],
        out_specs=row,
        out_shape=jax.ShapeDtypeStruct((n, D_MODEL), F32),
        compiler_params=_params("parallel"), name="hgrn_out",
    )(x2d, o_f, o_b, g, norm_w, w_out)


def _xattn_kernel(x_ref, g_ref, wq_ref, kv_ref, wo_ref, out_ref):
    x = x_ref[...]
    q = _mm(_rms_normed(x, g_ref[...]), wq_ref[...])
    scale = XA_HEAD_DIM ** -0.5
    scores = [_mm_nt(q[:, h * XA_HEAD_DIM:(h + 1) * XA_HEAD_DIM], kv_ref[:, h * XA_HEAD_DIM:(h + 1) * XA_HEAD_DIM])
              * scale for h in range(XA_HEADS)]
    probs = []
    for s in scores:
        p = jnp.exp(s - jnp.max(s, axis=-1, keepdims=True))
        probs.append(p * (1.0 / jnp.sum(p, axis=-1, keepdims=True)))
    heads = [_mm(p, kv_ref[:, D_MODEL + h * XA_HEAD_DIM:D_MODEL + (h + 1) * XA_HEAD_DIM])
             for h, p in enumerate(probs)]
    out_ref[...] = x + _mm(jnp.concatenate(heads, axis=-1), wo_ref[...])


def _xattn(x2d, gain, wq, kv, wo, nb, t_len, mem_len):
    n = x2d.shape[0]
    tm = _tile(t_len, 512)
    per_seq = t_len // tm
    row = pl.BlockSpec((tm, D_MODEL), lambda b, j: (b * per_seq + j, 0))
    return pl.pallas_call(
        _xattn_kernel,
        grid=(nb, per_seq),
        in_specs=[row, pl.BlockSpec((1, D_MODEL), lambda b, j: (0, 0)),
                  pl.BlockSpec(wq.shape, lambda b, j: (0, 0)),
                  pl.BlockSpec((mem_len, 2 * D_MODEL), lambda b, j: (b, 0)),
                  pl.BlockSpec(wo.shape, lambda b, j: (0, 0))],
        out_specs=row,
        out_shape=jax.ShapeDtypeStruct((n, D_MODEL), F32),
        compiler_params=_params("parallel", "parallel"), name="xattn",
    )(x2d, gain.reshape(1, D_MODEL), wq, kv, wo)


def _ffn_kernel(x_ref, g_ref, wi_ref, wo_ref, fg_ref, out_ref, *, final_norm):
    x = x_ref[...]
    gu = _mm(_rms_normed(x, g_ref[...]), wi_ref[...])
    act = _silu(gu[:, :FFN_DIM]) * gu[:, FFN_DIM:]
    y = x + _mm(act, wo_ref[...])
    if final_norm:
        y = _rms_normed(y, fg_ref[...])
    out_ref[...] = y


def _ffn(x2d, gain, w_in, w_out, final_gain, final_norm):
    n = x2d.shape[0]
    tm = _tile(n, 256)
    row = pl.BlockSpec((tm, D_MODEL), lambda i: (i, 0))
    vec = pl.BlockSpec((1, D_MODEL), lambda i: (0, 0))
    return pl.pallas_call(
        functools.partial(_ffn_kernel, final_norm=final_norm),
        grid=(n // tm,),
        in_specs=[row, vec, pl.BlockSpec(w_in.shape, lambda i: (0, 0)),
                  pl.BlockSpec(w_out.shape, lambda i: (0, 0)), vec],
        out_specs=row,
        out_shape=jax.ShapeDtypeStruct((n, D_MODEL), F32),
        compiler_params=_params("parallel"), name="ffn",
    )(x2d, gain.reshape(1, D_MODEL), w_in, w_out, final_gain.reshape(1, D_MODEL))


def _pad_lanes(a, width=LANES):
    return jnp.pad(a, [(0, 0)] * (a.ndim - 1) + [(0, width - a.shape[-1])])


def _rwkv_ssd_layer(x2d, nb, t_len, norm_g, w_in, w_out, rw_mu, rw_w0, rw_w2, rw_a0, rw_a2, rw_g2, rw_k_k, rw_k_a,
                    rw_r_k, rw_gn_w, rw_gn_b, conv_w, conv_b, dt_bias, a_log, d_skip, ssd_norm_w):
    o_lora = 3 * RW_DIM
    o_z = RW_PROJ
    o_xbc = o_z + SSD_DIM
    o_dt = o_xbc + SSD_CONV_DIM
    wb = w_in.astype(BF16)
    weights = [wb[:, :o_lora], wb[:, o_lora:o_z], wb[:, o_z:o_xbc], wb[:, o_xbc:o_dt], _pad_lanes(wb[:, o_dt:])]
    head_id = jnp.arange(RW_DIM) // RW_HEAD_DIM
    blockdiag = (head_id[:, None] == head_id[None, :]).astype(BF16)
    row = lambda a: a.reshape(1, -1)
    rwkv_consts = (row(rw_mu[:o_lora]), row(rw_mu[o_lora:]), row(rw_a0), rw_a2.astype(BF16), rw_g2.astype(BF16),
                   row(rw_k_k), row(rw_k_a), rw_w0, rw_w2.astype(BF16), row(rw_r_k), blockdiag)
    ssd_consts = (conv_w, row(conv_b), _pad_lanes(dt_bias), _pad_lanes(a_log))
    (r, k, v, kk, kka, lw_f, lw_b, g, bonus, p_z, xbc, dt_f, dt_b, la_f, la_b) = _mix0_in(
        x2d, nb, t_len, norm_g, weights, rwkv_consts, ssd_consts)
    o_f = _rwkv_scan(r, k, v, kk, kka, lw_f, nb, t_len, reverse=False)
    o_b = _rwkv_scan(r, k, v, kk, kka, lw_b, nb, t_len, reverse=True)
    y_f = _ssd_scan(xbc, dt_f, la_f, nb, t_len, reverse=False)
    y_b = _ssd_scan(xbc, dt_b, la_b, nb, t_len, reverse=True)

    wo = w_out.astype(BF16)
    d_skip_lanes = jnp.repeat(d_skip, SSD_HEAD_DIM).reshape(1, SSD_DIM)
    return _mix0_out(x2d, o_f, o_b, bonus, g, y_f, y_b, xbc, p_z, row(rw_gn_w), row(rw_gn_b), blockdiag,
                     d_skip_lanes, row(ssd_norm_w), wo[:RW_DIM], wo[RW_DIM:])


def _hgrn_layer(x2d, nb, t_len, norm_g, w_in, w_out, norm_w, hg_lb, layer):
    kw, vw = HG_KEY_WIDTH, HG_VAL_WIDTH
    wb = w_in.astype(BF16)
    weights = [wb[:, :kw], wb[:, kw:2 * kw], wb[:, 2 * kw:3 * kw], wb[:, 3 * kw:3 * kw + vw], wb[:, 3 * kw + vw:]]
    q, lf_f, lf_b, kg_f, kg_b, i_val, g = _hgrn_in(x2d, norm_g, hg_lb, weights, layer)
    o_f = _hgrn_scan(q, lf_f, kg_f, i_val, hg_lb, nb, t_len, reverse=False, layer=layer)
    o_b = _hgrn_scan(q, lf_b, kg_b, i_val, hg_lb, nb, t_len, reverse=True, layer=layer)
    return _hgrn_out(x2d, o_f, o_b, g, norm_w.reshape(1, vw), w_out.astype(BF16))


def kernel(x, mem, mix_norm, ab_w_in, ab_w_out, rw_mu, rw_w0, rw_w2, rw_a0, rw_a2, rw_g2, rw_k_k, rw_k_a, rw_r_k, rw_gn_w, rw_gn_b, ssd_conv_w, ssd_conv_b, ssd_dt_bias, ssd_a_log, ssd_d, ssd_norm_w, hg_w_in, hg_w_out, hg_norm_w, hg_lb, xa_norm, mem_norm, xa_wq, xa_wkv, xa_wo, ffn_norm, ffn_w_in, ffn_w_out, final_norm):
    nb, t_len, d = x.shape
    mem_len = mem.shape[1]
    depth = mix_norm.shape[0]
    x2d = x.reshape(nb * t_len, d)
    mem2d = mem.reshape(nb * mem_len, d)
    for layer in range(depth):
        if layer % 2 == 0:
            e = layer // 2
            x2d = _rwkv_ssd_layer(x2d, nb, t_len, mix_norm[layer], ab_w_in[e], ab_w_out[e], rw_mu[e], rw_w0[e],
                                  rw_w2[e], rw_a0[e], rw_a2[e], rw_g2[e], rw_k_k[e], rw_k_a[e],
                                  rw_r_k[e].reshape(-1), rw_gn_w[e], rw_gn_b[e], ssd_conv_w[e], ssd_conv_b[e],
                                  ssd_dt_bias[e], ssd_a_log[e], ssd_d[e], ssd_norm_w[e])
        else:
            o = layer // 2
            x2d = _hgrn_layer(x2d, nb, t_len, mix_norm[layer], hg_w_in[o], hg_w_out[o], hg_norm_w[o], hg_lb, layer)
        (kv,) = _norm_matmul(mem2d, mem_norm[layer], [xa_wkv[layer].astype(BF16)], [BF16])
        x2d = _xattn(x2d, xa_norm[layer], xa_wq[layer].astype(BF16), kv, xa_wo[layer].astype(BF16),
                     nb, t_len, mem_len)
        x2d = _ffn(x2d, ffn_norm[layer], ffn_w_in[layer].astype(BF16), ffn_w_out[layer].astype(BF16),
                   final_norm, final_norm=(layer == depth - 1))
    return x2d.reshape(nb, t_len, d)
```

```python
import functools
import math

import jax
import jax.numpy as jnp
from jax import lax
from jax.experimental import pallas as pl
from jax.experimental.pallas import tpu as pltpu

F32 = jnp.float32
BF16 = jnp.bfloat16
ACT = BF16

D_MODEL = 1024
NORM_EPS = 1e-6

RW_HEAD_DIM = 64
RW_DIM = D_MODEL // 2
RW_HEADS = RW_DIM // RW_HEAD_DIM
RW_DECAY_RANK = 64
RW_A_RANK = 64
RW_GATE_RANK = 128
RW_GN_EPS = 64e-5
RW_LORA = RW_DECAY_RANK + RW_A_RANK + RW_GATE_RANK
RW_PROJ = 3 * RW_DIM + RW_LORA
RW_CHUNK = 64
RW_BLOCK_CHUNKS = 4
RW_HEAD_PAIRS = RW_HEADS // 2
assert RW_CHUNK == RW_HEAD_DIM and 2 * RW_HEAD_DIM == 128

SSD_HEAD_DIM = 64
SSD_DIM = D_MODEL // 2
SSD_HEADS = SSD_DIM // SSD_HEAD_DIM
SSD_GROUPS = 2
SSD_STATE = 128
SSD_CONV_DIM = SSD_DIM + 2 * SSD_GROUPS * SSD_STATE
SSD_CHUNK = 128
SSD_BLOCK_CHUNKS = 2
SSD_GROUP_HEADS = SSD_HEADS // SSD_GROUPS

HG_HEADS = 8
HG_KEY_DIM = 128
HG_VAL_DIM = D_MODEL // HG_HEADS
HG_KEY_WIDTH = HG_HEADS * HG_KEY_DIM
HG_VAL_WIDTH = HG_HEADS * HG_VAL_DIM
HG_CHUNK = 64
HG_SUB = 16
HG_BLOCK_CHUNKS = 4
HG_SAFE_LOG_RANGE = 64.0
assert HG_KEY_DIM == HG_VAL_DIM

XA_HEADS = 4
XA_HEAD_DIM = D_MODEL // XA_HEADS
FFN_DIM = ((8 * D_MODEL + 3 * 256 - 1) // (3 * 256)) * 256

SUBLANES = 8
HALO_ROWS = 16
LANES = 128
VMEM_LIMIT_BYTES = 56 * 1024 * 1024
NEG_BIG = -1e30


def _params(*semantics):
    return pltpu.CompilerParams(dimension_semantics=semantics, vmem_limit_bytes=VMEM_LIMIT_BYTES)


def _tile(n, pref):
    t = min(n, pref)
    while n % t or t % SUBLANES:
        t -= 1
    return t


def _mm(a, b):
    return jnp.dot(a.astype(BF16), b.astype(BF16), preferred_element_type=F32)


def _mm_nt(a, b):
    return lax.dot_general(a.astype(BF16), b.astype(BF16), (((1,), (1,)), ((), ())),
                           preferred_element_type=F32)


def _mm_tn(a, b):
    return lax.dot_general(a.astype(BF16), b.astype(BF16), (((0,), (0,)), ((), ())),
                           preferred_element_type=F32)


def _split3(x):
    hi = x.astype(BF16)
    r1 = x - hi.astype(F32)
    mid = r1.astype(BF16)
    lo = (r1 - mid.astype(F32)).astype(BF16)
    return hi, mid, lo


def _mm_exact_lhs(m_bf16, x):
    hi, mid, lo = _split3(x)
    dot = functools.partial(jnp.dot, preferred_element_type=F32)
    return dot(m_bf16, hi) + dot(m_bf16, mid) + dot(m_bf16, lo)


def _mm_exact_rhs(x, m_bf16, terms=3):
    dot = functools.partial(jnp.dot, preferred_element_type=F32)
    pieces = _split3(x)[:terms]
    out = dot(pieces[0], m_bf16)
    for piece in pieces[1:]:
        out = out + dot(piece, m_bf16)
    return out


def _rms_normed(x, gain):
    ms = jnp.mean(x * x, axis=-1, keepdims=True)
    return x * lax.rsqrt(ms + NORM_EPS) * gain


def _sigmoid(x):
    return 0.5 * jnp.tanh(0.5 * x) + 0.5


def _silu(x):
    return x * _sigmoid(x)


def _softplus(x):
    return jnp.maximum(x, 0.0) + jnp.log(1.0 + jnp.exp(-jnp.abs(x)))


def _order_masks(n, reverse):
    row = lax.broadcasted_iota(jnp.int32, (n, n), 0)
    col = lax.broadcasted_iota(jnp.int32, (n, n), 1)
    if reverse:
        return col >= row, col > row
    return col <= row, col < row


def _norm_matmul_kernel(x_ref, g_ref, *refs, n_out):
    xn = _rms_normed(x_ref[...], g_ref[...]).astype(BF16)
    for w_ref, o_ref in zip(refs[:n_out], refs[n_out:]):
        o_ref[...] = jnp.dot(xn, w_ref[...], preferred_element_type=F32).astype(o_ref.dtype)


def _norm_matmul(x2d, gain, weights, out_dtypes, tm_pref=512):
    n, d = x2d.shape
    tm = _tile(n, tm_pref)
    in_specs = [pl.BlockSpec((tm, d), lambda i: (i, 0)), pl.BlockSpec((1, d), lambda i: (0, 0))]
    in_specs += [pl.BlockSpec(w.shape, lambda i: (0, 0)) for w in weights]
    out_specs = [pl.BlockSpec((tm, w.shape[1]), lambda i: (i, 0)) for w in weights]
    out_shape = [jax.ShapeDtypeStruct((n, w.shape[1]), dt) for w, dt in zip(weights, out_dtypes)]
    return pl.pallas_call(
        functools.partial(_norm_matmul_kernel, n_out=len(weights)),
        grid=(n // tm,), in_specs=in_specs, out_specs=out_specs, out_shape=out_shape,
        compiler_params=_params("parallel"), name="norm_matmul",
    )(x2d, gain.reshape(1, d), *weights)


def _neighbours(p, prev_blk, next_blk, first, last):
    tt = p.shape[0]
    ridx = lax.broadcasted_iota(jnp.int32, p.shape, 0)
    prev_row = jnp.where(first, 0.0, prev_blk[HALO_ROWS - 1:HALO_ROWS, :].astype(F32))
    next_row = jnp.where(last, 0.0, next_blk[0:1, :].astype(F32))
    prev = jnp.where(ridx == 0, prev_row, pltpu.roll(p, 1, 0))
    nxt = jnp.where(ridx == tt - 1, next_row, pltpu.roll(p, tt - 1, 0))
    return prev, nxt


def _shift_matrix(tt, offsets, weight):
    t = jnp.arange(tt)[:, None]
    c = jnp.arange(tt + 2 * HALO_ROWS)[None, :]
    hit = functools.reduce(jnp.logical_or, [c == HALO_ROWS + t + d for d in offsets])
    return jnp.where(hit, weight, 0.0).astype(BF16)


def _head_sum(x, blockdiag_bf16):
    return _mm_exact_rhs(x, blockdiag_bf16, terms=2)


def _mix0_project(x_ref, xp_ref, xn_ref, gain_ref, wrkv_ref, wlora_ref, wz_ref, wxbc_ref, wdt_ref, first, last,
                  prkv_scr, plora_scr, u_scr, z_scr, dt_scr):
    tt = x_ref.shape[0]
    dot = functools.partial(jnp.dot, preferred_element_type=F32)
    x_ext = jnp.concatenate([jnp.where(first, 0.0, xp_ref[...]), x_ref[...], jnp.where(last, 0.0, xn_ref[...])], axis=0)
    xn_ext = _rms_normed(x_ext, gain_ref[...]).astype(BF16)
    xn = xn_ext[HALO_ROWS:HALO_ROWS + tt]

    def column_block(lhs, w_ref, scr, lo, hi):
        def piece():
            scr[:, lo:hi] = dot(lhs, w_ref[:, lo:hi]).astype(scr.dtype)
        return piece

    pieces = [column_block(xn_ext, wrkv_ref, prkv_scr, c, c + RW_DIM) for c in range(0, 3 * RW_DIM, RW_DIM)]
    pieces.append(column_block(xn_ext, wlora_ref, plora_scr, 0, RW_LORA))
    pieces += [column_block(xn_ext, wxbc_ref, u_scr, c, c + SSD_DIM) for c in range(0, SSD_CONV_DIM, SSD_DIM)]
    pieces.append(column_block(xn, wz_ref, z_scr, 0, SSD_DIM))
    pieces.append(column_block(xn, wdt_ref, dt_scr, 0, LANES))
    return pieces


def _mix0_operands(prkv_scr, plora_scr, u_scr, z_scr, dt_scr,
                   savg_ref, mu_ref, mul_ref, a0_ref, a2_ref, g2_ref, kk_ref, ka_ref, w0_ref, w2_ref, rk_ref, bd_ref,
                   cw_ref, cb_ref, dtb_ref, alog_ref,
                   r_out, k_out, v_out, kkn_out, kka_out, lwf_out, lwb_out, g_out, bonus_out,
                   z_out, xbc_out, dtf_out, dtb_out, laf_out, lab_out):
    tt = z_scr.shape[0]
    dot = functools.partial(jnp.dot, preferred_element_type=F32)

    def token_shift(p_scr, m_ref, lo_col, hi_col):
        p_ext = p_scr[:, lo_col:hi_col]
        x = p_ext[HALO_ROWS:HALO_ROWS + tt].astype(F32)
        avg = dot(savg_ref[...], p_ext)
        return x + m_ref[:, lo_col:hi_col] * (avg - x)

    r = token_shift(prkv_scr, mu_ref, 0, RW_DIM)
    k = token_shift(prkv_scr, mu_ref, RW_DIM, 2 * RW_DIM)
    v = token_shift(prkv_scr, mu_ref, 2 * RW_DIM, 3 * RW_DIM)
    lo = token_shift(plora_scr, mul_ref, 0, RW_LORA)
    wd = lo[:, :RW_DECAY_RANK]
    ad = lo[:, RW_DECAY_RANK:RW_DECAY_RANK + RW_A_RANK]
    gd = lo[:, RW_DECAY_RANK + RW_A_RANK:]
    bd = bd_ref[...]

    a = _sigmoid(a0_ref[...] + _mm(ad, a2_ref[...]))
    g = _mm(_sigmoid(gd), g2_ref[...])
    kk = k * kk_ref[...]
    kk = kk / jnp.maximum(jnp.sqrt(_head_sum(kk * kk, bd)), 1e-12)
    k = k * (1.0 + (a - 1.0) * ka_ref[...])
    w_lora = jnp.tanh(wd)

    def log_decay(d):
        wl = w0_ref[d:d + 1, :] + _mm(w_lora, w2_ref[d])
        return -jnp.exp(-_softplus(-wl) - 0.5)

    r_out[...] = r.astype(r_out.dtype)
    k_out[...] = k.astype(k_out.dtype)
    v_out[...] = v.astype(v_out.dtype)
    kkn_out[...] = kk.astype(kkn_out.dtype)
    kka_out[...] = (kk * a).astype(kka_out.dtype)
    lwf_out[...] = log_decay(0)
    lwb_out[...] = log_decay(1)
    g_out[...] = g.astype(g_out.dtype)
    bonus_out[...] = (_head_sum(r * k * rk_ref[...], bd) * v).astype(bonus_out.dtype)

    z_out[...] = z_scr[...]
    u_ext = u_scr[...]
    u = u_ext[HALO_ROWS:HALO_ROWS + tt].astype(F32)
    prev, nxt = _neighbours(u, u_ext[:HALO_ROWS], u_ext[HALO_ROWS + tt:], False, False)
    conv = prev * cw_ref[0:1, :] + u * cw_ref[1:2, :] + nxt * cw_ref[2:3, :] + cb_ref[...]
    xbc_out[...] = _silu(conv).astype(xbc_out.dtype)
    dt_raw = dt_scr[...]
    dt_f = _softplus(dt_raw + dtb_ref[0:1, :])
    dt_b = _softplus(dt_raw + dtb_ref[1:2, :])
    dtf_out[...] = dt_f
    dtb_out[...] = dt_b
    laf_out[...] = dt_f * (-jnp.exp(alog_ref[0:1, :]))
    lab_out[...] = dt_b * (-jnp.exp(alog_ref[1:2, :]))


MIX0_N_PROJ_REFS = 9
MIX0_N_CONSTS = 16
MIX0_N_OUTS = 15
MIX0_N_STAGED = 5


def _mix0_in_kernel(*refs, per_seq):
    proj_refs = refs[:MIX0_N_PROJ_REFS]
    const_refs = refs[MIX0_N_PROJ_REFS:MIX0_N_PROJ_REFS + MIX0_N_CONSTS]
    out_refs = refs[MIX0_N_PROJ_REFS + MIX0_N_CONSTS:MIX0_N_PROJ_REFS + MIX0_N_CONSTS + MIX0_N_OUTS]
    staged = refs[MIX0_N_PROJ_REFS + MIX0_N_CONSTS + MIX0_N_OUTS:]
    pos = lax.rem(pl.program_id(0), per_seq)
    for piece in _mix0_project(*proj_refs, pos == 0, pos == per_seq - 1, *staged):
        piece()
    _mix0_operands(*staged, *const_refs, *out_refs)


def _mix0_in(x2d, nb, t_len, gain, weights, rwkv_consts, ssd_consts):
    n, d = x2d.shape
    tt = _tile(t_len, 256)
    assert tt % HALO_ROWS == 0
    per_seq = t_len // tt
    n_tiles = n // tt
    blocks_per_tile = tt // HALO_ROWS
    n_blocks = n // HALO_ROWS

    def full(a):
        nd = a.ndim
        return pl.BlockSpec(a.shape, lambda s: (0,) * nd)

    x_specs = [pl.BlockSpec((tt, d), lambda s: (s, 0)),
               pl.BlockSpec((HALO_ROWS, d), lambda s: (jnp.maximum(s * blocks_per_tile - 1, 0), 0)),
               pl.BlockSpec((HALO_ROWS, d), lambda s: (jnp.minimum((s + 1) * blocks_per_tile, n_blocks - 1), 0))]
    proj_consts = (gain.reshape(1, d),) + tuple(weights)
    consts = (_shift_matrix(tt, (-1, 1), 0.5),) + tuple(rwkv_consts) + tuple(ssd_consts)
    assert 3 + len(proj_consts) == MIX0_N_PROJ_REFS and len(consts) == MIX0_N_CONSTS
    wide = [(RW_DIM, ACT)] * 5 + [(RW_DIM, F32)] * 2 + [(RW_DIM, ACT)] * 2 + [(SSD_DIM, ACT), (SSD_CONV_DIM, ACT)] \
        + [(LANES, F32)] * 4
    assert len(wide) == MIX0_N_OUTS
    ext = tt + 2 * HALO_ROWS
    staged = [pltpu.VMEM((ext, 3 * RW_DIM), ACT), pltpu.VMEM((ext, RW_LORA), ACT), pltpu.VMEM((ext, SSD_CONV_DIM), ACT),
              pltpu.VMEM((tt, SSD_DIM), ACT), pltpu.VMEM((tt, LANES), F32)]
    assert len(staged) == MIX0_N_STAGED
    return pl.pallas_call(
        functools.partial(_mix0_in_kernel, per_seq=per_seq),
        grid=(n_tiles,),
        in_specs=x_specs + [full(a) for a in proj_consts + consts],
        out_specs=[pl.BlockSpec((tt, w), lambda s: (s, 0)) for w, _ in wide],
        out_shape=[jax.ShapeDtypeStruct((n, w), dt) for w, dt in wide],
        scratch_shapes=staged,
        compiler_params=_params("parallel"), name="mix0_in",
    )(x2d, x2d, x2d, *proj_consts, *consts)


def _pair_blockdiag(y):
    lane = lax.broadcasted_iota(jnp.int32, y.shape, 1)
    top = jnp.where(lane < RW_HEAD_DIM, y, 0.0).astype(BF16)
    bottom = jnp.where(lane >= RW_HEAD_DIM, y, 0.0).astype(BF16)
    return jnp.concatenate([top, bottom], axis=0)


def _pair_diag_blocks(full):
    lane = lax.broadcasted_iota(jnp.int32, (RW_HEAD_DIM, LANES), 1)
    return jnp.where(lane < RW_HEAD_DIM, full[:RW_HEAD_DIM], full[RW_HEAD_DIM:])


def _unit_lower_inverses(n_strict_list, eye):
    size = n_strict_list[0].shape[0]
    levels = int(math.log2(size)) - 1
    accs = [eye + n for n in n_strict_list]
    powers = [_mm(n, _pair_blockdiag(n)) for n in n_strict_list]
    for level in range(levels):
        power_bds = [_pair_blockdiag(p) for p in powers]
        if level == levels - 1:
            return [a + _mm(a, b) for a, b in zip(accs, power_bds)]
        both = [_mm(jnp.concatenate([a, p], axis=0), b) for a, p, b in zip(accs, powers, power_bds)]
        accs = [a + ab[:size] for a, ab in zip(accs, both)]
        powers = [ab[size:] for ab in both]


def _rwkv_scan_kernel(r_ref, k_ref, v_ref, kk_ref, kka_ref, lw_ref, o_ref, h_ref, *, reverse):
    @pl.when(pl.program_id(1) == 0)
    def _():
        h_ref[...] = jnp.zeros_like(h_ref)

    chunk = RW_CHUNK
    n_chunks = r_ref.shape[0] // chunk
    incl_bf = _order_masks(chunk, reverse)[0].astype(BF16)
    row = lax.broadcasted_iota(jnp.int32, (chunk, LANES), 0)
    pos = lax.broadcasted_iota(jnp.int32, (chunk, LANES), 1) % RW_HEAD_DIM
    incl, strict = (pos >= row, pos > row) if reverse else (pos <= row, pos < row)
    eye = (pos == row).astype(F32)
    end = 0 if reverse else chunk - 1
    chunk_order = list(range(n_chunks - 1, -1, -1)) if reverse else list(range(n_chunks))

    items = []
    for ci in chunk_order:
        rows = slice(ci * chunk, (ci + 1) * chunk)
        lw = lw_ref[rows, :]
        cum = _mm_exact_lhs(incl_bf, lw)
        cum_end = cum[end:end + 1, :]
        e_neg = jnp.exp(-cum)
        e_tail = jnp.exp(cum_end - cum)
        p_end = jnp.exp(cum_end)
        k = k_ref[rows, :].astype(F32)
        kka = kka_ref[rows, :].astype(F32)
        at_all = -kk_ref[rows, :].astype(F32) * jnp.exp(cum - lw)
        bt_all = kka * e_neg
        kt_all = k * e_neg
        rt_all = r_ref[rows, :].astype(F32) * jnp.exp(cum)
        bh_all = kka * e_tail
        kh_all = k * e_tail
        v = v_ref[rows, :].astype(F32)
        for j in range(RW_HEAD_PAIRS):
            sl = slice(j * LANES, (j + 1) * LANES)
            items.append(dict(ci=ci, j=j, at=at_all[:, sl], bt=bt_all[:, sl], kt=kt_all[:, sl], rt=rt_all[:, sl],
                              bh=bh_all[:, sl], kh=kh_all[:, sl], v=v[:, sl], p_end=p_end[:, sl]))

    for it in items:
        lhs = jnp.concatenate([it["at"], it["rt"]], axis=0)
        ab_rb = _mm_nt(lhs, _pair_blockdiag(it["bt"]))
        ak_rk = _mm_nt(lhs, _pair_blockdiag(it["kt"]))
        it["a_ab"] = jnp.where(strict, ab_rb[:chunk], 0.0)
        it["a_rb"] = jnp.where(incl, ab_rb[chunk:], 0.0)
        it["a_ak"] = jnp.where(strict, ak_rk[:chunk], 0.0)
        it["a_rk"] = jnp.where(incl, ak_rk[chunk:], 0.0)
    t_invs = _unit_lower_inverses([it["a_ab"] for it in items], eye)
    for it in items:
        akv_rkv = _mm(jnp.concatenate([it["a_ak"], it["a_rk"]], axis=0), _pair_blockdiag(it["v"]))
        it["x1"] = akv_rkv[:chunk]
        it["rk_v"] = akv_rkv[chunk:]
    for it, t_inv in zip(items, t_invs):
        it["w_u"] = _mm(t_inv, _pair_blockdiag(it["at"]))
        it["u0"] = _mm(t_inv, _pair_blockdiag(it["x1"]))
    for it in items:
        w_o = it["rt"] + _mm(it["a_rb"], _pair_blockdiag(it["w_u"]))
        g_mat = eye * it["p_end"] + _pair_diag_blocks(_mm_tn(it["bh"], it["w_u"]))
        it["wo_g"] = jnp.concatenate([w_o, g_mat], axis=0)
        it["o0"] = _mm(it["a_rb"], _pair_blockdiag(it["u0"])) + it["rk_v"]
        it["h0"] = _pair_diag_blocks(_mm_tn(it["bh"], it["u0"]) + _mm_tn(it["kh"], it["v"]))

    states = [h_ref[j] for j in range(RW_HEAD_PAIRS)]
    for it in items:
        j = it["j"]
        res = _mm(it["wo_g"], _pair_blockdiag(states[j]))
        rows = slice(it["ci"] * chunk, (it["ci"] + 1) * chunk)
        o_ref[rows, j * LANES:(j + 1) * LANES] = (res[:chunk] + it["o0"]).astype(o_ref.dtype)
        states[j] = res[chunk:] + it["h0"]
    for j in range(RW_HEAD_PAIRS):
        h_ref[j] = states[j]


def _rwkv_scan(r, k, v, kk, kka, lw, nb, t_len, reverse):
    n = r.shape[0]
    tb = _tile(t_len, RW_BLOCK_CHUNKS * RW_CHUNK)
    assert tb % RW_CHUNK == 0
    nc = t_len // tb

    def idx(b, c):
        return (b * nc + (nc - 1 - c if reverse else c), 0)

    spec = pl.BlockSpec((tb, RW_DIM), idx)
    return pl.pallas_call(
        functools.partial(_rwkv_scan_kernel, reverse=reverse),
        grid=(nb, nc), in_specs=[spec] * 6, out_specs=spec,
        out_shape=jax.ShapeDtypeStruct((n, RW_DIM), ACT),
        scratch_shapes=[pltpu.VMEM((RW_HEAD_PAIRS, RW_HEAD_DIM, LANES), F32)],
        compiler_params=_params("parallel", "arbitrary"), name="rwkv_scan_bwd" if reverse else "rwkv_scan_fwd",
    )(r, k, v, kk, kka, lw)


def _ssd_scan_kernel(xbc_ref, dt_ref, la_ref, ex64_ref, ex128_ref, y_ref, s_ref, *, reverse):
    @pl.when(pl.program_id(1) == 0)
    def _():
        s_ref[...] = jnp.zeros_like(s_ref)

    chunk = SSD_CHUNK
    n_chunks = xbc_ref.shape[0] // chunk
    incl, _ = _order_masks(chunk, reverse)
    incl_bf = incl.astype(BF16)
    incl_t_bf = _order_masks(chunk, not reverse)[0].astype(BF16)
    end = 0 if reverse else chunk - 1
    gn = SSD_GROUPS * SSD_STATE
    gw = SSD_GROUP_HEADS * SSD_HEAD_DIM
    chunk_order = list(range(n_chunks - 1, -1, -1)) if reverse else list(range(n_chunks))
    ex64 = ex64_ref[...]
    ex128 = ex128_ref[...]

    pre = []
    for ci in chunk_order:
        rows = slice(ci * chunk, (ci + 1) * chunk)
        la = la_ref[rows, :]
        cum = _mm_exact_lhs(incl_bf, la)
        cum_rows = _mm_exact_rhs(la.T, incl_t_bf)
        cum_end = cum[end:end + 1, :]
        ecum64 = _mm_exact_rhs(jnp.exp(cum), ex64, terms=2)
        etail64 = _mm_exact_rhs(jnp.exp(cum_end - cum), ex64, terms=2)
        dt64 = _mm_exact_rhs(dt_ref[rows, :], ex64, terms=2)
        cumcol = _mm_exact_rhs(cum, ex128)
        xdt = xbc_ref[rows, :SSD_DIM].astype(F32) * dt64
        pre.append(dict(rows=rows, cum_rows=cum_rows, ecum64=ecum64, eend64=ecum64[end:end + 1, :],
                        cumcol=cumcol, xdt=xdt, xtail=xdt * etail64,
                        bm=[xbc_ref[rows, SSD_DIM + g * SSD_STATE:SSD_DIM + (g + 1) * SSD_STATE]
                            for g in range(SSD_GROUPS)],
                        cm=[xbc_ref[rows, SSD_DIM + gn + g * SSD_STATE:SSD_DIM + gn + (g + 1) * SSD_STATE]
                            for g in range(SSD_GROUPS)]))
    for p in pre:
        p["scores"] = [_mm_nt(p["cm"][g], p["bm"][g]) for g in range(SSD_GROUPS)]
        p["inject"] = [_mm_tn(p["bm"][g], p["xtail"][:, g * gw:(g + 1) * gw]) for g in range(SSD_GROUPS)]
    for p in pre:
        p["ydiag"] = []
        for h in range(SSD_HEADS):
            diff = p["cumcol"][:, h * LANES:(h + 1) * LANES] - p["cum_rows"][h:h + 1, :]
            decay = jnp.exp(jnp.where(incl, diff, NEG_BIG))
            w = p["scores"][h // SSD_GROUP_HEADS] * decay
            p["ydiag"].append(_mm(w, p["xdt"][:, h * SSD_HEAD_DIM:(h + 1) * SSD_HEAD_DIM]))

    states = [s_ref[g] for g in range(SSD_GROUPS)]
    for p in pre:
        for g in range(SSD_GROUPS):
            y_off = _mm(p["cm"][g], states[g]) * p["ecum64"][:, g * gw:(g + 1) * gw]
            for jh in range(SSD_GROUP_HEADS):
                h = g * SSD_GROUP_HEADS + jh
                y_ref[p["rows"], h * SSD_HEAD_DIM:(h + 1) * SSD_HEAD_DIM] = (
                    p["ydiag"][h] + y_off[:, jh * SSD_HEAD_DIM:(jh + 1) * SSD_HEAD_DIM]).astype(y_ref.dtype)
            states[g] = states[g] * p["eend64"][:, g * gw:(g + 1) * gw] + p["inject"][g]
    for g in range(SSD_GROUPS):
        s_ref[g] = states[g]


def _ssd_scan(xbc, dt, la, nb, t_len, reverse):
    n = xbc.shape[0]
    tb = _tile(t_len, SSD_BLOCK_CHUNKS * SSD_CHUNK)
    assert tb % SSD_CHUNK == 0 and SSD_CHUNK == LANES
    nc = t_len // tb
    lane_head = jnp.arange(LANES)[:, None]
    ex64 = (lane_head == jnp.arange(SSD_DIM)[None, :] // SSD_HEAD_DIM).astype(BF16)
    ex128 = (lane_head == jnp.arange(SSD_HEADS * LANES)[None, :] // LANES).astype(BF16)

    def idx(b, c):
        return (b * nc + (nc - 1 - c if reverse else c), 0)

    def full(a):
        return pl.BlockSpec(a.shape, lambda b, c: (0, 0))

    return pl.pallas_call(
        functools.partial(_ssd_scan_kernel, reverse=reverse),
        grid=(nb, nc),
        in_specs=[pl.BlockSpec((tb, SSD_CONV_DIM), idx), pl.BlockSpec((tb, LANES), idx),
                  pl.BlockSpec((tb, LANES), idx), full(ex64), full(ex128)],
        out_specs=pl.BlockSpec((tb, SSD_DIM), idx),
        out_shape=jax.ShapeDtypeStruct((n, SSD_DIM), ACT),
        scratch_shapes=[pltpu.VMEM((SSD_GROUPS, SSD_STATE, SSD_GROUP_HEADS * SSD_HEAD_DIM), F32)],
        compiler_params=_params("parallel", "arbitrary"), name="ssd_scan_bwd" if reverse else "ssd_scan_fwd",
    )(xbc, dt, la, ex64, ex128)


def _mix0_out_kernel(x_ref, of_ref, ob_ref, bonus_ref, g_ref, yf_ref, yb_ref, xbc_ref, z_ref,
                     gnw_ref, gnb_ref, bd_ref, dsk_ref, nw_ref, wo_rw_ref, wo_ssd_ref, out_ref):
    bd = bd_ref[...]
    o = of_ref[...].astype(F32) + ob_ref[...].astype(F32)
    inv_n = 1.0 / RW_HEAD_DIM
    mu = _head_sum(o, bd) * inv_n
    oc = o - mu
    var = _head_sum(oc * oc, bd) * inv_n
    o = oc * lax.rsqrt(var + RW_GN_EPS) * gnw_ref[...] + gnb_ref[...]
    o_rw = (o + bonus_ref[...].astype(F32)) * g_ref[...].astype(F32)

    y = yf_ref[...].astype(F32) + yb_ref[...].astype(F32) + dsk_ref[...] * xbc_ref[...].astype(F32)
    y = y * _silu(z_ref[...].astype(F32))
    gw = SSD_DIM // SSD_GROUPS
    parts = []
    for g in range(SSD_GROUPS):
        yg = y[:, g * gw:(g + 1) * gw]
        parts.append(yg * lax.rsqrt(jnp.mean(yg * yg, axis=-1, keepdims=True) + NORM_EPS))
    o_ssd = jnp.concatenate(parts, axis=-1) * nw_ref[...]
    out_ref[...] = x_ref[...] + _mm(o_rw, wo_rw_ref[...]) + _mm(o_ssd, wo_ssd_ref[...])


def _mix0_out(x2d, o_f, o_b, bonus, g, y_f, y_b, xbc, z, gn_w, gn_b, blockdiag, d_skip, norm_w, wo_rw, wo_ssd):
    n = x2d.shape[0]
    tm = _tile(n, 512)

    def row(width):
        return pl.BlockSpec((tm, width), lambda i: (i, 0))

    def full(a):
        return pl.BlockSpec(a.shape, lambda i: (0, 0))

    consts = (gn_w, gn_b, blockdiag, d_skip, norm_w, wo_rw, wo_ssd)
    return pl.pallas_call(
        _mix0_out_kernel,
        grid=(n // tm,),
        in_specs=[row(D_MODEL)] + [row(RW_DIM)] * 6 + [row(SSD_DIM), row(SSD_DIM)] + [full(a) for a in consts],
        out_specs=row(D_MODEL),
        out_shape=jax.ShapeDtypeStruct((n, D_MODEL), F32),
        compiler_params=_params("parallel"), name="mix0_out",
    )(x2d, o_f, o_b, bonus, g, y_f, y_b, xbc, z, *consts)


def _hgrn_lower_bound(lb_ref, layer):
    x = lb_ref[...]
    m = jnp.max(x, axis=0, keepdims=True)
    e = jnp.exp(x - m)
    s = e / jnp.sum(e, axis=0, keepdims=True)
    lb = jnp.zeros_like(s[0:1, :])
    for i in range(1, layer + 1):
        lb = lb + s[i:i + 1, :]
    return lb


def _hgrn_in_kernel(x_ref, g_ref, lb_ref, wq_ref, wff_ref, wfb_ref, wi_ref, wg_ref,
                    q_out, lff_out, lfb_out, kgf_out, kgb_out, i_out, g_out, *, layer):
    xn = _rms_normed(x_ref[...], g_ref[...]).astype(BF16)
    lb = _hgrn_lower_bound(lb_ref, layer)
    dot = functools.partial(jnp.dot, preferred_element_type=F32)
    gate_logits = [dot(xn, wff_ref[...]), dot(xn, wfb_ref[...])]
    q_out[...] = dot(xn, wq_ref[...]).astype(q_out.dtype)
    for logits, lf_out, kg_out in zip(gate_logits, (lff_out, lfb_out), (kgf_out, kgb_out)):
        f = lb + (1.0 - lb) * _sigmoid(logits)
        lf_out[...] = jnp.log(f)
        kg_out[...] = (1.0 - f).astype(kg_out.dtype)
    i_out[...] = dot(xn, wi_ref[...]).astype(i_out.dtype)
    g_out[...] = dot(xn, wg_ref[...]).astype(g_out.dtype)


def _hgrn_in(x2d, gain, hg_lb, weights, layer):
    n, d = x2d.shape
    tm = _tile(n, 512)
    row = pl.BlockSpec((tm, d), lambda i: (i, 0))
    in_specs = [row, pl.BlockSpec((1, d), lambda i: (0, 0)), pl.BlockSpec(hg_lb.shape, lambda i: (0, 0))]
    in_specs += [pl.BlockSpec(w.shape, lambda i: (0, 0)) for w in weights]
    dtypes = (ACT, F32, F32, ACT, ACT, ACT, ACT)
    return pl.pallas_call(
        functools.partial(_hgrn_in_kernel, layer=layer),
        grid=(n // tm,), in_specs=in_specs, out_specs=[row] * len(dtypes),
        out_shape=[jax.ShapeDtypeStruct((n, d), dt) for dt in dtypes],
        compiler_params=_params("parallel"), name="hgrn_in",
    )(x2d, gain.reshape(1, d), hg_lb, *weights)


def _hgrn_chunk(q, lf, kg, vv, state_t, reverse):
    chunk = q.shape[0]
    n_sub = chunk // HG_SUB
    incl, _ = _order_masks(chunk, reverse)
    end = 0 if reverse else chunk - 1
    s_idx = lax.broadcasted_iota(jnp.int32, (HG_SUB, 1), 0)

    bcum = _mm_exact_lhs(incl.astype(BF16), lf)
    b_end = bcum[end:end + 1, :]
    carried = _mm_nt(q * jnp.exp(bcum), state_t)

    order = list(range(n_sub - 1, -1, -1)) if reverse else list(range(n_sub))
    outs = [None] * n_sub
    for pos, sb in enumerate(order):
        rows = slice(sb * HG_SUB, (sb + 1) * HG_SUB)
        q_s, k_s, b_s, v_s = q[rows], kg[rows], bcum[rows], vv[rows]
        acc = carried[rows]
        if pos > 0:
            prev_sb = order[pos - 1]
            edge = prev_sb * HG_SUB if reverse else prev_sb * HG_SUB + HG_SUB - 1
            b_edge = bcum[edge:edge + 1, :]
            src = slice((sb + 1) * HG_SUB, chunk) if reverse else slice(0, sb * HG_SUB)
            q_hat = q_s * jnp.exp(b_s - b_edge)
            k_hat = kg[src] * jnp.exp(b_edge - bcum[src])
            acc = acc + _mm(_mm_nt(q_hat, k_hat), vv[src])
        rows_out = []
        for l in range(HG_SUB):
            w = jnp.exp(jnp.minimum(b_s[l:l + 1, :] - b_s, 0.0)) * k_s * q_s[l:l + 1, :]
            att = jnp.sum(w, axis=-1, keepdims=True)
            att = jnp.where((s_idx >= l) if reverse else (s_idx <= l), att, 0.0)
            rows_out.append(jnp.sum(att * v_s, axis=0, keepdims=True))
        outs[sb] = acc + jnp.concatenate(rows_out, axis=0)

    k_tail = kg * jnp.exp(b_end - bcum)
    new_state_t = state_t * jnp.exp(b_end) + _mm_tn(vv, k_tail)
    return jnp.concatenate(outs, axis=0), new_state_t


def _hgrn_scan_kernel(q_ref, lf_ref, kg_ref, i_ref, lb_ref, o_ref, s_ref, hq_ref, hf_ref, hk_ref, hv_ref, ho_ref, *,
                      reverse, layer):
    @pl.when(pl.program_id(1) == 0)
    def _():
        s_ref[...] = jnp.zeros_like(s_ref)

    chunk = HG_CHUNK
    n_chunks = q_ref.shape[0] // chunk
    incl, _ = _order_masks(chunk, reverse)
    incl_bf = incl.astype(BF16)
    end = 0 if reverse else chunk - 1
    chunk_order = list(range(n_chunks - 1, -1, -1)) if reverse else list(range(n_chunks))

    def head_cols(h):
        return slice(h * HG_KEY_DIM, (h + 1) * HG_KEY_DIM)

    pre = []
    worst = None
    for ci in chunk_order:
        rows = slice(ci * chunk, (ci + 1) * chunk)
        bcum = _mm_exact_lhs(incl_bf, lf_ref[rows, :])
        b_end = bcum[end:end + 1, :]
        worst = b_end if worst is None else jnp.minimum(worst, b_end)
        pre.append((ci, rows, bcum, b_end))
    lb_min = jnp.min(_hgrn_lower_bound(lb_ref, layer))
    flag = lax.cond(lb_min >= math.exp(-HG_SAFE_LOG_RANGE / HG_CHUNK),
                    lambda: jnp.int32(1),
                    lambda: (jnp.min(worst) >= -HG_SAFE_LOG_RANGE).astype(jnp.int32))
    safe = flag == 1

    @pl.when(safe)
    def _():
        items = []
        for ci, rows, bcum, b_end in pre:
            kg = kg_ref[rows, :].astype(F32)
            e_neg = jnp.exp(-bcum)
            e_end = jnp.exp(b_end)
            qt_all = q_ref[rows, :].astype(F32) * jnp.exp(bcum)
            kt_all = kg * e_neg
            ktail_all = kt_all * e_end
            v_all = i_ref[rows, :].astype(F32)
            for h in range(HG_HEADS):
                c = head_cols(h)
                items.append(dict(rows=rows, h=h, qt=qt_all[:, c], kt=kt_all[:, c], ktail=ktail_all[:, c],
                                  v=v_all[:, c], e_end=e_end[:, c]))
        for it in items:
            it["attn"] = jnp.where(incl, _mm_nt(it["qt"], it["kt"]), 0.0)
        for it in items:
            it["local"] = _mm(it["attn"], it["v"])
            it["inject"] = _mm_tn(it["v"], it["ktail"])
        states = [s_ref[h] for h in range(HG_HEADS)]
        for it in items:
            h = it["h"]
            o_ref[it["rows"], head_cols(h)] = (it["local"] + _mm_nt(it["qt"], states[h])).astype(o_ref.dtype)
            states[h] = states[h] * it["e_end"] + it["inject"]
        for h in range(HG_HEADS):
            s_ref[h] = states[h]

    @pl.when(jnp.logical_not(safe))
    def _():
        def head_body(h, carry):
            out, new_state = _hgrn_chunk(hq_ref[h], hf_ref[h], hk_ref[h], hv_ref[h], s_ref[h], reverse)
            ho_ref[h] = out
            s_ref[h] = new_state
            return carry

        def chunk_body(i, carry):
            ci = (n_chunks - 1 - i) if reverse else i
            rows = pl.ds(pl.multiple_of(ci * chunk, chunk), chunk)
            for h in range(HG_HEADS):
                hq_ref[h] = q_ref[rows, head_cols(h)].astype(F32)
                hf_ref[h] = lf_ref[rows, head_cols(h)]
                hk_ref[h] = kg_ref[rows, head_cols(h)].astype(F32)
                hv_ref[h] = i_ref[rows, head_cols(h)].astype(F32)
            lax.fori_loop(0, HG_HEADS, head_body, 0)
            for h in range(HG_HEADS):
                o_ref[rows, head_cols(h)] = ho_ref[h].astype(o_ref.dtype)
            return carry

        lax.fori_loop(0, n_chunks, chunk_body, 0)


def _hgrn_scan(q, lf, kg, i_val, hg_lb, nb, t_len, reverse, layer):
    n = q.shape[0]
    tb = _tile(t_len, HG_BLOCK_CHUNKS * HG_CHUNK)
    assert tb % HG_CHUNK == 0
    n_outer = t_len // tb

    def idx(b, c):
        return (b * n_outer + (n_outer - 1 - c if reverse else c), 0)

    spec = pl.BlockSpec((tb, HG_KEY_WIDTH), idx)
    head_buf = pltpu.VMEM((HG_HEADS, HG_CHUNK, HG_KEY_DIM), F32)
    return pl.pallas_call(
        functools.partial(_hgrn_scan_kernel, reverse=reverse, layer=layer),
        grid=(nb, n_outer),
        in_specs=[spec, spec, spec, spec, pl.BlockSpec(hg_lb.shape, lambda b, c: (0, 0))],
        out_specs=spec,
        out_shape=jax.ShapeDtypeStruct((n, HG_VAL_WIDTH), ACT),
        scratch_shapes=[pltpu.VMEM((HG_HEADS, HG_VAL_DIM, HG_KEY_DIM), F32)] + [head_buf] * 5,
        compiler_params=_params("parallel", "arbitrary"),
        name="hgrn_scan_bwd" if reverse else "hgrn_scan_fwd",
    )(q, lf, kg, i_val, hg_lb)


def _hgrn_out_kernel(x_ref, of_ref, ob_ref, g_ref, nw_ref, wo_ref, out_ref):
    o = of_ref[...].astype(F32) + ob_ref[...].astype(F32)
    parts = []
    for h in range(HG_HEADS):
        oh = o[:, h * HG_VAL_DIM:(h + 1) * HG_VAL_DIM]
        parts.append(oh * lax.rsqrt(jnp.mean(oh * oh, axis=-1, keepdims=True) + NORM_EPS))
    o = jnp.concatenate(parts, axis=-1) * nw_ref[...] * _silu(g_ref[...].astype(F32))
    out_ref[...] = x_ref[...] + _mm(o, wo_ref[...])


def _hgrn_out(x2d, o_f, o_b, g, norm_w, w_out):
    n = x2d.shape[0]
    tm = _tile(n, 512)
    row = pl.BlockSpec((tm, D_MODEL), lambda i: (i, 0))
    return pl.pallas_call(
        _hgrn_out_kernel,
        grid=(n // tm,),
        in_specs=[row] * 4 + [pl.BlockSpec((1, HG_VAL_WIDTH), lambda i: (0, 0)),
                              pl.BlockSpec(w_out.shape, lambda i: (0, 0))],
        out_specs=row,
        out_shape=jax.ShapeDtypeStruct((n, D_MODEL), F32),
        compiler_params=_params("parallel"), name="hgrn_out",
    )(x2d, o_f, o_b, g, norm_w, w_out)


def _xattn_kernel(x_ref, g_ref, wq_ref, kv_ref, wo_ref, out_ref):
    x = x_ref[...]
    q = _mm(_rms_normed(x, g_ref[...]), wq_ref[...])
    scale = XA_HEAD_DIM ** -0.5
    scores = [_mm_nt(q[:, h * XA_HEAD_DIM:(h + 1) * XA_HEAD_DIM], kv_ref[:, h * XA_HEAD_DIM:(h + 1) * XA_HEAD_DIM])
              * scale for h in range(XA_HEADS)]
    probs = []
    for s in scores:
        p = jnp.exp(s - jnp.max(s, axis=-1, keepdims=True))
        probs.append(p * (1.0 / jnp.sum(p, axis=-1, keepdims=True)))
    heads = [_mm(p, kv_ref[:, D_MODEL + h * XA_HEAD_DIM:D_MODEL + (h + 1) * XA_HEAD_DIM])
             for h, p in enumerate(probs)]
    out_ref[...] = x + _mm(jnp.concatenate(heads, axis=-1), wo_ref[...])


def _xattn(x2d, gain, wq, kv, wo, nb, t_len, mem_len):
    n = x2d.shape[0]
    tm = _tile(t_len, 512)
    per_seq = t_len // tm
    row = pl.BlockSpec((tm, D_MODEL), lambda b, j: (b * per_seq + j, 0))
    return pl.pallas_call(
        _xattn_kernel,
        grid=(nb, per_seq),
        in_specs=[row, pl.BlockSpec((1, D_MODEL), lambda b, j: (0, 0)),
                  pl.BlockSpec(wq.shape, lambda b, j: (0, 0)),
                  pl.BlockSpec((mem_len, 2 * D_MODEL), lambda b, j: (b, 0)),
                  pl.BlockSpec(wo.shape, lambda b, j: (0, 0))],
        out_specs=row,
        out_shape=jax.ShapeDtypeStruct((n, D_MODEL), F32),
        compiler_params=_params("parallel", "parallel"), name="xattn",
    )(x2d, gain.reshape(1, D_MODEL), wq, kv, wo)


def _ffn_kernel(x_ref, g_ref, wi_ref, wo_ref, fg_ref, out_ref, *, final_norm):
    x = x_ref[...]
    gu = _mm(_rms_normed(x, g_ref[...]), wi_ref[...])
    act = _silu(gu[:, :FFN_DIM]) * gu[:, FFN_DIM:]
    y = x + _mm(act, wo_ref[...])
    if final_norm:
        y = _rms_normed(y, fg_ref[...])
    out_ref[...] = y


def _ffn(x2d, gain, w_in, w_out, final_gain, final_norm):
    n = x2d.shape[0]
    tm = _tile(n, 512)
    row = pl.BlockSpec((tm, D_MODEL), lambda i: (i, 0))
    vec = pl.BlockSpec((1, D_MODEL), lambda i: (0, 0))
    resident = pl.Buffered(1)
    return pl.pallas_call(
        functools.partial(_ffn_kernel, final_norm=final_norm),
        grid=(n // tm,),
        in_specs=[row, vec, pl.BlockSpec(w_in.shape, lambda i: (0, 0), pipeline_mode=resident),
                  pl.BlockSpec(w_out.shape, lambda i: (0, 0), pipeline_mode=resident), vec],
        out_specs=row,
        out_shape=jax.ShapeDtypeStruct((n, D_MODEL), F32),
        compiler_params=_params("parallel"), name="ffn",
    )(x2d, gain.reshape(1, D_MODEL), w_in, w_out, final_gain.reshape(1, D_MODEL))


def _pad_lanes(a, width=LANES):
    return jnp.pad(a, [(0, 0)] * (a.ndim - 1) + [(0, width - a.shape[-1])])


def _rwkv_ssd_layer(x2d, nb, t_len, norm_g, w_in, w_out, rw_mu, rw_w0, rw_w2, rw_a0, rw_a2, rw_g2, rw_k_k, rw_k_a,
                    rw_r_k, rw_gn_w, rw_gn_b, conv_w, conv_b, dt_bias, a_log, d_skip, ssd_norm_w):
    o_lora = 3 * RW_DIM
    o_z = RW_PROJ
    o_xbc = o_z + SSD_DIM
    o_dt = o_xbc + SSD_CONV_DIM
    wb = w_in.astype(BF16)
    weights = [wb[:, :o_lora], wb[:, o_lora:o_z], wb[:, o_z:o_xbc], wb[:, o_xbc:o_dt], _pad_lanes(wb[:, o_dt:])]
    head_id = jnp.arange(RW_DIM) // RW_HEAD_DIM
    blockdiag = (head_id[:, None] == head_id[None, :]).astype(BF16)
    row = lambda a: a.reshape(1, -1)
    rwkv_consts = (row(rw_mu[:o_lora]), row(rw_mu[o_lora:]), row(rw_a0), rw_a2.astype(BF16), rw_g2.astype(BF16),
                   row(rw_k_k), row(rw_k_a), rw_w0, rw_w2.astype(BF16), row(rw_r_k), blockdiag)
    ssd_consts = (conv_w, row(conv_b), _pad_lanes(dt_bias), _pad_lanes(a_log))
    (r, k, v, kk, kka, lw_f, lw_b, g, bonus, p_z, xbc, dt_f, dt_b, la_f, la_b) = _mix0_in(
        x2d, nb, t_len, norm_g, weights, rwkv_consts, ssd_consts)
    o_f = _rwkv_scan(r, k, v, kk, kka, lw_f, nb, t_len, reverse=False)
    o_b = _rwkv_scan(r, k, v, kk, kka, lw_b, nb, t_len, reverse=True)
    y_f = _ssd_scan(xbc, dt_f, la_f, nb, t_len, reverse=False)
    y_b = _ssd_scan(xbc, dt_b, la_b, nb, t_len, reverse=True)

    wo = w_out.astype(BF16)
    d_skip_lanes = jnp.repeat(d_skip, SSD_HEAD_DIM).reshape(1, SSD_DIM)
    return _mix0_out(x2d, o_f, o_b, bonus, g, y_f, y_b, xbc, p_z, row(rw_gn_w), row(rw_gn_b), blockdiag,
                     d_skip_lanes, row(ssd_norm_w), wo[:RW_DIM], wo[RW_DIM:])


def _hgrn_layer(x2d, nb, t_len, norm_g, w_in, w_out, norm_w, hg_lb, layer):
    kw, vw = HG_KEY_WIDTH, HG_VAL_WIDTH
    wb = w_in.astype(BF16)
    weights = [wb[:, :kw], wb[:, kw:2 * kw], wb[:, 2 * kw:3 * kw], wb[:, 3 * kw:3 * kw + vw], wb[:, 3 * kw + vw:]]
    q, lf_f, lf_b, kg_f, kg_b, i_val, g = _hgrn_in(x2d, norm_g, hg_lb, weights, layer)
    o_f = _hgrn_scan(q, lf_f, kg_f, i_val, hg_lb, nb, t_len, reverse=False, layer=layer)
    o_b = _hgrn_scan(q, lf_b, kg_b, i_val, hg_lb, nb, t_len, reverse=True, layer=layer)
    return _hgrn_out(x2d, o_f, o_b, g, norm_w.reshape(1, vw), w_out.astype(BF16))


def kernel(x, mem, mix_norm, ab_w_in, ab_w_out, rw_mu, rw_w0, rw_w2, rw_a0, rw_a2, rw_g2, rw_k_k, rw_k_a, rw_r_k, rw_gn_w, rw_gn_b, ssd_conv_w, ssd_conv_b, ssd_dt_bias, ssd_a_log, ssd_d, ssd_norm_w, hg_w_in, hg_w_out, hg_norm_w, hg_lb, xa_norm, mem_norm, xa_wq, xa_wkv, xa_wo, ffn_norm, ffn_w_in, ffn_w_out, final_norm):
    nb, t_len, d = x.shape
    mem_len = mem.shape[1]
    depth = mix_norm.shape[0]
    x2d = x.reshape(nb * t_len, d)
    mem2d = mem.reshape(nb * mem_len, d)
    for layer in range(depth):
        if layer % 2 == 0:
            e = layer // 2
            x2d = _rwkv_ssd_layer(x2d, nb, t_len, mix_norm[layer], ab_w_in[e], ab_w_out[e], rw_mu[e], rw_w0[e],
                                  rw_w2[e], rw_a0[e], rw_a2[e], rw_g2[e], rw_k_k[e], rw_k_a[e],
                                  rw_r_k[e].reshape(-1), rw_gn_w[e], rw_gn_b[e], ssd_conv_w[e], ssd_conv_b[e],
                                  ssd_dt_bias[e], ssd_a_log[e], ssd_d[e], ssd_norm_w[e])
        else:
            o = layer // 2
            x2d = _hgrn_layer(x2d, nb, t_len, mix_norm[layer], hg_w_in[o], hg_w_out[o], hg_norm_w[o], hg_lb, layer)
        (kv,) = _norm_matmul(mem2d, mem_norm[layer], [xa_wkv[layer].astype(BF16)], [BF16])
        x2d = _xattn(x2d, xa_norm[layer], xa_wq[layer].astype(BF16), kv, xa_wo[layer].astype(BF16),
                     nb, t_len, mem_len)
        x2d = _ffn(x2d, ffn_norm[layer], ffn_w_in[layer].astype(BF16), ffn_w_out[layer].astype(BF16),
                   final_norm, final_norm=(layer == depth - 1))
    return x2d.reshape(nb, t_len, d)
```

```python
import functools
import math

import jax
import jax.numpy as jnp
from jax import lax
from jax.experimental import pallas as pl
from jax.experimental.pallas import tpu as pltpu

F32 = jnp.float32
BF16 = jnp.bfloat16
ACT = BF16

D_MODEL = 1024
NORM_EPS = 1e-6

RW_HEAD_DIM = 64
RW_DIM = D_MODEL // 2
RW_HEADS = RW_DIM // RW_HEAD_DIM
RW_DECAY_RANK = 64
RW_A_RANK = 64
RW_GATE_RANK = 128
RW_GN_EPS = 64e-5
RW_LORA = RW_DECAY_RANK + RW_A_RANK + RW_GATE_RANK
RW_PROJ = 3 * RW_DIM + RW_LORA
RW_CHUNK = 64
RW_BLOCK_CHUNKS = 4
RW_HEAD_PAIRS = RW_HEADS // 2
assert RW_CHUNK == RW_HEAD_DIM and 2 * RW_HEAD_DIM == 128

SSD_HEAD_DIM = 64
SSD_DIM = D_MODEL // 2
SSD_HEADS = SSD_DIM // SSD_HEAD_DIM
SSD_GROUPS = 2
SSD_STATE = 128
SSD_CONV_DIM = SSD_DIM + 2 * SSD_GROUPS * SSD_STATE
SSD_CHUNK = 128
SSD_BLOCK_CHUNKS = 4
SSD_GROUP_HEADS = SSD_HEADS // SSD_GROUPS

HG_HEADS = 8
HG_KEY_DIM = 128
HG_VAL_DIM = D_MODEL // HG_HEADS
HG_KEY_WIDTH = HG_HEADS * HG_KEY_DIM
HG_VAL_WIDTH = HG_HEADS * HG_VAL_DIM
HG_CHUNK = 64
HG_SUB = 16
HG_BLOCK_CHUNKS = 4
HG_SAFE_LOG_RANGE = 64.0
assert HG_KEY_DIM == HG_VAL_DIM

XA_HEADS = 4
XA_HEAD_DIM = D_MODEL // XA_HEADS
FFN_DIM = ((8 * D_MODEL + 3 * 256 - 1) // (3 * 256)) * 256

SUBLANES = 8
HALO_ROWS = 16
LANES = 128
VMEM_LIMIT_BYTES = 56 * 1024 * 1024
SCAN_STEP_ROWS = 1024
NEG_BIG = -1e30


def _params(*semantics):
    return pltpu.CompilerParams(dimension_semantics=semantics, vmem_limit_bytes=VMEM_LIMIT_BYTES)


def _tile(n, pref):
    t = min(n, pref)
    while n % t or t % SUBLANES:
        t -= 1
    return t


def _scan_rows(t_len, sub_rows):
    assert t_len % sub_rows == 0
    rows = sub_rows
    while rows * 2 <= SCAN_STEP_ROWS and t_len % (rows * 2) == 0:
        rows *= 2
    return rows


def _mm(a, b):
    return jnp.dot(a.astype(BF16), b.astype(BF16), preferred_element_type=F32)


def _mm_nt(a, b):
    return lax.dot_general(a.astype(BF16), b.astype(BF16), (((1,), (1,)), ((), ())),
                           preferred_element_type=F32)


def _mm_tn(a, b):
    return lax.dot_general(a.astype(BF16), b.astype(BF16), (((0,), (0,)), ((), ())),
                           preferred_element_type=F32)


def _split3(x):
    hi = x.astype(BF16)
    r1 = x - hi.astype(F32)
    mid = r1.astype(BF16)
    lo = (r1 - mid.astype(F32)).astype(BF16)
    return hi, mid, lo


def _mm_exact_lhs(m_bf16, x):
    hi, mid, lo = _split3(x)
    dot = functools.partial(jnp.dot, preferred_element_type=F32)
    return dot(m_bf16, hi) + dot(m_bf16, mid) + dot(m_bf16, lo)


def _mm_exact_rhs_many(xs, m_bf16, terms=3):
    rows = xs[0].shape[0]
    pieces = [piece for x in xs for piece in _split3(x)[:terms]]
    out = jnp.dot(jnp.concatenate(pieces, axis=0), m_bf16, preferred_element_type=F32)
    results = []
    for i in range(len(xs)):
        total = out[i * terms * rows:(i * terms + 1) * rows]
        for t in range(1, terms):
            total = total + out[(i * terms + t) * rows:(i * terms + t + 1) * rows]
        results.append(total)
    return results


def _mm_exact_rhs(x, m_bf16, terms=3):
    return _mm_exact_rhs_many([x], m_bf16, terms)[0]


def _rms_normed(x, gain):
    ms = jnp.mean(x * x, axis=-1, keepdims=True)
    return x * lax.rsqrt(ms + NORM_EPS) * gain


def _sigmoid(x):
    return 0.5 * jnp.tanh(0.5 * x) + 0.5


def _silu(x):
    return x * _sigmoid(x)


def _softplus(x):
    return jnp.maximum(x, 0.0) + jnp.log(1.0 + jnp.exp(-jnp.abs(x)))


def _order_masks(n, reverse):
    row = lax.broadcasted_iota(jnp.int32, (n, n), 0)
    col = lax.broadcasted_iota(jnp.int32, (n, n), 1)
    if reverse:
        return col >= row, col > row
    return col <= row, col < row


def _norm_matmul_kernel(x_ref, g_ref, *refs, n_out):
    xn = _rms_normed(x_ref[...], g_ref[...]).astype(BF16)
    for w_ref, o_ref in zip(refs[:n_out], refs[n_out:]):
        o_ref[...] = jnp.dot(xn, w_ref[...], preferred_element_type=F32).astype(o_ref.dtype)


def _norm_matmul(x2d, gain, weights, out_dtypes, tm_pref=512):
    n, d = x2d.shape
    tm = _tile(n, tm_pref)
    in_specs = [pl.BlockSpec((tm, d), lambda i: (i, 0)), pl.BlockSpec((1, d), lambda i: (0, 0))]
    in_specs += [pl.BlockSpec(w.shape, lambda i: (0, 0)) for w in weights]
    out_specs = [pl.BlockSpec((tm, w.shape[1]), lambda i: (i, 0)) for w in weights]
    out_shape = [jax.ShapeDtypeStruct((n, w.shape[1]), dt) for w, dt in zip(weights, out_dtypes)]
    return pl.pallas_call(
        functools.partial(_norm_matmul_kernel, n_out=len(weights)),
        grid=(n // tm,), in_specs=in_specs, out_specs=out_specs, out_shape=out_shape,
        compiler_params=_params("parallel"), name="norm_matmul",
    )(x2d, gain.reshape(1, d), *weights)


def _neighbours(p, prev_blk, next_blk, first, last):
    tt = p.shape[0]
    ridx = lax.broadcasted_iota(jnp.int32, p.shape, 0)
    prev_row = jnp.where(first, 0.0, prev_blk[HALO_ROWS - 1:HALO_ROWS, :].astype(F32))
    next_row = jnp.where(last, 0.0, next_blk[0:1, :].astype(F32))
    prev = jnp.where(ridx == 0, prev_row, pltpu.roll(p, 1, 0))
    nxt = jnp.where(ridx == tt - 1, next_row, pltpu.roll(p, tt - 1, 0))
    return prev, nxt


def _shift_matrix(tt, offsets, weight):
    t = jnp.arange(tt)[:, None]
    c = jnp.arange(tt + 2 * HALO_ROWS)[None, :]
    hit = functools.reduce(jnp.logical_or, [c == HALO_ROWS + t + d for d in offsets])
    return jnp.where(hit, weight, 0.0).astype(BF16)


def _head_sum(x, blockdiag_bf16):
    return _mm_exact_rhs(x, blockdiag_bf16, terms=2)


def _mix0_project(x_ref, xp_ref, xn_ref, gain_ref, wrkv_ref, wlora_ref, wz_ref, wxbc_ref, wdt_ref, first, last,
                  prkv_scr, plora_scr, u_scr, z_scr, dt_scr):
    tt = x_ref.shape[0]
    dot = functools.partial(jnp.dot, preferred_element_type=F32)
    x_ext = jnp.concatenate([jnp.where(first, 0.0, xp_ref[...]), x_ref[...], jnp.where(last, 0.0, xn_ref[...])], axis=0)
    xn_ext = _rms_normed(x_ext, gain_ref[...]).astype(BF16)
    xn = xn_ext[HALO_ROWS:HALO_ROWS + tt]

    def column_block(lhs, w_ref, scr, lo, hi):
        def piece():
            scr[:, lo:hi] = dot(lhs, w_ref[:, lo:hi]).astype(scr.dtype)
        return piece

    pieces = [column_block(xn_ext, wrkv_ref, prkv_scr, c, c + RW_DIM) for c in range(0, 3 * RW_DIM, RW_DIM)]
    pieces.append(column_block(xn_ext, wlora_ref, plora_scr, 0, RW_LORA))
    pieces += [column_block(xn_ext, wxbc_ref, u_scr, c, c + SSD_DIM) for c in range(0, SSD_CONV_DIM, SSD_DIM)]
    pieces.append(column_block(xn, wz_ref, z_scr, 0, SSD_DIM))
    pieces.append(column_block(xn, wdt_ref, dt_scr, 0, LANES))
    return pieces


def _mix0_operands(prkv_scr, plora_scr, u_scr, z_scr, dt_scr,
                   savg_ref, mu_ref, mul_ref, a0_ref, a2_ref, g2_ref, kk_ref, ka_ref, w0_ref, w2_ref, rk_ref, bd_ref,
                   cw_ref, cb_ref, dtb_ref, alog_ref,
                   r_out, k_out, v_out, kkn_out, kka_out, lwf_out, lwb_out, g_out, bonus_out,
                   z_out, xbc_out, dtf_out, dtb_out, laf_out, lab_out):
    tt = z_scr.shape[0]
    dot = functools.partial(jnp.dot, preferred_element_type=F32)

    def token_shift(p_scr, m_ref, lo_col, hi_col):
        p_ext = p_scr[:, lo_col:hi_col]
        x = p_ext[HALO_ROWS:HALO_ROWS + tt].astype(F32)
        avg = dot(savg_ref[...], p_ext)
        return x + m_ref[:, lo_col:hi_col] * (avg - x)

    r = token_shift(prkv_scr, mu_ref, 0, RW_DIM)
    k = token_shift(prkv_scr, mu_ref, RW_DIM, 2 * RW_DIM)
    v = token_shift(prkv_scr, mu_ref, 2 * RW_DIM, 3 * RW_DIM)
    lo = token_shift(plora_scr, mul_ref, 0, RW_LORA)
    wd = lo[:, :RW_DECAY_RANK]
    ad = lo[:, RW_DECAY_RANK:RW_DECAY_RANK + RW_A_RANK]
    gd = lo[:, RW_DECAY_RANK + RW_A_RANK:]
    bd = bd_ref[...]

    a = _sigmoid(a0_ref[...] + _mm(ad, a2_ref[...]))
    g = _mm(_sigmoid(gd), g2_ref[...])
    kk = k * kk_ref[...]
    kk = kk / jnp.maximum(jnp.sqrt(_head_sum(kk * kk, bd)), 1e-12)
    k = k * (1.0 + (a - 1.0) * ka_ref[...])
    w_lora = jnp.tanh(wd)

    def log_decay(d):
        wl = w0_ref[d:d + 1, :] + _mm(w_lora, w2_ref[d])
        return -jnp.exp(-_softplus(-wl) - 0.5)

    r_out[...] = r.astype(r_out.dtype)
    k_out[...] = k.astype(k_out.dtype)
    v_out[...] = v.astype(v_out.dtype)
    kkn_out[...] = kk.astype(kkn_out.dtype)
    kka_out[...] = (kk * a).astype(kka_out.dtype)
    lwf_out[...] = log_decay(0)
    lwb_out[...] = log_decay(1)
    g_out[...] = g.astype(g_out.dtype)
    bonus_out[...] = (_head_sum(r * k * rk_ref[...], bd) * v).astype(bonus_out.dtype)

    z_out[...] = z_scr[...]
    u_ext = u_scr[...]
    u = u_ext[HALO_ROWS:HALO_ROWS + tt].astype(F32)
    prev, nxt = _neighbours(u, u_ext[:HALO_ROWS], u_ext[HALO_ROWS + tt:], False, False)
    conv = prev * cw_ref[0:1, :] + u * cw_ref[1:2, :] + nxt * cw_ref[2:3, :] + cb_ref[...]
    xbc_out[...] = _silu(conv).astype(xbc_out.dtype)
    dt_raw = dt_scr[...]
    dt_f = _softplus(dt_raw + dtb_ref[0:1, :])
    dt_b = _softplus(dt_raw + dtb_ref[1:2, :])
    dtf_out[...] = dt_f
    dtb_out[...] = dt_b
    laf_out[...] = dt_f * (-jnp.exp(alog_ref[0:1, :]))
    lab_out[...] = dt_b * (-jnp.exp(alog_ref[1:2, :]))


MIX0_N_PROJ_REFS = 9
MIX0_N_CONSTS = 16
MIX0_N_OUTS = 15
MIX0_N_STAGED = 5


def _mix0_in_kernel(*refs, per_seq):
    proj_refs = refs[:MIX0_N_PROJ_REFS]
    const_refs = refs[MIX0_N_PROJ_REFS:MIX0_N_PROJ_REFS + MIX0_N_CONSTS]
    out_refs = refs[MIX0_N_PROJ_REFS + MIX0_N_CONSTS:MIX0_N_PROJ_REFS + MIX0_N_CONSTS + MIX0_N_OUTS]
    staged = refs[MIX0_N_PROJ_REFS + MIX0_N_CONSTS + MIX0_N_OUTS:]
    pos = lax.rem(pl.program_id(0), per_seq)
    for piece in _mix0_project(*proj_refs, pos == 0, pos == per_seq - 1, *staged):
        piece()
    _mix0_operands(*staged, *const_refs, *out_refs)


def _mix0_in(x2d, nb, t_len, gain, weights, rwkv_consts, ssd_consts):
    n, d = x2d.shape
    tt = _tile(t_len, 512)
    assert tt % HALO_ROWS == 0
    per_seq = t_len // tt
    n_tiles = n // tt
    blocks_per_tile = tt // HALO_ROWS
    n_blocks = n // HALO_ROWS

    def full(a):
        nd = a.ndim
        return pl.BlockSpec(a.shape, lambda s: (0,) * nd)

    x_specs = [pl.BlockSpec((tt, d), lambda s: (s, 0)),
               pl.BlockSpec((HALO_ROWS, d), lambda s: (jnp.maximum(s * blocks_per_tile - 1, 0), 0)),
               pl.BlockSpec((HALO_ROWS, d), lambda s: (jnp.minimum((s + 1) * blocks_per_tile, n_blocks - 1), 0))]
    proj_consts = (gain.reshape(1, d),) + tuple(weights)
    consts = (_shift_matrix(tt, (-1, 1), 0.5),) + tuple(rwkv_consts) + tuple(ssd_consts)
    assert 3 + len(proj_consts) == MIX0_N_PROJ_REFS and len(consts) == MIX0_N_CONSTS
    wide = [(RW_DIM, ACT)] * 5 + [(RW_DIM, F32)] * 2 + [(RW_DIM, ACT)] * 2 + [(SSD_DIM, ACT), (SSD_CONV_DIM, ACT)] \
        + [(LANES, F32)] * 4
    assert len(wide) == MIX0_N_OUTS
    ext = tt + 2 * HALO_ROWS
    staged = [pltpu.VMEM((ext, 3 * RW_DIM), ACT), pltpu.VMEM((ext, RW_LORA), ACT), pltpu.VMEM((ext, SSD_CONV_DIM), ACT),
              pltpu.VMEM((tt, SSD_DIM), ACT), pltpu.VMEM((tt, LANES), F32)]
    assert len(staged) == MIX0_N_STAGED
    return pl.pallas_call(
        functools.partial(_mix0_in_kernel, per_seq=per_seq),
        grid=(n_tiles,),
        in_specs=x_specs + [full(a) for a in proj_consts + consts],
        out_specs=[pl.BlockSpec((tt, w), lambda s: (s, 0)) for w, _ in wide],
        out_shape=[jax.ShapeDtypeStruct((n, w), dt) for w, dt in wide],
        scratch_shapes=staged,
        compiler_params=_params("parallel"), name="mix0_in",
    )(x2d, x2d, x2d, *proj_consts, *consts)


def _pair_blockdiag(y):
    lane = lax.broadcasted_iota(jnp.int32, y.shape, 1)
    top = jnp.where(lane < RW_HEAD_DIM, y, 0.0).astype(BF16)
    bottom = jnp.where(lane >= RW_HEAD_DIM, y, 0.0).astype(BF16)
    return jnp.concatenate([top, bottom], axis=0)


def _pair_diag_blocks(full):
    lane = lax.broadcasted_iota(jnp.int32, (RW_HEAD_DIM, LANES), 1)
    return jnp.where(lane < RW_HEAD_DIM, full[:RW_HEAD_DIM], full[RW_HEAD_DIM:])


def _unit_lower_inverses(n_strict_list, eye):
    size = n_strict_list[0].shape[0]
    levels = int(math.log2(size)) - 1
    accs = [eye + n for n in n_strict_list]
    powers = [_mm(n, _pair_blockdiag(n)) for n in n_strict_list]
    for level in range(levels):
        power_bds = [_pair_blockdiag(p) for p in powers]
        if level == levels - 1:
            return [a + _mm(a, b) for a, b in zip(accs, power_bds)]
        both = [_mm(jnp.concatenate([a, p], axis=0), b) for a, p, b in zip(accs, powers, power_bds)]
        accs = [a + ab[:size] for a, ab in zip(accs, both)]
        powers = [ab[size:] for ab in both]


def _rwkv_scan_kernel(r_ref, k_ref, v_ref, kk_ref, kka_ref, lw_ref, o_ref, h_ref, *, reverse):
    @pl.when(pl.program_id(1) == 0)
    def _():
        h_ref[...] = jnp.zeros_like(h_ref)

    sub_rows = RW_BLOCK_CHUNKS * RW_CHUNK
    n_sub = r_ref.shape[0] // sub_rows

    def body(i, carry):
        blk = (n_sub - 1 - i) if reverse else i
        _rwkv_scan_block(r_ref, k_ref, v_ref, kk_ref, kka_ref, lw_ref, o_ref, h_ref,
                         pl.multiple_of(blk * sub_rows, sub_rows), reverse)
        return carry

    lax.fori_loop(0, n_sub, body, 0)


def _rwkv_scan_block(r_ref, k_ref, v_ref, kk_ref, kka_ref, lw_ref, o_ref, h_ref, base, reverse):
    chunk = RW_CHUNK
    n_chunks = RW_BLOCK_CHUNKS
    incl_bf = _order_masks(chunk, reverse)[0].astype(BF16)
    row = lax.broadcasted_iota(jnp.int32, (chunk, LANES), 0)
    pos = lax.broadcasted_iota(jnp.int32, (chunk, LANES), 1) % RW_HEAD_DIM
    incl, strict = (pos >= row, pos > row) if reverse else (pos <= row, pos < row)
    eye = (pos == row).astype(F32)
    end = 0 if reverse else chunk - 1
    chunk_order = list(range(n_chunks - 1, -1, -1)) if reverse else list(range(n_chunks))

    items = []
    for ci in chunk_order:
        rows = pl.ds(base + ci * chunk, chunk)
        lw = lw_ref[rows, :]
        cum = _mm_exact_lhs(incl_bf, lw)
        cum_end = cum[end:end + 1, :]
        e_neg = jnp.exp(-cum)
        e_tail = jnp.exp(cum_end - cum)
        p_end = jnp.exp(cum_end)
        k = k_ref[rows, :].astype(F32)
        kka = kka_ref[rows, :].astype(F32)
        at_all = -kk_ref[rows, :].astype(F32) * jnp.exp(cum - lw)
        bt_all = kka * e_neg
        kt_all = k * e_neg
        rt_all = r_ref[rows, :].astype(F32) * jnp.exp(cum)
        bh_all = kka * e_tail
        kh_all = k * e_tail
        v = v_ref[rows, :].astype(F32)
        for j in range(RW_HEAD_PAIRS):
            sl = slice(j * LANES, (j + 1) * LANES)
            items.append(dict(ci=ci, j=j, at=at_all[:, sl], bt=bt_all[:, sl], kt=kt_all[:, sl], rt=rt_all[:, sl],
                              bh=bh_all[:, sl], kh=kh_all[:, sl], v=v[:, sl], p_end=p_end[:, sl]))

    for it in items:
        lhs = jnp.concatenate([it["at"], it["rt"]], axis=0)
        ab_rb = _mm_nt(lhs, _pair_blockdiag(it["bt"]))
        ak_rk = _mm_nt(lhs, _pair_blockdiag(it["kt"]))
        it["a_ab"] = jnp.where(strict, ab_rb[:chunk], 0.0)
        it["a_rb"] = jnp.where(incl, ab_rb[chunk:], 0.0)
        it["a_ak"] = jnp.where(strict, ak_rk[:chunk], 0.0)
        it["a_rk"] = jnp.where(incl, ak_rk[chunk:], 0.0)
    t_invs = _unit_lower_inverses([it["a_ab"] for it in items], eye)
    for it in items:
        akv_rkv = _mm(jnp.concatenate([it["a_ak"], it["a_rk"]], axis=0), _pair_blockdiag(it["v"]))
        it["x1"] = akv_rkv[:chunk]
        it["rk_v"] = akv_rkv[chunk:]
    for it, t_inv in zip(items, t_invs):
        it["w_u"] = _mm(t_inv, _pair_blockdiag(it["at"]))
        it["u0"] = _mm(t_inv, _pair_blockdiag(it["x1"]))
    for it in items:
        w_o = it["rt"] + _mm(it["a_rb"], _pair_blockdiag(it["w_u"]))
        g_mat = eye * it["p_end"] + _pair_diag_blocks(_mm_tn(it["bh"], it["w_u"]))
        it["wo_g"] = jnp.concatenate([w_o, g_mat], axis=0)
        it["o0"] = _mm(it["a_rb"], _pair_blockdiag(it["u0"])) + it["rk_v"]
        it["h0"] = _pair_diag_blocks(_mm_tn(it["bh"], it["u0"]) + _mm_tn(it["kh"], it["v"]))

    states = [h_ref[j] for j in range(RW_HEAD_PAIRS)]
    for it in items:
        j = it["j"]
        res = _mm(it["wo_g"], _pair_blockdiag(states[j]))
        rows = pl.ds(base + it["ci"] * chunk, chunk)
        o_ref[rows, j * LANES:(j + 1) * LANES] = (res[:chunk] + it["o0"]).astype(o_ref.dtype)
        states[j] = res[chunk:] + it["h0"]
    for j in range(RW_HEAD_PAIRS):
        h_ref[j] = states[j]


def _rwkv_scan(r, k, v, kk, kka, lw, nb, t_len, reverse):
    n = r.shape[0]
    tb = _scan_rows(t_len, RW_BLOCK_CHUNKS * RW_CHUNK)
    nc = t_len // tb

    def idx(b, c):
        return (b * nc + (nc - 1 - c if reverse else c), 0)

    spec = pl.BlockSpec((tb, RW_DIM), idx)
    return pl.pallas_call(
        functools.partial(_rwkv_scan_kernel, reverse=reverse),
        grid=(nb, nc), in_specs=[spec] * 6, out_specs=spec,
        out_shape=jax.ShapeDtypeStruct((n, RW_DIM), ACT),
        scratch_shapes=[pltpu.VMEM((RW_HEAD_PAIRS, RW_HEAD_DIM, LANES), F32)],
        compiler_params=_params("parallel", "arbitrary"), name="rwkv_scan_bwd" if reverse else "rwkv_scan_fwd",
    )(r, k, v, kk, kka, lw)


def _ssd_scan_kernel(xbc_ref, dt_ref, la_ref, ex64_ref, ex128_ref, y_ref, s_ref, *, reverse):
    @pl.when(pl.program_id(1) == 0)
    def _():
        s_ref[...] = jnp.zeros_like(s_ref)

    sub_rows = SSD_BLOCK_CHUNKS * SSD_CHUNK
    n_sub = xbc_ref.shape[0] // sub_rows

    def body(i, carry):
        blk = (n_sub - 1 - i) if reverse else i
        _ssd_scan_block(xbc_ref, dt_ref, la_ref, ex64_ref, ex128_ref, y_ref, s_ref,
                        pl.multiple_of(blk * sub_rows, sub_rows), reverse)
        return carry

    lax.fori_loop(0, n_sub, body, 0)


def _ssd_scan_block(xbc_ref, dt_ref, la_ref, ex64_ref, ex128_ref, y_ref, s_ref, base, reverse):
    chunk = SSD_CHUNK
    n_chunks = SSD_BLOCK_CHUNKS
    incl, _ = _order_masks(chunk, reverse)
    incl_bf = incl.astype(BF16)
    incl_t_bf = _order_masks(chunk, not reverse)[0].astype(BF16)
    end = 0 if reverse else chunk - 1
    gn = SSD_GROUPS * SSD_STATE
    gw = SSD_GROUP_HEADS * SSD_HEAD_DIM
    chunk_order = list(range(n_chunks - 1, -1, -1)) if reverse else list(range(n_chunks))
    ex64 = ex64_ref[...]
    ex128 = ex128_ref[...]

    pre = []
    for ci in chunk_order:
        rows = pl.ds(base + ci * chunk, chunk)
        la = la_ref[rows, :]
        cum = _mm_exact_lhs(incl_bf, la)
        cum_rows = _mm_exact_rhs(la.T, incl_t_bf)
        cum_end = cum[end:end + 1, :]
        ecum64, etail64, dt64 = _mm_exact_rhs_many(
            [jnp.exp(cum), jnp.exp(cum_end - cum), dt_ref[rows, :]], ex64, terms=2)
        cumcol = _mm_exact_rhs(cum, ex128)
        xdt = xbc_ref[rows, :SSD_DIM].astype(F32) * dt64
        pre.append(dict(rows=rows, cum_rows=cum_rows, ecum64=ecum64, eend64=ecum64[end:end + 1, :],
                        cumcol=cumcol, xdt=xdt, xtail=xdt * etail64,
                        bm=[xbc_ref[rows, SSD_DIM + g * SSD_STATE:SSD_DIM + (g + 1) * SSD_STATE]
                            for g in range(SSD_GROUPS)],
                        cm=[xbc_ref[rows, SSD_DIM + gn + g * SSD_STATE:SSD_DIM + gn + (g + 1) * SSD_STATE]
                            for g in range(SSD_GROUPS)]))
    for p in pre:
        p["scores"] = [_mm_nt(p["cm"][g], p["bm"][g]) for g in range(SSD_GROUPS)]
        p["inject"] = [_mm_tn(p["bm"][g], p["xtail"][:, g * gw:(g + 1) * gw]) for g in range(SSD_GROUPS)]
    for p in pre:
        p["ydiag"] = []
        for h in range(SSD_HEADS):
            diff = p["cumcol"][:, h * LANES:(h + 1) * LANES] - p["cum_rows"][h:h + 1, :]
            decay = jnp.exp(jnp.where(incl, diff, NEG_BIG))
            w = p["scores"][h // SSD_GROUP_HEADS] * decay
            p["ydiag"].append(_mm(w, p["xdt"][:, h * SSD_HEAD_DIM:(h + 1) * SSD_HEAD_DIM]))

    states = [s_ref[g] for g in range(SSD_GROUPS)]
    for p in pre:
        for g in range(SSD_GROUPS):
            y_off = _mm(p["cm"][g], states[g]) * p["ecum64"][:, g * gw:(g + 1) * gw]
            for jh in range(SSD_GROUP_HEADS):
                h = g * SSD_GROUP_HEADS + jh
                y_ref[p["rows"], h * SSD_HEAD_DIM:(h + 1) * SSD_HEAD_DIM] = (
                    p["ydiag"][h] + y_off[:, jh * SSD_HEAD_DIM:(jh + 1) * SSD_HEAD_DIM]).astype(y_ref.dtype)
            states[g] = states[g] * p["eend64"][:, g * gw:(g + 1) * gw] + p["inject"][g]
    for g in range(SSD_GROUPS):
        s_ref[g] = states[g]


def _ssd_scan(xbc, dt, la, nb, t_len, reverse):
    n = xbc.shape[0]
    tb = _scan_rows(t_len, SSD_BLOCK_CHUNKS * SSD_CHUNK)
    assert SSD_CHUNK == LANES
    nc = t_len // tb
    lane_head = jnp.arange(LANES)[:, None]
    ex64 = (lane_head == jnp.arange(SSD_DIM)[None, :] // SSD_HEAD_DIM).astype(BF16)
    ex128 = (lane_head == jnp.arange(SSD_HEADS * LANES)[None, :] // LANES).astype(BF16)

    def idx(b, c):
        return (b * nc + (nc - 1 - c if reverse else c), 0)

    def full(a):
        return pl.BlockSpec(a.shape, lambda b, c: (0, 0))

    return pl.pallas_call(
        functools.partial(_ssd_scan_kernel, reverse=reverse),
        grid=(nb, nc),
        in_specs=[pl.BlockSpec((tb, SSD_CONV_DIM), idx), pl.BlockSpec((tb, LANES), idx),
                  pl.BlockSpec((tb, LANES), idx), full(ex64), full(ex128)],
        out_specs=pl.BlockSpec((tb, SSD_DIM), idx),
        out_shape=jax.ShapeDtypeStruct((n, SSD_DIM), ACT),
        scratch_shapes=[pltpu.VMEM((SSD_GROUPS, SSD_STATE, SSD_GROUP_HEADS * SSD_HEAD_DIM), F32)],
        compiler_params=_params("parallel", "arbitrary"), name="ssd_scan_bwd" if reverse else "ssd_scan_fwd",
    )(xbc, dt, la, ex64, ex128)


def _mix0_out_kernel(x_ref, of_ref, ob_ref, bonus_ref, g_ref, yf_ref, yb_ref, xbc_ref, z_ref,
                     gnw_ref, gnb_ref, bd_ref, dsk_ref, nw_ref, wo_rw_ref, wo_ssd_ref, out_ref):
    bd = bd_ref[...]
    o = of_ref[...].astype(F32) + ob_ref[...].astype(F32)
    inv_n = 1.0 / RW_HEAD_DIM
    mu = _head_sum(o, bd) * inv_n
    oc = o - mu
    var = _head_sum(oc * oc, bd) * inv_n
    o = oc * lax.rsqrt(var + RW_GN_EPS) * gnw_ref[...] + gnb_ref[...]
    o_rw = (o + bonus_ref[...].astype(F32)) * g_ref[...].astype(F32)

    y = yf_ref[...].astype(F32) + yb_ref[...].astype(F32) + dsk_ref[...] * xbc_ref[...].astype(F32)
    y = y * _silu(z_ref[...].astype(F32))
    gw = SSD_DIM // SSD_GROUPS
    parts = []
    for g in range(SSD_GROUPS):
        yg = y[:, g * gw:(g + 1) * gw]
        parts.append(yg * lax.rsqrt(jnp.mean(yg * yg, axis=-1, keepdims=True) + NORM_EPS))
    o_ssd = jnp.concatenate(parts, axis=-1) * nw_ref[...]
    out_ref[...] = x_ref[...] + _mm(o_rw, wo_rw_ref[...]) + _mm(o_ssd, wo_ssd_ref[...])


def _mix0_out(x2d, o_f, o_b, bonus, g, y_f, y_b, xbc, z, gn_w, gn_b, blockdiag, d_skip, norm_w, wo_rw, wo_ssd):
    n = x2d.shape[0]
    tm = _tile(n, 512)

    def row(width):
        return pl.BlockSpec((tm, width), lambda i: (i, 0))

    def full(a):
        return pl.BlockSpec(a.shape, lambda i: (0, 0))

    consts = (gn_w, gn_b, blockdiag, d_skip, norm_w, wo_rw, wo_ssd)
    return pl.pallas_call(
        _mix0_out_kernel,
        grid=(n // tm,),
        in_specs=[row(D_MODEL)] + [row(RW_DIM)] * 6 + [row(SSD_DIM), row(SSD_DIM)] + [full(a) for a in consts],
        out_specs=row(D_MODEL),
        out_shape=jax.ShapeDtypeStruct((n, D_MODEL), F32),
        compiler_params=_params("parallel"), name="mix0_out",
    )(x2d, o_f, o_b, bonus, g, y_f, y_b, xbc, z, *consts)


def _hgrn_lower_bound(lb_ref, layer):
    x = lb_ref[...]
    m = jnp.max(x, axis=0, keepdims=True)
    e = jnp.exp(x - m)
    s = e / jnp.sum(e, axis=0, keepdims=True)
    lb = jnp.zeros_like(s[0:1, :])
    for i in range(1, layer + 1):
        lb = lb + s[i:i + 1, :]
    return lb


def _hgrn_in_kernel(x_ref, g_ref, lb_ref, wq_ref, wff_ref, wfb_ref, wi_ref, wg_ref,
                    q_out, lff_out, lfb_out, kgf_out, kgb_out, i_out, g_out, *, layer):
    xn = _rms_normed(x_ref[...], g_ref[...]).astype(BF16)
    lb = _hgrn_lower_bound(lb_ref, layer)
    dot = functools.partial(jnp.dot, preferred_element_type=F32)
    gate_logits = [dot(xn, wff_ref[...]), dot(xn, wfb_ref[...])]
    q_out[...] = dot(xn, wq_ref[...]).astype(q_out.dtype)
    for logits, lf_out, kg_out in zip(gate_logits, (lff_out, lfb_out), (kgf_out, kgb_out)):
        f = lb + (1.0 - lb) * _sigmoid(logits)
        lf_out[...] = jnp.log(f)
        kg_out[...] = (1.0 - f).astype(kg_out.dtype)
    i_out[...] = dot(xn, wi_ref[...]).astype(i_out.dtype)
    g_out[...] = dot(xn, wg_ref[...]).astype(g_out.dtype)


def _hgrn_in(x2d, gain, hg_lb, weights, layer):
    n, d = x2d.shape
    tm = _tile(n, 512)
    row = pl.BlockSpec((tm, d), lambda i: (i, 0))
    in_specs = [row, pl.BlockSpec((1, d), lambda i: (0, 0)), pl.BlockSpec(hg_lb.shape, lambda i: (0, 0))]
    in_specs += [pl.BlockSpec(w.shape, lambda i: (0, 0)) for w in weights]
    dtypes = (ACT, F32, F32, ACT, ACT, ACT, ACT)
    return pl.pallas_call(
        functools.partial(_hgrn_in_kernel, layer=layer),
        grid=(n // tm,), in_specs=in_specs, out_specs=[row] * len(dtypes),
        out_shape=[jax.ShapeDtypeStruct((n, d), dt) for dt in dtypes],
        compiler_params=_params("parallel"), name="hgrn_in",
    )(x2d, gain.reshape(1, d), hg_lb, *weights)


def _hgrn_chunk(q, lf, kg, vv, state_t, reverse):
    chunk = q.shape[0]
    n_sub = chunk // HG_SUB
    incl, _ = _order_masks(chunk, reverse)
    end = 0 if reverse else chunk - 1
    s_idx = lax.broadcasted_iota(jnp.int32, (HG_SUB, 1), 0)

    bcum = _mm_exact_lhs(incl.astype(BF16), lf)
    b_end = bcum[end:end + 1, :]
    carried = _mm_nt(q * jnp.exp(bcum), state_t)

    order = list(range(n_sub - 1, -1, -1)) if reverse else list(range(n_sub))
    outs = [None] * n_sub
    for pos, sb in enumerate(order):
        rows = slice(sb * HG_SUB, (sb + 1) * HG_SUB)
        q_s, k_s, b_s, v_s = q[rows], kg[rows], bcum[rows], vv[rows]
        acc = carried[rows]
        if pos > 0:
            prev_sb = order[pos - 1]
            edge = prev_sb * HG_SUB if reverse else prev_sb * HG_SUB + HG_SUB - 1
            b_edge = bcum[edge:edge + 1, :]
            src = slice((sb + 1) * HG_SUB, chunk) if reverse else slice(0, sb * HG_SUB)
            q_hat = q_s * jnp.exp(b_s - b_edge)
            k_hat = kg[src] * jnp.exp(b_edge - bcum[src])
            acc = acc + _mm(_mm_nt(q_hat, k_hat), vv[src])
        rows_out = []
        for l in range(HG_SUB):
            w = jnp.exp(jnp.minimum(b_s[l:l + 1, :] - b_s, 0.0)) * k_s * q_s[l:l + 1, :]
            att = jnp.sum(w, axis=-1, keepdims=True)
            att = jnp.where((s_idx >= l) if reverse else (s_idx <= l), att, 0.0)
            rows_out.append(jnp.sum(att * v_s, axis=0, keepdims=True))
        outs[sb] = acc + jnp.concatenate(rows_out, axis=0)

    k_tail = kg * jnp.exp(b_end - bcum)
    new_state_t = state_t * jnp.exp(b_end) + _mm_tn(vv, k_tail)
    return jnp.concatenate(outs, axis=0), new_state_t


def _hgrn_scan_kernel(q_ref, lf_ref, kg_ref, i_ref, lb_ref, o_ref, s_ref, hq_ref, hf_ref, hk_ref, hv_ref, ho_ref, *,
                      reverse, layer):
    @pl.when(pl.program_id(1) == 0)
    def _():
        s_ref[...] = jnp.zeros_like(s_ref)

    sub_rows = HG_BLOCK_CHUNKS * HG_CHUNK
    n_sub = q_ref.shape[0] // sub_rows

    def body(i, carry):
        blk = (n_sub - 1 - i) if reverse else i
        _hgrn_scan_block(q_ref, lf_ref, kg_ref, i_ref, lb_ref, o_ref, s_ref, hq_ref, hf_ref, hk_ref, hv_ref, ho_ref,
                         pl.multiple_of(blk * sub_rows, sub_rows), reverse, layer)
        return carry

    lax.fori_loop(0, n_sub, body, 0)


def _hgrn_scan_block(q_ref, lf_ref, kg_ref, i_ref, lb_ref, o_ref, s_ref, hq_ref, hf_ref, hk_ref, hv_ref, ho_ref,
                     base, reverse, layer):
    chunk = HG_CHUNK
    n_chunks = HG_BLOCK_CHUNKS
    incl, _ = _order_masks(chunk, reverse)
    incl_bf = incl.astype(BF16)
    end = 0 if reverse else chunk - 1
    chunk_order = list(range(n_chunks - 1, -1, -1)) if reverse else list(range(n_chunks))

    def head_cols(h):
        return slice(h * HG_KEY_DIM, (h + 1) * HG_KEY_DIM)

    pre = []
    worst = None
    for ci in chunk_order:
        rows = pl.ds(base + ci * chunk, chunk)
        bcum = _mm_exact_lhs(incl_bf, lf_ref[rows, :])
        b_end = bcum[end:end + 1, :]
        worst = b_end if worst is None else jnp.minimum(worst, b_end)
        pre.append((ci, rows, bcum, b_end))
    lb_min = jnp.min(_hgrn_lower_bound(lb_ref, layer))
    flag = lax.cond(lb_min >= math.exp(-HG_SAFE_LOG_RANGE / HG_CHUNK),
                    lambda: jnp.int32(1),
                    lambda: (jnp.min(worst) >= -HG_SAFE_LOG_RANGE).astype(jnp.int32))
    safe = flag == 1

    @pl.when(safe)
    def _():
        items = []
        for ci, rows, bcum, b_end in pre:
            kg = kg_ref[rows, :].astype(F32)
            e_neg = jnp.exp(-bcum)
            e_end = jnp.exp(b_end)
            qt_all = q_ref[rows, :].astype(F32) * jnp.exp(bcum)
            kt_all = kg * e_neg
            ktail_all = kt_all * e_end
            v_all = i_ref[rows, :].astype(F32)
            for h in range(HG_HEADS):
                c = head_cols(h)
                items.append(dict(rows=rows, h=h, qt=qt_all[:, c], kt=kt_all[:, c], ktail=ktail_all[:, c],
                                  v=v_all[:, c], e_end=e_end[:, c]))
        for it in items:
            it["attn"] = jnp.where(incl, _mm_nt(it["qt"], it["kt"]), 0.0)
        for it in items:
            it["local"] = _mm(it["attn"], it["v"])
            it["inject"] = _mm_tn(it["v"], it["ktail"])
        states = [s_ref[h] for h in range(HG_HEADS)]
        for it in items:
            h = it["h"]
            o_ref[it["rows"], head_cols(h)] = (it["local"] + _mm_nt(it["qt"], states[h])).astype(o_ref.dtype)
            states[h] = states[h] * it["e_end"] + it["inject"]
        for h in range(HG_HEADS):
            s_ref[h] = states[h]

    @pl.when(jnp.logical_not(safe))
    def _():
        def head_body(h, carry):
            out, new_state = _hgrn_chunk(hq_ref[h], hf_ref[h], hk_ref[h], hv_ref[h], s_ref[h], reverse)
            ho_ref[h] = out
            s_ref[h] = new_state
            return carry

        def chunk_body(i, carry):
            ci = (n_chunks - 1 - i) if reverse else i
            rows = pl.ds(pl.multiple_of(base + ci * chunk, chunk), chunk)
            for h in range(HG_HEADS):
                hq_ref[h] = q_ref[rows, head_cols(h)].astype(F32)
                hf_ref[h] = lf_ref[rows, head_cols(h)]
                hk_ref[h] = kg_ref[rows, head_cols(h)].astype(F32)
                hv_ref[h] = i_ref[rows, head_cols(h)].astype(F32)
            lax.fori_loop(0, HG_HEADS, head_body, 0)
            for h in range(HG_HEADS):
                o_ref[rows, head_cols(h)] = ho_ref[h].astype(o_ref.dtype)
            return carry

        lax.fori_loop(0, n_chunks, chunk_body, 0)


def _hgrn_scan(q, lf, kg, i_val, hg_lb, nb, t_len, reverse, layer):
    n = q.shape[0]
    tb = _scan_rows(t_len, HG_BLOCK_CHUNKS * HG_CHUNK)
    n_outer = t_len // tb

    def idx(b, c):
        return (b * n_outer + (n_outer - 1 - c if reverse else c), 0)

    spec = pl.BlockSpec((tb, HG_KEY_WIDTH), idx)
    head_buf = pltpu.VMEM((HG_HEADS, HG_CHUNK, HG_KEY_DIM), F32)
    return pl.pallas_call(
        functools.partial(_hgrn_scan_kernel, reverse=reverse, layer=layer),
        grid=(nb, n_outer),
        in_specs=[spec, spec, spec, spec, pl.BlockSpec(hg_lb.shape, lambda b, c: (0, 0))],
        out_specs=spec,
        out_shape=jax.ShapeDtypeStruct((n, HG_VAL_WIDTH), ACT),
        scratch_shapes=[pltpu.VMEM((HG_HEADS, HG_VAL_DIM, HG_KEY_DIM), F32)] + [head_buf] * 5,
        compiler_params=_params("parallel", "arbitrary"),
        name="hgrn_scan_bwd" if reverse else "hgrn_scan_fwd",
    )(q, lf, kg, i_val, hg_lb)


def _hgrn_out_kernel(x_ref, of_ref, ob_ref, g_ref, nw_ref, wo_ref, out_ref):
    o = of_ref[...].astype(F32) + ob_ref[...].astype(F32)
    parts = []
    for h in range(HG_HEADS):
        oh = o[:, h * HG_VAL_DIM:(h + 1) * HG_VAL_DIM]
        parts.append(oh * lax.rsqrt(jnp.mean(oh * oh, axis=-1, keepdims=True) + NORM_EPS))
    o = jnp.concatenate(parts, axis=-1) * nw_ref[...] * _silu(g_ref[...].astype(F32))
    out_ref[...] = x_ref[...] + _mm(o, wo_ref[...])


def _hgrn_out(x2d, o_f, o_b, g, norm_w, w_out):
    n = x2d.shape[0]
    tm = _tile(n, 512)
    row = pl.BlockSpec((tm, D_MODEL), lambda i: (i, 0))
    return pl.pallas_call(
        _hgrn_out_kernel,
        grid=(n // tm,),
        in_specs=[row] * 4 + [pl.BlockSpec((1, HG_VAL_WIDTH), lambda i: (0, 0)),
                              pl.BlockSpec(w_out.shape, lambda i: (0, 0))],
        out_specs=row,
        out_shape=jax.ShapeDtypeStruct((n, D_MODEL), F32),
        compiler_params=_params("parallel"), name="hgrn_out",
    )(x2d, o_f, o_b, g, norm_w, w_out)


def _xattn_kernel(x_ref, g_ref, wq_ref, kv_ref, wo_ref, out_ref):
    x = x_ref[...]
    q = _mm(_rms_normed(x, g_ref[...]), wq_ref[...])
    scale = XA_HEAD_DIM ** -0.5
    scores = [_mm_nt(q[:, h * XA_HEAD_DIM:(h + 1) * XA_HEAD_DIM], kv_ref[:, h * XA_HEAD_DIM:(h + 1) * XA_HEAD_DIM])
              * scale for h in range(XA_HEADS)]
    probs = []
    for s in scores:
        p = jnp.exp(s - jnp.max(s, axis=-1, keepdims=True))
        probs.append(p * (1.0 / jnp.sum(p, axis=-1, keepdims=True)))
    heads = [_mm(p, kv_ref[:, D_MODEL + h * XA_HEAD_DIM:D_MODEL + (h + 1) * XA_HEAD_DIM])
             for h, p in enumerate(probs)]
    out_ref[...] = x + _mm(jnp.concatenate(heads, axis=-1), wo_ref[...])


def _xattn(x2d, gain, wq, kv, wo, nb, t_len, mem_len):
    n = x2d.shape[0]
    tm = _tile(t_len, 512)
    per_seq = t_len // tm
    row = pl.BlockSpec((tm, D_MODEL), lambda b, j: (b * per_seq + j, 0))
    return pl.pallas_call(
        _xattn_kernel,
        grid=(nb, per_seq),
        in_specs=[row, pl.BlockSpec((1, D_MODEL), lambda b, j: (0, 0)),
                  pl.BlockSpec(wq.shape, lambda b, j: (0, 0)),
                  pl.BlockSpec((mem_len, 2 * D_MODEL), lambda b, j: (b, 0)),
                  pl.BlockSpec(wo.shape, lambda b, j: (0, 0))],
        out_specs=row,
        out_shape=jax.ShapeDtypeStruct((n, D_MODEL), F32),
        compiler_params=_params("parallel", "parallel"), name="xattn",
    )(x2d, gain.reshape(1, D_MODEL), wq, kv, wo)


def _ffn_kernel(x_ref, g_ref, wi_ref, wo_ref, fg_ref, out_ref, *, final_norm):
    x = x_ref[...]
    gu = _mm(_rms_normed(x, g_ref[...]), wi_ref[...])
    act = _silu(gu[:, :FFN_DIM]) * gu[:, FFN_DIM:]
    y = x + _mm(act, wo_ref[...])
    if final_norm:
        y = _rms_normed(y, fg_ref[...])
    out_ref[...] = y


def _ffn(x2d, gain, w_in, w_out, final_gain, final_norm):
    n = x2d.shape[0]
    tm = _tile(n, 512)
    row = pl.BlockSpec((tm, D_MODEL), lambda i: (i, 0))
    vec = pl.BlockSpec((1, D_MODEL), lambda i: (0, 0))
    resident = pl.Buffered(1)
    return pl.pallas_call(
        functools.partial(_ffn_kernel, final_norm=final_norm),
        grid=(n // tm,),
        in_specs=[row, vec, pl.BlockSpec(w_in.shape, lambda i: (0, 0), pipeline_mode=resident),
                  pl.BlockSpec(w_out.shape, lambda i: (0, 0), pipeline_mode=resident), vec],
        out_specs=row,
        out_shape=jax.ShapeDtypeStruct((n, D_MODEL), F32),
        compiler_params=_params("parallel"), name="ffn",
    )(x2d, gain.reshape(1, D_MODEL), w_in, w_out, final_gain.reshape(1, D_MODEL))


def _pad_lanes(a, width=LANES):
    return jnp.pad(a, [(0, 0)] * (a.ndim - 1) + [(0, width - a.shape[-1])])


def _rwkv_ssd_layer(x2d, nb, t_len, norm_g, w_in, w_out, rw_mu, rw_w0, rw_w2, rw_a0, rw_a2, rw_g2, rw_k_k, rw_k_a,
                    rw_r_k, rw_gn_w, rw_gn_b, conv_w, conv_b, dt_bias, a_log, d_skip, ssd_norm_w):
    o_lora = 3 * RW_DIM
    o_z = RW_PROJ
    o_xbc = o_z + SSD_DIM
    o_dt = o_xbc + SSD_CONV_DIM
    wb = w_in.astype(BF16)
    weights = [wb[:, :o_lora], wb[:, o_lora:o_z], wb[:, o_z:o_xbc], wb[:, o_xbc:o_dt], _pad_lanes(wb[:, o_dt:])]
    head_id = jnp.arange(RW_DIM) // RW_HEAD_DIM
    blockdiag = (head_id[:, None] == head_id[None, :]).astype(BF16)
    row = lambda a: a.reshape(1, -1)
    rwkv_consts = (row(rw_mu[:o_lora]), row(rw_mu[o_lora:]), row(rw_a0), rw_a2.astype(BF16), rw_g2.astype(BF16),
                   row(rw_k_k), row(rw_k_a), rw_w0, rw_w2.astype(BF16), row(rw_r_k), blockdiag)
    ssd_consts = (conv_w, row(conv_b), _pad_lanes(dt_bias), _pad_lanes(a_log))
    (r, k, v, kk, kka, lw_f, lw_b, g, bonus, p_z, xbc, dt_f, dt_b, la_f, la_b) = _mix0_in(
        x2d, nb, t_len, norm_g, weights, rwkv_consts, ssd_consts)
    o_f = _rwkv_scan(r, k, v, kk, kka, lw_f, nb, t_len, reverse=False)
    o_b = _rwkv_scan(r, k, v, kk, kka, lw_b, nb, t_len, reverse=True)
    y_f = _ssd_scan(xbc, dt_f, la_f, nb, t_len, reverse=False)
    y_b = _ssd_scan(xbc, dt_b, la_b, nb, t_len, reverse=True)

    wo = w_out.astype(BF16)
    d_skip_lanes = jnp.repeat(d_skip, SSD_HEAD_DIM).reshape(1, SSD_DIM)
    return _mix0_out(x2d, o_f, o_b, bonus, g, y_f, y_b, xbc, p_z, row(rw_gn_w), row(rw_gn_b), blockdiag,
                     d_skip_lanes, row(ssd_norm_w), wo[:RW_DIM], wo[RW_DIM:])


def _hgrn_layer(x2d, nb, t_len, norm_g, w_in, w_out, norm_w, hg_lb, layer):
    kw, vw = HG_KEY_WIDTH, HG_VAL_WIDTH
    wb = w_in.astype(BF16)
    weights = [wb[:, :kw], wb[:, kw:2 * kw], wb[:, 2 * kw:3 * kw], wb[:, 3 * kw:3 * kw + vw], wb[:, 3 * kw + vw:]]
    q, lf_f, lf_b, kg_f, kg_b, i_val, g = _hgrn_in(x2d, norm_g, hg_lb, weights, layer)
    o_f = _hgrn_scan(q, lf_f, kg_f, i_val, hg_lb, nb, t_len, reverse=False, layer=layer)
    o_b = _hgrn_scan(q, lf_b, kg_b, i_val, hg_lb, nb, t_len, reverse=True, layer=layer)
    return _hgrn_out(x2d, o_f, o_b, g, norm_w.reshape(1, vw), w_out.astype(BF16))


def kernel(x, mem, mix_norm, ab_w_in, ab_w_out, rw_mu, rw_w0, rw_w2, rw_a0, rw_a2, rw_g2, rw_k_k, rw_k_a, rw_r_k, rw_gn_w, rw_gn_b, ssd_conv_w, ssd_conv_b, ssd_dt_bias, ssd_a_log, ssd_d, ssd_norm_w, hg_w_in, hg_w_out, hg_norm_w, hg_lb, xa_norm, mem_norm, xa_wq, xa_wkv, xa_wo, ffn_norm, ffn_w_in, ffn_w_out, final_norm):
    nb, t_len, d = x.shape
    mem_len = mem.shape[1]
    depth = mix_norm.shape[0]
    x2d = x.reshape(nb * t_len, d)
    mem2d = mem.reshape(nb * mem_len, d)
    for layer in range(depth):
        if layer % 2 == 0:
            e = layer // 2
            x2d = _rwkv_ssd_layer(x2d, nb, t_len, mix_norm[layer], ab_w_in[e], ab_w_out[e], rw_mu[e], rw_w0[e],
                                  rw_w2[e], rw_a0[e], rw_a2[e], rw_g2[e], rw_k_k[e], rw_k_a[e],
                                  rw_r_k[e].reshape(-1), rw_gn_w[e], rw_gn_b[e], ssd_conv_w[e], ssd_conv_b[e],
                                  ssd_dt_bias[e], ssd_a_log[e], ssd_d[e], ssd_norm_w[e])
        else:
            o = layer // 2
            x2d = _hgrn_layer(x2d, nb, t_len, mix_norm[layer], hg_w_in[o], hg_w_out[o], hg_norm_w[o], hg_lb, layer)
        (kv,) = _norm_matmul(mem2d, mem_norm[layer], [xa_wkv[layer].astype(BF16)], [BF16])
        x2d = _xattn(x2d, xa_norm[layer], xa_wq[layer].astype(BF16), kv, xa_wo[layer].astype(BF16),
                     nb, t_len, mem_len)
        x2d = _ffn(x2d, ffn_norm[layer], ffn_w_in[layer].astype(BF16), ffn_w_out[layer].astype(BF16),
                   final_norm, final_norm=(layer == depth - 1))
    return x2d.reshape(nb, t_len, d)
```

```python
import functools
import math

import jax
import jax.numpy as jnp
from jax import lax
from jax.experimental import pallas as pl
from jax.experimental.pallas import tpu as pltpu

F32 = jnp.float32
BF16 = jnp.bfloat16
ACT = BF16

D_MODEL = 1024
NORM_EPS = 1e-6

RW_HEAD_DIM = 64
RW_DIM = D_MODEL // 2
RW_HEADS = RW_DIM // RW_HEAD_DIM
RW_DECAY_RANK = 64
RW_A_RANK = 64
RW_GATE_RANK = 128
RW_GN_EPS = 64e-5
RW_LORA = RW_DECAY_RANK + RW_A_RANK + RW_GATE_RANK
RW_PROJ = 3 * RW_DIM + RW_LORA
RW_CHUNK = 64
RW_BLOCK_CHUNKS = 4
RW_HEAD_PAIRS = RW_HEADS // 2
assert RW_CHUNK == RW_HEAD_DIM and 2 * RW_HEAD_DIM == 128

SSD_HEAD_DIM = 64
SSD_DIM = D_MODEL // 2
SSD_HEADS = SSD_DIM // SSD_HEAD_DIM
SSD_GROUPS = 2
SSD_STATE = 128
SSD_CONV_DIM = SSD_DIM + 2 * SSD_GROUPS * SSD_STATE
SSD_CHUNK = 128
SSD_BLOCK_CHUNKS = 4
SSD_GROUP_HEADS = SSD_HEADS // SSD_GROUPS

HG_HEADS = 8
HG_KEY_DIM = 128
HG_VAL_DIM = D_MODEL // HG_HEADS
HG_KEY_WIDTH = HG_HEADS * HG_KEY_DIM
HG_VAL_WIDTH = HG_HEADS * HG_VAL_DIM
HG_CHUNK = 64
HG_SUB = 16
HG_BLOCK_CHUNKS = 4
HG_SAFE_LOG_RANGE = 64.0
assert HG_KEY_DIM == HG_VAL_DIM

XA_HEADS = 4
XA_HEAD_DIM = D_MODEL // XA_HEADS
FFN_DIM = ((8 * D_MODEL + 3 * 256 - 1) // (3 * 256)) * 256

SUBLANES = 8
HALO_ROWS = 16
SHIFT_ROWS = 128
LANES = 128
VMEM_LIMIT_BYTES = 56 * 1024 * 1024
SCAN_STEP_ROWS = 1024
NEG_BIG = -1e30


def _params(*semantics):
    return pltpu.CompilerParams(dimension_semantics=semantics, vmem_limit_bytes=VMEM_LIMIT_BYTES)


def _tile(n, pref):
    t = min(n, pref)
    while n % t or t % SUBLANES:
        t -= 1
    return t


def _scan_rows(t_len, sub_rows):
    assert t_len % sub_rows == 0
    rows = sub_rows
    while rows * 2 <= SCAN_STEP_ROWS and t_len % (rows * 2) == 0:
        rows *= 2
    return rows


def _mm(a, b):
    return jnp.dot(a.astype(BF16), b.astype(BF16), preferred_element_type=F32)


def _mm_nt(a, b):
    return lax.dot_general(a.astype(BF16), b.astype(BF16), (((1,), (1,)), ((), ())),
                           preferred_element_type=F32)


def _mm_tn(a, b):
    return lax.dot_general(a.astype(BF16), b.astype(BF16), (((0,), (0,)), ((), ())),
                           preferred_element_type=F32)


def _split3(x):
    hi = x.astype(BF16)
    r1 = x - hi.astype(F32)
    mid = r1.astype(BF16)
    lo = (r1 - mid.astype(F32)).astype(BF16)
    return hi, mid, lo


def _mm_exact_lhs(m_bf16, x):
    hi, mid, lo = _split3(x)
    dot = functools.partial(jnp.dot, preferred_element_type=F32)
    return dot(m_bf16, hi) + dot(m_bf16, mid) + dot(m_bf16, lo)


def _mm_exact_rhs_many(xs, m_bf16, terms=3):
    rows = xs[0].shape[0]
    pieces = [piece for x in xs for piece in _split3(x)[:terms]]
    out = jnp.dot(jnp.concatenate(pieces, axis=0), m_bf16, preferred_element_type=F32)
    results = []
    for i in range(len(xs)):
        total = out[i * terms * rows:(i * terms + 1) * rows]
        for t in range(1, terms):
            total = total + out[(i * terms + t) * rows:(i * terms + t + 1) * rows]
        results.append(total)
    return results


def _mm_exact_rhs(x, m_bf16, terms=3):
    return _mm_exact_rhs_many([x], m_bf16, terms)[0]


def _rms_normed(x, gain):
    ms = jnp.mean(x * x, axis=-1, keepdims=True)
    return x * lax.rsqrt(ms + NORM_EPS) * gain


def _sigmoid(x):
    return 0.5 * jnp.tanh(0.5 * x) + 0.5


def _silu(x):
    h = 0.5 * x
    return h * jnp.tanh(h) + h


def _softplus(x):
    return jnp.maximum(x, 0.0) + jnp.log(1.0 + jnp.exp(-jnp.abs(x)))


def _order_masks(n, reverse):
    row = lax.broadcasted_iota(jnp.int32, (n, n), 0)
    col = lax.broadcasted_iota(jnp.int32, (n, n), 1)
    if reverse:
        return col >= row, col > row
    return col <= row, col < row


def _norm_matmul_kernel(x_ref, g_ref, *refs, n_out):
    xn = _rms_normed(x_ref[...], g_ref[...]).astype(BF16)
    for w_ref, o_ref in zip(refs[:n_out], refs[n_out:]):
        o_ref[...] = jnp.dot(xn, w_ref[...], preferred_element_type=F32).astype(o_ref.dtype)


def _norm_matmul(x2d, gain, weights, out_dtypes, tm_pref=512):
    n, d = x2d.shape
    tm = _tile(n, tm_pref)
    in_specs = [pl.BlockSpec((tm, d), lambda i: (i, 0)), pl.BlockSpec((1, d), lambda i: (0, 0))]
    in_specs += [pl.BlockSpec(w.shape, lambda i: (0, 0)) for w in weights]
    out_specs = [pl.BlockSpec((tm, w.shape[1]), lambda i: (i, 0)) for w in weights]
    out_shape = [jax.ShapeDtypeStruct((n, w.shape[1]), dt) for w, dt in zip(weights, out_dtypes)]
    return pl.pallas_call(
        functools.partial(_norm_matmul_kernel, n_out=len(weights)),
        grid=(n // tm,), in_specs=in_specs, out_specs=out_specs, out_shape=out_shape,
        compiler_params=_params("parallel"), name="norm_matmul",
    )(x2d, gain.reshape(1, d), *weights)


def _neighbours(p, prev_blk, next_blk, first, last):
    tt = p.shape[0]
    ridx = lax.broadcasted_iota(jnp.int32, p.shape, 0)
    prev_row = jnp.where(first, 0.0, prev_blk[HALO_ROWS - 1:HALO_ROWS, :].astype(F32))
    next_row = jnp.where(last, 0.0, next_blk[0:1, :].astype(F32))
    prev = jnp.where(ridx == 0, prev_row, pltpu.roll(p, 1, 0))
    nxt = jnp.where(ridx == tt - 1, next_row, pltpu.roll(p, tt - 1, 0))
    return prev, nxt


def _shift_matrix(tt, offsets, weight):
    t = jnp.arange(tt)[:, None]
    c = jnp.arange(tt + 2 * HALO_ROWS)[None, :]
    hit = functools.reduce(jnp.logical_or, [c == HALO_ROWS + t + d for d in offsets])
    return jnp.where(hit, weight, 0.0).astype(BF16)


def _lane_group_sum(x, width):
    assert width % LANES == 0 and x.shape[1] % width == 0
    ones = jnp.ones((LANES, LANES), BF16)
    groups = []
    for c in range(0, x.shape[1], width):
        total = _mm(x[:, c:c + LANES], ones)
        for c2 in range(c + LANES, c + width, LANES):
            total = total + _mm(x[:, c2:c2 + LANES], ones)
        groups += [total] * (width // LANES)
    return jnp.concatenate(groups, axis=1)


def _head_sum(x, pair_ones_bf16):
    blocks = [_mm(x[:, c:c + LANES], pair_ones_bf16) for c in range(0, x.shape[1], LANES)]
    return jnp.concatenate(blocks, axis=1)


def _mix0_project(x_ref, xp_ref, xn_ref, gain_ref, wrkv_ref, wlora_ref, wz_ref, wxbc_ref, wdt_ref, first, last,
                  prkv_scr, plora_scr, u_scr, z_scr, dt_scr):
    tt = x_ref.shape[0]
    dot = functools.partial(jnp.dot, preferred_element_type=F32)
    x_ext = jnp.concatenate([jnp.where(first, 0.0, xp_ref[...]), x_ref[...], jnp.where(last, 0.0, xn_ref[...])], axis=0)
    xn_ext = _rms_normed(x_ext, gain_ref[...]).astype(BF16)
    xn = xn_ext[HALO_ROWS:HALO_ROWS + tt]

    def column_block(lhs, w_ref, scr, lo, hi):
        def piece():
            scr[:, lo:hi] = dot(lhs, w_ref[:, lo:hi]).astype(scr.dtype)
        return piece

    pieces = [column_block(xn_ext, wrkv_ref, prkv_scr, c, c + RW_DIM) for c in range(0, 3 * RW_DIM, RW_DIM)]
    pieces.append(column_block(xn_ext, wlora_ref, plora_scr, 0, RW_LORA))
    pieces += [column_block(xn_ext, wxbc_ref, u_scr, c, c + SSD_DIM) for c in range(0, SSD_CONV_DIM, SSD_DIM)]
    pieces.append(column_block(xn, wz_ref, z_scr, 0, SSD_DIM))
    pieces.append(column_block(xn, wdt_ref, dt_scr, 0, LANES))
    return pieces


def _mix0_operands(prkv_scr, plora_scr, u_scr, z_scr, dt_scr,
                   savg_ref, mu_ref, mul_ref, a0_ref, a2_ref, g2_ref, kk_ref, ka_ref, w0_ref, w2_ref, rk_ref, bd_ref,
                   cw_ref, cb_ref, dtb_ref, alog_ref,
                   r_out, k_out, v_out, kkn_out, kka_out, lwf_out, lwb_out, g_out, bonus_out,
                   z_out, xbc_out, dtf_out, dtb_out, laf_out, lab_out):
    tt = z_scr.shape[0]
    dot = functools.partial(jnp.dot, preferred_element_type=F32)

    def token_shift(p_scr, m_ref, lo_col, hi_col):
        x = p_scr[HALO_ROWS:HALO_ROWS + tt, lo_col:hi_col].astype(F32)
        avg = jnp.concatenate(
            [dot(savg_ref[...], p_scr[t0:t0 + SHIFT_ROWS + 2 * HALO_ROWS, lo_col:hi_col])
             for t0 in range(0, tt, SHIFT_ROWS)], axis=0)
        return x + m_ref[:, lo_col:hi_col] * (avg - x)

    r = token_shift(prkv_scr, mu_ref, 0, RW_DIM)
    k = token_shift(prkv_scr, mu_ref, RW_DIM, 2 * RW_DIM)
    v = token_shift(prkv_scr, mu_ref, 2 * RW_DIM, 3 * RW_DIM)
    lo = token_shift(plora_scr, mul_ref, 0, RW_LORA)
    wd = lo[:, :RW_DECAY_RANK]
    ad = lo[:, RW_DECAY_RANK:RW_DECAY_RANK + RW_A_RANK]
    gd = lo[:, RW_DECAY_RANK + RW_A_RANK:]
    bd = bd_ref[...]

    a = _sigmoid(a0_ref[...] + _mm(ad, a2_ref[...]))
    g = _mm(_sigmoid(gd), g2_ref[...])
    kk = k * kk_ref[...]
    kk = kk / jnp.maximum(jnp.sqrt(_head_sum(kk * kk, bd)), 1e-12)
    k = k * (1.0 + (a - 1.0) * ka_ref[...])
    w_lora = jnp.tanh(wd)

    def log_decay(d):
        wl = w0_ref[d:d + 1, :] + _mm(w_lora, w2_ref[d])
        return -math.exp(-0.5) * _sigmoid(wl)

    r_out[...] = r.astype(r_out.dtype)
    k_out[...] = k.astype(k_out.dtype)
    v_out[...] = v.astype(v_out.dtype)
    kkn_out[...] = kk.astype(kkn_out.dtype)
    kka_out[...] = (kk * a).astype(kka_out.dtype)
    lwf_out[...] = log_decay(0)
    lwb_out[...] = log_decay(1)
    g_out[...] = g.astype(g_out.dtype)
    bonus_out[...] = (_head_sum(r * k * rk_ref[...], bd) * v).astype(bonus_out.dtype)

    z_out[...] = z_scr[...]
    u_ext = u_scr[...]
    u = u_ext[HALO_ROWS:HALO_ROWS + tt].astype(F32)
    prev, nxt = _neighbours(u, u_ext[:HALO_ROWS], u_ext[HALO_ROWS + tt:], False, False)
    conv = prev * cw_ref[0:1, :] + u * cw_ref[1:2, :] + nxt * cw_ref[2:3, :] + cb_ref[...]
    xbc_out[...] = _silu(conv).astype(xbc_out.dtype)
    dt_raw = dt_scr[...]
    dt_f = _softplus(dt_raw + dtb_ref[0:1, :])
    dt_b = _softplus(dt_raw + dtb_ref[1:2, :])
    dtf_out[...] = dt_f
    dtb_out[...] = dt_b
    laf_out[...] = dt_f * (-jnp.exp(alog_ref[0:1, :]))
    lab_out[...] = dt_b * (-jnp.exp(alog_ref[1:2, :]))


MIX0_N_PROJ_REFS = 9
MIX0_N_CONSTS = 16
MIX0_N_OUTS = 15
MIX0_N_STAGED = 5


def _mix0_in_kernel(*refs, per_seq):
    proj_refs = refs[:MIX0_N_PROJ_REFS]
    const_refs = refs[MIX0_N_PROJ_REFS:MIX0_N_PROJ_REFS + MIX0_N_CONSTS]
    out_refs = refs[MIX0_N_PROJ_REFS + MIX0_N_CONSTS:MIX0_N_PROJ_REFS + MIX0_N_CONSTS + MIX0_N_OUTS]
    staged = refs[MIX0_N_PROJ_REFS + MIX0_N_CONSTS + MIX0_N_OUTS:]
    pos = lax.rem(pl.program_id(0), per_seq)
    for piece in _mix0_project(*proj_refs, pos == 0, pos == per_seq - 1, *staged):
        piece()
    _mix0_operands(*staged, *const_refs, *out_refs)


def _mix0_in(x2d, nb, t_len, gain, weights, rwkv_consts, ssd_consts):
    n, d = x2d.shape
    tt = _tile(t_len, 512)
    assert tt % HALO_ROWS == 0
    per_seq = t_len // tt
    n_tiles = n // tt
    blocks_per_tile = tt // HALO_ROWS
    n_blocks = n // HALO_ROWS

    def full(a):
        nd = a.ndim
        return pl.BlockSpec(a.shape, lambda s: (0,) * nd)

    x_specs = [pl.BlockSpec((tt, d), lambda s: (s, 0)),
               pl.BlockSpec((HALO_ROWS, d), lambda s: (jnp.maximum(s * blocks_per_tile - 1, 0), 0)),
               pl.BlockSpec((HALO_ROWS, d), lambda s: (jnp.minimum((s + 1) * blocks_per_tile, n_blocks - 1), 0))]
    proj_consts = (gain.reshape(1, d),) + tuple(weights)
    assert tt % SHIFT_ROWS == 0
    consts = (_shift_matrix(SHIFT_ROWS, (-1, 1), 0.5),) + tuple(rwkv_consts) + tuple(ssd_consts)
    assert 3 + len(proj_consts) == MIX0_N_PROJ_REFS and len(consts) == MIX0_N_CONSTS
    wide = [(RW_DIM, ACT)] * 5 + [(RW_DIM, F32)] * 2 + [(RW_DIM, ACT)] * 2 + [(SSD_DIM, ACT), (SSD_CONV_DIM, ACT)] \
        + [(LANES, F32)] * 4
    assert len(wide) == MIX0_N_OUTS
    ext = tt + 2 * HALO_ROWS
    staged = [pltpu.VMEM((ext, 3 * RW_DIM), ACT), pltpu.VMEM((ext, RW_LORA), ACT), pltpu.VMEM((ext, SSD_CONV_DIM), ACT),
              pltpu.VMEM((tt, SSD_DIM), ACT), pltpu.VMEM((tt, LANES), F32)]
    assert len(staged) == MIX0_N_STAGED
    return pl.pallas_call(
        functools.partial(_mix0_in_kernel, per_seq=per_seq),
        grid=(n_tiles,),
        in_specs=x_specs + [full(a) for a in proj_consts + consts],
        out_specs=[pl.BlockSpec((tt, w), lambda s: (s, 0)) for w, _ in wide],
        out_shape=[jax.ShapeDtypeStruct((n, w), dt) for w, dt in wide],
        scratch_shapes=staged,
        compiler_params=_params("parallel"), name="mix0_in",
    )(x2d, x2d, x2d, *proj_consts, *consts)


def _pair_blockdiag(y):
    lane = lax.broadcasted_iota(jnp.int32, y.shape, 1)
    top = jnp.where(lane < RW_HEAD_DIM, y, 0.0).astype(BF16)
    bottom = jnp.where(lane >= RW_HEAD_DIM, y, 0.0).astype(BF16)
    return jnp.concatenate([top, bottom], axis=0)


def _pair_diag_blocks(full):
    lane = lax.broadcasted_iota(jnp.int32, (RW_HEAD_DIM, LANES), 1)
    return jnp.where(lane < RW_HEAD_DIM, full[:RW_HEAD_DIM], full[RW_HEAD_DIM:])


def _unit_lower_inverses(n_strict_list, eye):
    size = n_strict_list[0].shape[0]
    levels = int(math.log2(size)) - 1
    accs = [eye + n for n in n_strict_list]
    powers = [_mm(n, _pair_blockdiag(n)) for n in n_strict_list]
    for level in range(levels):
        power_bds = [_pair_blockdiag(p) for p in powers]
        if level == levels - 1:
            return [a + _mm(a, b) for a, b in zip(accs, power_bds)]
        both = [_mm(jnp.concatenate([a, p], axis=0), b) for a, p, b in zip(accs, powers, power_bds)]
        accs = [a + ab[:size] for a, ab in zip(accs, both)]
        powers = [ab[size:] for ab in both]


def _rwkv_scan_kernel(r_ref, k_ref, v_ref, kk_ref, kka_ref, lw_ref, o_ref, h_ref, *, reverse):
    @pl.when(pl.program_id(1) == 0)
    def _():
        h_ref[...] = jnp.zeros_like(h_ref)

    sub_rows = RW_BLOCK_CHUNKS * RW_CHUNK
    n_sub = r_ref.shape[0] // sub_rows

    def body(i, carry):
        blk = (n_sub - 1 - i) if reverse else i
        _rwkv_scan_block(r_ref, k_ref, v_ref, kk_ref, kka_ref, lw_ref, o_ref, h_ref,
                         pl.multiple_of(blk * sub_rows, sub_rows), reverse)
        return carry

    lax.fori_loop(0, n_sub, body, 0)


def _rwkv_scan_block(r_ref, k_ref, v_ref, kk_ref, kka_ref, lw_ref, o_ref, h_ref, base, reverse):
    chunk = RW_CHUNK
    n_chunks = RW_BLOCK_CHUNKS
    incl_bf = _order_masks(chunk, reverse)[0].astype(BF16)
    row = lax.broadcasted_iota(jnp.int32, (chunk, LANES), 0)
    pos = lax.broadcasted_iota(jnp.int32, (chunk, LANES), 1) % RW_HEAD_DIM
    incl, strict = (pos >= row, pos > row) if reverse else (pos <= row, pos < row)
    eye = (pos == row).astype(F32)
    end = 0 if reverse else chunk - 1
    chunk_order = list(range(n_chunks - 1, -1, -1)) if reverse else list(range(n_chunks))

    items = []
    for ci in chunk_order:
        rows = pl.ds(base + ci * chunk, chunk)
        lw = lw_ref[rows, :]
        cum = _mm_exact_lhs(incl_bf, lw)
        cum_end = cum[end:end + 1, :]
        e_neg = jnp.exp(-cum)
        e_tail = jnp.exp(cum_end - cum)
        p_end = jnp.exp(cum_end)
        k = k_ref[rows, :].astype(F32)
        kka = kka_ref[rows, :].astype(F32)
        at_all = -kk_ref[rows, :].astype(F32) * jnp.exp(cum - lw)
        bt_all = kka * e_neg
        kt_all = k * e_neg
        rt_all = r_ref[rows, :].astype(F32) * jnp.exp(cum)
        bh_all = kka * e_tail
        kh_all = k * e_tail
        v = v_ref[rows, :].astype(F32)
        for j in range(RW_HEAD_PAIRS):
            sl = slice(j * LANES, (j + 1) * LANES)
            items.append(dict(ci=ci, j=j, at=at_all[:, sl], bt=bt_all[:, sl], kt=kt_all[:, sl], rt=rt_all[:, sl],
                              bh=bh_all[:, sl], kh=kh_all[:, sl], v=v[:, sl], p_end=p_end[:, sl]))

    for it in items:
        lhs = jnp.concatenate([it["at"], it["rt"]], axis=0)
        ab_rb = _mm_nt(lhs, _pair_blockdiag(it["bt"]))
        ak_rk = _mm_nt(lhs, _pair_blockdiag(it["kt"]))
        it["a_ab"] = jnp.where(strict, ab_rb[:chunk], 0.0)
        it["a_rb"] = jnp.where(incl, ab_rb[chunk:], 0.0)
        it["a_ak"] = jnp.where(strict, ak_rk[:chunk], 0.0)
        it["a_rk"] = jnp.where(incl, ak_rk[chunk:], 0.0)
    t_invs = _unit_lower_inverses([it["a_ab"] for it in items], eye)
    for it in items:
        akv_rkv = _mm(jnp.concatenate([it["a_ak"], it["a_rk"]], axis=0), _pair_blockdiag(it["v"]))
        it["x1"] = akv_rkv[:chunk]
        it["rk_v"] = akv_rkv[chunk:]
    for it, t_inv in zip(items, t_invs):
        it["w_u"] = _mm(t_inv, _pair_blockdiag(it["at"]))
        it["u0"] = _mm(t_inv, _pair_blockdiag(it["x1"]))
    for it in items:
        w_o = it["rt"] + _mm(it["a_rb"], _pair_blockdiag(it["w_u"]))
        g_mat = eye * it["p_end"] + _pair_diag_blocks(_mm_tn(it["bh"], it["w_u"]))
        it["wo_g"] = jnp.concatenate([w_o, g_mat], axis=0)
        it["o0"] = _mm(it["a_rb"], _pair_blockdiag(it["u0"])) + it["rk_v"]
        it["h0"] = _pair_diag_blocks(_mm_tn(it["bh"], it["u0"]) + _mm_tn(it["kh"], it["v"]))

    states = [h_ref[j] for j in range(RW_HEAD_PAIRS)]
    for it in items:
        j = it["j"]
        res = _mm(it["wo_g"], _pair_blockdiag(states[j]))
        rows = pl.ds(base + it["ci"] * chunk, chunk)
        o_ref[rows, j * LANES:(j + 1) * LANES] = (res[:chunk] + it["o0"]).astype(o_ref.dtype)
        states[j] = res[chunk:] + it["h0"]
    for j in range(RW_HEAD_PAIRS):
        h_ref[j] = states[j]


def _rwkv_scan(r, k, v, kk, kka, lw, nb, t_len, reverse):
    n = r.shape[0]
    tb = _scan_rows(t_len, RW_BLOCK_CHUNKS * RW_CHUNK)
    nc = t_len // tb

    def idx(b, c):
        return (b * nc + (nc - 1 - c if reverse else c), 0)

    spec = pl.BlockSpec((tb, RW_DIM), idx)
    return pl.pallas_call(
        functools.partial(_rwkv_scan_kernel, reverse=reverse),
        grid=(nb, nc), in_specs=[spec] * 6, out_specs=spec,
        out_shape=jax.ShapeDtypeStruct((n, RW_DIM), ACT),
        scratch_shapes=[pltpu.VMEM((RW_HEAD_PAIRS, RW_HEAD_DIM, LANES), F32)],
        compiler_params=_params("parallel", "arbitrary"), name="rwkv_scan_bwd" if reverse else "rwkv_scan_fwd",
    )(r, k, v, kk, kka, lw)


def _ssd_scan_kernel(xbc_ref, dt_ref, la_ref, ex64_ref, y_ref, s_ref, *, reverse):
    @pl.when(pl.program_id(1) == 0)
    def _():
        s_ref[...] = jnp.zeros_like(s_ref)

    sub_rows = SSD_BLOCK_CHUNKS * SSD_CHUNK
    n_sub = xbc_ref.shape[0] // sub_rows

    def body(i, carry):
        blk = (n_sub - 1 - i) if reverse else i
        _ssd_scan_block(xbc_ref, dt_ref, la_ref, ex64_ref, y_ref, s_ref,
                        pl.multiple_of(blk * sub_rows, sub_rows), reverse)
        return carry

    lax.fori_loop(0, n_sub, body, 0)


def _ssd_scan_block(xbc_ref, dt_ref, la_ref, ex64_ref, y_ref, s_ref, base, reverse):
    chunk = SSD_CHUNK
    n_chunks = SSD_BLOCK_CHUNKS
    incl, _ = _order_masks(chunk, reverse)
    incl_bf = incl.astype(BF16)
    end = 0 if reverse else chunk - 1
    gn = SSD_GROUPS * SSD_STATE
    gw = SSD_GROUP_HEADS * SSD_HEAD_DIM
    chunk_order = list(range(n_chunks - 1, -1, -1)) if reverse else list(range(n_chunks))
    ex64 = ex64_ref[...]

    pre = []
    for ci in chunk_order:
        rows = pl.ds(base + ci * chunk, chunk)
        la = la_ref[rows, :]
        cum = _mm_exact_lhs(incl_bf, la)
        cum_rows = cum.T
        cum_end = cum[end:end + 1, :]
        ecum64, etail64, dt64 = _mm_exact_rhs_many(
            [jnp.exp(cum), jnp.exp(cum_end - cum), dt_ref[rows, :]], ex64, terms=2)
        xdt = xbc_ref[rows, :SSD_DIM].astype(F32) * dt64
        pre.append(dict(rows=rows, cum=cum, cum_rows=cum_rows, ecum64=ecum64, eend64=ecum64[end:end + 1, :],
                        xdt=xdt, xtail=xdt * etail64,
                        bm=[xbc_ref[rows, SSD_DIM + g * SSD_STATE:SSD_DIM + (g + 1) * SSD_STATE]
                            for g in range(SSD_GROUPS)],
                        cm=[xbc_ref[rows, SSD_DIM + gn + g * SSD_STATE:SSD_DIM + gn + (g + 1) * SSD_STATE]
                            for g in range(SSD_GROUPS)]))
    for p in pre:
        p["scores"] = [_mm_nt(p["cm"][g], p["bm"][g]) for g in range(SSD_GROUPS)]
        p["inject"] = [_mm_tn(p["bm"][g], p["xtail"][:, g * gw:(g + 1) * gw]) for g in range(SSD_GROUPS)]
    for p in pre:
        p["ydiag"] = []
        for h in range(SSD_HEADS):
            diff = p["cum"][:, h:h + 1] - p["cum_rows"][h:h + 1, :]
            decay = jnp.exp(jnp.where(incl, diff, NEG_BIG))
            w = p["scores"][h // SSD_GROUP_HEADS] * decay
            p["ydiag"].append(_mm(w, p["xdt"][:, h * SSD_HEAD_DIM:(h + 1) * SSD_HEAD_DIM]))

    states = [s_ref[g] for g in range(SSD_GROUPS)]
    for p in pre:
        for g in range(SSD_GROUPS):
            y_off = _mm(p["cm"][g], states[g]) * p["ecum64"][:, g * gw:(g + 1) * gw]
            for jh in range(SSD_GROUP_HEADS):
                h = g * SSD_GROUP_HEADS + jh
                y_ref[p["rows"], h * SSD_HEAD_DIM:(h + 1) * SSD_HEAD_DIM] = (
                    p["ydiag"][h] + y_off[:, jh * SSD_HEAD_DIM:(jh + 1) * SSD_HEAD_DIM]).astype(y_ref.dtype)
            states[g] = states[g] * p["eend64"][:, g * gw:(g + 1) * gw] + p["inject"][g]
    for g in range(SSD_GROUPS):
        s_ref[g] = states[g]


def _ssd_scan(xbc, dt, la, nb, t_len, reverse):
    n = xbc.shape[0]
    tb = _scan_rows(t_len, SSD_BLOCK_CHUNKS * SSD_CHUNK)
    assert SSD_CHUNK == LANES
    nc = t_len // tb
    lane_head = jnp.arange(LANES)[:, None]
    ex64 = (lane_head == jnp.arange(SSD_DIM)[None, :] // SSD_HEAD_DIM).astype(BF16)

    def idx(b, c):
        return (b * nc + (nc - 1 - c if reverse else c), 0)

    def full(a):
        return pl.BlockSpec(a.shape, lambda b, c: (0, 0))

    return pl.pallas_call(
        functools.partial(_ssd_scan_kernel, reverse=reverse),
        grid=(nb, nc),
        in_specs=[pl.BlockSpec((tb, SSD_CONV_DIM), idx), pl.BlockSpec((tb, LANES), idx),
                  pl.BlockSpec((tb, LANES), idx), full(ex64)],
        out_specs=pl.BlockSpec((tb, SSD_DIM), idx),
        out_shape=jax.ShapeDtypeStruct((n, SSD_DIM), ACT),
        scratch_shapes=[pltpu.VMEM((SSD_GROUPS, SSD_STATE, SSD_GROUP_HEADS * SSD_HEAD_DIM), F32)],
        compiler_params=_params("parallel", "arbitrary"), name="ssd_scan_bwd" if reverse else "ssd_scan_fwd",
    )(xbc, dt, la, ex64)


def _mix0_out_kernel(x_ref, of_ref, ob_ref, bonus_ref, g_ref, yf_ref, yb_ref, xbc_ref, z_ref,
                     gnw_ref, gnb_ref, bd_ref, dsk_ref, nw_ref, wo_rw_ref, wo_ssd_ref, out_ref):
    bd = bd_ref[...]
    o = of_ref[...].astype(F32) + ob_ref[...].astype(F32)
    inv_n = 1.0 / RW_HEAD_DIM
    mu = _head_sum(o, bd) * inv_n
    oc = o - mu
    var = _head_sum(oc * oc, bd) * inv_n
    o = oc * lax.rsqrt(var + RW_GN_EPS) * gnw_ref[...] + gnb_ref[...]
    o_rw = (o + bonus_ref[...].astype(F32)) * g_ref[...].astype(F32)

    y = yf_ref[...].astype(F32) + yb_ref[...].astype(F32) + dsk_ref[...] * xbc_ref[...].astype(F32)
    y = y * _silu(z_ref[...].astype(F32))
    gw = SSD_DIM // SSD_GROUPS
    parts = []
    for g in range(SSD_GROUPS):
        yg = y[:, g * gw:(g + 1) * gw]
        parts.append(yg * lax.rsqrt(jnp.mean(yg * yg, axis=-1, keepdims=True) + NORM_EPS))
    o_ssd = jnp.concatenate(parts, axis=-1) * nw_ref[...]
    out_ref[...] = x_ref[...] + _mm(o_rw, wo_rw_ref[...]) + _mm(o_ssd, wo_ssd_ref[...])


def _mix0_out(x2d, o_f, o_b, bonus, g, y_f, y_b, xbc, z, gn_w, gn_b, blockdiag, d_skip, norm_w, wo_rw, wo_ssd):
    n = x2d.shape[0]
    tm = _tile(n, 512)

    def row(width):
        return pl.BlockSpec((tm, width), lambda i: (i, 0))

    def full(a):
        return pl.BlockSpec(a.shape, lambda i: (0, 0))

    consts = (gn_w, gn_b, blockdiag, d_skip, norm_w, wo_rw, wo_ssd)
    return pl.pallas_call(
        _mix0_out_kernel,
        grid=(n // tm,),
        in_specs=[row(D_MODEL)] + [row(RW_DIM)] * 6 + [row(SSD_DIM), row(SSD_DIM)] + [full(a) for a in consts],
        out_specs=row(D_MODEL),
        out_shape=jax.ShapeDtypeStruct((n, D_MODEL), F32),
        compiler_params=_params("parallel"), name="mix0_out",
    )(x2d, o_f, o_b, bonus, g, y_f, y_b, xbc, z, *consts)


def _hgrn_lower_bound(lb_ref, layer):
    x = lb_ref[...]
    m = jnp.max(x, axis=0, keepdims=True)
    e = jnp.exp(x - m)
    s = e / jnp.sum(e, axis=0, keepdims=True)
    lb = jnp.zeros_like(s[0:1, :])
    for i in range(1, layer + 1):
        lb = lb + s[i:i + 1, :]
    return lb


def _hgrn_in_kernel(x_ref, g_ref, lb_ref, wq_ref, wff_ref, wfb_ref, wi_ref, wg_ref,
                    q_out, lff_out, lfb_out, kgf_out, kgb_out, i_out, g_out, *, layer):
    xn = _rms_normed(x_ref[...], g_ref[...]).astype(BF16)
    lb = _hgrn_lower_bound(lb_ref, layer)
    dot = functools.partial(jnp.dot, preferred_element_type=F32)
    gate_logits = [dot(xn, wff_ref[...]), dot(xn, wfb_ref[...])]
    q_out[...] = dot(xn, wq_ref[...]).astype(q_out.dtype)
    for logits, lf_out, kg_out in zip(gate_logits, (lff_out, lfb_out), (kgf_out, kgb_out)):
        f = lb + (1.0 - lb) * _sigmoid(logits)
        lf_out[...] = jnp.log(f)
        kg_out[...] = (1.0 - f).astype(kg_out.dtype)
    i_out[...] = dot(xn, wi_ref[...]).astype(i_out.dtype)
    g_out[...] = dot(xn, wg_ref[...]).astype(g_out.dtype)


def _hgrn_in(x2d, gain, hg_lb, weights, layer):
    n, d = x2d.shape
    tm = _tile(n, 512)
    row = pl.BlockSpec((tm, d), lambda i: (i, 0))
    in_specs = [row, pl.BlockSpec((1, d), lambda i: (0, 0)), pl.BlockSpec(hg_lb.shape, lambda i: (0, 0))]
    in_specs += [pl.BlockSpec(w.shape, lambda i: (0, 0)) for w in weights]
    dtypes = (ACT, F32, F32, ACT, ACT, ACT, ACT)
    return pl.pallas_call(
        functools.partial(_hgrn_in_kernel, layer=layer),
        grid=(n // tm,), in_specs=in_specs, out_specs=[row] * len(dtypes),
        out_shape=[jax.ShapeDtypeStruct((n, d), dt) for dt in dtypes],
        compiler_params=_params("parallel"), name="hgrn_in",
    )(x2d, gain.reshape(1, d), hg_lb, *weights)


def _hgrn_chunk(q, lf, kg, vv, state_t, reverse):
    chunk = q.shape[0]
    n_sub = chunk // HG_SUB
    incl, _ = _order_masks(chunk, reverse)
    end = 0 if reverse else chunk - 1
    s_idx = lax.broadcasted_iota(jnp.int32, (HG_SUB, 1), 0)

    bcum = _mm_exact_lhs(incl.astype(BF16), lf)
    b_end = bcum[end:end + 1, :]
    carried = _mm_nt(q * jnp.exp(bcum), state_t)

    order = list(range(n_sub - 1, -1, -1)) if reverse else list(range(n_sub))
    outs = [None] * n_sub
    for pos, sb in enumerate(order):
        rows = slice(sb * HG_SUB, (sb + 1) * HG_SUB)
        q_s, k_s, b_s, v_s = q[rows], kg[rows], bcum[rows], vv[rows]
        acc = carried[rows]
        if pos > 0:
            prev_sb = order[pos - 1]
            edge = prev_sb * HG_SUB if reverse else prev_sb * HG_SUB + HG_SUB - 1
            b_edge = bcum[edge:edge + 1, :]
            src = slice((sb + 1) * HG_SUB, chunk) if reverse else slice(0, sb * HG_SUB)
            q_hat = q_s * jnp.exp(b_s - b_edge)
            k_hat = kg[src] * jnp.exp(b_edge - bcum[src])
            acc = acc + _mm(_mm_nt(q_hat, k_hat), vv[src])
        rows_out = []
        for l in range(HG_SUB):
            w = jnp.exp(jnp.minimum(b_s[l:l + 1, :] - b_s, 0.0)) * k_s * q_s[l:l + 1, :]
            att = jnp.sum(w, axis=-1, keepdims=True)
            att = jnp.where((s_idx >= l) if reverse else (s_idx <= l), att, 0.0)
            rows_out.append(jnp.sum(att * v_s, axis=0, keepdims=True))
        outs[sb] = acc + jnp.concatenate(rows_out, axis=0)

    k_tail = kg * jnp.exp(b_end - bcum)
    new_state_t = state_t * jnp.exp(b_end) + _mm_tn(vv, k_tail)
    return jnp.concatenate(outs, axis=0), new_state_t


def _hgrn_scan_kernel(q_ref, lf_ref, kg_ref, i_ref, lb_ref, o_ref, s_ref, hq_ref, hf_ref, hk_ref, hv_ref, ho_ref, *,
                      reverse, layer):
    @pl.when(pl.program_id(1) == 0)
    def _():
        s_ref[...] = jnp.zeros_like(s_ref)

    sub_rows = HG_BLOCK_CHUNKS * HG_CHUNK
    n_sub = q_ref.shape[0] // sub_rows

    def body(i, carry):
        blk = (n_sub - 1 - i) if reverse else i
        _hgrn_scan_block(q_ref, lf_ref, kg_ref, i_ref, lb_ref, o_ref, s_ref, hq_ref, hf_ref, hk_ref, hv_ref, ho_ref,
                         pl.multiple_of(blk * sub_rows, sub_rows), reverse, layer)
        return carry

    lax.fori_loop(0, n_sub, body, 0)


def _hgrn_scan_block(q_ref, lf_ref, kg_ref, i_ref, lb_ref, o_ref, s_ref, hq_ref, hf_ref, hk_ref, hv_ref, ho_ref,
                     base, reverse, layer):
    chunk = HG_CHUNK
    n_chunks = HG_BLOCK_CHUNKS
    incl, _ = _order_masks(chunk, reverse)
    incl_bf = incl.astype(BF16)
    end = 0 if reverse else chunk - 1
    chunk_order = list(range(n_chunks - 1, -1, -1)) if reverse else list(range(n_chunks))

    def head_cols(h):
        return slice(h * HG_KEY_DIM, (h + 1) * HG_KEY_DIM)

    pre = []
    worst = None
    for ci in chunk_order:
        rows = pl.ds(base + ci * chunk, chunk)
        bcum = _mm_exact_lhs(incl_bf, lf_ref[rows, :])
        b_end = bcum[end:end + 1, :]
        worst = b_end if worst is None else jnp.minimum(worst, b_end)
        pre.append((ci, rows, bcum, b_end))
    lb_min = jnp.min(_hgrn_lower_bound(lb_ref, layer))
    flag = lax.cond(lb_min >= math.exp(-HG_SAFE_LOG_RANGE / HG_CHUNK),
                    lambda: jnp.int32(1),
                    lambda: (jnp.min(worst) >= -HG_SAFE_LOG_RANGE).astype(jnp.int32))
    safe = flag == 1

    @pl.when(safe)
    def _():
        items = []
        for ci, rows, bcum, b_end in pre:
            kg = kg_ref[rows, :].astype(F32)
            e_neg = jnp.exp(-bcum)
            e_end = jnp.exp(b_end)
            qt_all = q_ref[rows, :].astype(F32) * jnp.exp(bcum)
            kt_all = kg * e_neg
            ktail_all = kt_all * e_end
            v_all = i_ref[rows, :].astype(F32)
            for h in range(HG_HEADS):
                c = head_cols(h)
                items.append(dict(rows=rows, h=h, qt=qt_all[:, c], kt=kt_all[:, c], ktail=ktail_all[:, c],
                                  v=v_all[:, c], e_end=e_end[:, c]))
        for it in items:
            it["attn"] = jnp.where(incl, _mm_nt(it["qt"], it["kt"]), 0.0)
        for it in items:
            it["local"] = _mm(it["attn"], it["v"])
            it["inject"] = _mm_tn(it["v"], it["ktail"])
        states = [s_ref[h] for h in range(HG_HEADS)]
        for it in items:
            h = it["h"]
            o_ref[it["rows"], head_cols(h)] = (it["local"] + _mm_nt(it["qt"], states[h])).astype(o_ref.dtype)
            states[h] = states[h] * it["e_end"] + it["inject"]
        for h in range(HG_HEADS):
            s_ref[h] = states[h]

    @pl.when(jnp.logical_not(safe))
    def _():
        def head_body(h, carry):
            out, new_state = _hgrn_chunk(hq_ref[h], hf_ref[h], hk_ref[h], hv_ref[h], s_ref[h], reverse)
            ho_ref[h] = out
            s_ref[h] = new_state
            return carry

        def chunk_body(i, carry):
            ci = (n_chunks - 1 - i) if reverse else i
            rows = pl.ds(pl.multiple_of(base + ci * chunk, chunk), chunk)
            for h in range(HG_HEADS):
                hq_ref[h] = q_ref[rows, head_cols(h)].astype(F32)
                hf_ref[h] = lf_ref[rows, head_cols(h)]
                hk_ref[h] = kg_ref[rows, head_cols(h)].astype(F32)
                hv_ref[h] = i_ref[rows, head_cols(h)].astype(F32)
            lax.fori_loop(0, HG_HEADS, head_body, 0)
            for h in range(HG_HEADS):
                o_ref[rows, head_cols(h)] = ho_ref[h].astype(o_ref.dtype)
            return carry

        lax.fori_loop(0, n_chunks, chunk_body, 0)


def _hgrn_scan(q, lf, kg, i_val, hg_lb, nb, t_len, reverse, layer):
    n = q.shape[0]
    tb = _scan_rows(t_len, HG_BLOCK_CHUNKS * HG_CHUNK)
    n_outer = t_len // tb

    def idx(b, c):
        return (b * n_outer + (n_outer - 1 - c if reverse else c), 0)

    spec = pl.BlockSpec((tb, HG_KEY_WIDTH), idx)
    head_buf = pltpu.VMEM((HG_HEADS, HG_CHUNK, HG_KEY_DIM), F32)
    return pl.pallas_call(
        functools.partial(_hgrn_scan_kernel, reverse=reverse, layer=layer),
        grid=(nb, n_outer),
        in_specs=[spec, spec, spec, spec, pl.BlockSpec(hg_lb.shape, lambda b, c: (0, 0))],
        out_specs=spec,
        out_shape=jax.ShapeDtypeStruct((n, HG_VAL_WIDTH), ACT),
        scratch_shapes=[pltpu.VMEM((HG_HEADS, HG_VAL_DIM, HG_KEY_DIM), F32)] + [head_buf] * 5,
        compiler_params=_params("parallel", "arbitrary"),
        name="hgrn_scan_bwd" if reverse else "hgrn_scan_fwd",
    )(q, lf, kg, i_val, hg_lb)


def _hgrn_out_kernel(x_ref, of_ref, ob_ref, g_ref, nw_ref, wo_ref, out_ref):
    o = of_ref[...].astype(F32) + ob_ref[...].astype(F32)
    mean_sq = _lane_group_sum(o * o, HG_VAL_DIM) * (1.0 / HG_VAL_DIM)
    o = o * lax.rsqrt(mean_sq + NORM_EPS) * nw_ref[...] * _silu(g_ref[...].astype(F32))
    out_ref[...] = x_ref[...] + _mm(o, wo_ref[...])


def _hgrn_out(x2d, o_f, o_b, g, norm_w, w_out):
    n = x2d.shape[0]
    tm = _tile(n, 512)
    row = pl.BlockSpec((tm, D_MODEL), lambda i: (i, 0))
    return pl.pallas_call(
        _hgrn_out_kernel,
        grid=(n // tm,),
        in_specs=[row] * 4 + [pl.BlockSpec((1, HG_VAL_WIDTH), lambda i: (0, 0)),
                              pl.BlockSpec(w_out.shape, lambda i: (0, 0))],
        out_specs=row,
        out_shape=jax.ShapeDtypeStruct((n, D_MODEL), F32),
        compiler_params=_params("parallel"), name="hgrn_out",
    )(x2d, o_f, o_b, g, norm_w, w_out)


def _xattn_kernel(x_ref, g_ref, wq_ref, kv_ref, wo_ref, out_ref):
    x = x_ref[...]
    q = _mm(_rms_normed(x, g_ref[...]), wq_ref[...])
    scale = XA_HEAD_DIM ** -0.5
    scores = [_mm_nt(q[:, h * XA_HEAD_DIM:(h + 1) * XA_HEAD_DIM], kv_ref[:, h * XA_HEAD_DIM:(h + 1) * XA_HEAD_DIM])
              * scale for h in range(XA_HEADS)]
    probs = []
    for s in scores:
        p = jnp.exp(s - jnp.max(s, axis=-1, keepdims=True))
        probs.append(p * (1.0 / jnp.sum(p, axis=-1, keepdims=True)))
    heads = [_mm(p, kv_ref[:, D_MODEL + h * XA_HEAD_DIM:D_MODEL + (h + 1) * XA_HEAD_DIM])
             for h, p in enumerate(probs)]
    out_ref[...] = x + _mm(jnp.concatenate(heads, axis=-1), wo_ref[...])


def _xattn(x2d, gain, wq, kv, wo, nb, t_len, mem_len):
    n = x2d.shape[0]
    tm = _tile(t_len, 512)
    per_seq = t_len // tm
    row = pl.BlockSpec((tm, D_MODEL), lambda b, j: (b * per_seq + j, 0))
    return pl.pallas_call(
        _xattn_kernel,
        grid=(nb, per_seq),
        in_specs=[row, pl.BlockSpec((1, D_MODEL), lambda b, j: (0, 0)),
                  pl.BlockSpec(wq.shape, lambda b, j: (0, 0)),
                  pl.BlockSpec((mem_len, 2 * D_MODEL), lambda b, j: (b, 0)),
                  pl.BlockSpec(wo.shape, lambda b, j: (0, 0))],
        out_specs=row,
        out_shape=jax.ShapeDtypeStruct((n, D_MODEL), F32),
        compiler_params=_params("parallel", "parallel"), name="xattn",
    )(x2d, gain.reshape(1, D_MODEL), wq, kv, wo)


def _ffn_kernel(x_ref, g_ref, wi_ref, wo_ref, fg_ref, out_ref, *, final_norm):
    x = x_ref[...]
    gu = _mm(_rms_normed(x, g_ref[...]), wi_ref[...])
    act = _silu(gu[:, :FFN_DIM]) * gu[:, FFN_DIM:]
    y = x + _mm(act, wo_ref[...])
    if final_norm:
        y = _rms_normed(y, fg_ref[...])
    out_ref[...] = y


def _ffn(x2d, gain, w_in, w_out, final_gain, final_norm):
    n = x2d.shape[0]
    tm = _tile(n, 512)
    row = pl.BlockSpec((tm, D_MODEL), lambda i: (i, 0))
    vec = pl.BlockSpec((1, D_MODEL), lambda i: (0, 0))
    resident = pl.Buffered(1)
    return pl.pallas_call(
        functools.partial(_ffn_kernel, final_norm=final_norm),
        grid=(n // tm,),
        in_specs=[row, vec, pl.BlockSpec(w_in.shape, lambda i: (0, 0), pipeline_mode=resident),
                  pl.BlockSpec(w_out.shape, lambda i: (0, 0), pipeline_mode=resident), vec],
        out_specs=row,
        out_shape=jax.ShapeDtypeStruct((n, D_MODEL), F32),
        compiler_params=_params("parallel"), name="ffn",
    )(x2d, gain.reshape(1, D_MODEL), w_in, w_out, final_gain.reshape(1, D_MODEL))


def _pad_lanes(a, width=LANES):
    return jnp.pad(a, [(0, 0)] * (a.ndim - 1) + [(0, width - a.shape[-1])])


def _rwkv_ssd_layer(x2d, nb, t_len, norm_g, w_in, w_out, rw_mu, rw_w0, rw_w2, rw_a0, rw_a2, rw_g2, rw_k_k, rw_k_a,
                    rw_r_k, rw_gn_w, rw_gn_b, conv_w, conv_b, dt_bias, a_log, d_skip, ssd_norm_w):
    o_lora = 3 * RW_DIM
    o_z = RW_PROJ
    o_xbc = o_z + SSD_DIM
    o_dt = o_xbc + SSD_CONV_DIM
    wb = w_in.astype(BF16)
    weights = [wb[:, :o_lora], wb[:, o_lora:o_z], wb[:, o_z:o_xbc], wb[:, o_xbc:o_dt], _pad_lanes(wb[:, o_dt:])]
    head_id = jnp.arange(LANES) // RW_HEAD_DIM
    blockdiag = (head_id[:, None] == head_id[None, :]).astype(BF16)
    row = lambda a: a.reshape(1, -1)
    rwkv_consts = (row(rw_mu[:o_lora]), row(rw_mu[o_lora:]), row(rw_a0), rw_a2.astype(BF16), rw_g2.astype(BF16),
                   row(rw_k_k), row(rw_k_a), rw_w0, rw_w2.astype(BF16), row(rw_r_k), blockdiag)
    ssd_consts = (conv_w, row(conv_b), _pad_lanes(dt_bias), _pad_lanes(a_log))
    (r, k, v, kk, kka, lw_f, lw_b, g, bonus, p_z, xbc, dt_f, dt_b, la_f, la_b) = _mix0_in(
        x2d, nb, t_len, norm_g, weights, rwkv_consts, ssd_consts)
    o_f = _rwkv_scan(r, k, v, kk, kka, lw_f, nb, t_len, reverse=False)
    o_b = _rwkv_scan(r, k, v, kk, kka, lw_b, nb, t_len, reverse=True)
    y_f = _ssd_scan(xbc, dt_f, la_f, nb, t_len, reverse=False)
    y_b = _ssd_scan(xbc, dt_b, la_b, nb, t_len, reverse=True)

    wo = w_out.astype(BF16)
    d_skip_lanes = jnp.repeat(d_skip, SSD_HEAD_DIM).reshape(1, SSD_DIM)
    return _mix0_out(x2d, o_f, o_b, bonus, g, y_f, y_b, xbc, p_z, row(rw_gn_w), row(rw_gn_b), blockdiag,
                     d_skip_lanes, row(ssd_norm_w), wo[:RW_DIM], wo[RW_DIM:])


def _hgrn_layer(x2d, nb, t_len, norm_g, w_in, w_out, norm_w, hg_lb, layer):
    kw, vw = HG_KEY_WIDTH, HG_VAL_WIDTH
    wb = w_in.astype(BF16)
    weights = [wb[:, :kw], wb[:, kw:2 * kw], wb[:, 2 * kw:3 * kw], wb[:, 3 * kw:3 * kw + vw], wb[:, 3 * kw + vw:]]
    q, lf_f, lf_b, kg_f, kg_b, i_val, g = _hgrn_in(x2d, norm_g, hg_lb, weights, layer)
    o_f = _hgrn_scan(q, lf_f, kg_f, i_val, hg_lb, nb, t_len, reverse=False, layer=layer)
    o_b = _hgrn_scan(q, lf_b, kg_b, i_val, hg_lb, nb, t_len, reverse=True, layer=layer)
    return _hgrn_out(x2d, o_f, o_b, g, norm_w.reshape(1, vw), w_out.astype(BF16))


def kernel(x, mem, mix_norm, ab_w_in, ab_w_out, rw_mu, rw_w0, rw_w2, rw_a0, rw_a2, rw_g2, rw_k_k, rw_k_a, rw_r_k, rw_gn_w, rw_gn_b, ssd_conv_w, ssd_conv_b, ssd_dt_bias, ssd_a_log, ssd_d, ssd_norm_w, hg_w_in, hg_w_out, hg_norm_w, hg_lb, xa_norm, mem_norm, xa_wq, xa_wkv, xa_wo, ffn_norm, ffn_w_in, ffn_w_out, final_norm):
    nb, t_len, d = x.shape
    mem_len = mem.shape[1]
    depth = mix_norm.shape[0]
    x2d = x.reshape(nb * t_len, d)
    mem2d = mem.reshape(nb * mem_len, d)
    for layer in range(depth):
        if layer % 2 == 0:
            e = layer // 2
            x2d = _rwkv_ssd_layer(x2d, nb, t_len, mix_norm[layer], ab_w_in[e], ab_w_out[e], rw_mu[e], rw_w0[e],
                                  rw_w2[e], rw_a0[e], rw_a2[e], rw_g2[e], rw_k_k[e], rw_k_a[e],
                                  rw_r_k[e].reshape(-1), rw_gn_w[e], rw_gn_b[e], ssd_conv_w[e], ssd_conv_b[e],
                                  ssd_dt_bias[e], ssd_a_log[e], ssd_d[e], ssd_norm_w[e])
        else:
            o = layer // 2
            x2d = _hgrn_layer(x2d, nb, t_len, mix_norm[layer], hg_w_in[o], hg_w_out[o], hg_norm_w[o], hg_lb, layer)
        (kv,) = _norm_matmul(mem2d, mem_norm[layer], [xa_wkv[layer].astype(BF16)], [BF16])
        x2d = _xattn(x2d, xa_norm[layer], xa_wq[layer].astype(BF16), kv, xa_wo[layer].astype(BF16),
                     nb, t_len, mem_len)
        x2d = _ffn(x2d, ffn_norm[layer], ffn_w_in[layer].astype(BF16), ffn_w_out[layer].astype(BF16),
                   final_norm, final_norm=(layer == depth - 1))
    return x2d.reshape(nb, t_len, d)
```

```python
import functools
import math

import jax
import jax.numpy as jnp
from jax import lax
from jax.experimental import pallas as pl
from jax.experimental.pallas import tpu as pltpu

F32 = jnp.float32
BF16 = jnp.bfloat16
ACT = BF16

D_MODEL = 1024
NORM_EPS = 1e-6

RW_HEAD_DIM = 64
RW_DIM = D_MODEL // 2
RW_HEADS = RW_DIM // RW_HEAD_DIM
RW_DECAY_RANK = 64
RW_A_RANK = 64
RW_GATE_RANK = 128
RW_GN_EPS = 64e-5
RW_LORA = RW_DECAY_RANK + RW_A_RANK + RW_GATE_RANK
RW_PROJ = 3 * RW_DIM + RW_LORA
RW_CHUNK = 64
RW_BLOCK_CHUNKS = 4
RW_HEAD_PAIRS = RW_HEADS // 2
assert RW_CHUNK == RW_HEAD_DIM and 2 * RW_HEAD_DIM == 128

SSD_HEAD_DIM = 64
SSD_DIM = D_MODEL // 2
SSD_HEADS = SSD_DIM // SSD_HEAD_DIM
SSD_GROUPS = 2
SSD_STATE = 128
SSD_CONV_DIM = SSD_DIM + 2 * SSD_GROUPS * SSD_STATE
SSD_CHUNK = 128
SSD_BLOCK_CHUNKS = 4
SSD_GROUP_HEADS = SSD_HEADS // SSD_GROUPS

HG_HEADS = 8
HG_KEY_DIM = 128
HG_VAL_DIM = D_MODEL // HG_HEADS
HG_KEY_WIDTH = HG_HEADS * HG_KEY_DIM
HG_VAL_WIDTH = HG_HEADS * HG_VAL_DIM
HG_CHUNK = 64
HG_SUB = 16
HG_BLOCK_CHUNKS = 4
HG_SAFE_LOG_RANGE = 64.0
assert HG_KEY_DIM == HG_VAL_DIM

XA_HEADS = 4
XA_HEAD_DIM = D_MODEL // XA_HEADS
FFN_DIM = ((8 * D_MODEL + 3 * 256 - 1) // (3 * 256)) * 256

SUBLANES = 8
HALO_ROWS = 16
SHIFT_ROWS = 128
PROJ_COLS = 256
LANES = 128
VMEM_LIMIT_BYTES = 56 * 1024 * 1024
SCAN_STEP_ROWS = 1024
NEG_BIG = -1e30


def _params(*semantics):
    return pltpu.CompilerParams(dimension_semantics=semantics, vmem_limit_bytes=VMEM_LIMIT_BYTES)


def _tile(n, pref):
    t = min(n, pref)
    while n % t or t % SUBLANES:
        t -= 1
    return t


def _scan_rows(t_len, sub_rows):
    assert t_len % sub_rows == 0
    rows = sub_rows
    while rows * 2 <= SCAN_STEP_ROWS and t_len % (rows * 2) == 0:
        rows *= 2
    return rows


def _mm(a, b):
    return jnp.dot(a.astype(BF16), b.astype(BF16), preferred_element_type=F32)


def _mm_nt(a, b):
    return lax.dot_general(a.astype(BF16), b.astype(BF16), (((1,), (1,)), ((), ())),
                           preferred_element_type=F32)


def _mm_tn(a, b):
    return lax.dot_general(a.astype(BF16), b.astype(BF16), (((0,), (0,)), ((), ())),
                           preferred_element_type=F32)


def _split3(x):
    hi = x.astype(BF16)
    r1 = x - hi.astype(F32)
    mid = r1.astype(BF16)
    lo = (r1 - mid.astype(F32)).astype(BF16)
    return hi, mid, lo


def _mm_exact_lhs(m_bf16, x):
    hi, mid, lo = _split3(x)
    dot = functools.partial(jnp.dot, preferred_element_type=F32)
    return dot(m_bf16, hi) + dot(m_bf16, mid) + dot(m_bf16, lo)


def _mm_exact_rhs_many(xs, m_bf16, terms=3):
    rows = xs[0].shape[0]
    pieces = [piece for x in xs for piece in _split3(x)[:terms]]
    out = jnp.dot(jnp.concatenate(pieces, axis=0), m_bf16, preferred_element_type=F32)
    results = []
    for i in range(len(xs)):
        total = out[i * terms * rows:(i * terms + 1) * rows]
        for t in range(1, terms):
            total = total + out[(i * terms + t) * rows:(i * terms + t + 1) * rows]
        results.append(total)
    return results


def _mm_exact_rhs(x, m_bf16, terms=3):
    return _mm_exact_rhs_many([x], m_bf16, terms)[0]


def _rms_normed(x, gain):
    ms = jnp.mean(x * x, axis=-1, keepdims=True)
    return x * lax.rsqrt(ms + NORM_EPS) * gain


def _sigmoid(x):
    return 0.5 * jnp.tanh(0.5 * x) + 0.5


def _silu(x):
    h = 0.5 * x
    return h * jnp.tanh(h) + h


def _softplus(x):
    return jnp.maximum(x, 0.0) + jnp.log(1.0 + jnp.exp(-jnp.abs(x)))


def _order_masks(n, reverse):
    row = lax.broadcasted_iota(jnp.int32, (n, n), 0)
    col = lax.broadcasted_iota(jnp.int32, (n, n), 1)
    if reverse:
        return col >= row, col > row
    return col <= row, col < row


def _norm_matmul_kernel(x_ref, g_ref, *refs, n_out):
    xn = _rms_normed(x_ref[...], g_ref[...]).astype(BF16)
    for w_ref, o_ref in zip(refs[:n_out], refs[n_out:]):
        o_ref[...] = jnp.dot(xn, w_ref[...], preferred_element_type=F32).astype(o_ref.dtype)


def _norm_matmul(x2d, gain, weights, out_dtypes, tm_pref=512):
    n, d = x2d.shape
    tm = _tile(n, tm_pref)
    in_specs = [pl.BlockSpec((tm, d), lambda i: (i, 0)), pl.BlockSpec((1, d), lambda i: (0, 0))]
    in_specs += [pl.BlockSpec(w.shape, lambda i: (0, 0)) for w in weights]
    out_specs = [pl.BlockSpec((tm, w.shape[1]), lambda i: (i, 0)) for w in weights]
    out_shape = [jax.ShapeDtypeStruct((n, w.shape[1]), dt) for w, dt in zip(weights, out_dtypes)]
    return pl.pallas_call(
        functools.partial(_norm_matmul_kernel, n_out=len(weights)),
        grid=(n // tm,), in_specs=in_specs, out_specs=out_specs, out_shape=out_shape,
        compiler_params=_params("parallel"), name="norm_matmul",
    )(x2d, gain.reshape(1, d), *weights)


def _neighbours(p, prev_blk, next_blk, first, last):
    tt = p.shape[0]
    ridx = lax.broadcasted_iota(jnp.int32, p.shape, 0)
    prev_row = jnp.where(first, 0.0, prev_blk[HALO_ROWS - 1:HALO_ROWS, :].astype(F32))
    next_row = jnp.where(last, 0.0, next_blk[0:1, :].astype(F32))
    prev = jnp.where(ridx == 0, prev_row, pltpu.roll(p, 1, 0))
    nxt = jnp.where(ridx == tt - 1, next_row, pltpu.roll(p, tt - 1, 0))
    return prev, nxt


def _shift_matrix(tt, offsets, weight):
    t = jnp.arange(tt)[:, None]
    c = jnp.arange(tt + 2 * HALO_ROWS)[None, :]
    hit = functools.reduce(jnp.logical_or, [c == HALO_ROWS + t + d for d in offsets])
    return jnp.where(hit, weight, 0.0).astype(BF16)


def _lane_group_sum(x, width):
    assert width % LANES == 0 and x.shape[1] % width == 0
    ones = jnp.ones((LANES, LANES), BF16)
    groups = []
    for c in range(0, x.shape[1], width):
        total = _mm(x[:, c:c + LANES], ones)
        for c2 in range(c + LANES, c + width, LANES):
            total = total + _mm(x[:, c2:c2 + LANES], ones)
        groups += [total] * (width // LANES)
    return jnp.concatenate(groups, axis=1)


def _head_sum(x, pair_ones_bf16):
    blocks = [_mm(x[:, c:c + LANES], pair_ones_bf16) for c in range(0, x.shape[1], LANES)]
    return jnp.concatenate(blocks, axis=1)


def _mix0_project(x_ref, xp_ref, xn_ref, gain_ref, wrkv_ref, wlora_ref, wz_ref, wxbc_ref, wdt_ref, first, last,
                  prkv_scr, plora_scr, u_scr, z_scr, dt_scr):
    tt = x_ref.shape[0]
    dot = functools.partial(jnp.dot, preferred_element_type=F32)
    x_ext = jnp.concatenate([jnp.where(first, 0.0, xp_ref[...]), x_ref[...], jnp.where(last, 0.0, xn_ref[...])], axis=0)
    xn_ext = _rms_normed(x_ext, gain_ref[...]).astype(BF16)
    xn = xn_ext[HALO_ROWS:HALO_ROWS + tt]

    def column_block(lhs, w_ref, scr, lo, hi):
        def piece():
            scr[:, lo:hi] = dot(lhs, w_ref[:, lo:hi]).astype(scr.dtype)
        return piece

    pieces = [column_block(xn_ext, wrkv_ref, prkv_scr, c, c + RW_DIM) for c in range(0, 3 * RW_DIM, RW_DIM)]
    pieces.append(column_block(xn_ext, wlora_ref, plora_scr, 0, RW_LORA))
    pieces += [column_block(xn_ext, wxbc_ref, u_scr, c, c + SSD_DIM) for c in range(0, SSD_CONV_DIM, SSD_DIM)]
    pieces.append(column_block(xn, wz_ref, z_scr, 0, SSD_DIM))
    pieces.append(column_block(xn, wdt_ref, dt_scr, 0, LANES))
    return pieces


def _mix0_operands(prkv_scr, plora_scr, u_scr, z_scr, dt_scr,
                   savg_ref, mu_ref, mul_ref, a0_ref, a2_ref, g2_ref, kk_ref, ka_ref, w0_ref, w2_ref, rk_ref, bd_ref,
                   cw_ref, cb_ref, dtb_ref, alog_ref,
                   r_out, k_out, v_out, kkn_out, kka_out, lwf_out, lwb_out, g_out, bonus_out,
                   z_out, xbc_out, dtf_out, dtb_out, laf_out, lab_out):
    tt = z_scr.shape[0]
    dot = functools.partial(jnp.dot, preferred_element_type=F32)

    def token_shift(p_scr, m_ref, lo_col, hi_col):
        x = p_scr[HALO_ROWS:HALO_ROWS + tt, lo_col:hi_col].astype(F32)
        avg = jnp.concatenate(
            [dot(savg_ref[...], p_scr[t0:t0 + SHIFT_ROWS + 2 * HALO_ROWS, lo_col:hi_col])
             for t0 in range(0, tt, SHIFT_ROWS)], axis=0)
        return x + m_ref[:, lo_col:hi_col] * (avg - x)

    r = token_shift(prkv_scr, mu_ref, 0, RW_DIM)
    k = token_shift(prkv_scr, mu_ref, RW_DIM, 2 * RW_DIM)
    v = token_shift(prkv_scr, mu_ref, 2 * RW_DIM, 3 * RW_DIM)
    lo = token_shift(plora_scr, mul_ref, 0, RW_LORA)
    wd = lo[:, :RW_DECAY_RANK]
    ad = lo[:, RW_DECAY_RANK:RW_DECAY_RANK + RW_A_RANK]
    gd = lo[:, RW_DECAY_RANK + RW_A_RANK:]
    bd = bd_ref[...]

    a = _sigmoid(a0_ref[...] + _mm(ad, a2_ref[...]))
    g = _mm(_sigmoid(gd), g2_ref[...])
    kk = k * kk_ref[...]
    kk = kk / jnp.maximum(jnp.sqrt(_head_sum(kk * kk, bd)), 1e-12)
    k = k * (1.0 + (a - 1.0) * ka_ref[...])
    w_lora = jnp.tanh(wd)

    def log_decay(d):
        wl = w0_ref[d:d + 1, :] + _mm(w_lora, w2_ref[d])
        return -math.exp(-0.5) * _sigmoid(wl)

    r_out[...] = r.astype(r_out.dtype)
    k_out[...] = k.astype(k_out.dtype)
    v_out[...] = v.astype(v_out.dtype)
    kkn_out[...] = kk.astype(kkn_out.dtype)
    kka_out[...] = (kk * a).astype(kka_out.dtype)
    lwf_out[...] = log_decay(0)
    lwb_out[...] = log_decay(1)
    g_out[...] = g.astype(g_out.dtype)
    bonus_out[...] = (_head_sum(r * k * rk_ref[...], bd) * v).astype(bonus_out.dtype)

    z_out[...] = z_scr[...]
    u_ext = u_scr[...]
    u = u_ext[HALO_ROWS:HALO_ROWS + tt].astype(F32)
    prev, nxt = _neighbours(u, u_ext[:HALO_ROWS], u_ext[HALO_ROWS + tt:], False, False)
    conv = prev * cw_ref[0:1, :] + u * cw_ref[1:2, :] + nxt * cw_ref[2:3, :] + cb_ref[...]
    xbc_out[...] = _silu(conv).astype(xbc_out.dtype)
    dt_raw = dt_scr[...]
    dt_f = _softplus(dt_raw + dtb_ref[0:1, :])
    dt_b = _softplus(dt_raw + dtb_ref[1:2, :])
    dtf_out[...] = dt_f
    dtb_out[...] = dt_b
    laf_out[...] = dt_f * (-jnp.exp(alog_ref[0:1, :]))
    lab_out[...] = dt_b * (-jnp.exp(alog_ref[1:2, :]))


MIX0_N_PROJ_REFS = 9
MIX0_N_CONSTS = 16
MIX0_N_OUTS = 15
MIX0_N_STAGED = 5


def _mix0_in_kernel(*refs, per_seq):
    proj_refs = refs[:MIX0_N_PROJ_REFS]
    const_refs = refs[MIX0_N_PROJ_REFS:MIX0_N_PROJ_REFS + MIX0_N_CONSTS]
    out_refs = refs[MIX0_N_PROJ_REFS + MIX0_N_CONSTS:MIX0_N_PROJ_REFS + MIX0_N_CONSTS + MIX0_N_OUTS]
    staged = refs[MIX0_N_PROJ_REFS + MIX0_N_CONSTS + MIX0_N_OUTS:]
    pos = lax.rem(pl.program_id(0), per_seq)
    for piece in _mix0_project(*proj_refs, pos == 0, pos == per_seq - 1, *staged):
        piece()
    _mix0_operands(*staged, *const_refs, *out_refs)


def _mix0_in(x2d, nb, t_len, gain, weights, rwkv_consts, ssd_consts):
    n, d = x2d.shape
    tt = _tile(t_len, 512)
    assert tt % HALO_ROWS == 0
    per_seq = t_len // tt
    n_tiles = n // tt
    blocks_per_tile = tt // HALO_ROWS
    n_blocks = n // HALO_ROWS

    def full(a):
        nd = a.ndim
        return pl.BlockSpec(a.shape, lambda s: (0,) * nd)

    x_specs = [pl.BlockSpec((tt, d), lambda s: (s, 0)),
               pl.BlockSpec((HALO_ROWS, d), lambda s: (jnp.maximum(s * blocks_per_tile - 1, 0), 0)),
               pl.BlockSpec((HALO_ROWS, d), lambda s: (jnp.minimum((s + 1) * blocks_per_tile, n_blocks - 1), 0))]
    proj_consts = (gain.reshape(1, d),) + tuple(weights)
    assert tt % SHIFT_ROWS == 0
    consts = (_shift_matrix(SHIFT_ROWS, (-1, 1), 0.5),) + tuple(rwkv_consts) + tuple(ssd_consts)
    assert 3 + len(proj_consts) == MIX0_N_PROJ_REFS and len(consts) == MIX0_N_CONSTS
    wide = [(RW_DIM, ACT)] * 5 + [(RW_DIM, F32)] * 2 + [(RW_DIM, ACT)] * 2 + [(SSD_DIM, ACT), (SSD_CONV_DIM, ACT)] \
        + [(LANES, F32)] * 4
    assert len(wide) == MIX0_N_OUTS
    ext = tt + 2 * HALO_ROWS
    staged = [pltpu.VMEM((ext, 3 * RW_DIM), ACT), pltpu.VMEM((ext, RW_LORA), ACT), pltpu.VMEM((ext, SSD_CONV_DIM), ACT),
              pltpu.VMEM((tt, SSD_DIM), ACT), pltpu.VMEM((tt, LANES), F32)]
    assert len(staged) == MIX0_N_STAGED
    return pl.pallas_call(
        functools.partial(_mix0_in_kernel, per_seq=per_seq),
        grid=(n_tiles,),
        in_specs=x_specs + [full(a) for a in proj_consts + consts],
        out_specs=[pl.BlockSpec((tt, w), lambda s: (s, 0)) for w, _ in wide],
        out_shape=[jax.ShapeDtypeStruct((n, w), dt) for w, dt in wide],
        scratch_shapes=staged,
        compiler_params=_params("parallel"), name="mix0_in",
    )(x2d, x2d, x2d, *proj_consts, *consts)


def _pair_blockdiag(y):
    lane = lax.broadcasted_iota(jnp.int32, y.shape, 1)
    top = jnp.where(lane < RW_HEAD_DIM, y, 0.0).astype(BF16)
    bottom = jnp.where(lane >= RW_HEAD_DIM, y, 0.0).astype(BF16)
    return jnp.concatenate([top, bottom], axis=0)


def _pair_diag_blocks(full):
    lane = lax.broadcasted_iota(jnp.int32, (RW_HEAD_DIM, LANES), 1)
    return jnp.where(lane < RW_HEAD_DIM, full[:RW_HEAD_DIM], full[RW_HEAD_DIM:])


def _unit_lower_inverses(n_strict_list, eye):
    size = n_strict_list[0].shape[0]
    levels = int(math.log2(size)) - 1
    accs = [eye + n for n in n_strict_list]
    powers = [_mm(n, _pair_blockdiag(n)) for n in n_strict_list]
    for level in range(levels):
        power_bds = [_pair_blockdiag(p) for p in powers]
        if level == levels - 1:
            return [a + _mm(a, b) for a, b in zip(accs, power_bds)]
        both = [_mm(jnp.concatenate([a, p], axis=0), b) for a, p, b in zip(accs, powers, power_bds)]
        accs = [a + ab[:size] for a, ab in zip(accs, both)]
        powers = [ab[size:] for ab in both]


def _rwkv_scan_kernel(r_ref, k_ref, v_ref, kk_ref, kka_ref, lw_ref, o_ref, h_ref, *, reverse):
    @pl.when(pl.program_id(1) == 0)
    def _():
        h_ref[...] = jnp.zeros_like(h_ref)

    sub_rows = RW_BLOCK_CHUNKS * RW_CHUNK
    n_sub = r_ref.shape[0] // sub_rows

    def body(i, carry):
        blk = (n_sub - 1 - i) if reverse else i
        _rwkv_scan_block(r_ref, k_ref, v_ref, kk_ref, kka_ref, lw_ref, o_ref, h_ref,
                         pl.multiple_of(blk * sub_rows, sub_rows), reverse)
        return carry

    lax.fori_loop(0, n_sub, body, 0)


def _rwkv_scan_block(r_ref, k_ref, v_ref, kk_ref, kka_ref, lw_ref, o_ref, h_ref, base, reverse):
    chunk = RW_CHUNK
    n_chunks = RW_BLOCK_CHUNKS
    incl_bf = _order_masks(chunk, reverse)[0].astype(BF16)
    row = lax.broadcasted_iota(jnp.int32, (chunk, LANES), 0)
    pos = lax.broadcasted_iota(jnp.int32, (chunk, LANES), 1) % RW_HEAD_DIM
    incl, strict = (pos >= row, pos > row) if reverse else (pos <= row, pos < row)
    eye = (pos == row).astype(F32)
    end = 0 if reverse else chunk - 1
    chunk_order = list(range(n_chunks - 1, -1, -1)) if reverse else list(range(n_chunks))

    items = []
    for ci in chunk_order:
        rows = pl.ds(base + ci * chunk, chunk)
        lw = lw_ref[rows, :]
        cum = _mm_exact_lhs(incl_bf, lw)
        cum_end = cum[end:end + 1, :]
        e_neg = jnp.exp(-cum)
        e_tail = jnp.exp(cum_end - cum)
        p_end = jnp.exp(cum_end)
        k = k_ref[rows, :].astype(F32)
        kka = kka_ref[rows, :].astype(F32)
        at_all = -kk_ref[rows, :].astype(F32) * jnp.exp(cum - lw)
        bt_all = kka * e_neg
        kt_all = k * e_neg
        rt_all = r_ref[rows, :].astype(F32) * jnp.exp(cum)
        bh_all = kka * e_tail
        kh_all = k * e_tail
        v = v_ref[rows, :].astype(F32)
        for j in range(RW_HEAD_PAIRS):
            sl = slice(j * LANES, (j + 1) * LANES)
            items.append(dict(ci=ci, j=j, at=at_all[:, sl], bt=bt_all[:, sl], kt=kt_all[:, sl], rt=rt_all[:, sl],
                              bh=bh_all[:, sl], kh=kh_all[:, sl], v=v[:, sl], p_end=p_end[:, sl]))

    for it in items:
        lhs = jnp.concatenate([it["at"], it["rt"]], axis=0)
        both = _mm_nt(lhs, jnp.concatenate([_pair_blockdiag(it["bt"]), _pair_blockdiag(it["kt"])], axis=0))
        it["a_ab"] = jnp.where(strict, both[:chunk, :LANES], 0.0)
        it["a_rb"] = jnp.where(incl, both[chunk:, :LANES], 0.0)
        it["a_ak"] = jnp.where(strict, both[:chunk, LANES:], 0.0)
        it["a_rk"] = jnp.where(incl, both[chunk:, LANES:], 0.0)
    t_invs = _unit_lower_inverses([it["a_ab"] for it in items], eye)
    for it in items:
        akv_rkv = _mm(jnp.concatenate([it["a_ak"], it["a_rk"]], axis=0), _pair_blockdiag(it["v"]))
        it["x1"] = akv_rkv[:chunk]
        it["rk_v"] = akv_rkv[chunk:]
    for it, t_inv in zip(items, t_invs):
        wu_u0 = _mm(t_inv, jnp.concatenate([_pair_blockdiag(it["at"]), _pair_blockdiag(it["x1"])], axis=1))
        it["w_u"] = wu_u0[:, :LANES]
        it["u0"] = wu_u0[:, LANES:]
    for it in items:
        rb_wu_u0 = _mm(it["a_rb"],
                       jnp.concatenate([_pair_blockdiag(it["w_u"]), _pair_blockdiag(it["u0"])], axis=1))
        w_o = it["rt"] + rb_wu_u0[:, :LANES]
        g_mat = eye * it["p_end"] + _pair_diag_blocks(_mm_tn(it["bh"], it["w_u"]))
        it["wo_g"] = jnp.concatenate([w_o, g_mat], axis=0)
        it["o0"] = rb_wu_u0[:, LANES:] + it["rk_v"]
        it["h0"] = _pair_diag_blocks(_mm_tn(jnp.concatenate([it["bh"], it["kh"]], axis=0),
                                            jnp.concatenate([it["u0"], it["v"]], axis=0)))

    states = [h_ref[j] for j in range(RW_HEAD_PAIRS)]
    for it in items:
        j = it["j"]
        res = _mm(it["wo_g"], _pair_blockdiag(states[j]))
        rows = pl.ds(base + it["ci"] * chunk, chunk)
        o_ref[rows, j * LANES:(j + 1) * LANES] = (res[:chunk] + it["o0"]).astype(o_ref.dtype)
        states[j] = res[chunk:] + it["h0"]
    for j in range(RW_HEAD_PAIRS):
        h_ref[j] = states[j]


def _rwkv_scan(r, k, v, kk, kka, lw, nb, t_len, reverse):
    n = r.shape[0]
    tb = _scan_rows(t_len, RW_BLOCK_CHUNKS * RW_CHUNK)
    nc = t_len // tb

    def idx(b, c):
        return (b * nc + (nc - 1 - c if reverse else c), 0)

    spec = pl.BlockSpec((tb, RW_DIM), idx)
    return pl.pallas_call(
        functools.partial(_rwkv_scan_kernel, reverse=reverse),
        grid=(nb, nc), in_specs=[spec] * 6, out_specs=spec,
        out_shape=jax.ShapeDtypeStruct((n, RW_DIM), ACT),
        scratch_shapes=[pltpu.VMEM((RW_HEAD_PAIRS, RW_HEAD_DIM, LANES), F32)],
        compiler_params=_params("parallel", "arbitrary"), name="rwkv_scan_bwd" if reverse else "rwkv_scan_fwd",
    )(r, k, v, kk, kka, lw)


def _ssd_scan_kernel(xbc_ref, dt_ref, la_ref, ex64_ref, y_ref, s_ref, *, reverse):
    @pl.when(pl.program_id(1) == 0)
    def _():
        s_ref[...] = jnp.zeros_like(s_ref)

    sub_rows = SSD_BLOCK_CHUNKS * SSD_CHUNK
    n_sub = xbc_ref.shape[0] // sub_rows

    def body(i, carry):
        blk = (n_sub - 1 - i) if reverse else i
        _ssd_scan_block(xbc_ref, dt_ref, la_ref, ex64_ref, y_ref, s_ref,
                        pl.multiple_of(blk * sub_rows, sub_rows), reverse)
        return carry

    lax.fori_loop(0, n_sub, body, 0)


def _ssd_scan_block(xbc_ref, dt_ref, la_ref, ex64_ref, y_ref, s_ref, base, reverse):
    chunk = SSD_CHUNK
    n_chunks = SSD_BLOCK_CHUNKS
    incl, _ = _order_masks(chunk, reverse)
    incl_bf = incl.astype(BF16)
    end = 0 if reverse else chunk - 1
    gn = SSD_GROUPS * SSD_STATE
    gw = SSD_GROUP_HEADS * SSD_HEAD_DIM
    chunk_order = list(range(n_chunks - 1, -1, -1)) if reverse else list(range(n_chunks))
    ex64 = ex64_ref[...]

    pre = []
    for ci in chunk_order:
        rows = pl.ds(base + ci * chunk, chunk)
        la = la_ref[rows, :]
        cum = _mm_exact_lhs(incl_bf, la)
        cum_rows = cum.T
        cum_end = cum[end:end + 1, :]
        ecum64, etail64, dt64 = _mm_exact_rhs_many(
            [jnp.exp(cum), jnp.exp(cum_end - cum), dt_ref[rows, :]], ex64, terms=2)
        xdt = xbc_ref[rows, :SSD_DIM].astype(F32) * dt64
        pre.append(dict(rows=rows, cum=cum, cum_rows=cum_rows, ecum64=ecum64, eend64=ecum64[end:end + 1, :],
                        xdt=xdt, xtail=xdt * etail64,
                        bm=[xbc_ref[rows, SSD_DIM + g * SSD_STATE:SSD_DIM + (g + 1) * SSD_STATE]
                            for g in range(SSD_GROUPS)],
                        cm=[xbc_ref[rows, SSD_DIM + gn + g * SSD_STATE:SSD_DIM + gn + (g + 1) * SSD_STATE]
                            for g in range(SSD_GROUPS)]))
    for p in pre:
        p["scores"] = [_mm_nt(p["cm"][g], p["bm"][g]) for g in range(SSD_GROUPS)]
        p["inject"] = [_mm_tn(p["bm"][g], p["xtail"][:, g * gw:(g + 1) * gw]) for g in range(SSD_GROUPS)]
    for p in pre:
        p["ydiag"] = []
        for h in range(SSD_HEADS):
            diff = p["cum"][:, h:h + 1] - p["cum_rows"][h:h + 1, :]
            decay = jnp.exp(jnp.where(incl, diff, NEG_BIG))
            w = p["scores"][h // SSD_GROUP_HEADS] * decay
            p["ydiag"].append(_mm(w, p["xdt"][:, h * SSD_HEAD_DIM:(h + 1) * SSD_HEAD_DIM]))

    states = [s_ref[g] for g in range(SSD_GROUPS)]
    for p in pre:
        for g in range(SSD_GROUPS):
            y_off = _mm(p["cm"][g], states[g]) * p["ecum64"][:, g * gw:(g + 1) * gw]
            for jh in range(SSD_GROUP_HEADS):
                h = g * SSD_GROUP_HEADS + jh
                y_ref[p["rows"], h * SSD_HEAD_DIM:(h + 1) * SSD_HEAD_DIM] = (
                    p["ydiag"][h] + y_off[:, jh * SSD_HEAD_DIM:(jh + 1) * SSD_HEAD_DIM]).astype(y_ref.dtype)
            states[g] = states[g] * p["eend64"][:, g * gw:(g + 1) * gw] + p["inject"][g]
    for g in range(SSD_GROUPS):
        s_ref[g] = states[g]


def _ssd_scan(xbc, dt, la, nb, t_len, reverse):
    n = xbc.shape[0]
    tb = _scan_rows(t_len, SSD_BLOCK_CHUNKS * SSD_CHUNK)
    assert SSD_CHUNK == LANES
    nc = t_len // tb
    lane_head = jnp.arange(LANES)[:, None]
    ex64 = (lane_head == jnp.arange(SSD_DIM)[None, :] // SSD_HEAD_DIM).astype(BF16)

    def idx(b, c):
        return (b * nc + (nc - 1 - c if reverse else c), 0)

    def full(a):
        return pl.BlockSpec(a.shape, lambda b, c: (0, 0))

    return pl.pallas_call(
        functools.partial(_ssd_scan_kernel, reverse=reverse),
        grid=(nb, nc),
        in_specs=[pl.BlockSpec((tb, SSD_CONV_DIM), idx), pl.BlockSpec((tb, LANES), idx),
                  pl.BlockSpec((tb, LANES), idx), full(ex64)],
        out_specs=pl.BlockSpec((tb, SSD_DIM), idx),
        out_shape=jax.ShapeDtypeStruct((n, SSD_DIM), ACT),
        scratch_shapes=[pltpu.VMEM((SSD_GROUPS, SSD_STATE, SSD_GROUP_HEADS * SSD_HEAD_DIM), F32)],
        compiler_params=_params("parallel", "arbitrary"), name="ssd_scan_bwd" if reverse else "ssd_scan_fwd",
    )(xbc, dt, la, ex64)


def _mix0_out_kernel(x_ref, of_ref, ob_ref, bonus_ref, g_ref, yf_ref, yb_ref, xbc_ref, z_ref,
                     gnw_ref, gnb_ref, bd_ref, dsk_ref, nw_ref, wo_rw_ref, wo_ssd_ref, out_ref):
    bd = bd_ref[...]
    o = of_ref[...].astype(F32) + ob_ref[...].astype(F32)
    inv_n = 1.0 / RW_HEAD_DIM
    mu = _head_sum(o, bd) * inv_n
    oc = o - mu
    var = _head_sum(oc * oc, bd) * inv_n
    o = oc * lax.rsqrt(var + RW_GN_EPS) * gnw_ref[...] + gnb_ref[...]
    o_rw = (o + bonus_ref[...].astype(F32)) * g_ref[...].astype(F32)

    y = yf_ref[...].astype(F32) + yb_ref[...].astype(F32) + dsk_ref[...] * xbc_ref[...].astype(F32)
    y = y * _silu(z_ref[...].astype(F32))
    gw = SSD_DIM // SSD_GROUPS
    parts = []
    for g in range(SSD_GROUPS):
        yg = y[:, g * gw:(g + 1) * gw]
        parts.append(yg * lax.rsqrt(jnp.mean(yg * yg, axis=-1, keepdims=True) + NORM_EPS))
    o_ssd = jnp.concatenate(parts, axis=-1) * nw_ref[...]
    out_ref[...] = x_ref[...] + _mm(o_rw, wo_rw_ref[...]) + _mm(o_ssd, wo_ssd_ref[...])


def _mix0_out(x2d, o_f, o_b, bonus, g, y_f, y_b, xbc, z, gn_w, gn_b, blockdiag, d_skip, norm_w, wo_rw, wo_ssd):
    n = x2d.shape[0]
    tm = _tile(n, 512)

    def row(width):
        return pl.BlockSpec((tm, width), lambda i: (i, 0))

    def full(a):
        return pl.BlockSpec(a.shape, lambda i: (0, 0))

    consts = (gn_w, gn_b, blockdiag, d_skip, norm_w, wo_rw, wo_ssd)
    return pl.pallas_call(
        _mix0_out_kernel,
        grid=(n // tm,),
        in_specs=[row(D_MODEL)] + [row(RW_DIM)] * 6 + [row(SSD_DIM), row(SSD_DIM)] + [full(a) for a in consts],
        out_specs=row(D_MODEL),
        out_shape=jax.ShapeDtypeStruct((n, D_MODEL), F32),
        compiler_params=_params("parallel"), name="mix0_out",
    )(x2d, o_f, o_b, bonus, g, y_f, y_b, xbc, z, *consts)


def _hgrn_lower_bound(lb_ref, layer):
    x = lb_ref[...]
    m = jnp.max(x, axis=0, keepdims=True)
    e = jnp.exp(x - m)
    s = e / jnp.sum(e, axis=0, keepdims=True)
    lb = jnp.zeros_like(s[0:1, :])
    for i in range(1, layer + 1):
        lb = lb + s[i:i + 1, :]
    return lb


def _hgrn_in_kernel(x_ref, g_ref, lb_ref, wq_ref, wff_ref, wfb_ref, wi_ref, wg_ref,
                    q_out, lff_out, lfb_out, kgf_out, kgb_out, i_out, g_out, *, layer):
    xn = _rms_normed(x_ref[...], g_ref[...]).astype(BF16)
    lb = _hgrn_lower_bound(lb_ref, layer)
    dot = functools.partial(jnp.dot, preferred_element_type=F32)
    width = x_ref.shape[1]
    jobs = []
    for c in range(0, width, PROJ_COLS):
        cols = slice(c, c + PROJ_COLS)
        jobs.append((wff_ref, cols, ("gate", lff_out, kgf_out)))
        jobs.append((wq_ref, cols, ("plain", q_out)))
        jobs.append((wfb_ref, cols, ("gate", lfb_out, kgb_out)))
        jobs.append((wi_ref, cols, ("plain", i_out)))
        jobs.append((wg_ref, cols, ("plain", g_out)))

    def finish(acc, cols, how):
        if how[0] == "gate":
            f = lb[:, cols] + (1.0 - lb[:, cols]) * _sigmoid(acc)
            how[1][:, cols] = jnp.log(f)
            how[2][:, cols] = (1.0 - f).astype(how[2].dtype)
        else:
            how[1][:, cols] = acc.astype(how[1].dtype)

    pending = None
    for w_ref, cols, how in jobs:
        acc = dot(xn, w_ref[:, cols])
        if pending is not None:
            finish(*pending)
        pending = (acc, cols, how)
    finish(*pending)


def _hgrn_in(x2d, gain, hg_lb, weights, layer):
    n, d = x2d.shape
    tm = _tile(n, 512)
    row = pl.BlockSpec((tm, d), lambda i: (i, 0))
    in_specs = [row, pl.BlockSpec((1, d), lambda i: (0, 0)), pl.BlockSpec(hg_lb.shape, lambda i: (0, 0))]
    in_specs += [pl.BlockSpec(w.shape, lambda i: (0, 0)) for w in weights]
    dtypes = (ACT, F32, F32, ACT, ACT, ACT, ACT)
    return pl.pallas_call(
        functools.partial(_hgrn_in_kernel, layer=layer),
        grid=(n // tm,), in_specs=in_specs, out_specs=[row] * len(dtypes),
        out_shape=[jax.ShapeDtypeStruct((n, d), dt) for dt in dtypes],
        compiler_params=_params("parallel"), name="hgrn_in",
    )(x2d, gain.reshape(1, d), hg_lb, *weights)


def _hgrn_chunk(q, lf, kg, vv, state_t, reverse):
    chunk = q.shape[0]
    n_sub = chunk // HG_SUB
    incl, _ = _order_masks(chunk, reverse)
    end = 0 if reverse else chunk - 1
    s_idx = lax.broadcasted_iota(jnp.int32, (HG_SUB, 1), 0)

    bcum = _mm_exact_lhs(incl.astype(BF16), lf)
    b_end = bcum[end:end + 1, :]
    carried = _mm_nt(q * jnp.exp(bcum), state_t)

    order = list(range(n_sub - 1, -1, -1)) if reverse else list(range(n_sub))
    outs = [None] * n_sub
    for pos, sb in enumerate(order):
        rows = slice(sb * HG_SUB, (sb + 1) * HG_SUB)
        q_s, k_s, b_s, v_s = q[rows], kg[rows], bcum[rows], vv[rows]
        acc = carried[rows]
        if pos > 0:
            prev_sb = order[pos - 1]
            edge = prev_sb * HG_SUB if reverse else prev_sb * HG_SUB + HG_SUB - 1
            b_edge = bcum[edge:edge + 1, :]
            src = slice((sb + 1) * HG_SUB, chunk) if reverse else slice(0, sb * HG_SUB)
            q_hat = q_s * jnp.exp(b_s - b_edge)
            k_hat = kg[src] * jnp.exp(b_edge - bcum[src])
            acc = acc + _mm(_mm_nt(q_hat, k_hat), vv[src])
        rows_out = []
        for l in range(HG_SUB):
            w = jnp.exp(jnp.minimum(b_s[l:l + 1, :] - b_s, 0.0)) * k_s * q_s[l:l + 1, :]
            att = jnp.sum(w, axis=-1, keepdims=True)
            att = jnp.where((s_idx >= l) if reverse else (s_idx <= l), att, 0.0)
            rows_out.append(jnp.sum(att * v_s, axis=0, keepdims=True))
        outs[sb] = acc + jnp.concatenate(rows_out, axis=0)

    k_tail = kg * jnp.exp(b_end - bcum)
    new_state_t = state_t * jnp.exp(b_end) + _mm_tn(vv, k_tail)
    return jnp.concatenate(outs, axis=0), new_state_t


def _hgrn_scan_kernel(q_ref, lf_ref, kg_ref, i_ref, lb_ref, o_ref, s_ref, hq_ref, hf_ref, hk_ref, hv_ref, ho_ref, *,
                      reverse, layer):
    @pl.when(pl.program_id(1) == 0)
    def _():
        s_ref[...] = jnp.zeros_like(s_ref)

    sub_rows = HG_BLOCK_CHUNKS * HG_CHUNK
    n_sub = q_ref.shape[0] // sub_rows

    def body(i, carry):
        blk = (n_sub - 1 - i) if reverse else i
        _hgrn_scan_block(q_ref, lf_ref, kg_ref, i_ref, lb_ref, o_ref, s_ref, hq_ref, hf_ref, hk_ref, hv_ref, ho_ref,
                         pl.multiple_of(blk * sub_rows, sub_rows), reverse, layer)
        return carry

    lax.fori_loop(0, n_sub, body, 0)


def _hgrn_scan_block(q_ref, lf_ref, kg_ref, i_ref, lb_ref, o_ref, s_ref, hq_ref, hf_ref, hk_ref, hv_ref, ho_ref,
                     base, reverse, layer):
    chunk = HG_CHUNK
    n_chunks = HG_BLOCK_CHUNKS
    incl, _ = _order_masks(chunk, reverse)
    incl_bf = incl.astype(BF16)
    end = 0 if reverse else chunk - 1
    chunk_order = list(range(n_chunks - 1, -1, -1)) if reverse else list(range(n_chunks))

    def head_cols(h):
        return slice(h * HG_KEY_DIM, (h + 1) * HG_KEY_DIM)

    pre = []
    worst = None
    for ci in chunk_order:
        rows = pl.ds(base + ci * chunk, chunk)
        bcum = _mm_exact_lhs(incl_bf, lf_ref[rows, :])
        b_end = bcum[end:end + 1, :]
        worst = b_end if worst is None else jnp.minimum(worst, b_end)
        pre.append((ci, rows, bcum, b_end))
    lb_min = jnp.min(_hgrn_lower_bound(lb_ref, layer))
    flag = lax.cond(lb_min >= math.exp(-HG_SAFE_LOG_RANGE / HG_CHUNK),
                    lambda: jnp.int32(1),
                    lambda: (jnp.min(worst) >= -HG_SAFE_LOG_RANGE).astype(jnp.int32))
    safe = flag == 1

    @pl.when(safe)
    def _():
        items = []
        for ci, rows, bcum, b_end in pre:
            kg = kg_ref[rows, :].astype(F32)
            e_neg = jnp.exp(-bcum)
            e_end = jnp.exp(b_end)
            qt_all = q_ref[rows, :].astype(F32) * jnp.exp(bcum)
            kt_all = kg * e_neg
            ktail_all = kt_all * e_end
            v_all = i_ref[rows, :].astype(F32)
            for h in range(HG_HEADS):
                c = head_cols(h)
                items.append(dict(rows=rows, h=h, qt=qt_all[:, c], kt=kt_all[:, c], ktail=ktail_all[:, c],
                                  v=v_all[:, c], e_end=e_end[:, c]))
        for it in items:
            it["attn"] = jnp.where(incl, _mm_nt(it["qt"], it["kt"]), 0.0)
        for it in items:
            it["local"] = _mm(it["attn"], it["v"])
            it["inject"] = _mm_tn(it["v"], it["ktail"])
        states = [s_ref[h] for h in range(HG_HEADS)]
        for it in items:
            h = it["h"]
            o_ref[it["rows"], head_cols(h)] = (it["local"] + _mm_nt(it["qt"], states[h])).astype(o_ref.dtype)
            states[h] = states[h] * it["e_end"] + it["inject"]
        for h in range(HG_HEADS):
            s_ref[h] = states[h]

    @pl.when(jnp.logical_not(safe))
    def _():
        def head_body(h, carry):
            out, new_state = _hgrn_chunk(hq_ref[h], hf_ref[h], hk_ref[h], hv_ref[h], s_ref[h], reverse)
            ho_ref[h] = out
            s_ref[h] = new_state
            return carry

        def chunk_body(i, carry):
            ci = (n_chunks - 1 - i) if reverse else i
            rows = pl.ds(pl.multiple_of(base + ci * chunk, chunk), chunk)
            for h in range(HG_HEADS):
                hq_ref[h] = q_ref[rows, head_cols(h)].astype(F32)
                hf_ref[h] = lf_ref[rows, head_cols(h)]
                hk_ref[h] = kg_ref[rows, head_cols(h)].astype(F32)
                hv_ref[h] = i_ref[rows, head_cols(h)].astype(F32)
            lax.fori_loop(0, HG_HEADS, head_body, 0)
            for h in range(HG_HEADS):
                o_ref[rows, head_cols(h)] = ho_ref[h].astype(o_ref.dtype)
            return carry

        lax.fori_loop(0, n_chunks, chunk_body, 0)


def _hgrn_scan(q, lf, kg, i_val, hg_lb, nb, t_len, reverse, layer):
    n = q.shape[0]
    tb = _scan_rows(t_len, HG_BLOCK_CHUNKS * HG_CHUNK)
    n_outer = t_len // tb

    def idx(b, c):
        return (b * n_outer + (n_outer - 1 - c if reverse else c), 0)

    spec = pl.BlockSpec((tb, HG_KEY_WIDTH), idx)
    head_buf = pltpu.VMEM((HG_HEADS, HG_CHUNK, HG_KEY_DIM), F32)
    return pl.pallas_call(
        functools.partial(_hgrn_scan_kernel, reverse=reverse, layer=layer),
        grid=(nb, n_outer),
        in_specs=[spec, spec, spec, spec, pl.BlockSpec(hg_lb.shape, lambda b, c: (0, 0))],
        out_specs=spec,
        out_shape=jax.ShapeDtypeStruct((n, HG_VAL_WIDTH), ACT),
        scratch_shapes=[pltpu.VMEM((HG_HEADS, HG_VAL_DIM, HG_KEY_DIM), F32)] + [head_buf] * 5,
        compiler_params=_params("parallel", "arbitrary"),
        name="hgrn_scan_bwd" if reverse else "hgrn_scan_fwd",
    )(q, lf, kg, i_val, hg_lb)


def _hgrn_out_kernel(x_ref, of_ref, ob_ref, g_ref, nw_ref, wo_ref, out_ref):
    o = of_ref[...].astype(F32) + ob_ref[...].astype(F32)
    mean_sq = _lane_group_sum(o * o, HG_VAL_DIM) * (1.0 / HG_VAL_DIM)
    o = o * lax.rsqrt(mean_sq + NORM_EPS) * nw_ref[...] * _silu(g_ref[...].astype(F32))
    out_ref[...] = x_ref[...] + _mm(o, wo_ref[...])


def _hgrn_out(x2d, o_f, o_b, g, norm_w, w_out):
    n = x2d.shape[0]
    tm = _tile(n, 512)
    row = pl.BlockSpec((tm, D_MODEL), lambda i: (i, 0))
    return pl.pallas_call(
        _hgrn_out_kernel,
        grid=(n // tm,),
        in_specs=[row] * 4 + [pl.BlockSpec((1, HG_VAL_WIDTH), lambda i: (0, 0)),
                              pl.BlockSpec(w_out.shape, lambda i: (0, 0))],
        out_specs=row,
        out_shape=jax.ShapeDtypeStruct((n, D_MODEL), F32),
        compiler_params=_params("parallel"), name="hgrn_out",
    )(x2d, o_f, o_b, g, norm_w, w_out)


def _xattn_kernel(x_ref, g_ref, wq_ref, kv_ref, wo_ref, out_ref):
    x = x_ref[...]
    q = _mm(_rms_normed(x, g_ref[...]), wq_ref[...])
    scale = XA_HEAD_DIM ** -0.5
    scores = [_mm_nt(q[:, h * XA_HEAD_DIM:(h + 1) * XA_HEAD_DIM], kv_ref[:, h * XA_HEAD_DIM:(h + 1) * XA_HEAD_DIM])
              * scale for h in range(XA_HEADS)]
    probs = []
    for s in scores:
        p = jnp.exp(s - jnp.max(s, axis=-1, keepdims=True))
        probs.append(p * (1.0 / jnp.sum(p, axis=-1, keepdims=True)))
    heads = [_mm(p, kv_ref[:, D_MODEL + h * XA_HEAD_DIM:D_MODEL + (h + 1) * XA_HEAD_DIM])
             for h, p in enumerate(probs)]
    out_ref[...] = x + _mm(jnp.concatenate(heads, axis=-1), wo_ref[...])


def _xattn(x2d, gain, wq, kv, wo, nb, t_len, mem_len):
    n = x2d.shape[0]
    tm = _tile(t_len, 512)
    per_seq = t_len // tm
    row = pl.BlockSpec((tm, D_MODEL), lambda b, j: (b * per_seq + j, 0))
    return pl.pallas_call(
        _xattn_kernel,
        grid=(nb, per_seq),
        in_specs=[row, pl.BlockSpec((1, D_MODEL), lambda b, j: (0, 0)),
                  pl.BlockSpec(wq.shape, lambda b, j: (0, 0)),
                  pl.BlockSpec((mem_len, 2 * D_MODEL), lambda b, j: (b, 0)),
                  pl.BlockSpec(wo.shape, lambda b, j: (0, 0))],
        out_specs=row,
        out_shape=jax.ShapeDtypeStruct((n, D_MODEL), F32),
        compiler_params=_params("parallel", "parallel"), name="xattn",
    )(x2d, gain.reshape(1, D_MODEL), wq, kv, wo)


def _ffn_kernel(x_ref, g_ref, wi_ref, wo_ref, fg_ref, out_ref, *, final_norm):
    x = x_ref[...]
    gu = _mm(_rms_normed(x, g_ref[...]), wi_ref[...])
    act = _silu(gu[:, :FFN_DIM]) * gu[:, FFN_DIM:]
    y = x + _mm(act, wo_ref[...])
    if final_norm:
        y = _rms_normed(y, fg_ref[...])
    out_ref[...] = y


def _ffn(x2d, gain, w_in, w_out, final_gain, final_norm):
    n = x2d.shape[0]
    tm = _tile(n, 512)
    row = pl.BlockSpec((tm, D_MODEL), lambda i: (i, 0))
    vec = pl.BlockSpec((1, D_MODEL), lambda i: (0, 0))
    resident = pl.Buffered(1)
    return pl.pallas_call(
        functools.partial(_ffn_kernel, final_norm=final_norm),
        grid=(n // tm,),
        in_specs=[row, vec, pl.BlockSpec(w_in.shape, lambda i: (0, 0), pipeline_mode=resident),
                  pl.BlockSpec(w_out.shape, lambda i: (0, 0), pipeline_mode=resident), vec],
        out_specs=row,
        out_shape=jax.ShapeDtypeStruct((n, D_MODEL), F32),
        compiler_params=_params("parallel"), name="ffn",
    )(x2d, gain.reshape(1, D_MODEL), w_in, w_out, final_gain.reshape(1, D_MODEL))


def _pad_lanes(a, width=LANES):
    return jnp.pad(a, [(0, 0)] * (a.ndim - 1) + [(0, width - a.shape[-1])])


def _rwkv_ssd_layer(x2d, nb, t_len, norm_g, w_in, w_out, rw_mu, rw_w0, rw_w2, rw_a0, rw_a2, rw_g2, rw_k_k, rw_k_a,
                    rw_r_k, rw_gn_w, rw_gn_b, conv_w, conv_b, dt_bias, a_log, d_skip, ssd_norm_w):
    o_lora = 3 * RW_DIM
    o_z = RW_PROJ
    o_xbc = o_z + SSD_DIM
    o_dt = o_xbc + SSD_CONV_DIM
    wb = w_in.astype(BF16)
    weights = [wb[:, :o_lora], wb[:, o_lora:o_z], wb[:, o_z:o_xbc], wb[:, o_xbc:o_dt], _pad_lanes(wb[:, o_dt:])]
    head_id = jnp.arange(LANES) // RW_HEAD_DIM
    blockdiag = (head_id[:, None] == head_id[None, :]).astype(BF16)
    row = lambda a: a.reshape(1, -1)
    rwkv_consts = (row(rw_mu[:o_lora]), row(rw_mu[o_lora:]), row(rw_a0), rw_a2.astype(BF16), rw_g2.astype(BF16),
                   row(rw_k_k), row(rw_k_a), rw_w0, rw_w2.astype(BF16), row(rw_r_k), blockdiag)
    ssd_consts = (conv_w, row(conv_b), _pad_lanes(dt_bias), _pad_lanes(a_log))
    (r, k, v, kk, kka, lw_f, lw_b, g, bonus, p_z, xbc, dt_f, dt_b, la_f, la_b) = _mix0_in(
        x2d, nb, t_len, norm_g, weights, rwkv_consts, ssd_consts)
    o_f = _rwkv_scan(r, k, v, kk, kka, lw_f, nb, t_len, reverse=False)
    o_b = _rwkv_scan(r, k, v, kk, kka, lw_b, nb, t_len, reverse=True)
    y_f = _ssd_scan(xbc, dt_f, la_f, nb, t_len, reverse=False)
    y_b = _ssd_scan(xbc, dt_b, la_b, nb, t_len, reverse=True)

    wo = w_out.astype(BF16)
    d_skip_lanes = jnp.repeat(d_skip, SSD_HEAD_DIM).reshape(1, SSD_DIM)
    return _mix0_out(x2d, o_f, o_b, bonus, g, y_f, y_b, xbc, p_z, row(rw_gn_w), row(rw_gn_b), blockdiag,
                     d_skip_lanes, row(ssd_norm_w), wo[:RW_DIM], wo[RW_DIM:])


def _hgrn_layer(x2d, nb, t_len, norm_g, w_in, w_out, norm_w, hg_lb, layer):
    kw, vw = HG_KEY_WIDTH, HG_VAL_WIDTH
    wb = w_in.astype(BF16)
    weights = [wb[:, :kw], wb[:, kw:2 * kw], wb[:, 2 * kw:3 * kw], wb[:, 3 * kw:3 * kw + vw], wb[:, 3 * kw + vw:]]
    q, lf_f, lf_b, kg_f, kg_b, i_val, g = _hgrn_in(x2d, norm_g, hg_lb, weights, layer)
    o_f = _hgrn_scan(q, lf_f, kg_f, i_val, hg_lb, nb, t_len, reverse=False, layer=layer)
    o_b = _hgrn_scan(q, lf_b, kg_b, i_val, hg_lb, nb, t_len, reverse=True, layer=layer)
    return _hgrn_out(x2d, o_f, o_b, g, norm_w.reshape(1, vw), w_out.astype(BF16))


def kernel(x, mem, mix_norm, ab_w_in, ab_w_out, rw_mu, rw_w0, rw_w2, rw_a0, rw_a2, rw_g2, rw_k_k, rw_k_a, rw_r_k, rw_gn_w, rw_gn_b, ssd_conv_w, ssd_conv_b, ssd_dt_bias, ssd_a_log, ssd_d, ssd_norm_w, hg_w_in, hg_w_out, hg_norm_w, hg_lb, xa_norm, mem_norm, xa_wq, xa_wkv, xa_wo, ffn_norm, ffn_w_in, ffn_w_out, final_norm):
    nb, t_len, d = x.shape
    mem_len = mem.shape[1]
    depth = mix_norm.shape[0]
    x2d = x.reshape(nb * t_len, d)
    mem2d = mem.reshape(nb * mem_len, d)
    for layer in range(depth):
        if layer % 2 == 0:
            e = layer // 2
            x2d = _rwkv_ssd_layer(x2d, nb, t_len, mix_norm[layer], ab_w_in[e], ab_w_out[e], rw_mu[e], rw_w0[e],
                                  rw_w2[e], rw_a0[e], rw_a2[e], rw_g2[e], rw_k_k[e], rw_k_a[e],
                                  rw_r_k[e].reshape(-1), rw_gn_w[e], rw_gn_b[e], ssd_conv_w[e], ssd_conv_b[e],
                                  ssd_dt_bias[e], ssd_a_log[e], ssd_d[e], ssd_norm_w[e])
        else:
            o = layer // 2
            x2d = _hgrn_layer(x2d, nb, t_len, mix_norm[layer], hg_w_in[o], hg_w_out[o], hg_norm_w[o], hg_lb, layer)
        (kv,) = _norm_matmul(mem2d, mem_norm[layer], [xa_wkv[layer].astype(BF16)], [BF16])
        x2d = _xattn(x2d, xa_norm[layer], xa_wq[layer].astype(BF16), kv, xa_wo[layer].astype(BF16),
                     nb, t_len, mem_len)
        x2d = _ffn(x2d, ffn_norm[layer], ffn_w_in[layer].astype(BF16), ffn_w_out[layer].astype(BF16),
                   final_norm, final_norm=(layer == depth - 1))
    return x2d.reshape(nb, t_len, d)
```

```python
import functools
import math

import jax
import jax.numpy as jnp
from jax import lax
from jax.experimental import pallas as pl
from jax.experimental.pallas import tpu as pltpu

F32 = jnp.float32
BF16 = jnp.bfloat16
ACT = BF16

D_MODEL = 1024
NORM_EPS = 1e-6

RW_HEAD_DIM = 64
RW_DIM = D_MODEL // 2
RW_HEADS = RW_DIM // RW_HEAD_DIM
RW_DECAY_RANK = 64
RW_A_RANK = 64
RW_GATE_RANK = 128
RW_GN_EPS = 64e-5
RW_LORA = RW_DECAY_RANK + RW_A_RANK + RW_GATE_RANK
RW_PROJ = 3 * RW_DIM + RW_LORA
RW_CHUNK = 64
RW_BLOCK_CHUNKS = 4
RW_HEAD_PAIRS = RW_HEADS // 2
assert RW_CHUNK == RW_HEAD_DIM and 2 * RW_HEAD_DIM == 128

SSD_HEAD_DIM = 64
SSD_DIM = D_MODEL // 2
SSD_HEADS = SSD_DIM // SSD_HEAD_DIM
SSD_GROUPS = 2
SSD_STATE = 128
SSD_CONV_DIM = SSD_DIM + 2 * SSD_GROUPS * SSD_STATE
SSD_CHUNK = 128
SSD_BLOCK_CHUNKS = 4
SSD_GROUP_HEADS = SSD_HEADS // SSD_GROUPS

HG_HEADS = 8
HG_KEY_DIM = 128
HG_VAL_DIM = D_MODEL // HG_HEADS
HG_KEY_WIDTH = HG_HEADS * HG_KEY_DIM
HG_VAL_WIDTH = HG_HEADS * HG_VAL_DIM
HG_CHUNK = 64
HG_SUB = 16
HG_BLOCK_CHUNKS = 4
HG_SAFE_LOG_RANGE = 64.0
assert HG_KEY_DIM == HG_VAL_DIM

XA_HEADS = 4
XA_HEAD_DIM = D_MODEL // XA_HEADS
FFN_DIM = ((8 * D_MODEL + 3 * 256 - 1) // (3 * 256)) * 256

SUBLANES = 8
HALO_ROWS = 16
SHIFT_ROWS = 128
PROJ_COLS = 256
LANES = 128
VMEM_LIMIT_BYTES = 56 * 1024 * 1024
SCAN_STEP_ROWS = 1024
NEG_BIG = -1e30


def _params(*semantics):
    return pltpu.CompilerParams(dimension_semantics=semantics, vmem_limit_bytes=VMEM_LIMIT_BYTES)


def _tile(n, pref):
    t = min(n, pref)
    while n % t or t % SUBLANES:
        t -= 1
    return t


def _scan_rows(t_len, sub_rows):
    assert t_len % sub_rows == 0
    rows = sub_rows
    while rows * 2 <= SCAN_STEP_ROWS and t_len % (rows * 2) == 0:
        rows *= 2
    return rows


def _mm(a, b):
    return jnp.dot(a.astype(BF16), b.astype(BF16), preferred_element_type=F32)


def _mm_nt(a, b):
    return lax.dot_general(a.astype(BF16), b.astype(BF16), (((1,), (1,)), ((), ())),
                           preferred_element_type=F32)


def _mm_tn(a, b):
    return lax.dot_general(a.astype(BF16), b.astype(BF16), (((0,), (0,)), ((), ())),
                           preferred_element_type=F32)


def _split3(x):
    hi = x.astype(BF16)
    r1 = x - hi.astype(F32)
    mid = r1.astype(BF16)
    lo = (r1 - mid.astype(F32)).astype(BF16)
    return hi, mid, lo


def _mm_exact_lhs(m_bf16, x, terms=3):
    dot = functools.partial(jnp.dot, preferred_element_type=F32)
    pieces = _split3(x)[:terms]
    out = dot(m_bf16, pieces[0])
    for piece in pieces[1:]:
        out = out + dot(m_bf16, piece)
    return out


def _mm_exact_rhs_many(xs, m_bf16, terms=3):
    rows = xs[0].shape[0]
    pieces = [piece for x in xs for piece in _split3(x)[:terms]]
    out = jnp.dot(jnp.concatenate(pieces, axis=0), m_bf16, preferred_element_type=F32)
    results = []
    for i in range(len(xs)):
        total = out[i * terms * rows:(i * terms + 1) * rows]
        for t in range(1, terms):
            total = total + out[(i * terms + t) * rows:(i * terms + t + 1) * rows]
        results.append(total)
    return results


def _mm_exact_rhs(x, m_bf16, terms=3):
    return _mm_exact_rhs_many([x], m_bf16, terms)[0]


def _rms_normed(x, gain):
    ms = jnp.mean(x * x, axis=-1, keepdims=True)
    return x * lax.rsqrt(ms + NORM_EPS) * gain


def _sigmoid(x):
    return 0.5 * jnp.tanh(0.5 * x) + 0.5


def _silu(x):
    h = 0.5 * x
    return h * jnp.tanh(h) + h


def _softplus(x):
    return jnp.maximum(x, 0.0) + jnp.log(1.0 + jnp.exp(-jnp.abs(x)))


def _order_masks(n, reverse):
    row = lax.broadcasted_iota(jnp.int32, (n, n), 0)
    col = lax.broadcasted_iota(jnp.int32, (n, n), 1)
    if reverse:
        return col >= row, col > row
    return col <= row, col < row


def _norm_matmul_kernel(x_ref, g_ref, *refs, n_out):
    xn = _rms_normed(x_ref[...], g_ref[...]).astype(BF16)
    for w_ref, o_ref in zip(refs[:n_out], refs[n_out:]):
        o_ref[...] = jnp.dot(xn, w_ref[...], preferred_element_type=F32).astype(o_ref.dtype)


def _norm_matmul(x2d, gain, weights, out_dtypes, tm_pref=512):
    n, d = x2d.shape
    tm = _tile(n, tm_pref)
    in_specs = [pl.BlockSpec((tm, d), lambda i: (i, 0)), pl.BlockSpec((1, d), lambda i: (0, 0))]
    in_specs += [pl.BlockSpec(w.shape, lambda i: (0, 0)) for w in weights]
    out_specs = [pl.BlockSpec((tm, w.shape[1]), lambda i: (i, 0)) for w in weights]
    out_shape = [jax.ShapeDtypeStruct((n, w.shape[1]), dt) for w, dt in zip(weights, out_dtypes)]
    return pl.pallas_call(
        functools.partial(_norm_matmul_kernel, n_out=len(weights)),
        grid=(n // tm,), in_specs=in_specs, out_specs=out_specs, out_shape=out_shape,
        compiler_params=_params("parallel"), name="norm_matmul",
    )(x2d, gain.reshape(1, d), *weights)


def _neighbours(p, prev_blk, next_blk, first, last):
    tt = p.shape[0]
    ridx = lax.broadcasted_iota(jnp.int32, p.shape, 0)
    prev_row = jnp.where(first, 0.0, prev_blk[HALO_ROWS - 1:HALO_ROWS, :].astype(F32))
    next_row = jnp.where(last, 0.0, next_blk[0:1, :].astype(F32))
    prev = jnp.where(ridx == 0, prev_row, pltpu.roll(p, 1, 0))
    nxt = jnp.where(ridx == tt - 1, next_row, pltpu.roll(p, tt - 1, 0))
    return prev, nxt


def _shift_matrix(tt, offsets, weight):
    t = jnp.arange(tt)[:, None]
    c = jnp.arange(tt + 2 * HALO_ROWS)[None, :]
    hit = functools.reduce(jnp.logical_or, [c == HALO_ROWS + t + d for d in offsets])
    return jnp.where(hit, weight, 0.0).astype(BF16)


def _lane_group_sum(x, width):
    assert width % LANES == 0 and x.shape[1] % width == 0
    ones = jnp.ones((LANES, LANES), BF16)
    groups = []
    for c in range(0, x.shape[1], width):
        total = _mm(x[:, c:c + LANES], ones)
        for c2 in range(c + LANES, c + width, LANES):
            total = total + _mm(x[:, c2:c2 + LANES], ones)
        groups += [total] * (width // LANES)
    return jnp.concatenate(groups, axis=1)


def _head_sum(x, pair_ones_bf16):
    blocks = [_mm(x[:, c:c + LANES], pair_ones_bf16) for c in range(0, x.shape[1], LANES)]
    return jnp.concatenate(blocks, axis=1)


def _mix0_project(x_ref, xp_ref, xn_ref, gain_ref, wrkv_ref, wlora_ref, wz_ref, wxbc_ref, wdt_ref, first, last,
                  prkv_scr, plora_scr, u_scr, z_scr, dt_scr):
    tt = x_ref.shape[0]
    dot = functools.partial(jnp.dot, preferred_element_type=F32)
    x_ext = jnp.concatenate([jnp.where(first, 0.0, xp_ref[...]), x_ref[...], jnp.where(last, 0.0, xn_ref[...])], axis=0)
    xn_ext = _rms_normed(x_ext, gain_ref[...]).astype(BF16)
    xn = xn_ext[HALO_ROWS:HALO_ROWS + tt]

    def column_block(lhs, w_ref, scr, lo, hi):
        def piece():
            scr[:, lo:hi] = dot(lhs, w_ref[:, lo:hi]).astype(scr.dtype)
        return piece

    pieces = [column_block(xn_ext, wrkv_ref, prkv_scr, c, c + RW_DIM) for c in range(0, 3 * RW_DIM, RW_DIM)]
    pieces.append(column_block(xn_ext, wlora_ref, plora_scr, 0, RW_LORA))
    pieces += [column_block(xn_ext, wxbc_ref, u_scr, c, c + SSD_DIM) for c in range(0, SSD_CONV_DIM, SSD_DIM)]
    pieces.append(column_block(xn, wz_ref, z_scr, 0, SSD_DIM))
    pieces.append(column_block(xn, wdt_ref, dt_scr, 0, LANES))
    return pieces


def _mix0_operands(prkv_scr, plora_scr, u_scr, z_scr, dt_scr,
                   savg_ref, mu_ref, mul_ref, a0_ref, a2_ref, g2_ref, kk_ref, ka_ref, w0_ref, w2_ref, rk_ref, bd_ref,
                   cw_ref, cb_ref, dtb_ref, alog_ref,
                   r_out, k_out, v_out, kkn_out, kka_out, lwf_out, lwb_out, g_out, bonus_out,
                   z_out, xbc_out, dtf_out, dtb_out, laf_out, lab_out):
    tt = z_scr.shape[0]
    dot = functools.partial(jnp.dot, preferred_element_type=F32)

    def token_shift(p_scr, m_ref, lo_col, hi_col):
        x = p_scr[HALO_ROWS:HALO_ROWS + tt, lo_col:hi_col].astype(F32)
        avg = jnp.concatenate(
            [dot(savg_ref[...], p_scr[t0:t0 + SHIFT_ROWS + 2 * HALO_ROWS, lo_col:hi_col])
             for t0 in range(0, tt, SHIFT_ROWS)], axis=0)
        return x + m_ref[:, lo_col:hi_col] * (avg - x)

    r = token_shift(prkv_scr, mu_ref, 0, RW_DIM)
    k = token_shift(prkv_scr, mu_ref, RW_DIM, 2 * RW_DIM)
    v = token_shift(prkv_scr, mu_ref, 2 * RW_DIM, 3 * RW_DIM)
    lo = token_shift(plora_scr, mul_ref, 0, RW_LORA)
    wd = lo[:, :RW_DECAY_RANK]
    ad = lo[:, RW_DECAY_RANK:RW_DECAY_RANK + RW_A_RANK]
    gd = lo[:, RW_DECAY_RANK + RW_A_RANK:]
    bd = bd_ref[...]

    a = _sigmoid(a0_ref[...] + _mm(ad, a2_ref[...]))
    g = _mm(_sigmoid(gd), g2_ref[...])
    kk = k * kk_ref[...]
    kk = kk / jnp.maximum(jnp.sqrt(_head_sum(kk * kk, bd)), 1e-12)
    k = k * (1.0 + (a - 1.0) * ka_ref[...])
    w_lora = jnp.tanh(wd)

    def log_decay(d):
        wl = w0_ref[d:d + 1, :] + _mm(w_lora, w2_ref[d])
        return -math.exp(-0.5) * _sigmoid(wl)

    r_out[...] = r.astype(r_out.dtype)
    k_out[...] = k.astype(k_out.dtype)
    v_out[...] = v.astype(v_out.dtype)
    kkn_out[...] = kk.astype(kkn_out.dtype)
    kka_out[...] = (kk * a).astype(kka_out.dtype)
    lwf_out[...] = log_decay(0)
    lwb_out[...] = log_decay(1)
    g_out[...] = g.astype(g_out.dtype)
    bonus_out[...] = (_head_sum(r * k * rk_ref[...], bd) * v).astype(bonus_out.dtype)

    z_out[...] = z_scr[...]
    u_ext = u_scr[...]
    u = u_ext[HALO_ROWS:HALO_ROWS + tt].astype(F32)
    prev, nxt = _neighbours(u, u_ext[:HALO_ROWS], u_ext[HALO_ROWS + tt:], False, False)
    conv = prev * cw_ref[0:1, :] + u * cw_ref[1:2, :] + nxt * cw_ref[2:3, :] + cb_ref[...]
    xbc_out[...] = _silu(conv).astype(xbc_out.dtype)
    dt_raw = dt_scr[...]
    dt_f = _softplus(dt_raw + dtb_ref[0:1, :])
    dt_b = _softplus(dt_raw + dtb_ref[1:2, :])
    dtf_out[...] = dt_f
    dtb_out[...] = dt_b
    laf_out[...] = dt_f * (-jnp.exp(alog_ref[0:1, :]))
    lab_out[...] = dt_b * (-jnp.exp(alog_ref[1:2, :]))


MIX0_N_PROJ_REFS = 9
MIX0_N_CONSTS = 16
MIX0_N_OUTS = 15
MIX0_N_STAGED = 5


def _mix0_in_kernel(*refs, per_seq):
    proj_refs = refs[:MIX0_N_PROJ_REFS]
    const_refs = refs[MIX0_N_PROJ_REFS:MIX0_N_PROJ_REFS + MIX0_N_CONSTS]
    out_refs = refs[MIX0_N_PROJ_REFS + MIX0_N_CONSTS:MIX0_N_PROJ_REFS + MIX0_N_CONSTS + MIX0_N_OUTS]
    staged = refs[MIX0_N_PROJ_REFS + MIX0_N_CONSTS + MIX0_N_OUTS:]
    pos = lax.rem(pl.program_id(0), per_seq)
    for piece in _mix0_project(*proj_refs, pos == 0, pos == per_seq - 1, *staged):
        piece()
    _mix0_operands(*staged, *const_refs, *out_refs)


def _mix0_in(x2d, nb, t_len, gain, weights, rwkv_consts, ssd_consts):
    n, d = x2d.shape
    tt = _tile(t_len, 512)
    assert tt % HALO_ROWS == 0
    per_seq = t_len // tt
    n_tiles = n // tt
    blocks_per_tile = tt // HALO_ROWS
    n_blocks = n // HALO_ROWS

    def full(a):
        nd = a.ndim
        return pl.BlockSpec(a.shape, lambda s: (0,) * nd)

    x_specs = [pl.BlockSpec((tt, d), lambda s: (s, 0)),
               pl.BlockSpec((HALO_ROWS, d), lambda s: (jnp.maximum(s * blocks_per_tile - 1, 0), 0)),
               pl.BlockSpec((HALO_ROWS, d), lambda s: (jnp.minimum((s + 1) * blocks_per_tile, n_blocks - 1), 0))]
    proj_consts = (gain.reshape(1, d),) + tuple(weights)
    assert tt % SHIFT_ROWS == 0
    consts = (_shift_matrix(SHIFT_ROWS, (-1, 1), 0.5),) + tuple(rwkv_consts) + tuple(ssd_consts)
    assert 3 + len(proj_consts) == MIX0_N_PROJ_REFS and len(consts) == MIX0_N_CONSTS
    wide = [(RW_DIM, ACT)] * 5 + [(RW_DIM, F32)] * 2 + [(RW_DIM, ACT)] * 2 + [(SSD_DIM, ACT), (SSD_CONV_DIM, ACT)] \
        + [(LANES, F32)] * 4
    assert len(wide) == MIX0_N_OUTS
    ext = tt + 2 * HALO_ROWS
    staged = [pltpu.VMEM((ext, 3 * RW_DIM), ACT), pltpu.VMEM((ext, RW_LORA), ACT), pltpu.VMEM((ext, SSD_CONV_DIM), ACT),
              pltpu.VMEM((tt, SSD_DIM), ACT), pltpu.VMEM((tt, LANES), F32)]
    assert len(staged) == MIX0_N_STAGED
    return pl.pallas_call(
        functools.partial(_mix0_in_kernel, per_seq=per_seq),
        grid=(n_tiles,),
        in_specs=x_specs + [full(a) for a in proj_consts + consts],
        out_specs=[pl.BlockSpec((tt, w), lambda s: (s, 0)) for w, _ in wide],
        out_shape=[jax.ShapeDtypeStruct((n, w), dt) for w, dt in wide],
        scratch_shapes=staged,
        compiler_params=_params("parallel"), name="mix0_in",
    )(x2d, x2d, x2d, *proj_consts, *consts)


def _pair_blockdiag(y):
    lane = lax.broadcasted_iota(jnp.int32, y.shape, 1)
    top = jnp.where(lane < RW_HEAD_DIM, y, 0.0).astype(BF16)
    bottom = jnp.where(lane >= RW_HEAD_DIM, y, 0.0).astype(BF16)
    return jnp.concatenate([top, bottom], axis=0)


def _pair_diag_blocks(full):
    lane = lax.broadcasted_iota(jnp.int32, (RW_HEAD_DIM, LANES), 1)
    return jnp.where(lane < RW_HEAD_DIM, full[:RW_HEAD_DIM], full[RW_HEAD_DIM:])


def _unit_lower_inverses(n_strict_list, eye):
    size = n_strict_list[0].shape[0]
    levels = int(math.log2(size)) - 1
    accs = [eye + n for n in n_strict_list]
    powers = [_mm(n, _pair_blockdiag(n)) for n in n_strict_list]
    for level in range(levels):
        power_bds = [_pair_blockdiag(p) for p in powers]
        if level == levels - 1:
            return [a + _mm(a, b) for a, b in zip(accs, power_bds)]
        both = [_mm(jnp.concatenate([a, p], axis=0), b) for a, p, b in zip(accs, powers, power_bds)]
        accs = [a + ab[:size] for a, ab in zip(accs, both)]
        powers = [ab[size:] for ab in both]


def _rwkv_scan_kernel(r_ref, k_ref, v_ref, kk_ref, kka_ref, lw_ref, o_ref, h_ref, *, reverse):
    @pl.when(pl.program_id(1) == 0)
    def _():
        h_ref[...] = jnp.zeros_like(h_ref)

    sub_rows = RW_BLOCK_CHUNKS * RW_CHUNK
    n_sub = r_ref.shape[0] // sub_rows

    def body(i, carry):
        blk = (n_sub - 1 - i) if reverse else i
        _rwkv_scan_block(r_ref, k_ref, v_ref, kk_ref, kka_ref, lw_ref, o_ref, h_ref,
                         pl.multiple_of(blk * sub_rows, sub_rows), reverse)
        return carry

    lax.fori_loop(0, n_sub, body, 0)


def _rwkv_scan_block(r_ref, k_ref, v_ref, kk_ref, kka_ref, lw_ref, o_ref, h_ref, base, reverse):
    chunk = RW_CHUNK
    n_chunks = RW_BLOCK_CHUNKS
    incl_bf = _order_masks(chunk, reverse)[0].astype(BF16)
    row = lax.broadcasted_iota(jnp.int32, (chunk, LANES), 0)
    pos = lax.broadcasted_iota(jnp.int32, (chunk, LANES), 1) % RW_HEAD_DIM
    incl, strict = (pos >= row, pos > row) if reverse else (pos <= row, pos < row)
    eye = (pos == row).astype(F32)
    end = 0 if reverse else chunk - 1
    chunk_order = list(range(n_chunks - 1, -1, -1)) if reverse else list(range(n_chunks))

    items = []
    for ci in chunk_order:
        rows = pl.ds(base + ci * chunk, chunk)
        lw = lw_ref[rows, :]
        cum = _mm_exact_lhs(incl_bf, lw, terms=2)
        cum_end = cum[end:end + 1, :]
        e_neg = jnp.exp(-cum)
        e_tail = jnp.exp(cum_end - cum)
        p_end = jnp.exp(cum_end)
        k = k_ref[rows, :].astype(F32)
        kka = kka_ref[rows, :].astype(F32)
        at_all = -kk_ref[rows, :].astype(F32) * jnp.exp(cum - lw)
        bt_all = kka * e_neg
        kt_all = k * e_neg
        rt_all = r_ref[rows, :].astype(F32) * jnp.exp(cum)
        bh_all = kka * e_tail
        kh_all = k * e_tail
        v = v_ref[rows, :].astype(F32)
        for j in range(RW_HEAD_PAIRS):
            sl = slice(j * LANES, (j + 1) * LANES)
            items.append(dict(ci=ci, j=j, at=at_all[:, sl], bt=bt_all[:, sl], kt=kt_all[:, sl], rt=rt_all[:, sl],
                              bh=bh_all[:, sl], kh=kh_all[:, sl], v=v[:, sl], p_end=p_end[:, sl]))

    for it in items:
        lhs = jnp.concatenate([it["at"], it["rt"]], axis=0)
        both = _mm_nt(lhs, jnp.concatenate([_pair_blockdiag(it["bt"]), _pair_blockdiag(it["kt"])], axis=0))
        it["a_ab"] = jnp.where(strict, both[:chunk, :LANES], 0.0)
        it["a_rb"] = jnp.where(incl, both[chunk:, :LANES], 0.0)
        it["a_ak"] = jnp.where(strict, both[:chunk, LANES:], 0.0)
        it["a_rk"] = jnp.where(incl, both[chunk:, LANES:], 0.0)
    t_invs = _unit_lower_inverses([it["a_ab"] for it in items], eye)
    for it in items:
        akv_rkv = _mm(jnp.concatenate([it["a_ak"], it["a_rk"]], axis=0), _pair_blockdiag(it["v"]))
        it["x1"] = akv_rkv[:chunk]
        it["rk_v"] = akv_rkv[chunk:]
    for it, t_inv in zip(items, t_invs):
        wu_u0 = _mm(t_inv, jnp.concatenate([_pair_blockdiag(it["at"]), _pair_blockdiag(it["x1"])], axis=1))
        it["w_u"] = wu_u0[:, :LANES]
        it["u0"] = wu_u0[:, LANES:]
    for it in items:
        rb_wu_u0 = _mm(it["a_rb"],
                       jnp.concatenate([_pair_blockdiag(it["w_u"]), _pair_blockdiag(it["u0"])], axis=1))
        w_o = it["rt"] + rb_wu_u0[:, :LANES]
        g_mat = eye * it["p_end"] + _pair_diag_blocks(_mm_tn(it["bh"], it["w_u"]))
        it["wo_g"] = jnp.concatenate([w_o, g_mat], axis=0)
        it["o0"] = rb_wu_u0[:, LANES:] + it["rk_v"]
        it["h0"] = _pair_diag_blocks(_mm_tn(jnp.concatenate([it["bh"], it["kh"]], axis=0),
                                            jnp.concatenate([it["u0"], it["v"]], axis=0)))

    states = [h_ref[j] for j in range(RW_HEAD_PAIRS)]
    for it in items:
        j = it["j"]
        res = _mm(it["wo_g"], _pair_blockdiag(states[j]))
        rows = pl.ds(base + it["ci"] * chunk, chunk)
        o_ref[rows, j * LANES:(j + 1) * LANES] = (res[:chunk] + it["o0"]).astype(o_ref.dtype)
        states[j] = res[chunk:] + it["h0"]
    for j in range(RW_HEAD_PAIRS):
        h_ref[j] = states[j]


def _rwkv_scan(r, k, v, kk, kka, lw, nb, t_len, reverse):
    n = r.shape[0]
    tb = _scan_rows(t_len, RW_BLOCK_CHUNKS * RW_CHUNK)
    nc = t_len // tb

    def idx(b, c):
        return (b * nc + (nc - 1 - c if reverse else c), 0)

    spec = pl.BlockSpec((tb, RW_DIM), idx)
    return pl.pallas_call(
        functools.partial(_rwkv_scan_kernel, reverse=reverse),
        grid=(nb, nc), in_specs=[spec] * 6, out_specs=spec,
        out_shape=jax.ShapeDtypeStruct((n, RW_DIM), ACT),
        scratch_shapes=[pltpu.VMEM((RW_HEAD_PAIRS, RW_HEAD_DIM, LANES), F32)],
        compiler_params=_params("parallel", "arbitrary"), name="rwkv_scan_bwd" if reverse else "rwkv_scan_fwd",
    )(r, k, v, kk, kka, lw)


def _ssd_scan_kernel(xbc_ref, dt_ref, la_ref, ex64_ref, y_ref, s_ref, *, reverse):
    @pl.when(pl.program_id(1) == 0)
    def _():
        s_ref[...] = jnp.zeros_like(s_ref)

    sub_rows = SSD_BLOCK_CHUNKS * SSD_CHUNK
    n_sub = xbc_ref.shape[0] // sub_rows

    def body(i, carry):
        blk = (n_sub - 1 - i) if reverse else i
        _ssd_scan_block(xbc_ref, dt_ref, la_ref, ex64_ref, y_ref, s_ref,
                        pl.multiple_of(blk * sub_rows, sub_rows), reverse)
        return carry

    lax.fori_loop(0, n_sub, body, 0)


def _ssd_scan_block(xbc_ref, dt_ref, la_ref, ex64_ref, y_ref, s_ref, base, reverse):
    chunk = SSD_CHUNK
    n_chunks = SSD_BLOCK_CHUNKS
    incl, _ = _order_masks(chunk, reverse)
    incl_bf = incl.astype(BF16)
    end = 0 if reverse else chunk - 1
    gn = SSD_GROUPS * SSD_STATE
    gw = SSD_GROUP_HEADS * SSD_HEAD_DIM
    chunk_order = list(range(n_chunks - 1, -1, -1)) if reverse else list(range(n_chunks))
    ex64 = ex64_ref[...]

    pre = []
    for ci in chunk_order:
        rows = pl.ds(base + ci * chunk, chunk)
        la = la_ref[rows, :]
        cum = _mm_exact_lhs(incl_bf, la, terms=2)
        cum_rows = cum.T
        cum_end = cum[end:end + 1, :]
        ecum64, etail64, dt64 = _mm_exact_rhs_many(
            [jnp.exp(cum), jnp.exp(cum_end - cum), dt_ref[rows, :]], ex64, terms=2)
        xdt = xbc_ref[rows, :SSD_DIM].astype(F32) * dt64
        pre.append(dict(rows=rows, cum=cum, cum_rows=cum_rows, ecum64=ecum64, eend64=ecum64[end:end + 1, :],
                        xdt=xdt, xtail=xdt * etail64,
                        bm=[xbc_ref[rows, SSD_DIM + g * SSD_STATE:SSD_DIM + (g + 1) * SSD_STATE]
                            for g in range(SSD_GROUPS)],
                        cm=[xbc_ref[rows, SSD_DIM + gn + g * SSD_STATE:SSD_DIM + gn + (g + 1) * SSD_STATE]
                            for g in range(SSD_GROUPS)]))
    for p in pre:
        p["scores"] = [_mm_nt(p["cm"][g], p["bm"][g]) for g in range(SSD_GROUPS)]
        p["inject"] = [_mm_tn(p["bm"][g], p["xtail"][:, g * gw:(g + 1) * gw]) for g in range(SSD_GROUPS)]
    for p in pre:
        weights = []
        for h in range(SSD_HEADS):
            diff = p["cum"][:, h:h + 1] - p["cum_rows"][h:h + 1, :]
            decay = jnp.exp(jnp.where(incl, diff, NEG_BIG))
            weights.append(p["scores"][h // SSD_GROUP_HEADS] * decay)
        p["ydiag"] = [_mm(jnp.concatenate([weights[h], weights[h + 1]], axis=1),
                          _pair_blockdiag(p["xdt"][:, h * SSD_HEAD_DIM:(h + 2) * SSD_HEAD_DIM]))
                      for h in range(0, SSD_HEADS, 2)]

    states = [s_ref[g] for g in range(SSD_GROUPS)]
    for p in pre:
        for g in range(SSD_GROUPS):
            y_off = _mm(p["cm"][g], states[g]) * p["ecum64"][:, g * gw:(g + 1) * gw]
            for jp in range(SSD_GROUP_HEADS // 2):
                pair = g * (SSD_GROUP_HEADS // 2) + jp
                y_ref[p["rows"], pair * LANES:(pair + 1) * LANES] = (
                    p["ydiag"][pair] + y_off[:, jp * LANES:(jp + 1) * LANES]).astype(y_ref.dtype)
            states[g] = states[g] * p["eend64"][:, g * gw:(g + 1) * gw] + p["inject"][g]
    for g in range(SSD_GROUPS):
        s_ref[g] = states[g]


def _ssd_scan(xbc, dt, la, nb, t_len, reverse):
    n = xbc.shape[0]
    tb = _scan_rows(t_len, SSD_BLOCK_CHUNKS * SSD_CHUNK)
    assert SSD_CHUNK == LANES
    nc = t_len // tb
    lane_head = jnp.arange(LANES)[:, None]
    ex64 = (lane_head == jnp.arange(SSD_DIM)[None, :] // SSD_HEAD_DIM).astype(BF16)

    def idx(b, c):
        return (b * nc + (nc - 1 - c if reverse else c), 0)

    def full(a):
        return pl.BlockSpec(a.shape, lambda b, c: (0, 0))

    return pl.pallas_call(
        functools.partial(_ssd_scan_kernel, reverse=reverse),
        grid=(nb, nc),
        in_specs=[pl.BlockSpec((tb, SSD_CONV_DIM), idx), pl.BlockSpec((tb, LANES), idx),
                  pl.BlockSpec((tb, LANES), idx), full(ex64)],
        out_specs=pl.BlockSpec((tb, SSD_DIM), idx),
        out_shape=jax.ShapeDtypeStruct((n, SSD_DIM), ACT),
        scratch_shapes=[pltpu.VMEM((SSD_GROUPS, SSD_STATE, SSD_GROUP_HEADS * SSD_HEAD_DIM), F32)],
        compiler_params=_params("parallel", "arbitrary"), name="ssd_scan_bwd" if reverse else "ssd_scan_fwd",
    )(xbc, dt, la, ex64)


def _mix0_out_kernel(x_ref, of_ref, ob_ref, bonus_ref, g_ref, yf_ref, yb_ref, xbc_ref, z_ref,
                     gnw_ref, gnb_ref, bd_ref, dsk_ref, nw_ref, wo_rw_ref, wo_ssd_ref, out_ref):
    bd = bd_ref[...]
    o = of_ref[...].astype(F32) + ob_ref[...].astype(F32)
    inv_n = 1.0 / RW_HEAD_DIM
    mu = _head_sum(o, bd) * inv_n
    oc = o - mu
    var = _head_sum(oc * oc, bd) * inv_n
    o = oc * lax.rsqrt(var + RW_GN_EPS) * gnw_ref[...] + gnb_ref[...]
    o_rw = (o + bonus_ref[...].astype(F32)) * g_ref[...].astype(F32)

    y = yf_ref[...].astype(F32) + yb_ref[...].astype(F32) + dsk_ref[...] * xbc_ref[...].astype(F32)
    y = y * _silu(z_ref[...].astype(F32))
    gw = SSD_DIM // SSD_GROUPS
    parts = []
    for g in range(SSD_GROUPS):
        yg = y[:, g * gw:(g + 1) * gw]
        parts.append(yg * lax.rsqrt(jnp.mean(yg * yg, axis=-1, keepdims=True) + NORM_EPS))
    o_ssd = jnp.concatenate(parts, axis=-1) * nw_ref[...]
    out_ref[...] = x_ref[...] + _mm(o_rw, wo_rw_ref[...]) + _mm(o_ssd, wo_ssd_ref[...])


def _mix0_out(x2d, o_f, o_b, bonus, g, y_f, y_b, xbc, z, gn_w, gn_b, blockdiag, d_skip, norm_w, wo_rw, wo_ssd):
    n = x2d.shape[0]
    tm = _tile(n, 512)

    def row(width):
        return pl.BlockSpec((tm, width), lambda i: (i, 0))

    def full(a):
        return pl.BlockSpec(a.shape, lambda i: (0, 0))

    consts = (gn_w, gn_b, blockdiag, d_skip, norm_w, wo_rw, wo_ssd)
    return pl.pallas_call(
        _mix0_out_kernel,
        grid=(n // tm,),
        in_specs=[row(D_MODEL)] + [row(RW_DIM)] * 6 + [row(SSD_DIM), row(SSD_DIM)] + [full(a) for a in consts],
        out_specs=row(D_MODEL),
        out_shape=jax.ShapeDtypeStruct((n, D_MODEL), F32),
        compiler_params=_params("parallel"), name="mix0_out",
    )(x2d, o_f, o_b, bonus, g, y_f, y_b, xbc, z, *consts)


def _hgrn_lower_bound(lb_ref, layer):
    x = lb_ref[...]
    m = jnp.max(x, axis=0, keepdims=True)
    e = jnp.exp(x - m)
    s = e / jnp.sum(e, axis=0, keepdims=True)
    lb = jnp.zeros_like(s[0:1, :])
    for i in range(1, layer + 1):
        lb = lb + s[i:i + 1, :]
    return lb


def _hgrn_in_kernel(x_ref, g_ref, lb_ref, wq_ref, wff_ref, wfb_ref, wi_ref, wg_ref,
                    q_out, lff_out, lfb_out, kgf_out, kgb_out, i_out, g_out, *, layer):
    xn = _rms_normed(x_ref[...], g_ref[...]).astype(BF16)
    lb = _hgrn_lower_bound(lb_ref, layer)
    dot = functools.partial(jnp.dot, preferred_element_type=F32)
    width = x_ref.shape[1]
    jobs = []
    for c in range(0, width, PROJ_COLS):
        cols = slice(c, c + PROJ_COLS)
        jobs.append((wff_ref, cols, ("gate", lff_out, kgf_out)))
        jobs.append((wq_ref, cols, ("plain", q_out)))
        jobs.append((wfb_ref, cols, ("gate", lfb_out, kgb_out)))
        jobs.append((wi_ref, cols, ("plain", i_out)))
        jobs.append((wg_ref, cols, ("plain", g_out)))

    def finish(acc, cols, how):
        if how[0] == "gate":
            f = lb[:, cols] + (1.0 - lb[:, cols]) * _sigmoid(acc)
            how[1][:, cols] = jnp.log(f)
            how[2][:, cols] = (1.0 - f).astype(how[2].dtype)
        else:
            how[1][:, cols] = acc.astype(how[1].dtype)

    pending = None
    for w_ref, cols, how in jobs:
        acc = dot(xn, w_ref[:, cols])
        if pending is not None:
            finish(*pending)
        pending = (acc, cols, how)
    finish(*pending)


def _hgrn_in(x2d, gain, hg_lb, weights, layer):
    n, d = x2d.shape
    tm = _tile(n, 512)
    row = pl.BlockSpec((tm, d), lambda i: (i, 0))
    in_specs = [row, pl.BlockSpec((1, d), lambda i: (0, 0)), pl.BlockSpec(hg_lb.shape, lambda i: (0, 0))]
    in_specs += [pl.BlockSpec(w.shape, lambda i: (0, 0)) for w in weights]
    dtypes = (ACT, F32, F32, ACT, ACT, ACT, ACT)
    return pl.pallas_call(
        functools.partial(_hgrn_in_kernel, layer=layer),
        grid=(n // tm,), in_specs=in_specs, out_specs=[row] * len(dtypes),
        out_shape=[jax.ShapeDtypeStruct((n, d), dt) for dt in dtypes],
        compiler_params=_params("parallel"), name="hgrn_in",
    )(x2d, gain.reshape(1, d), hg_lb, *weights)


def _hgrn_chunk(q, lf, kg, vv, state_t, reverse):
    chunk = q.shape[0]
    n_sub = chunk // HG_SUB
    incl, _ = _order_masks(chunk, reverse)
    end = 0 if reverse else chunk - 1
    s_idx = lax.broadcasted_iota(jnp.int32, (HG_SUB, 1), 0)

    bcum = _mm_exact_lhs(incl.astype(BF16), lf)
    b_end = bcum[end:end + 1, :]
    carried = _mm_nt(q * jnp.exp(bcum), state_t)

    order = list(range(n_sub - 1, -1, -1)) if reverse else list(range(n_sub))
    outs = [None] * n_sub
    for pos, sb in enumerate(order):
        rows = slice(sb * HG_SUB, (sb + 1) * HG_SUB)
        q_s, k_s, b_s, v_s = q[rows], kg[rows], bcum[rows], vv[rows]
        acc = carried[rows]
        if pos > 0:
            prev_sb = order[pos - 1]
            edge = prev_sb * HG_SUB if reverse else prev_sb * HG_SUB + HG_SUB - 1
            b_edge = bcum[edge:edge + 1, :]
            src = slice((sb + 1) * HG_SUB, chunk) if reverse else slice(0, sb * HG_SUB)
            q_hat = q_s * jnp.exp(b_s - b_edge)
            k_hat = kg[src] * jnp.exp(b_edge - bcum[src])
            acc = acc + _mm(_mm_nt(q_hat, k_hat), vv[src])
        rows_out = []
        for l in range(HG_SUB):
            w = jnp.exp(jnp.minimum(b_s[l:l + 1, :] - b_s, 0.0)) * k_s * q_s[l:l + 1, :]
            att = jnp.sum(w, axis=-1, keepdims=True)
            att = jnp.where((s_idx >= l) if reverse else (s_idx <= l), att, 0.0)
            rows_out.append(jnp.sum(att * v_s, axis=0, keepdims=True))
        outs[sb] = acc + jnp.concatenate(rows_out, axis=0)

    k_tail = kg * jnp.exp(b_end - bcum)
    new_state_t = state_t * jnp.exp(b_end) + _mm_tn(vv, k_tail)
    return jnp.concatenate(outs, axis=0), new_state_t


def _hgrn_scan_kernel(q_ref, lf_ref, kg_ref, i_ref, lb_ref, o_ref, s_ref, hq_ref, hf_ref, hk_ref, hv_ref, ho_ref, *,
                      reverse, layer):
    @pl.when(pl.program_id(1) == 0)
    def _():
        s_ref[...] = jnp.zeros_like(s_ref)

    sub_rows = HG_BLOCK_CHUNKS * HG_CHUNK
    n_sub = q_ref.shape[0] // sub_rows

    def body(i, carry):
        blk = (n_sub - 1 - i) if reverse else i
        _hgrn_scan_block(q_ref, lf_ref, kg_ref, i_ref, lb_ref, o_ref, s_ref, hq_ref, hf_ref, hk_ref, hv_ref, ho_ref,
                         pl.multiple_of(blk * sub_rows, sub_rows), reverse, layer)
        return carry

    lax.fori_loop(0, n_sub, body, 0)


def _hgrn_scan_block(q_ref, lf_ref, kg_ref, i_ref, lb_ref, o_ref, s_ref, hq_ref, hf_ref, hk_ref, hv_ref, ho_ref,
                     base, reverse, layer):
    chunk = HG_CHUNK
    n_chunks = HG_BLOCK_CHUNKS
    incl, _ = _order_masks(chunk, reverse)
    incl_bf = incl.astype(BF16)
    end = 0 if reverse else chunk - 1
    chunk_order = list(range(n_chunks - 1, -1, -1)) if reverse else list(range(n_chunks))

    def head_cols(h):
        return slice(h * HG_KEY_DIM, (h + 1) * HG_KEY_DIM)

    pre = []
    worst = None
    for ci in chunk_order:
        rows = pl.ds(base + ci * chunk, chunk)
        bcum = _mm_exact_lhs(incl_bf, lf_ref[rows, :], terms=2)
        b_end = bcum[end:end + 1, :]
        worst = b_end if worst is None else jnp.minimum(worst, b_end)
        pre.append((ci, rows, bcum, b_end))
    lb_min = jnp.min(_hgrn_lower_bound(lb_ref, layer))
    flag = lax.cond(lb_min >= math.exp(-HG_SAFE_LOG_RANGE / HG_CHUNK),
                    lambda: jnp.int32(1),
                    lambda: (jnp.min(worst) >= -HG_SAFE_LOG_RANGE).astype(jnp.int32))
    safe = flag == 1

    @pl.when(safe)
    def _():
        items = []
        for ci, rows, bcum, b_end in pre:
            kg = kg_ref[rows, :].astype(F32)
            e_neg = jnp.exp(-bcum)
            e_end = jnp.exp(b_end)
            qt_all = q_ref[rows, :].astype(F32) * jnp.exp(bcum)
            kt_all = kg * e_neg
            ktail_all = kt_all * e_end
            v_all = i_ref[rows, :].astype(F32)
            for h in range(HG_HEADS):
                c = head_cols(h)
                items.append(dict(rows=rows, h=h, qt=qt_all[:, c], kt=kt_all[:, c], ktail=ktail_all[:, c],
                                  v=v_all[:, c], e_end=e_end[:, c]))
        for it in items:
            it["attn"] = jnp.where(incl, _mm_nt(it["qt"], it["kt"]), 0.0)
        for it in items:
            it["local"] = _mm(it["attn"], it["v"])
            it["inject"] = _mm_tn(it["v"], it["ktail"])
        states = [s_ref[h] for h in range(HG_HEADS)]
        for it in items:
            h = it["h"]
            o_ref[it["rows"], head_cols(h)] = (it["local"] + _mm_nt(it["qt"], states[h])).astype(o_ref.dtype)
            states[h] = states[h] * it["e_end"] + it["inject"]
        for h in range(HG_HEADS):
            s_ref[h] = states[h]

    @pl.when(jnp.logical_not(safe))
    def _():
        def head_body(h, carry):
            out, new_state = _hgrn_chunk(hq_ref[h], hf_ref[h], hk_ref[h], hv_ref[h], s_ref[h], reverse)
            ho_ref[h] = out
            s_ref[h] = new_state
            return carry

        def chunk_body(i, carry):
            ci = (n_chunks - 1 - i) if reverse else i
            rows = pl.ds(pl.multiple_of(base + ci * chunk, chunk), chunk)
            for h in range(HG_HEADS):
                hq_ref[h] = q_ref[rows, head_cols(h)].astype(F32)
                hf_ref[h] = lf_ref[rows, head_cols(h)]
                hk_ref[h] = kg_ref[rows, head_cols(h)].astype(F32)
                hv_ref[h] = i_ref[rows, head_cols(h)].astype(F32)
            lax.fori_loop(0, HG_HEADS, head_body, 0)
            for h in range(HG_HEADS):
                o_ref[rows, head_cols(h)] = ho_ref[h].astype(o_ref.dtype)
            return carry

        lax.fori_loop(0, n_chunks, chunk_body, 0)


def _hgrn_scan(q, lf, kg, i_val, hg_lb, nb, t_len, reverse, layer):
    n = q.shape[0]
    tb = _scan_rows(t_len, HG_BLOCK_CHUNKS * HG_CHUNK)
    n_outer = t_len // tb

    def idx(b, c):
        return (b * n_outer + (n_outer - 1 - c if reverse else c), 0)

    spec = pl.BlockSpec((tb, HG_KEY_WIDTH), idx)
    head_buf = pltpu.VMEM((HG_HEADS, HG_CHUNK, HG_KEY_DIM), F32)
    return pl.pallas_call(
        functools.partial(_hgrn_scan_kernel, reverse=reverse, layer=layer),
        grid=(nb, n_outer),
        in_specs=[spec, spec, spec, spec, pl.BlockSpec(hg_lb.shape, lambda b, c: (0, 0))],
        out_specs=spec,
        out_shape=jax.ShapeDtypeStruct((n, HG_VAL_WIDTH), ACT),
        scratch_shapes=[pltpu.VMEM((HG_HEADS, HG_VAL_DIM, HG_KEY_DIM), F32)] + [head_buf] * 5,
        compiler_params=_params("parallel", "arbitrary"),
        name="hgrn_scan_bwd" if reverse else "hgrn_scan_fwd",
    )(q, lf, kg, i_val, hg_lb)


def _hgrn_out_kernel(x_ref, of_ref, ob_ref, g_ref, nw_ref, wo_ref, out_ref):
    o = of_ref[...].astype(F32) + ob_ref[...].astype(F32)
    mean_sq = _lane_group_sum(o * o, HG_VAL_DIM) * (1.0 / HG_VAL_DIM)
    o = o * lax.rsqrt(mean_sq + NORM_EPS) * nw_ref[...] * _silu(g_ref[...].astype(F32))
    out_ref[...] = x_ref[...] + _mm(o, wo_ref[...])


def _hgrn_out(x2d, o_f, o_b, g, norm_w, w_out):
    n = x2d.shape[0]
    tm = _tile(n, 512)
    row = pl.BlockSpec((tm, D_MODEL), lambda i: (i, 0))
    return pl.pallas_call(
        _hgrn_out_kernel,
        grid=(n // tm,),
        in_specs=[row] * 4 + [pl.BlockSpec((1, HG_VAL_WIDTH), lambda i: (0, 0)),
                              pl.BlockSpec(w_out.shape, lambda i: (0, 0))],
        out_specs=row,
        out_shape=jax.ShapeDtypeStruct((n, D_MODEL), F32),
        compiler_params=_params("parallel"), name="hgrn_out",
    )(x2d, o_f, o_b, g, norm_w, w_out)


def _xattn_kernel(x_ref, g_ref, wq_ref, kv_ref, wo_ref, out_ref):
    x = x_ref[...]
    q = _mm(_rms_normed(x, g_ref[...]), wq_ref[...])
    scale = XA_HEAD_DIM ** -0.5
    scores = [_mm_nt(q[:, h * XA_HEAD_DIM:(h + 1) * XA_HEAD_DIM], kv_ref[:, h * XA_HEAD_DIM:(h + 1) * XA_HEAD_DIM])
              * scale for h in range(XA_HEADS)]
    probs = []
    for s in scores:
        p = jnp.exp(s - jnp.max(s, axis=-1, keepdims=True))
        probs.append(p * (1.0 / jnp.sum(p, axis=-1, keepdims=True)))
    heads = [_mm(p, kv_ref[:, D_MODEL + h * XA_HEAD_DIM:D_MODEL + (h + 1) * XA_HEAD_DIM])
             for h, p in enumerate(probs)]
    out_ref[...] = x + _mm(jnp.concatenate(heads, axis=-1), wo_ref[...])


def _xattn(x2d, gain, wq, kv, wo, nb, t_len, mem_len):
    n = x2d.shape[0]
    tm = _tile(t_len, 512)
    per_seq = t_len // tm
    row = pl.BlockSpec((tm, D_MODEL), lambda b, j: (b * per_seq + j, 0))
    return pl.pallas_call(
        _xattn_kernel,
        grid=(nb, per_seq),
        in_specs=[row, pl.BlockSpec((1, D_MODEL), lambda b, j: (0, 0)),
                  pl.BlockSpec(wq.shape, lambda b, j: (0, 0)),
                  pl.BlockSpec((mem_len, 2 * D_MODEL), lambda b, j: (b, 0)),
                  pl.BlockSpec(wo.shape, lambda b, j: (0, 0))],
        out_specs=row,
        out_shape=jax.ShapeDtypeStruct((n, D_MODEL), F32),
        compiler_params=_params("parallel", "parallel"), name="xattn",
    )(x2d, gain.reshape(1, D_MODEL), wq, kv, wo)


def _ffn_kernel(x_ref, g_ref, wi_ref, wo_ref, fg_ref, out_ref, *, final_norm):
    x = x_ref[...]
    gu = _mm(_rms_normed(x, g_ref[...]), wi_ref[...])
    act = _silu(gu[:, :FFN_DIM]) * gu[:, FFN_DIM:]
    y = x + _mm(act, wo_ref[...])
    if final_norm:
        y = _rms_normed(y, fg_ref[...])
    out_ref[...] = y


def _ffn(x2d, gain, w_in, w_out, final_gain, final_norm):
    n = x2d.shape[0]
    tm = _tile(n, 512)
    row = pl.BlockSpec((tm, D_MODEL), lambda i: (i, 0))
    vec = pl.BlockSpec((1, D_MODEL), lambda i: (0, 0))
    resident = pl.Buffered(1)
    return pl.pallas_call(
        functools.partial(_ffn_kernel, final_norm=final_norm),
        grid=(n // tm,),
        in_specs=[row, vec, pl.BlockSpec(w_in.shape, lambda i: (0, 0), pipeline_mode=resident),
                  pl.BlockSpec(w_out.shape, lambda i: (0, 0), pipeline_mode=resident), vec],
        out_specs=row,
        out_shape=jax.ShapeDtypeStruct((n, D_MODEL), F32),
        compiler_params=_params("parallel"), name="ffn",
    )(x2d, gain.reshape(1, D_MODEL), w_in, w_out, final_gain.reshape(1, D_MODEL))


def _pad_lanes(a, width=LANES):
    return jnp.pad(a, [(0, 0)] * (a.ndim - 1) + [(0, width - a.shape[-1])])


def _rwkv_ssd_layer(x2d, nb, t_len, norm_g, w_in, w_out, rw_mu, rw_w0, rw_w2, rw_a0, rw_a2, rw_g2, rw_k_k, rw_k_a,
                    rw_r_k, rw_gn_w, rw_gn_b, conv_w, conv_b, dt_bias, a_log, d_skip, ssd_norm_w):
    o_lora = 3 * RW_DIM
    o_z = RW_PROJ
    o_xbc = o_z + SSD_DIM
    o_dt = o_xbc + SSD_CONV_DIM
    wb = w_in.astype(BF16)
    weights = [wb[:, :o_lora], wb[:, o_lora:o_z], wb[:, o_z:o_xbc], wb[:, o_xbc:o_dt], _pad_lanes(wb[:, o_dt:])]
    head_id = jnp.arange(LANES) // RW_HEAD_DIM
    blockdiag = (head_id[:, None] == head_id[None, :]).astype(BF16)
    row = lambda a: a.reshape(1, -1)
    rwkv_consts = (row(rw_mu[:o_lora]), row(rw_mu[o_lora:]), row(rw_a0), rw_a2.astype(BF16), rw_g2.astype(BF16),
                   row(rw_k_k), row(rw_k_a), rw_w0, rw_w2.astype(BF16), row(rw_r_k), blockdiag)
    ssd_consts = (conv_w, row(conv_b), _pad_lanes(dt_bias), _pad_lanes(a_log))
    (r, k, v, kk, kka, lw_f, lw_b, g, bonus, p_z, xbc, dt_f, dt_b, la_f, la_b) = _mix0_in(
        x2d, nb, t_len, norm_g, weights, rwkv_consts, ssd_consts)
    o_f = _rwkv_scan(r, k, v, kk, kka, lw_f, nb, t_len, reverse=False)
    o_b = _rwkv_scan(r, k, v, kk, kka, lw_b, nb, t_len, reverse=True)
    y_f = _ssd_scan(xbc, dt_f, la_f, nb, t_len, reverse=False)
    y_b = _ssd_scan(xbc, dt_b, la_b, nb, t_len, reverse=True)

    wo = w_out.astype(BF16)
    d_skip_lanes = jnp.repeat(d_skip, SSD_HEAD_DIM).reshape(1, SSD_DIM)
    return _mix0_out(x2d, o_f, o_b, bonus, g, y_f, y_b, xbc, p_z, row(rw_gn_w), row(rw_gn_b), blockdiag,
                     d_skip_lanes, row(ssd_norm_w), wo[:RW_DIM], wo[RW_DIM:])


def _hgrn_layer(x2d, nb, t_len, norm_g, w_in, w_out, norm_w, hg_lb, layer):
    kw, vw = HG_KEY_WIDTH, HG_VAL_WIDTH
    wb = w_in.astype(BF16)
    weights = [wb[:, :kw], wb[:, kw:2 * kw], wb[:, 2 * kw:3 * kw], wb[:, 3 * kw:3 * kw + vw], wb[:, 3 * kw + vw:]]
    q, lf_f, lf_b, kg_f, kg_b, i_val, g = _hgrn_in(x2d, norm_g, hg_lb, weights, layer)
    o_f = _hgrn_scan(q, lf_f, kg_f, i_val, hg_lb, nb, t_len, reverse=False, layer=layer)
    o_b = _hgrn_scan(q, lf_b, kg_b, i_val, hg_lb, nb, t_len, reverse=True, layer=layer)
    return _hgrn_out(x2d, o_f, o_b, g, norm_w.reshape(1, vw), w_out.astype(BF16))


def kernel(x, mem, mix_norm, ab_w_in, ab_w_out, rw_mu, rw_w0, rw_w2, rw_a0, rw_a2, rw_g2, rw_k_k, rw_k_a, rw_r_k, rw_gn_w, rw_gn_b, ssd_conv_w, ssd_conv_b, ssd_dt_bias, ssd_a_log, ssd_d, ssd_norm_w, hg_w_in, hg_w_out, hg_norm_w, hg_lb, xa_norm, mem_norm, xa_wq, xa_wkv, xa_wo, ffn_norm, ffn_w_in, ffn_w_out, final_norm):
    nb, t_len, d = x.shape
    mem_len = mem.shape[1]
    depth = mix_norm.shape[0]
    x2d = x.reshape(nb * t_len, d)
    mem2d = mem.reshape(nb * mem_len, d)
    for layer in range(depth):
        if layer % 2 == 0:
            e = layer // 2
            x2d = _rwkv_ssd_layer(x2d, nb, t_len, mix_norm[layer], ab_w_in[e], ab_w_out[e], rw_mu[e], rw_w0[e],
                                  rw_w2[e], rw_a0[e], rw_a2[e], rw_g2[e], rw_k_k[e], rw_k_a[e],
                                  rw_r_k[e].reshape(-1), rw_gn_w[e], rw_gn_b[e], ssd_conv_w[e], ssd_conv_b[e],
                                  ssd_dt_bias[e], ssd_a_log[e], ssd_d[e], ssd_norm_w[e])
        else:
            o = layer // 2
            x2d = _hgrn_layer(x2d, nb, t_len, mix_norm[layer], hg_w_in[o], hg_w_out[o], hg_norm_w[o], hg_lb, layer)
        (kv,) = _norm_matmul(mem2d, mem_norm[layer], [xa_wkv[layer].astype(BF16)], [BF16])
        x2d = _xattn(x2d, xa_norm[layer], xa_wq[layer].astype(BF16), kv, xa_wo[layer].astype(BF16),
                     nb, t_len, mem_len)
        x2d = _ffn(x2d, ffn_norm[layer], ffn_w_in[layer].astype(BF16), ffn_w_out[layer].astype(BF16),
                   final_norm, final_norm=(layer == depth - 1))
    return x2d.reshape(nb, t_len, d)
```

```python
import functools
import math

import jax
import jax.numpy as jnp
from jax import lax
from jax.experimental import pallas as pl
from jax.experimental.pallas import tpu as pltpu

F32 = jnp.float32
BF16 = jnp.bfloat16
ACT = BF16

D_MODEL = 1024
NORM_EPS = 1e-6

RW_HEAD_DIM = 64
RW_DIM = D_MODEL // 2
RW_HEADS = RW_DIM // RW_HEAD_DIM
RW_DECAY_RANK = 64
RW_A_RANK = 64
RW_GATE_RANK = 128
RW_GN_EPS = 64e-5
RW_LORA = RW_DECAY_RANK + RW_A_RANK + RW_GATE_RANK
RW_PROJ = 3 * RW_DIM + RW_LORA
RW_CHUNK = 64
RW_BLOCK_CHUNKS = 4
RW_HEAD_PAIRS = RW_HEADS // 2
assert RW_CHUNK == RW_HEAD_DIM and 2 * RW_HEAD_DIM == 128

SSD_HEAD_DIM = 64
SSD_DIM = D_MODEL // 2
SSD_HEADS = SSD_DIM // SSD_HEAD_DIM
SSD_GROUPS = 2
SSD_STATE = 128
SSD_CONV_DIM = SSD_DIM + 2 * SSD_GROUPS * SSD_STATE
SSD_CHUNK = 128
SSD_BLOCK_CHUNKS = 4
SSD_GROUP_HEADS = SSD_HEADS // SSD_GROUPS

HG_HEADS = 8
HG_KEY_DIM = 128
HG_VAL_DIM = D_MODEL // HG_HEADS
HG_KEY_WIDTH = HG_HEADS * HG_KEY_DIM
HG_VAL_WIDTH = HG_HEADS * HG_VAL_DIM
HG_CHUNK = 64
HG_SUB = 16
HG_BLOCK_CHUNKS = 4
HG_SAFE_LOG_RANGE = 64.0
assert HG_KEY_DIM == HG_VAL_DIM

XA_HEADS = 4
XA_HEAD_DIM = D_MODEL // XA_HEADS
FFN_DIM = ((8 * D_MODEL + 3 * 256 - 1) // (3 * 256)) * 256

SUBLANES = 8
HALO_ROWS = 16
SHIFT_ROWS = 128
PROJ_COLS = 256
LANES = 128
VMEM_LIMIT_BYTES = 56 * 1024 * 1024
SCAN_STEP_ROWS = 1024
NEG_BIG = -1e30


def _params(*semantics):
    return pltpu.CompilerParams(dimension_semantics=semantics, vmem_limit_bytes=VMEM_LIMIT_BYTES)


def _tile(n, pref):
    t = min(n, pref)
    while n % t or t % SUBLANES:
        t -= 1
    return t


def _scan_rows(t_len, sub_rows):
    assert t_len % sub_rows == 0
    rows = sub_rows
    while rows * 2 <= SCAN_STEP_ROWS and t_len % (rows * 2) == 0:
        rows *= 2
    return rows


def _mm(a, b):
    return jnp.dot(a.astype(BF16), b.astype(BF16), preferred_element_type=F32)


def _mm_nt(a, b):
    return lax.dot_general(a.astype(BF16), b.astype(BF16), (((1,), (1,)), ((), ())),
                           preferred_element_type=F32)


def _mm_tn(a, b):
    return lax.dot_general(a.astype(BF16), b.astype(BF16), (((0,), (0,)), ((), ())),
                           preferred_element_type=F32)


def _split3(x):
    hi = x.astype(BF16)
    r1 = x - hi.astype(F32)
    mid = r1.astype(BF16)
    lo = (r1 - mid.astype(F32)).astype(BF16)
    return hi, mid, lo


def _mm_exact_lhs(m_bf16, x, terms=3):
    dot = functools.partial(jnp.dot, preferred_element_type=F32)
    pieces = _split3(x)[:terms]
    out = dot(m_bf16, pieces[0])
    for piece in pieces[1:]:
        out = out + dot(m_bf16, piece)
    return out


def _mm_exact_rhs_many(xs, m_bf16, terms=3):
    rows = xs[0].shape[0]
    pieces = [piece for x in xs for piece in _split3(x)[:terms]]
    out = jnp.dot(jnp.concatenate(pieces, axis=0), m_bf16, preferred_element_type=F32)
    results = []
    for i in range(len(xs)):
        total = out[i * terms * rows:(i * terms + 1) * rows]
        for t in range(1, terms):
            total = total + out[(i * terms + t) * rows:(i * terms + t + 1) * rows]
        results.append(total)
    return results


def _rms_normed(x, gain):
    ms = jnp.mean(x * x, axis=-1, keepdims=True)
    return x * lax.rsqrt(ms + NORM_EPS) * gain


def _sigmoid(x):
    return 0.5 * jnp.tanh(0.5 * x) + 0.5


def _silu(x):
    h = 0.5 * x
    return h * jnp.tanh(h) + h


def _softplus(x):
    return jnp.maximum(x, 0.0) + jnp.log(1.0 + jnp.exp(-jnp.abs(x)))


def _order_masks(n, reverse):
    row = lax.broadcasted_iota(jnp.int32, (n, n), 0)
    col = lax.broadcasted_iota(jnp.int32, (n, n), 1)
    if reverse:
        return col >= row, col > row
    return col <= row, col < row


def _norm_matmul_kernel(x_ref, g_ref, *refs, n_out):
    xn = _rms_normed(x_ref[...], g_ref[...]).astype(BF16)
    for w_ref, o_ref in zip(refs[:n_out], refs[n_out:]):
        o_ref[...] = jnp.dot(xn, w_ref[...], preferred_element_type=F32).astype(o_ref.dtype)


def _norm_matmul(x2d, gain, weights, out_dtypes, tm_pref=512):
    n, d = x2d.shape
    tm = _tile(n, tm_pref)
    in_specs = [pl.BlockSpec((tm, d), lambda i: (i, 0)), pl.BlockSpec((1, d), lambda i: (0, 0))]
    in_specs += [pl.BlockSpec(w.shape, lambda i: (0, 0)) for w in weights]
    out_specs = [pl.BlockSpec((tm, w.shape[1]), lambda i: (i, 0)) for w in weights]
    out_shape = [jax.ShapeDtypeStruct((n, w.shape[1]), dt) for w, dt in zip(weights, out_dtypes)]
    return pl.pallas_call(
        functools.partial(_norm_matmul_kernel, n_out=len(weights)),
        grid=(n // tm,), in_specs=in_specs, out_specs=out_specs, out_shape=out_shape,
        compiler_params=_params("parallel"), name="norm_matmul",
    )(x2d, gain.reshape(1, d), *weights)


def _neighbours(p, prev_blk, next_blk, first, last):
    tt = p.shape[0]
    ridx = lax.broadcasted_iota(jnp.int32, p.shape, 0)
    prev_row = jnp.where(first, 0.0, prev_blk[HALO_ROWS - 1:HALO_ROWS, :].astype(F32))
    next_row = jnp.where(last, 0.0, next_blk[0:1, :].astype(F32))
    prev = jnp.where(ridx == 0, prev_row, pltpu.roll(p, 1, 0))
    nxt = jnp.where(ridx == tt - 1, next_row, pltpu.roll(p, tt - 1, 0))
    return prev, nxt


def _shift_matrix(tt, offsets, weight):
    t = jnp.arange(tt)[:, None]
    c = jnp.arange(tt + 2 * HALO_ROWS)[None, :]
    hit = functools.reduce(jnp.logical_or, [c == HALO_ROWS + t + d for d in offsets])
    return jnp.where(hit, weight, 0.0).astype(BF16)


def _lane_group_sum(x, width):
    assert width % LANES == 0 and x.shape[1] % width == 0
    ones = jnp.ones((LANES, LANES), BF16)
    groups = []
    for c in range(0, x.shape[1], width):
        total = _mm(x[:, c:c + LANES], ones)
        for c2 in range(c + LANES, c + width, LANES):
            total = total + _mm(x[:, c2:c2 + LANES], ones)
        groups += [total] * (width // LANES)
    return jnp.concatenate(groups, axis=1)


def _head_sum(x, pair_ones_bf16):
    blocks = [_mm(x[:, c:c + LANES], pair_ones_bf16) for c in range(0, x.shape[1], LANES)]
    return jnp.concatenate(blocks, axis=1)


def _mix0_project(x_ref, xp_ref, xn_ref, gain_ref, wrkv_ref, wlora_ref, wz_ref, wxbc_ref, wdt_ref, first, last,
                  prkv_scr, plora_scr, u_scr, z_scr, dt_scr):
    tt = x_ref.shape[0]
    dot = functools.partial(jnp.dot, preferred_element_type=F32)
    x_ext = jnp.concatenate([jnp.where(first, 0.0, xp_ref[...]), x_ref[...], jnp.where(last, 0.0, xn_ref[...])], axis=0)
    xn_ext = _rms_normed(x_ext, gain_ref[...]).astype(BF16)
    xn = xn_ext[HALO_ROWS:HALO_ROWS + tt]

    def column_blocks(lhs, w_ref, scr, block):
        for c in range(0, scr.shape[1], block):
            scr[:, c:c + block] = dot(lhs, w_ref[:, c:c + block]).astype(scr.dtype)

    column_blocks(xn_ext, wrkv_ref, prkv_scr, RW_DIM)
    column_blocks(xn_ext, wlora_ref, plora_scr, RW_LORA)
    column_blocks(xn_ext, wxbc_ref, u_scr, SSD_DIM)
    column_blocks(xn, wz_ref, z_scr, SSD_DIM)
    column_blocks(xn, wdt_ref, dt_scr, LANES)


def _mix0_operands(prkv_scr, plora_scr, u_scr, z_scr, dt_scr,
                   savg_ref, mu_ref, mul_ref, a0_ref, a2_ref, g2_ref, kk_ref, ka_ref, w0_ref, w2_ref, rk_ref, bd_ref,
                   cw_ref, cb_ref, dtb_ref, alog_ref,
                   r_out, k_out, v_out, kkn_out, kka_out, lwf_out, lwb_out, g_out, bonus_out,
                   z_out, xbc_out, dtf_out, dtb_out, laf_out, lab_out):
    tt = z_scr.shape[0]
    dot = functools.partial(jnp.dot, preferred_element_type=F32)

    def token_shift(p_scr, m_ref, lo_col, hi_col):
        x = p_scr[HALO_ROWS:HALO_ROWS + tt, lo_col:hi_col].astype(F32)
        avg = jnp.concatenate(
            [dot(savg_ref[...], p_scr[t0:t0 + SHIFT_ROWS + 2 * HALO_ROWS, lo_col:hi_col])
             for t0 in range(0, tt, SHIFT_ROWS)], axis=0)
        return x + m_ref[:, lo_col:hi_col] * (avg - x)

    r = token_shift(prkv_scr, mu_ref, 0, RW_DIM)
    k = token_shift(prkv_scr, mu_ref, RW_DIM, 2 * RW_DIM)
    v = token_shift(prkv_scr, mu_ref, 2 * RW_DIM, 3 * RW_DIM)
    lo = token_shift(plora_scr, mul_ref, 0, RW_LORA)
    wd = lo[:, :RW_DECAY_RANK]
    ad = lo[:, RW_DECAY_RANK:RW_DECAY_RANK + RW_A_RANK]
    gd = lo[:, RW_DECAY_RANK + RW_A_RANK:]
    bd = bd_ref[...]

    a = _sigmoid(a0_ref[...] + _mm(ad, a2_ref[...]))
    g = _mm(_sigmoid(gd), g2_ref[...])
    kk = k * kk_ref[...]
    kk = kk / jnp.maximum(jnp.sqrt(_head_sum(kk * kk, bd)), 1e-12)
    k = k * (1.0 + (a - 1.0) * ka_ref[...])
    w_lora = jnp.tanh(wd)

    def log_decay(d):
        wl = w0_ref[d:d + 1, :] + _mm(w_lora, w2_ref[d])
        return -math.exp(-0.5) * _sigmoid(wl)

    r_out[...] = r.astype(r_out.dtype)
    k_out[...] = k.astype(k_out.dtype)
    v_out[...] = v.astype(v_out.dtype)
    kkn_out[...] = kk.astype(kkn_out.dtype)
    kka_out[...] = (kk * a).astype(kka_out.dtype)
    lwf_out[...] = log_decay(0)
    lwb_out[...] = log_decay(1)
    g_out[...] = g.astype(g_out.dtype)
    bonus_out[...] = (_head_sum(r * k * rk_ref[...], bd) * v).astype(bonus_out.dtype)

    z_out[...] = z_scr[...]
    u_ext = u_scr[...]
    u = u_ext[HALO_ROWS:HALO_ROWS + tt].astype(F32)
    prev, nxt = _neighbours(u, u_ext[:HALO_ROWS], u_ext[HALO_ROWS + tt:], False, False)
    conv = prev * cw_ref[0:1, :] + u * cw_ref[1:2, :] + nxt * cw_ref[2:3, :] + cb_ref[...]
    xbc_out[...] = _silu(conv).astype(xbc_out.dtype)
    dt_raw = dt_scr[...]
    dt_f = _softplus(dt_raw + dtb_ref[0:1, :])
    dt_b = _softplus(dt_raw + dtb_ref[1:2, :])
    dtf_out[...] = dt_f
    dtb_out[...] = dt_b
    laf_out[...] = dt_f * (-jnp.exp(alog_ref[0:1, :]))
    lab_out[...] = dt_b * (-jnp.exp(alog_ref[1:2, :]))


MIX0_N_PROJ_REFS = 9
MIX0_N_CONSTS = 16
MIX0_N_OUTS = 15
MIX0_N_STAGED = 5


def _mix0_in_kernel(*refs, per_seq):
    proj_refs = refs[:MIX0_N_PROJ_REFS]
    const_refs = refs[MIX0_N_PROJ_REFS:MIX0_N_PROJ_REFS + MIX0_N_CONSTS]
    out_refs = refs[MIX0_N_PROJ_REFS + MIX0_N_CONSTS:MIX0_N_PROJ_REFS + MIX0_N_CONSTS + MIX0_N_OUTS]
    staged = refs[MIX0_N_PROJ_REFS + MIX0_N_CONSTS + MIX0_N_OUTS:]
    pos = lax.rem(pl.program_id(0), per_seq)
    _mix0_project(*proj_refs, pos == 0, pos == per_seq - 1, *staged)
    _mix0_operands(*staged, *const_refs, *out_refs)


def _mix0_in(x2d, nb, t_len, gain, weights, rwkv_consts, ssd_consts):
    n, d = x2d.shape
    tt = _tile(t_len, 512)
    assert tt % HALO_ROWS == 0
    per_seq = t_len // tt
    n_tiles = n // tt
    blocks_per_tile = tt // HALO_ROWS
    n_blocks = n // HALO_ROWS

    def full(a):
        nd = a.ndim
        return pl.BlockSpec(a.shape, lambda s: (0,) * nd)

    x_specs = [pl.BlockSpec((tt, d), lambda s: (s, 0)),
               pl.BlockSpec((HALO_ROWS, d), lambda s: (jnp.maximum(s * blocks_per_tile - 1, 0), 0)),
               pl.BlockSpec((HALO_ROWS, d), lambda s: (jnp.minimum((s + 1) * blocks_per_tile, n_blocks - 1), 0))]
    proj_consts = (gain.reshape(1, d),) + tuple(weights)
    assert tt % SHIFT_ROWS == 0
    consts = (_shift_matrix(SHIFT_ROWS, (-1, 1), 0.5),) + tuple(rwkv_consts) + tuple(ssd_consts)
    assert 3 + len(proj_consts) == MIX0_N_PROJ_REFS and len(consts) == MIX0_N_CONSTS
    wide = [(RW_DIM, ACT)] * 5 + [(RW_DIM, F32)] * 2 + [(RW_DIM, ACT)] * 2 + [(SSD_DIM, ACT), (SSD_CONV_DIM, ACT)] \
        + [(LANES, F32)] * 4
    assert len(wide) == MIX0_N_OUTS
    ext = tt + 2 * HALO_ROWS
    staged = [pltpu.VMEM((ext, 3 * RW_DIM), ACT), pltpu.VMEM((ext, RW_LORA), ACT), pltpu.VMEM((ext, SSD_CONV_DIM), ACT),
              pltpu.VMEM((tt, SSD_DIM), ACT), pltpu.VMEM((tt, LANES), F32)]
    assert len(staged) == MIX0_N_STAGED
    return pl.pallas_call(
        functools.partial(_mix0_in_kernel, per_seq=per_seq),
        grid=(n_tiles,),
        in_specs=x_specs + [full(a) for a in proj_consts + consts],
        out_specs=[pl.BlockSpec((tt, w), lambda s: (s, 0)) for w, _ in wide],
        out_shape=[jax.ShapeDtypeStruct((n, w), dt) for w, dt in wide],
        scratch_shapes=staged,
        compiler_params=_params("parallel"), name="mix0_in",
    )(x2d, x2d, x2d, *proj_consts, *consts)


def _pair_blockdiag(y):
    lane = lax.broadcasted_iota(jnp.int32, y.shape, 1)
    top = jnp.where(lane < RW_HEAD_DIM, y, 0.0).astype(BF16)
    bottom = jnp.where(lane >= RW_HEAD_DIM, y, 0.0).astype(BF16)
    return jnp.concatenate([top, bottom], axis=0)


def _pair_diag_blocks(full):
    lane = lax.broadcasted_iota(jnp.int32, (RW_HEAD_DIM, LANES), 1)
    return jnp.where(lane < RW_HEAD_DIM, full[:RW_HEAD_DIM], full[RW_HEAD_DIM:])


def _unit_lower_inverses(n_strict_list, eye):
    size = n_strict_list[0].shape[0]
    levels = int(math.log2(size)) - 1
    accs = [eye + n for n in n_strict_list]
    powers = [_mm(n, _pair_blockdiag(n)) for n in n_strict_list]
    for level in range(levels):
        power_bds = [_pair_blockdiag(p) for p in powers]
        if level == levels - 1:
            return [a + _mm(a, b) for a, b in zip(accs, power_bds)]
        both = [_mm(jnp.concatenate([a, p], axis=0), b) for a, p, b in zip(accs, powers, power_bds)]
        accs = [a + ab[:size] for a, ab in zip(accs, both)]
        powers = [ab[size:] for ab in both]


def _rwkv_scan_kernel(r_ref, k_ref, v_ref, kk_ref, kka_ref, lw_ref, o_ref, h_ref, *, reverse):
    @pl.when(pl.program_id(1) == 0)
    def _():
        h_ref[...] = jnp.zeros_like(h_ref)

    sub_rows = RW_BLOCK_CHUNKS * RW_CHUNK
    n_sub = r_ref.shape[0] // sub_rows

    def body(i, carry):
        blk = (n_sub - 1 - i) if reverse else i
        _rwkv_scan_block(r_ref, k_ref, v_ref, kk_ref, kka_ref, lw_ref, o_ref, h_ref,
                         pl.multiple_of(blk * sub_rows, sub_rows), reverse)
        return carry

    lax.fori_loop(0, n_sub, body, 0)


def _rwkv_scan_block(r_ref, k_ref, v_ref, kk_ref, kka_ref, lw_ref, o_ref, h_ref, base, reverse):
    chunk = RW_CHUNK
    n_chunks = RW_BLOCK_CHUNKS
    incl_bf = _order_masks(chunk, reverse)[0].astype(BF16)
    row = lax.broadcasted_iota(jnp.int32, (chunk, LANES), 0)
    pos = lax.broadcasted_iota(jnp.int32, (chunk, LANES), 1) % RW_HEAD_DIM
    incl, strict = (pos >= row, pos > row) if reverse else (pos <= row, pos < row)
    eye = (pos == row).astype(F32)
    end = 0 if reverse else chunk - 1
    chunk_order = list(range(n_chunks - 1, -1, -1)) if reverse else list(range(n_chunks))

    items = []
    for ci in chunk_order:
        rows = pl.ds(base + ci * chunk, chunk)
        lw = lw_ref[rows, :]
        cum = _mm_exact_lhs(incl_bf, lw, terms=2)
        cum_end = cum[end:end + 1, :]
        e_neg = jnp.exp(-cum)
        e_tail = jnp.exp(cum_end - cum)
        p_end = jnp.exp(cum_end)
        k = k_ref[rows, :].astype(F32)
        kka = kka_ref[rows, :].astype(F32)
        at_all = -kk_ref[rows, :].astype(F32) * jnp.exp(cum - lw)
        bt_all = kka * e_neg
        kt_all = k * e_neg
        rt_all = r_ref[rows, :].astype(F32) * jnp.exp(cum)
        bh_all = kka * e_tail
        kh_all = k * e_tail
        v = v_ref[rows, :].astype(F32)
        for j in range(RW_HEAD_PAIRS):
            sl = slice(j * LANES, (j + 1) * LANES)
            items.append(dict(ci=ci, j=j, at=at_all[:, sl], bt=bt_all[:, sl], kt=kt_all[:, sl], rt=rt_all[:, sl],
                              bh=bh_all[:, sl], kh=kh_all[:, sl], v=v[:, sl], p_end=p_end[:, sl]))

    for it in items:
        lhs = jnp.concatenate([it["at"], it["rt"]], axis=0)
        both = _mm_nt(lhs, jnp.concatenate([_pair_blockdiag(it["bt"]), _pair_blockdiag(it["kt"])], axis=0))
        it["a_ab"] = jnp.where(strict, both[:chunk, :LANES], 0.0)
        it["a_rb"] = jnp.where(incl, both[chunk:, :LANES], 0.0)
        it["a_ak"] = jnp.where(strict, both[:chunk, LANES:], 0.0)
        it["a_rk"] = jnp.where(incl, both[chunk:, LANES:], 0.0)
    t_invs = _unit_lower_inverses([it["a_ab"] for it in items], eye)
    for it in items:
        akv_rkv = _mm(jnp.concatenate([it["a_ak"], it["a_rk"]], axis=0), _pair_blockdiag(it["v"]))
        it["x1"] = akv_rkv[:chunk]
        it["rk_v"] = akv_rkv[chunk:]
    for it, t_inv in zip(items, t_invs):
        wu_u0 = _mm(t_inv, jnp.concatenate([_pair_blockdiag(it["at"]), _pair_blockdiag(it["x1"])], axis=1))
        it["w_u"] = wu_u0[:, :LANES]
        it["u0"] = wu_u0[:, LANES:]
    for it in items:
        rb_wu_u0 = _mm(it["a_rb"],
                       jnp.concatenate([_pair_blockdiag(it["w_u"]), _pair_blockdiag(it["u0"])], axis=1))
        w_o = it["rt"] + rb_wu_u0[:, :LANES]
        g_mat = eye * it["p_end"] + _pair_diag_blocks(_mm_tn(it["bh"], it["w_u"]))
        it["wo_g"] = jnp.concatenate([w_o, g_mat], axis=0)
        it["o0"] = rb_wu_u0[:, LANES:] + it["rk_v"]
        it["h0"] = _pair_diag_blocks(_mm_tn(jnp.concatenate([it["bh"], it["kh"]], axis=0),
                                            jnp.concatenate([it["u0"], it["v"]], axis=0)))

    states = [h_ref[j] for j in range(RW_HEAD_PAIRS)]
    for it in items:
        j = it["j"]
        res = _mm(it["wo_g"], _pair_blockdiag(states[j]))
        rows = pl.ds(base + it["ci"] * chunk, chunk)
        o_ref[rows, j * LANES:(j + 1) * LANES] = (res[:chunk] + it["o0"]).astype(o_ref.dtype)
        states[j] = res[chunk:] + it["h0"]
    for j in range(RW_HEAD_PAIRS):
        h_ref[j] = states[j]


def _rwkv_scan(r, k, v, kk, kka, lw, nb, t_len, reverse):
    n = r.shape[0]
    tb = _scan_rows(t_len, RW_BLOCK_CHUNKS * RW_CHUNK)
    nc = t_len // tb

    def idx(b, c):
        return (b * nc + (nc - 1 - c if reverse else c), 0)

    spec = pl.BlockSpec((tb, RW_DIM), idx)
    return pl.pallas_call(
        functools.partial(_rwkv_scan_kernel, reverse=reverse),
        grid=(nb, nc), in_specs=[spec] * 6, out_specs=spec,
        out_shape=jax.ShapeDtypeStruct((n, RW_DIM), ACT),
        scratch_shapes=[pltpu.VMEM((RW_HEAD_PAIRS, RW_HEAD_DIM, LANES), F32)],
        compiler_params=_params("parallel", "arbitrary"), name="rwkv_scan_bwd" if reverse else "rwkv_scan_fwd",
    )(r, k, v, kk, kka, lw)


def _ssd_scan_kernel(xbc_ref, dt_ref, la_ref, ex64_ref, y_ref, s_ref, *, reverse):
    @pl.when(pl.program_id(1) == 0)
    def _():
        s_ref[...] = jnp.zeros_like(s_ref)

    sub_rows = SSD_BLOCK_CHUNKS * SSD_CHUNK
    n_sub = xbc_ref.shape[0] // sub_rows

    def body(i, carry):
        blk = (n_sub - 1 - i) if reverse else i
        _ssd_scan_block(xbc_ref, dt_ref, la_ref, ex64_ref, y_ref, s_ref,
                        pl.multiple_of(blk * sub_rows, sub_rows), reverse)
        return carry

    lax.fori_loop(0, n_sub, body, 0)


def _ssd_scan_block(xbc_ref, dt_ref, la_ref, ex64_ref, y_ref, s_ref, base, reverse):
    chunk = SSD_CHUNK
    n_chunks = SSD_BLOCK_CHUNKS
    incl, _ = _order_masks(chunk, reverse)
    incl_bf = incl.astype(BF16)
    end = 0 if reverse else chunk - 1
    gn = SSD_GROUPS * SSD_STATE
    gw = SSD_GROUP_HEADS * SSD_HEAD_DIM
    chunk_order = list(range(n_chunks - 1, -1, -1)) if reverse else list(range(n_chunks))
    ex64 = ex64_ref[...]

    pre = []
    for ci in chunk_order:
        rows = pl.ds(base + ci * chunk, chunk)
        la = la_ref[rows, :]
        cum = _mm_exact_lhs(incl_bf, la, terms=2)
        cum_rows = cum.T
        cum_end = cum[end:end + 1, :]
        ecum64, etail64, dt64 = _mm_exact_rhs_many(
            [jnp.exp(cum), jnp.exp(cum_end - cum), dt_ref[rows, :]], ex64, terms=2)
        xdt = xbc_ref[rows, :SSD_DIM].astype(F32) * dt64
        pre.append(dict(rows=rows, cum=cum, cum_rows=cum_rows, ecum64=ecum64, eend64=ecum64[end:end + 1, :],
                        xdt=xdt, xtail=xdt * etail64,
                        bm=[xbc_ref[rows, SSD_DIM + g * SSD_STATE:SSD_DIM + (g + 1) * SSD_STATE]
                            for g in range(SSD_GROUPS)],
                        cm=[xbc_ref[rows, SSD_DIM + gn + g * SSD_STATE:SSD_DIM + gn + (g + 1) * SSD_STATE]
                            for g in range(SSD_GROUPS)]))
    for p in pre:
        p["scores"] = [_mm_nt(p["cm"][g], p["bm"][g]) for g in range(SSD_GROUPS)]
        p["inject"] = [_mm_tn(p["bm"][g], p["xtail"][:, g * gw:(g + 1) * gw]) for g in range(SSD_GROUPS)]
    for p in pre:
        weights = []
        for h in range(SSD_HEADS):
            diff = p["cum"][:, h:h + 1] - p["cum_rows"][h:h + 1, :]
            decay = jnp.exp(jnp.where(incl, diff, NEG_BIG))
            weights.append(p["scores"][h // SSD_GROUP_HEADS] * decay)
        p["ydiag"] = [_mm(jnp.concatenate([weights[h], weights[h + 1]], axis=1),
                          _pair_blockdiag(p["xdt"][:, h * SSD_HEAD_DIM:(h + 2) * SSD_HEAD_DIM]))
                      for h in range(0, SSD_HEADS, 2)]

    states = [s_ref[g] for g in range(SSD_GROUPS)]
    for p in pre:
        for g in range(SSD_GROUPS):
            y_off = _mm(p["cm"][g], states[g]) * p["ecum64"][:, g * gw:(g + 1) * gw]
            for jp in range(SSD_GROUP_HEADS // 2):
                pair = g * (SSD_GROUP_HEADS // 2) + jp
                y_ref[p["rows"], pair * LANES:(pair + 1) * LANES] = (
                    p["ydiag"][pair] + y_off[:, jp * LANES:(jp + 1) * LANES]).astype(y_ref.dtype)
            states[g] = states[g] * p["eend64"][:, g * gw:(g + 1) * gw] + p["inject"][g]
    for g in range(SSD_GROUPS):
        s_ref[g] = states[g]


def _ssd_scan(xbc, dt, la, nb, t_len, reverse):
    n = xbc.shape[0]
    tb = _scan_rows(t_len, SSD_BLOCK_CHUNKS * SSD_CHUNK)
    assert SSD_CHUNK == LANES
    nc = t_len // tb
    lane_head = jnp.arange(LANES)[:, None]
    ex64 = (lane_head == jnp.arange(SSD_DIM)[None, :] // SSD_HEAD_DIM).astype(BF16)

    def idx(b, c):
        return (b * nc + (nc - 1 - c if reverse else c), 0)

    def full(a):
        return pl.BlockSpec(a.shape, lambda b, c: (0, 0))

    return pl.pallas_call(
        functools.partial(_ssd_scan_kernel, reverse=reverse),
        grid=(nb, nc),
        in_specs=[pl.BlockSpec((tb, SSD_CONV_DIM), idx), pl.BlockSpec((tb, LANES), idx),
                  pl.BlockSpec((tb, LANES), idx), full(ex64)],
        out_specs=pl.BlockSpec((tb, SSD_DIM), idx),
        out_shape=jax.ShapeDtypeStruct((n, SSD_DIM), ACT),
        scratch_shapes=[pltpu.VMEM((SSD_GROUPS, SSD_STATE, SSD_GROUP_HEADS * SSD_HEAD_DIM), F32)],
        compiler_params=_params("parallel", "arbitrary"), name="ssd_scan_bwd" if reverse else "ssd_scan_fwd",
    )(xbc, dt, la, ex64)


def _mix0_out_kernel(x_ref, of_ref, ob_ref, bonus_ref, g_ref, yf_ref, yb_ref, xbc_ref, z_ref,
                     gnw_ref, gnb_ref, bd_ref, dsk_ref, nw_ref, wo_rw_ref, wo_ssd_ref, out_ref):
    bd = bd_ref[...]
    o = of_ref[...].astype(F32) + ob_ref[...].astype(F32)
    inv_n = 1.0 / RW_HEAD_DIM
    mu = _head_sum(o, bd) * inv_n
    oc = o - mu
    var = _head_sum(oc * oc, bd) * inv_n
    o = oc * lax.rsqrt(var + RW_GN_EPS) * gnw_ref[...] + gnb_ref[...]
    o_rw = (o + bonus_ref[...].astype(F32)) * g_ref[...].astype(F32)

    y = yf_ref[...].astype(F32) + yb_ref[...].astype(F32) + dsk_ref[...] * xbc_ref[...].astype(F32)
    y = y * _silu(z_ref[...].astype(F32))
    gw = SSD_DIM // SSD_GROUPS
    parts = []
    for g in range(SSD_GROUPS):
        yg = y[:, g * gw:(g + 1) * gw]
        parts.append(yg * lax.rsqrt(jnp.mean(yg * yg, axis=-1, keepdims=True) + NORM_EPS))
    o_ssd = jnp.concatenate(parts, axis=-1) * nw_ref[...]
    out_ref[...] = x_ref[...] + _mm(o_rw, wo_rw_ref[...]) + _mm(o_ssd, wo_ssd_ref[...])


def _mix0_out(x2d, o_f, o_b, bonus, g, y_f, y_b, xbc, z, gn_w, gn_b, blockdiag, d_skip, norm_w, wo_rw, wo_ssd):
    n = x2d.shape[0]
    tm = _tile(n, 512)

    def row(width):
        return pl.BlockSpec((tm, width), lambda i: (i, 0))

    def full(a):
        return pl.BlockSpec(a.shape, lambda i: (0, 0))

    consts = (gn_w, gn_b, blockdiag, d_skip, norm_w, wo_rw, wo_ssd)
    return pl.pallas_call(
        _mix0_out_kernel,
        grid=(n // tm,),
        in_specs=[row(D_MODEL)] + [row(RW_DIM)] * 6 + [row(SSD_DIM), row(SSD_DIM)] + [full(a) for a in consts],
        out_specs=row(D_MODEL),
        out_shape=jax.ShapeDtypeStruct((n, D_MODEL), F32),
        compiler_params=_params("parallel"), name="mix0_out",
    )(x2d, o_f, o_b, bonus, g, y_f, y_b, xbc, z, *consts)


def _hgrn_lower_bound(lb_ref, layer):
    x = lb_ref[...]
    m = jnp.max(x, axis=0, keepdims=True)
    e = jnp.exp(x - m)
    s = e / jnp.sum(e, axis=0, keepdims=True)
    lb = jnp.zeros_like(s[0:1, :])
    for i in range(1, layer + 1):
        lb = lb + s[i:i + 1, :]
    return lb


def _hgrn_in_kernel(x_ref, g_ref, lb_ref, wq_ref, wff_ref, wfb_ref, wi_ref, wg_ref,
                    q_out, lff_out, lfb_out, kgf_out, kgb_out, i_out, g_out, *, layer):
    xn = _rms_normed(x_ref[...], g_ref[...]).astype(BF16)
    lb = _hgrn_lower_bound(lb_ref, layer)
    dot = functools.partial(jnp.dot, preferred_element_type=F32)
    width = x_ref.shape[1]
    jobs = []
    for c in range(0, width, PROJ_COLS):
        cols = slice(c, c + PROJ_COLS)
        jobs.append((wff_ref, cols, ("gate", lff_out, kgf_out)))
        jobs.append((wq_ref, cols, ("plain", q_out)))
        jobs.append((wfb_ref, cols, ("gate", lfb_out, kgb_out)))
        jobs.append((wi_ref, cols, ("plain", i_out)))
        jobs.append((wg_ref, cols, ("plain", g_out)))

    def finish(acc, cols, how):
        if how[0] == "gate":
            f = lb[:, cols] + (1.0 - lb[:, cols]) * _sigmoid(acc)
            how[1][:, cols] = jnp.log(f)
            how[2][:, cols] = (1.0 - f).astype(how[2].dtype)
        else:
            how[1][:, cols] = acc.astype(how[1].dtype)

    pending = None
    for w_ref, cols, how in jobs:
        acc = dot(xn, w_ref[:, cols])
        if pending is not None:
            finish(*pending)
        pending = (acc, cols, how)
    finish(*pending)


def _hgrn_in(x2d, gain, hg_lb, weights, layer):
    n, d = x2d.shape
    tm = _tile(n, 512)
    row = pl.BlockSpec((tm, d), lambda i: (i, 0))
    in_specs = [row, pl.BlockSpec((1, d), lambda i: (0, 0)), pl.BlockSpec(hg_lb.shape, lambda i: (0, 0))]
    in_specs += [pl.BlockSpec(w.shape, lambda i: (0, 0)) for w in weights]
    dtypes = (ACT, F32, F32, ACT, ACT, ACT, ACT)
    return pl.pallas_call(
        functools.partial(_hgrn_in_kernel, layer=layer),
        grid=(n // tm,), in_specs=in_specs, out_specs=[row] * len(dtypes),
        out_shape=[jax.ShapeDtypeStruct((n, d), dt) for dt in dtypes],
        compiler_params=_params("parallel"), name="hgrn_in",
    )(x2d, gain.reshape(1, d), hg_lb, *weights)


def _hgrn_chunk(q, lf, kg, vv, state_t, reverse):
    chunk = q.shape[0]
    n_sub = chunk // HG_SUB
    incl, _ = _order_masks(chunk, reverse)
    end = 0 if reverse else chunk - 1
    s_idx = lax.broadcasted_iota(jnp.int32, (HG_SUB, 1), 0)

    bcum = _mm_exact_lhs(incl.astype(BF16), lf)
    b_end = bcum[end:end + 1, :]
    carried = _mm_nt(q * jnp.exp(bcum), state_t)

    order = list(range(n_sub - 1, -1, -1)) if reverse else list(range(n_sub))
    outs = [None] * n_sub
    for pos, sb in enumerate(order):
        rows = slice(sb * HG_SUB, (sb + 1) * HG_SUB)
        q_s, k_s, b_s, v_s = q[rows], kg[rows], bcum[rows], vv[rows]
        acc = carried[rows]
        if pos > 0:
            prev_sb = order[pos - 1]
            edge = prev_sb * HG_SUB if reverse else prev_sb * HG_SUB + HG_SUB - 1
            b_edge = bcum[edge:edge + 1, :]
            src = slice((sb + 1) * HG_SUB, chunk) if reverse else slice(0, sb * HG_SUB)
            q_hat = q_s * jnp.exp(b_s - b_edge)
            k_hat = kg[src] * jnp.exp(b_edge - bcum[src])
            acc = acc + _mm(_mm_nt(q_hat, k_hat), vv[src])
        rows_out = []
        for l in range(HG_SUB):
            w = jnp.exp(jnp.minimum(b_s[l:l + 1, :] - b_s, 0.0)) * k_s * q_s[l:l + 1, :]
            att = jnp.sum(w, axis=-1, keepdims=True)
            att = jnp.where((s_idx >= l) if reverse else (s_idx <= l), att, 0.0)
            rows_out.append(jnp.sum(att * v_s, axis=0, keepdims=True))
        outs[sb] = acc + jnp.concatenate(rows_out, axis=0)

    k_tail = kg * jnp.exp(b_end - bcum)
    new_state_t = state_t * jnp.exp(b_end) + _mm_tn(vv, k_tail)
    return jnp.concatenate(outs, axis=0), new_state_t


def _hgrn_scan_kernel(q_ref, lf_ref, kg_ref, i_ref, lb_ref, o_ref, s_ref, hq_ref, hf_ref, hk_ref, hv_ref, ho_ref, *,
                      reverse, layer):
    @pl.when(pl.program_id(1) == 0)
    def _():
        s_ref[...] = jnp.zeros_like(s_ref)

    sub_rows = HG_BLOCK_CHUNKS * HG_CHUNK
    n_sub = q_ref.shape[0] // sub_rows
    always_safe = jnp.min(_hgrn_lower_bound(lb_ref, layer)) >= math.exp(-HG_SAFE_LOG_RANGE / HG_CHUNK)

    def body(i, carry):
        blk = (n_sub - 1 - i) if reverse else i
        _hgrn_scan_block(q_ref, lf_ref, kg_ref, i_ref, o_ref, s_ref, hq_ref, hf_ref, hk_ref, hv_ref, ho_ref,
                         pl.multiple_of(blk * sub_rows, sub_rows), always_safe, reverse)
        return carry

    lax.fori_loop(0, n_sub, body, 0)


def _hgrn_scan_block(q_ref, lf_ref, kg_ref, i_ref, o_ref, s_ref, hq_ref, hf_ref, hk_ref, hv_ref, ho_ref,
                     base, always_safe, reverse):
    chunk = HG_CHUNK
    n_chunks = HG_BLOCK_CHUNKS
    incl, _ = _order_masks(chunk, reverse)
    incl_bf = incl.astype(BF16)
    end = 0 if reverse else chunk - 1
    chunk_order = list(range(n_chunks - 1, -1, -1)) if reverse else list(range(n_chunks))

    def head_cols(h):
        return slice(h * HG_KEY_DIM, (h + 1) * HG_KEY_DIM)

    pre = []
    worst = None
    for ci in chunk_order:
        rows = pl.ds(base + ci * chunk, chunk)
        bcum = _mm_exact_lhs(incl_bf, lf_ref[rows, :], terms=2)
        b_end = bcum[end:end + 1, :]
        worst = b_end if worst is None else jnp.minimum(worst, b_end)
        pre.append((ci, rows, bcum, b_end))
    def whole_chunk_form():
        items = []
        for ci, rows, bcum, b_end in pre:
            kg = kg_ref[rows, :].astype(F32)
            e_neg = jnp.exp(-bcum)
            e_end = jnp.exp(b_end)
            qt_all = q_ref[rows, :].astype(F32) * jnp.exp(bcum)
            kt_all = kg * e_neg
            ktail_all = kt_all * e_end
            v_all = i_ref[rows, :].astype(F32)
            for h in range(HG_HEADS):
                c = head_cols(h)
                items.append(dict(rows=rows, h=h, qt=qt_all[:, c], kt=kt_all[:, c], ktail=ktail_all[:, c],
                                  v=v_all[:, c], e_end=e_end[:, c]))
        for it in items:
            it["inject"] = _mm_tn(it["v"], it["ktail"])
        entering = [s_ref[h] for h in range(HG_HEADS)]
        states = list(entering)
        for it in items:
            h = it["h"]
            it["state_in"] = states[h]
            states[h] = states[h] * it["e_end"] + it["inject"]
        for h in range(HG_HEADS):
            s_ref[h] = states[h]
        for it in items:
            both = _mm_nt(it["qt"], jnp.concatenate([it["state_in"], it["kt"]], axis=0))
            it["carried"] = both[:, :HG_VAL_DIM]
            it["attn"] = jnp.where(incl, both[:, HG_VAL_DIM:], 0.0)
        for it in items:
            o_ref[it["rows"], head_cols(it["h"])] = (_mm(it["attn"], it["v"]) + it["carried"]).astype(o_ref.dtype)
        return entering

    entering = whole_chunk_form()
    out_of_range = jnp.logical_and(jnp.logical_not(always_safe), jnp.min(worst) < -HG_SAFE_LOG_RANGE)

    @pl.when(out_of_range)
    def _():
        for h in range(HG_HEADS):
            s_ref[h] = entering[h]

        def head_body(h, carry):
            out, new_state = _hgrn_chunk(hq_ref[h], hf_ref[h], hk_ref[h], hv_ref[h], s_ref[h], reverse)
            ho_ref[h] = out
            s_ref[h] = new_state
            return carry

        def chunk_body(i, carry):
            ci = (n_chunks - 1 - i) if reverse else i
            rows = pl.ds(pl.multiple_of(base + ci * chunk, chunk), chunk)
            for h in range(HG_HEADS):
                hq_ref[h] = q_ref[rows, head_cols(h)].astype(F32)
                hf_ref[h] = lf_ref[rows, head_cols(h)]
                hk_ref[h] = kg_ref[rows, head_cols(h)].astype(F32)
                hv_ref[h] = i_ref[rows, head_cols(h)].astype(F32)
            lax.fori_loop(0, HG_HEADS, head_body, 0)
            for h in range(HG_HEADS):
                o_ref[rows, head_cols(h)] = ho_ref[h].astype(o_ref.dtype)
            return carry

        lax.fori_loop(0, n_chunks, chunk_body, 0)


def _hgrn_scan(q, lf, kg, i_val, hg_lb, nb, t_len, reverse, layer):
    n = q.shape[0]
    tb = _scan_rows(t_len, HG_BLOCK_CHUNKS * HG_CHUNK)
    n_outer = t_len // tb

    def idx(b, c):
        return (b * n_outer + (n_outer - 1 - c if reverse else c), 0)

    spec = pl.BlockSpec((tb, HG_KEY_WIDTH), idx)
    head_buf = pltpu.VMEM((HG_HEADS, HG_CHUNK, HG_KEY_DIM), F32)
    return pl.pallas_call(
        functools.partial(_hgrn_scan_kernel, reverse=reverse, layer=layer),
        grid=(nb, n_outer),
        in_specs=[spec, spec, spec, spec, pl.BlockSpec(hg_lb.shape, lambda b, c: (0, 0))],
        out_specs=spec,
        out_shape=jax.ShapeDtypeStruct((n, HG_VAL_WIDTH), ACT),
        scratch_shapes=[pltpu.VMEM((HG_HEADS, HG_VAL_DIM, HG_KEY_DIM), F32)] + [head_buf] * 5,
        compiler_params=_params("parallel", "arbitrary"),
        name="hgrn_scan_bwd" if reverse else "hgrn_scan_fwd",
    )(q, lf, kg, i_val, hg_lb)


def _hgrn_out_kernel(x_ref, of_ref, ob_ref, g_ref, nw_ref, wo_ref, out_ref):
    o = of_ref[...].astype(F32) + ob_ref[...].astype(F32)
    mean_sq = _lane_group_sum(o * o, HG_VAL_DIM) * (1.0 / HG_VAL_DIM)
    o = o * lax.rsqrt(mean_sq + NORM_EPS) * nw_ref[...] * _silu(g_ref[...].astype(F32))
    out_ref[...] = x_ref[...] + _mm(o, wo_ref[...])


def _hgrn_out(x2d, o_f, o_b, g, norm_w, w_out):
    n = x2d.shape[0]
    tm = _tile(n, 512)
    row = pl.BlockSpec((tm, D_MODEL), lambda i: (i, 0))
    return pl.pallas_call(
        _hgrn_out_kernel,
        grid=(n // tm,),
        in_specs=[row] * 4 + [pl.BlockSpec((1, HG_VAL_WIDTH), lambda i: (0, 0)),
                              pl.BlockSpec(w_out.shape, lambda i: (0, 0))],
        out_specs=row,
        out_shape=jax.ShapeDtypeStruct((n, D_MODEL), F32),
        compiler_params=_params("parallel"), name="hgrn_out",
    )(x2d, o_f, o_b, g, norm_w, w_out)


def _xattn_kernel(x_ref, g_ref, wq_ref, kv_ref, wo_ref, out_ref):
    x = x_ref[...]
    q = _mm(_rms_normed(x, g_ref[...]), wq_ref[...])
    scale = XA_HEAD_DIM ** -0.5
    scores = [_mm_nt(q[:, h * XA_HEAD_DIM:(h + 1) * XA_HEAD_DIM], kv_ref[:, h * XA_HEAD_DIM:(h + 1) * XA_HEAD_DIM])
              * scale for h in range(XA_HEADS)]
    probs = []
    for s in scores:
        p = jnp.exp(s - jnp.max(s, axis=-1, keepdims=True))
        probs.append(p * (1.0 / jnp.sum(p, axis=-1, keepdims=True)))
    heads = [_mm(p, kv_ref[:, D_MODEL + h * XA_HEAD_DIM:D_MODEL + (h + 1) * XA_HEAD_DIM])
             for h, p in enumerate(probs)]
    out_ref[...] = x + _mm(jnp.concatenate(heads, axis=-1), wo_ref[...])


def _xattn(x2d, gain, wq, kv, wo, nb, t_len, mem_len):
    n = x2d.shape[0]
    tm = _tile(t_len, 512)
    per_seq = t_len // tm
    row = pl.BlockSpec((tm, D_MODEL), lambda b, j: (b * per_seq + j, 0))
    return pl.pallas_call(
        _xattn_kernel,
        grid=(nb, per_seq),
        in_specs=[row, pl.BlockSpec((1, D_MODEL), lambda b, j: (0, 0)),
                  pl.BlockSpec(wq.shape, lambda b, j: (0, 0)),
                  pl.BlockSpec((mem_len, 2 * D_MODEL), lambda b, j: (b, 0)),
                  pl.BlockSpec(wo.shape, lambda b, j: (0, 0))],
        out_specs=row,
        out_shape=jax.ShapeDtypeStruct((n, D_MODEL), F32),
        compiler_params=_params("parallel", "parallel"), name="xattn",
    )(x2d, gain.reshape(1, D_MODEL), wq, kv, wo)


def _ffn_kernel(x_ref, g_ref, wi_ref, wo_ref, fg_ref, out_ref, *, final_norm):
    x = x_ref[...]
    gu = _mm(_rms_normed(x, g_ref[...]), wi_ref[...])
    act = _silu(gu[:, :FFN_DIM]) * gu[:, FFN_DIM:]
    y = x + _mm(act, wo_ref[...])
    if final_norm:
        y = _rms_normed(y, fg_ref[...])
    out_ref[...] = y


def _ffn(x2d, gain, w_in, w_out, final_gain, final_norm):
    n = x2d.shape[0]
    tm = _tile(n, 512)
    row = pl.BlockSpec((tm, D_MODEL), lambda i: (i, 0))
    vec = pl.BlockSpec((1, D_MODEL), lambda i: (0, 0))
    resident = pl.Buffered(1)
    return pl.pallas_call(
        functools.partial(_ffn_kernel, final_norm=final_norm),
        grid=(n // tm,),
        in_specs=[row, vec, pl.BlockSpec(w_in.shape, lambda i: (0, 0), pipeline_mode=resident),
                  pl.BlockSpec(w_out.shape, lambda i: (0, 0), pipeline_mode=resident), vec],
        out_specs=row,
        out_shape=jax.ShapeDtypeStruct((n, D_MODEL), F32),
        compiler_params=_params("parallel"), name="ffn",
    )(x2d, gain.reshape(1, D_MODEL), w_in, w_out, final_gain.reshape(1, D_MODEL))


def _pad_lanes(a, width=LANES):
    return jnp.pad(a, [(0, 0)] * (a.ndim - 1) + [(0, width - a.shape[-1])])


def _rwkv_ssd_layer(x2d, nb, t_len, norm_g, w_in, w_out, rw_mu, rw_w0, rw_w2, rw_a0, rw_a2, rw_g2, rw_k_k, rw_k_a,
                    rw_r_k, rw_gn_w, rw_gn_b, conv_w, conv_b, dt_bias, a_log, d_skip, ssd_norm_w):
    o_lora = 3 * RW_DIM
    o_z = RW_PROJ
    o_xbc = o_z + SSD_DIM
    o_dt = o_xbc + SSD_CONV_DIM
    wb = w_in.astype(BF16)
    weights = [wb[:, :o_lora], wb[:, o_lora:o_z], wb[:, o_z:o_xbc], wb[:, o_xbc:o_dt], _pad_lanes(wb[:, o_dt:])]
    head_id = jnp.arange(LANES) // RW_HEAD_DIM
    blockdiag = (head_id[:, None] == head_id[None, :]).astype(BF16)
    row = lambda a: a.reshape(1, -1)
    rwkv_consts = (row(rw_mu[:o_lora]), row(rw_mu[o_lora:]), row(rw_a0), rw_a2.astype(BF16), rw_g2.astype(BF16),
                   row(rw_k_k), row(rw_k_a), rw_w0, rw_w2.astype(BF16), row(rw_r_k), blockdiag)
    ssd_consts = (conv_w, row(conv_b), _pad_lanes(dt_bias), _pad_lanes(a_log))
    (r, k, v, kk, kka, lw_f, lw_b, g, bonus, p_z, xbc, dt_f, dt_b, la_f, la_b) = _mix0_in(
        x2d, nb, t_len, norm_g, weights, rwkv_consts, ssd_consts)
    o_f = _rwkv_scan(r, k, v, kk, kka, lw_f, nb, t_len, reverse=False)
    o_b = _rwkv_scan(r, k, v, kk, kka, lw_b, nb, t_len, reverse=True)
    y_f = _ssd_scan(xbc, dt_f, la_f, nb, t_len, reverse=False)
    y_b = _ssd_scan(xbc, dt_b, la_b, nb, t_len, reverse=True)

    wo = w_out.astype(BF16)
    d_skip_lanes = jnp.repeat(d_skip, SSD_HEAD_DIM).reshape(1, SSD_DIM)
    return _mix0_out(x2d, o_f, o_b, bonus, g, y_f, y_b, xbc, p_z, row(rw_gn_w), row(rw_gn_b), blockdiag,
                     d_skip_lanes, row(ssd_norm_w), wo[:RW_DIM], wo[RW_DIM:])


def _hgrn_layer(x2d, nb, t_len, norm_g, w_in, w_out, norm_w, hg_lb, layer):
    kw, vw = HG_KEY_WIDTH, HG_VAL_WIDTH
    wb = w_in.astype(BF16)
    weights = [wb[:, :kw], wb[:, kw:2 * kw], wb[:, 2 * kw:3 * kw], wb[:, 3 * kw:3 * kw + vw], wb[:, 3 * kw + vw:]]
    q, lf_f, lf_b, kg_f, kg_b, i_val, g = _hgrn_in(x2d, norm_g, hg_lb, weights, layer)
    o_f = _hgrn_scan(q, lf_f, kg_f, i_val, hg_lb, nb, t_len, reverse=False, layer=layer)
    o_b = _hgrn_scan(q, lf_b, kg_b, i_val, hg_lb, nb, t_len, reverse=True, layer=layer)
    return _hgrn_out(x2d, o_f, o_b, g, norm_w.reshape(1, vw), w_out.astype(BF16))


def kernel(x, mem, mix_norm, ab_w_in, ab_w_out, rw_mu, rw_w0, rw_w2, rw_a0, rw_a2, rw_g2, rw_k_k, rw_k_a, rw_r_k, rw_gn_w, rw_gn_b, ssd_conv_w, ssd_conv_b, ssd_dt_bias, ssd_a_log, ssd_d, ssd_norm_w, hg_w_in, hg_w_out, hg_norm_w, hg_lb, xa_norm, mem_norm, xa_wq, xa_wkv, xa_wo, ffn_norm, ffn_w_in, ffn_w_out, final_norm):
    nb, t_len, d = x.shape
    mem_len = mem.shape[1]
    depth = mix_norm.shape[0]
    x2d = x.reshape(nb * t_len, d)
    mem2d = mem.reshape(nb * mem_len, d)
    for layer in range(depth):
        if layer % 2 == 0:
            e = layer // 2
            x2d = _rwkv_ssd_layer(x2d, nb, t_len, mix_norm[layer], ab_w_in[e], ab_w_out[e], rw_mu[e], rw_w0[e],
                                  rw_w2[e], rw_a0[e], rw_a2[e], rw_g2[e], rw_k_k[e], rw_k_a[e],
                                  rw_r_k[e].reshape(-1), rw_gn_w[e], rw_gn_b[e], ssd_conv_w[e], ssd_conv_b[e],
                                  ssd_dt_bias[e], ssd_a_log[e], ssd_d[e], ssd_norm_w[e])
        else:
            o = layer // 2
            x2d = _hgrn_layer(x2d, nb, t_len, mix_norm[layer], hg_w_in[o], hg_w_out[o], hg_norm_w[o], hg_lb, layer)
        (kv,) = _norm_matmul(mem2d, mem_norm[layer], [xa_wkv[layer].astype(BF16)], [BF16])
        x2d = _xattn(x2d, xa_norm[layer], xa_wq[layer].astype(BF16), kv, xa_wo[layer].astype(BF16),
                     nb, t_len, mem_len)
        x2d = _ffn(x2d, ffn_norm[layer], ffn_w_in[layer].astype(BF16), ffn_w_out[layer].astype(BF16),
                   final_norm, final_norm=(layer == depth - 1))
    return x2d.reshape(nb, t_len, d)
```

```python
import functools
import math

import jax
import jax.numpy as jnp
from jax import lax
from jax.experimental import pallas as pl
from jax.experimental.pallas import tpu as pltpu

F32 = jnp.float32
BF16 = jnp.bfloat16
ACT = BF16

D_MODEL = 1024
NORM_EPS = 1e-6

RW_HEAD_DIM = 64
RW_DIM = D_MODEL // 2
RW_HEADS = RW_DIM // RW_HEAD_DIM
RW_DECAY_RANK = 64
RW_A_RANK = 64
RW_GATE_RANK = 128
RW_GN_EPS = 64e-5
RW_LORA = RW_DECAY_RANK + RW_A_RANK + RW_GATE_RANK
RW_PROJ = 3 * RW_DIM + RW_LORA
RW_CHUNK = 64
RW_BLOCK_CHUNKS = 4
RW_HEAD_PAIRS = RW_HEADS // 2
assert RW_CHUNK == RW_HEAD_DIM and 2 * RW_HEAD_DIM == 128

SSD_HEAD_DIM = 64
SSD_DIM = D_MODEL // 2
SSD_HEADS = SSD_DIM // SSD_HEAD_DIM
SSD_GROUPS = 2
SSD_STATE = 128
SSD_CONV_DIM = SSD_DIM + 2 * SSD_GROUPS * SSD_STATE
SSD_CHUNK = 128
SSD_BLOCK_CHUNKS = 4
SSD_GROUP_HEADS = SSD_HEADS // SSD_GROUPS
assert SSD_HEAD_DIM == RW_HEAD_DIM and SSD_GROUP_HEADS % 2 == 0

HG_HEADS = 8
HG_KEY_DIM = 128
HG_VAL_DIM = D_MODEL // HG_HEADS
HG_KEY_WIDTH = HG_HEADS * HG_KEY_DIM
HG_VAL_WIDTH = HG_HEADS * HG_VAL_DIM
HG_CHUNK = 64
HG_SUB = 16
HG_BLOCK_CHUNKS = 4
HG_SAFE_LOG_RANGE = 64.0
assert HG_KEY_DIM == HG_VAL_DIM

XA_HEADS = 4
XA_HEAD_DIM = D_MODEL // XA_HEADS
FFN_DIM = ((8 * D_MODEL + 3 * 256 - 1) // (3 * 256)) * 256

SUBLANES = 8
HALO_ROWS = 16
SHIFT_ROWS = 128
PROJ_COLS = 256
LANES = 128
VMEM_LIMIT_BYTES = 56 * 1024 * 1024
SCAN_STEP_ROWS = 1024
NEG_BIG = -1e30


def _params(*semantics):
    return pltpu.CompilerParams(dimension_semantics=semantics, vmem_limit_bytes=VMEM_LIMIT_BYTES)


def _tile(n, pref):
    t = min(n, pref)
    while n % t or t % SUBLANES:
        t -= 1
    return t


def _scan_rows(t_len, sub_rows):
    assert t_len % sub_rows == 0
    rows = sub_rows
    while rows * 2 <= SCAN_STEP_ROWS and t_len % (rows * 2) == 0:
        rows *= 2
    return rows


def _mm(a, b):
    return jnp.dot(a.astype(BF16), b.astype(BF16), preferred_element_type=F32)


def _mm_nt(a, b):
    return lax.dot_general(a.astype(BF16), b.astype(BF16), (((1,), (1,)), ((), ())),
                           preferred_element_type=F32)


def _mm_tn(a, b):
    return lax.dot_general(a.astype(BF16), b.astype(BF16), (((0,), (0,)), ((), ())),
                           preferred_element_type=F32)


def _split3(x):
    hi = x.astype(BF16)
    r1 = x - hi.astype(F32)
    mid = r1.astype(BF16)
    lo = (r1 - mid.astype(F32)).astype(BF16)
    return hi, mid, lo


def _mm_exact_lhs(m_bf16, x, terms=3):
    dot = functools.partial(jnp.dot, preferred_element_type=F32)
    pieces = _split3(x)[:terms]
    out = dot(m_bf16, pieces[0])
    for piece in pieces[1:]:
        out = out + dot(m_bf16, piece)
    return out


def _mm_exact_rhs_many(xs, m_bf16, terms=3):
    rows = xs[0].shape[0]
    pieces = [piece for x in xs for piece in _split3(x)[:terms]]
    out = jnp.dot(jnp.concatenate(pieces, axis=0), m_bf16, preferred_element_type=F32)
    results = []
    for i in range(len(xs)):
        total = out[i * terms * rows:(i * terms + 1) * rows]
        for t in range(1, terms):
            total = total + out[(i * terms + t) * rows:(i * terms + t + 1) * rows]
        results.append(total)
    return results


def _rms_normed(x, gain):
    ms = jnp.mean(x * x, axis=-1, keepdims=True)
    return x * lax.rsqrt(ms + NORM_EPS) * gain


def _sigmoid(x):
    return 0.5 * jnp.tanh(0.5 * x) + 0.5


def _silu(x):
    h = 0.5 * x
    return h * jnp.tanh(h) + h


def _softplus(x):
    return jnp.maximum(x, 0.0) + jnp.log(1.0 + jnp.exp(-jnp.abs(x)))


def _order_masks(n, reverse):
    row = lax.broadcasted_iota(jnp.int32, (n, n), 0)
    col = lax.broadcasted_iota(jnp.int32, (n, n), 1)
    if reverse:
        return col >= row, col > row
    return col <= row, col < row


def _norm_matmul_kernel(x_ref, g_ref, *refs, n_out):
    xn = _rms_normed(x_ref[...], g_ref[...]).astype(BF16)
    for w_ref, o_ref in zip(refs[:n_out], refs[n_out:]):
        o_ref[...] = jnp.dot(xn, w_ref[...], preferred_element_type=F32).astype(o_ref.dtype)


def _norm_matmul(x2d, gain, weights, out_dtypes, tm_pref=512):
    n, d = x2d.shape
    tm = _tile(n, tm_pref)
    in_specs = [pl.BlockSpec((tm, d), lambda i: (i, 0)), pl.BlockSpec((1, d), lambda i: (0, 0))]
    in_specs += [pl.BlockSpec(w.shape, lambda i: (0, 0)) for w in weights]
    out_specs = [pl.BlockSpec((tm, w.shape[1]), lambda i: (i, 0)) for w in weights]
    out_shape = [jax.ShapeDtypeStruct((n, w.shape[1]), dt) for w, dt in zip(weights, out_dtypes)]
    return pl.pallas_call(
        functools.partial(_norm_matmul_kernel, n_out=len(weights)),
        grid=(n // tm,), in_specs=in_specs, out_specs=out_specs, out_shape=out_shape,
        compiler_params=_params("parallel"), name="norm_matmul",
    )(x2d, gain.reshape(1, d), *weights)


def _neighbours(p, prev_blk, next_blk, first, last):
    tt = p.shape[0]
    ridx = lax.broadcasted_iota(jnp.int32, p.shape, 0)
    prev_row = jnp.where(first, 0.0, prev_blk[HALO_ROWS - 1:HALO_ROWS, :].astype(F32))
    next_row = jnp.where(last, 0.0, next_blk[0:1, :].astype(F32))
    prev = jnp.where(ridx == 0, prev_row, pltpu.roll(p, 1, 0))
    nxt = jnp.where(ridx == tt - 1, next_row, pltpu.roll(p, tt - 1, 0))
    return prev, nxt


def _shift_matrix(tt, offsets, weight):
    t = jnp.arange(tt)[:, None]
    c = jnp.arange(tt + 2 * HALO_ROWS)[None, :]
    hit = functools.reduce(jnp.logical_or, [c == HALO_ROWS + t + d for d in offsets])
    return jnp.where(hit, weight, 0.0).astype(BF16)


def _lane_group_sum(x, width):
    assert width % LANES == 0 and x.shape[1] % width == 0
    ones = jnp.ones((LANES, LANES), BF16)
    groups = []
    for c in range(0, x.shape[1], width):
        total = _mm(x[:, c:c + LANES], ones)
        for c2 in range(c + LANES, c + width, LANES):
            total = total + _mm(x[:, c2:c2 + LANES], ones)
        groups += [total] * (width // LANES)
    return jnp.concatenate(groups, axis=1)


def _head_sum(x, pair_ones_bf16):
    blocks = [_mm(x[:, c:c + LANES], pair_ones_bf16) for c in range(0, x.shape[1], LANES)]
    return jnp.concatenate(blocks, axis=1)


def _mix0_project(x_ref, xp_ref, xn_ref, gain_ref, wrkv_ref, wlora_ref, wz_ref, wxbc_ref, wdt_ref, first, last,
                  prkv_scr, plora_scr, u_scr, z_scr, dt_scr):
    tt = x_ref.shape[0]
    dot = functools.partial(jnp.dot, preferred_element_type=F32)
    x_ext = jnp.concatenate([jnp.where(first, 0.0, xp_ref[...]), x_ref[...], jnp.where(last, 0.0, xn_ref[...])], axis=0)
    xn_ext = _rms_normed(x_ext, gain_ref[...]).astype(BF16)
    xn = xn_ext[HALO_ROWS:HALO_ROWS + tt]

    def column_blocks(lhs, w_ref, scr, block):
        for c in range(0, scr.shape[1], block):
            scr[:, c:c + block] = dot(lhs, w_ref[:, c:c + block]).astype(scr.dtype)

    column_blocks(xn_ext, wrkv_ref, prkv_scr, RW_DIM)
    column_blocks(xn_ext, wlora_ref, plora_scr, RW_LORA)
    column_blocks(xn_ext, wxbc_ref, u_scr, SSD_DIM)
    column_blocks(xn, wz_ref, z_scr, SSD_DIM)
    column_blocks(xn, wdt_ref, dt_scr, LANES)


def _mix0_operands(prkv_scr, plora_scr, u_scr, z_scr, dt_scr,
                   savg_ref, mu_ref, mul_ref, a0_ref, a2_ref, g2_ref, kk_ref, ka_ref, w0_ref, w2_ref, rk_ref, bd_ref,
                   cw_ref, cb_ref, dtb_ref, alog_ref,
                   r_out, k_out, v_out, kkn_out, kka_out, lwf_out, lwb_out, g_out, bonus_out,
                   z_out, xbc_out, dtf_out, dtb_out, laf_out, lab_out):
    tt = z_scr.shape[0]
    dot = functools.partial(jnp.dot, preferred_element_type=F32)

    def token_shift(p_scr, m_ref, lo_col, hi_col):
        x = p_scr[HALO_ROWS:HALO_ROWS + tt, lo_col:hi_col].astype(F32)
        avg = jnp.concatenate(
            [dot(savg_ref[...], p_scr[t0:t0 + SHIFT_ROWS + 2 * HALO_ROWS, lo_col:hi_col])
             for t0 in range(0, tt, SHIFT_ROWS)], axis=0)
        return x + m_ref[:, lo_col:hi_col] * (avg - x)

    r = token_shift(prkv_scr, mu_ref, 0, RW_DIM)
    k = token_shift(prkv_scr, mu_ref, RW_DIM, 2 * RW_DIM)
    v = token_shift(prkv_scr, mu_ref, 2 * RW_DIM, 3 * RW_DIM)
    lo = token_shift(plora_scr, mul_ref, 0, RW_LORA)
    wd = lo[:, :RW_DECAY_RANK]
    ad = lo[:, RW_DECAY_RANK:RW_DECAY_RANK + RW_A_RANK]
    gd = lo[:, RW_DECAY_RANK + RW_A_RANK:]
    bd = bd_ref[...]

    a = _sigmoid(a0_ref[...] + _mm(ad, a2_ref[...]))
    g = _mm(_sigmoid(gd), g2_ref[...])
    kk = k * kk_ref[...]
    kk = kk / jnp.maximum(jnp.sqrt(_head_sum(kk * kk, bd)), 1e-12)
    k = k * (1.0 + (a - 1.0) * ka_ref[...])
    w_lora = jnp.tanh(wd)

    def log_decay(d):
        wl = w0_ref[d:d + 1, :] + _mm(w_lora, w2_ref[d])
        return -math.exp(-0.5) * _sigmoid(wl)

    r_out[...] = r.astype(r_out.dtype)
    k_out[...] = k.astype(k_out.dtype)
    v_out[...] = v.astype(v_out.dtype)
    kkn_out[...] = kk.astype(kkn_out.dtype)
    kka_out[...] = (kk * a).astype(kka_out.dtype)
    lwf_out[...] = log_decay(0)
    lwb_out[...] = log_decay(1)
    g_out[...] = g.astype(g_out.dtype)
    bonus_out[...] = (_head_sum(r * k * rk_ref[...], bd) * v).astype(bonus_out.dtype)

    z_out[...] = z_scr[...]
    u_ext = u_scr[...]
    u = u_ext[HALO_ROWS:HALO_ROWS + tt].astype(F32)
    prev, nxt = _neighbours(u, u_ext[:HALO_ROWS], u_ext[HALO_ROWS + tt:], False, False)
    conv = prev * cw_ref[0:1, :] + u * cw_ref[1:2, :] + nxt * cw_ref[2:3, :] + cb_ref[...]
    xbc_out[...] = _silu(conv).astype(xbc_out.dtype)
    dt_raw = dt_scr[...]
    dt_f = _softplus(dt_raw + dtb_ref[0:1, :])
    dt_b = _softplus(dt_raw + dtb_ref[1:2, :])
    dtf_out[...] = dt_f
    dtb_out[...] = dt_b
    laf_out[...] = dt_f * (-jnp.exp(alog_ref[0:1, :]))
    lab_out[...] = dt_b * (-jnp.exp(alog_ref[1:2, :]))


MIX0_N_PROJ_REFS = 9
MIX0_N_CONSTS = 16
MIX0_N_OUTS = 15
MIX0_N_STAGED = 5


def _mix0_in_kernel(*refs, per_seq):
    proj_refs = refs[:MIX0_N_PROJ_REFS]
    const_refs = refs[MIX0_N_PROJ_REFS:MIX0_N_PROJ_REFS + MIX0_N_CONSTS]
    out_refs = refs[MIX0_N_PROJ_REFS + MIX0_N_CONSTS:MIX0_N_PROJ_REFS + MIX0_N_CONSTS + MIX0_N_OUTS]
    staged = refs[MIX0_N_PROJ_REFS + MIX0_N_CONSTS + MIX0_N_OUTS:]
    pos = lax.rem(pl.program_id(0), per_seq)
    _mix0_project(*proj_refs, pos == 0, pos == per_seq - 1, *staged)
    _mix0_operands(*staged, *const_refs, *out_refs)


def _mix0_in(x2d, nb, t_len, gain, weights, rwkv_consts, ssd_consts):
    n, d = x2d.shape
    tt = _tile(t_len, 512)
    assert tt % HALO_ROWS == 0
    per_seq = t_len // tt
    n_tiles = n // tt
    blocks_per_tile = tt // HALO_ROWS
    n_blocks = n // HALO_ROWS

    def full(a):
        nd = a.ndim
        return pl.BlockSpec(a.shape, lambda s: (0,) * nd)

    x_specs = [pl.BlockSpec((tt, d), lambda s: (s, 0)),
               pl.BlockSpec((HALO_ROWS, d), lambda s: (jnp.maximum(s * blocks_per_tile - 1, 0), 0)),
               pl.BlockSpec((HALO_ROWS, d), lambda s: (jnp.minimum((s + 1) * blocks_per_tile, n_blocks - 1), 0))]
    proj_consts = (gain.reshape(1, d),) + tuple(weights)
    assert tt % SHIFT_ROWS == 0
    consts = (_shift_matrix(SHIFT_ROWS, (-1, 1), 0.5),) + tuple(rwkv_consts) + tuple(ssd_consts)
    assert 3 + len(proj_consts) == MIX0_N_PROJ_REFS and len(consts) == MIX0_N_CONSTS
    wide = [(RW_DIM, ACT)] * 5 + [(RW_DIM, F32)] * 2 + [(RW_DIM, ACT)] * 2 + [(SSD_DIM, ACT), (SSD_CONV_DIM, ACT)] \
        + [(LANES, F32)] * 4
    assert len(wide) == MIX0_N_OUTS
    ext = tt + 2 * HALO_ROWS
    staged = [pltpu.VMEM((ext, 3 * RW_DIM), ACT), pltpu.VMEM((ext, RW_LORA), ACT), pltpu.VMEM((ext, SSD_CONV_DIM), ACT),
              pltpu.VMEM((tt, SSD_DIM), ACT), pltpu.VMEM((tt, LANES), F32)]
    assert len(staged) == MIX0_N_STAGED
    return pl.pallas_call(
        functools.partial(_mix0_in_kernel, per_seq=per_seq),
        grid=(n_tiles,),
        in_specs=x_specs + [full(a) for a in proj_consts + consts],
        out_specs=[pl.BlockSpec((tt, w), lambda s: (s, 0)) for w, _ in wide],
        out_shape=[jax.ShapeDtypeStruct((n, w), dt) for w, dt in wide],
        scratch_shapes=staged,
        compiler_params=_params("parallel"), name="mix0_in",
    )(x2d, x2d, x2d, *proj_consts, *consts)


def _pair_blockdiag(y):
    lane = lax.broadcasted_iota(jnp.int32, y.shape, 1)
    top = jnp.where(lane < RW_HEAD_DIM, y, 0.0).astype(BF16)
    bottom = jnp.where(lane >= RW_HEAD_DIM, y, 0.0).astype(BF16)
    return jnp.concatenate([top, bottom], axis=0)


def _pair_diag_blocks(full):
    lane = lax.broadcasted_iota(jnp.int32, (RW_HEAD_DIM, LANES), 1)
    return jnp.where(lane < RW_HEAD_DIM, full[:RW_HEAD_DIM], full[RW_HEAD_DIM:])


def _unit_lower_inverses(n_strict_list, eye):
    size = n_strict_list[0].shape[0]
    levels = int(math.log2(size)) - 1
    accs = [eye + n for n in n_strict_list]
    powers = [_mm(n, _pair_blockdiag(n)) for n in n_strict_list]
    for level in range(levels):
        power_bds = [_pair_blockdiag(p) for p in powers]
        if level == levels - 1:
            return [a + _mm(a, b) for a, b in zip(accs, power_bds)]
        both = [_mm(jnp.concatenate([a, p], axis=0), b) for a, p, b in zip(accs, powers, power_bds)]
        accs = [a + ab[:size] for a, ab in zip(accs, both)]
        powers = [ab[size:] for ab in both]


def _rwkv_scan_kernel(r_ref, k_ref, v_ref, kk_ref, kka_ref, lw_ref, o_ref, h_ref, *, reverse):
    @pl.when(pl.program_id(1) == 0)
    def _():
        h_ref[...] = jnp.zeros_like(h_ref)

    sub_rows = RW_BLOCK_CHUNKS * RW_CHUNK
    n_sub = r_ref.shape[0] // sub_rows

    def body(i, carry):
        blk = (n_sub - 1 - i) if reverse else i
        _rwkv_scan_block(r_ref, k_ref, v_ref, kk_ref, kka_ref, lw_ref, o_ref, h_ref,
                         pl.multiple_of(blk * sub_rows, sub_rows), reverse)
        return carry

    lax.fori_loop(0, n_sub, body, 0)


def _rwkv_scan_block(r_ref, k_ref, v_ref, kk_ref, kka_ref, lw_ref, o_ref, h_ref, base, reverse):
    chunk = RW_CHUNK
    n_chunks = RW_BLOCK_CHUNKS
    incl_bf = _order_masks(chunk, reverse)[0].astype(BF16)
    row = lax.broadcasted_iota(jnp.int32, (chunk, LANES), 0)
    pos = lax.broadcasted_iota(jnp.int32, (chunk, LANES), 1) % RW_HEAD_DIM
    incl, strict = (pos >= row, pos > row) if reverse else (pos <= row, pos < row)
    eye = (pos == row).astype(F32)
    end = 0 if reverse else chunk - 1
    chunk_order = list(range(n_chunks - 1, -1, -1)) if reverse else list(range(n_chunks))

    items = []
    for ci in chunk_order:
        rows = pl.ds(base + ci * chunk, chunk)
        lw = lw_ref[rows, :]
        cum = _mm_exact_lhs(incl_bf, lw, terms=2)
        cum_end = cum[end:end + 1, :]
        e_neg = jnp.exp(-cum)
        p_end = jnp.exp(cum_end)
        e_tail = p_end * e_neg
        k = k_ref[rows, :].astype(F32)
        kka = kka_ref[rows, :].astype(F32)
        at_all = -kk_ref[rows, :].astype(F32) * jnp.exp(cum - lw)
        bt_all = kka * e_neg
        kt_all = k * e_neg
        rt_all = r_ref[rows, :].astype(F32) * jnp.exp(cum)
        bh_all = kka * e_tail
        kh_all = k * e_tail
        v = v_ref[rows, :].astype(F32)
        for j in range(RW_HEAD_PAIRS):
            sl = slice(j * LANES, (j + 1) * LANES)
            items.append(dict(ci=ci, j=j, at=at_all[:, sl], bt=bt_all[:, sl], kt=kt_all[:, sl], rt=rt_all[:, sl],
                              bh=bh_all[:, sl], kh=kh_all[:, sl], v=v[:, sl], p_end=p_end[:, sl]))

    for it in items:
        lhs = jnp.concatenate([it["at"], it["rt"]], axis=0)
        both = _mm_nt(lhs, jnp.concatenate([_pair_blockdiag(it["bt"]), _pair_blockdiag(it["kt"])], axis=0))
        it["a_ab"] = jnp.where(strict, both[:chunk, :LANES], 0.0)
        it["a_rb"] = jnp.where(incl, both[chunk:, :LANES], 0.0)
        it["a_ak"] = jnp.where(strict, both[:chunk, LANES:], 0.0)
        it["a_rk"] = jnp.where(incl, both[chunk:, LANES:], 0.0)
    t_invs = _unit_lower_inverses([it["a_ab"] for it in items], eye)
    for it in items:
        akv_rkv = _mm(jnp.concatenate([it["a_ak"], it["a_rk"]], axis=0), _pair_blockdiag(it["v"]))
        it["x1"] = akv_rkv[:chunk]
        it["rk_v"] = akv_rkv[chunk:]
    for it, t_inv in zip(items, t_invs):
        wu_u0 = _mm(t_inv, jnp.concatenate([_pair_blockdiag(it["at"]), _pair_blockdiag(it["x1"])], axis=1))
        it["w_u"] = wu_u0[:, :LANES]
        it["u0"] = wu_u0[:, LANES:]
    for it in items:
        rb_wu_u0 = _mm(it["a_rb"],
                       jnp.concatenate([_pair_blockdiag(it["w_u"]), _pair_blockdiag(it["u0"])], axis=1))
        w_o = it["rt"] + rb_wu_u0[:, :LANES]
        g_mat = eye * it["p_end"] + _pair_diag_blocks(_mm_tn(it["bh"], it["w_u"]))
        it["wo_g"] = jnp.concatenate([w_o, g_mat], axis=0)
        it["o0"] = rb_wu_u0[:, LANES:] + it["rk_v"]
        it["h0"] = _pair_diag_blocks(_mm_tn(jnp.concatenate([it["bh"], it["kh"]], axis=0),
                                            jnp.concatenate([it["u0"], it["v"]], axis=0)))

    states = [h_ref[j] for j in range(RW_HEAD_PAIRS)]
    for it in items:
        j = it["j"]
        res = _mm(it["wo_g"], _pair_blockdiag(states[j]))
        rows = pl.ds(base + it["ci"] * chunk, chunk)
        o_ref[rows, j * LANES:(j + 1) * LANES] = (res[:chunk] + it["o0"]).astype(o_ref.dtype)
        states[j] = res[chunk:] + it["h0"]
    for j in range(RW_HEAD_PAIRS):
        h_ref[j] = states[j]


def _rwkv_scan(r, k, v, kk, kka, lw, nb, t_len, reverse):
    n = r.shape[0]
    tb = _scan_rows(t_len, RW_BLOCK_CHUNKS * RW_CHUNK)
    nc = t_len // tb

    def idx(b, c):
        return (b * nc + (nc - 1 - c if reverse else c), 0)

    spec = pl.BlockSpec((tb, RW_DIM), idx)
    return pl.pallas_call(
        functools.partial(_rwkv_scan_kernel, reverse=reverse),
        grid=(nb, nc), in_specs=[spec] * 6, out_specs=spec,
        out_shape=jax.ShapeDtypeStruct((n, RW_DIM), ACT),
        scratch_shapes=[pltpu.VMEM((RW_HEAD_PAIRS, RW_HEAD_DIM, LANES), F32)],
        compiler_params=_params("parallel", "arbitrary"), name="rwkv_scan_bwd" if reverse else "rwkv_scan_fwd",
    )(r, k, v, kk, kka, lw)


def _ssd_scan_kernel(xbc_ref, dt_ref, la_ref, ex64_ref, y_ref, s_ref, *, reverse):
    @pl.when(pl.program_id(1) == 0)
    def _():
        s_ref[...] = jnp.zeros_like(s_ref)

    sub_rows = SSD_BLOCK_CHUNKS * SSD_CHUNK
    n_sub = xbc_ref.shape[0] // sub_rows

    def body(i, carry):
        blk = (n_sub - 1 - i) if reverse else i
        _ssd_scan_block(xbc_ref, dt_ref, la_ref, ex64_ref, y_ref, s_ref,
                        pl.multiple_of(blk * sub_rows, sub_rows), reverse)
        return carry

    lax.fori_loop(0, n_sub, body, 0)


def _ssd_scan_block(xbc_ref, dt_ref, la_ref, ex64_ref, y_ref, s_ref, base, reverse):
    chunk = SSD_CHUNK
    n_chunks = SSD_BLOCK_CHUNKS
    incl, _ = _order_masks(chunk, reverse)
    incl_bf = incl.astype(BF16)
    end = 0 if reverse else chunk - 1
    gn = SSD_GROUPS * SSD_STATE
    gw = SSD_GROUP_HEADS * SSD_HEAD_DIM
    chunk_order = list(range(n_chunks - 1, -1, -1)) if reverse else list(range(n_chunks))
    ex64 = ex64_ref[...]

    pre = []
    for ci in chunk_order:
        rows = pl.ds(base + ci * chunk, chunk)
        la = la_ref[rows, :]
        cum = _mm_exact_lhs(incl_bf, la, terms=2)
        cum_rows = cum.T
        cum_end = cum[end:end + 1, :]
        ecum64, etail64, dt64 = _mm_exact_rhs_many(
            [jnp.exp(cum), jnp.exp(cum_end - cum), dt_ref[rows, :]], ex64, terms=2)
        xdt = xbc_ref[rows, :SSD_DIM].astype(F32) * dt64
        pre.append(dict(rows=rows, cum=cum, cum_rows=cum_rows, ecum64=ecum64, eend64=ecum64[end:end + 1, :],
                        xdt=xdt, xtail=xdt * etail64,
                        bm=[xbc_ref[rows, SSD_DIM + g * SSD_STATE:SSD_DIM + (g + 1) * SSD_STATE]
                            for g in range(SSD_GROUPS)],
                        cm=[xbc_ref[rows, SSD_DIM + gn + g * SSD_STATE:SSD_DIM + gn + (g + 1) * SSD_STATE]
                            for g in range(SSD_GROUPS)]))
    for p in pre:
        p["scores"] = [_mm_nt(p["cm"][g], p["bm"][g]) for g in range(SSD_GROUPS)]
        p["inject"] = [_mm_tn(p["bm"][g], p["xtail"][:, g * gw:(g + 1) * gw]) for g in range(SSD_GROUPS)]
    for p in pre:
        weights = []
        for h in range(SSD_HEADS):
            diff = p["cum"][:, h:h + 1] - p["cum_rows"][h:h + 1, :]
            decay = jnp.exp(jnp.where(incl, diff, NEG_BIG))
            weights.append(p["scores"][h // SSD_GROUP_HEADS] * decay)
        p["ydiag"] = [_mm(jnp.concatenate([weights[h], weights[h + 1]], axis=1),
                          _pair_blockdiag(p["xdt"][:, h * SSD_HEAD_DIM:(h + 2) * SSD_HEAD_DIM]))
                      for h in range(0, SSD_HEADS, 2)]

    states = [s_ref[g] for g in range(SSD_GROUPS)]
    for p in pre:
        for g in range(SSD_GROUPS):
            y_off = _mm(p["cm"][g], states[g]) * p["ecum64"][:, g * gw:(g + 1) * gw]
            for jp in range(SSD_GROUP_HEADS // 2):
                pair = g * (SSD_GROUP_HEADS // 2) + jp
                y_ref[p["rows"], pair * LANES:(pair + 1) * LANES] = (
                    p["ydiag"][pair] + y_off[:, jp * LANES:(jp + 1) * LANES]).astype(y_ref.dtype)
            states[g] = states[g] * p["eend64"][:, g * gw:(g + 1) * gw] + p["inject"][g]
    for g in range(SSD_GROUPS):
        s_ref[g] = states[g]


def _ssd_scan(xbc, dt, la, nb, t_len, reverse):
    n = xbc.shape[0]
    tb = _scan_rows(t_len, SSD_BLOCK_CHUNKS * SSD_CHUNK)
    assert SSD_CHUNK == LANES
    nc = t_len // tb
    lane_head = jnp.arange(LANES)[:, None]
    ex64 = (lane_head == jnp.arange(SSD_DIM)[None, :] // SSD_HEAD_DIM).astype(BF16)

    def idx(b, c):
        return (b * nc + (nc - 1 - c if reverse else c), 0)

    def full(a):
        return pl.BlockSpec(a.shape, lambda b, c: (0, 0))

    return pl.pallas_call(
        functools.partial(_ssd_scan_kernel, reverse=reverse),
        grid=(nb, nc),
        in_specs=[pl.BlockSpec((tb, SSD_CONV_DIM), idx), pl.BlockSpec((tb, LANES), idx),
                  pl.BlockSpec((tb, LANES), idx), full(ex64)],
        out_specs=pl.BlockSpec((tb, SSD_DIM), idx),
        out_shape=jax.ShapeDtypeStruct((n, SSD_DIM), ACT),
        scratch_shapes=[pltpu.VMEM((SSD_GROUPS, SSD_STATE, SSD_GROUP_HEADS * SSD_HEAD_DIM), F32)],
        compiler_params=_params("parallel", "arbitrary"), name="ssd_scan_bwd" if reverse else "ssd_scan_fwd",
    )(xbc, dt, la, ex64)


def _mix0_out_kernel(x_ref, of_ref, ob_ref, bonus_ref, g_ref, yf_ref, yb_ref, xbc_ref, z_ref,
                     gnw_ref, gnb_ref, bd_ref, dsk_ref, nw_ref, wo_rw_ref, wo_ssd_ref, out_ref):
    bd = bd_ref[...]
    o = of_ref[...].astype(F32) + ob_ref[...].astype(F32)
    inv_n = 1.0 / RW_HEAD_DIM
    mu = _head_sum(o, bd) * inv_n
    oc = o - mu
    var = _head_sum(oc * oc, bd) * inv_n
    o = oc * lax.rsqrt(var + RW_GN_EPS) * gnw_ref[...] + gnb_ref[...]
    o_rw = (o + bonus_ref[...].astype(F32)) * g_ref[...].astype(F32)

    y = yf_ref[...].astype(F32) + yb_ref[...].astype(F32) + dsk_ref[...] * xbc_ref[...].astype(F32)
    y = y * _silu(z_ref[...].astype(F32))
    gw = SSD_DIM // SSD_GROUPS
    parts = []
    for g in range(SSD_GROUPS):
        yg = y[:, g * gw:(g + 1) * gw]
        parts.append(yg * lax.rsqrt(jnp.mean(yg * yg, axis=-1, keepdims=True) + NORM_EPS))
    o_ssd = jnp.concatenate(parts, axis=-1) * nw_ref[...]
    out_ref[...] = x_ref[...] + _mm(o_rw, wo_rw_ref[...]) + _mm(o_ssd, wo_ssd_ref[...])


def _mix0_out(x2d, o_f, o_b, bonus, g, y_f, y_b, xbc, z, gn_w, gn_b, blockdiag, d_skip, norm_w, wo_rw, wo_ssd):
    n = x2d.shape[0]
    tm = _tile(n, 512)

    def row(width):
        return pl.BlockSpec((tm, width), lambda i: (i, 0))

    def full(a):
        return pl.BlockSpec(a.shape, lambda i: (0, 0))

    consts = (gn_w, gn_b, blockdiag, d_skip, norm_w, wo_rw, wo_ssd)
    return pl.pallas_call(
        _mix0_out_kernel,
        grid=(n // tm,),
        in_specs=[row(D_MODEL)] + [row(RW_DIM)] * 6 + [row(SSD_DIM), row(SSD_DIM)] + [full(a) for a in consts],
        out_specs=row(D_MODEL),
        out_shape=jax.ShapeDtypeStruct((n, D_MODEL), F32),
        compiler_params=_params("parallel"), name="mix0_out",
    )(x2d, o_f, o_b, bonus, g, y_f, y_b, xbc, z, *consts)


def _hgrn_lower_bound(lb_ref, layer):
    x = lb_ref[...]
    m = jnp.max(x, axis=0, keepdims=True)
    e = jnp.exp(x - m)
    s = e / jnp.sum(e, axis=0, keepdims=True)
    lb = jnp.zeros_like(s[0:1, :])
    for i in range(1, layer + 1):
        lb = lb + s[i:i + 1, :]
    return lb


def _hgrn_in_kernel(x_ref, g_ref, lb_ref, wq_ref, wff_ref, wfb_ref, wi_ref, wg_ref,
                    q_out, lff_out, lfb_out, kgf_out, kgb_out, i_out, g_out, *, layer):
    xn = _rms_normed(x_ref[...], g_ref[...]).astype(BF16)
    lb = _hgrn_lower_bound(lb_ref, layer)
    dot = functools.partial(jnp.dot, preferred_element_type=F32)
    width = x_ref.shape[1]
    jobs = []
    for c in range(0, width, PROJ_COLS):
        cols = slice(c, c + PROJ_COLS)
        jobs.append((wff_ref, cols, ("gate", lff_out, kgf_out)))
        jobs.append((wq_ref, cols, ("plain", q_out)))
        jobs.append((wfb_ref, cols, ("gate", lfb_out, kgb_out)))
        jobs.append((wi_ref, cols, ("plain", i_out)))
        jobs.append((wg_ref, cols, ("plain", g_out)))

    def finish(acc, cols, how):
        if how[0] == "gate":
            f = lb[:, cols] + (1.0 - lb[:, cols]) * _sigmoid(acc)
            how[1][:, cols] = jnp.log(f)
            how[2][:, cols] = (1.0 - f).astype(how[2].dtype)
        else:
            how[1][:, cols] = acc.astype(how[1].dtype)

    pending = None
    for w_ref, cols, how in jobs:
        acc = dot(xn, w_ref[:, cols])
        if pending is not None:
            finish(*pending)
        pending = (acc, cols, how)
    finish(*pending)


def _hgrn_in(x2d, gain, hg_lb, weights, layer):
    n, d = x2d.shape
    tm = _tile(n, 512)
    row = pl.BlockSpec((tm, d), lambda i: (i, 0))
    in_specs = [row, pl.BlockSpec((1, d), lambda i: (0, 0)), pl.BlockSpec(hg_lb.shape, lambda i: (0, 0))]
    in_specs += [pl.BlockSpec(w.shape, lambda i: (0, 0)) for w in weights]
    dtypes = (ACT, F32, F32, ACT, ACT, ACT, ACT)
    return pl.pallas_call(
        functools.partial(_hgrn_in_kernel, layer=layer),
        grid=(n // tm,), in_specs=in_specs, out_specs=[row] * len(dtypes),
        out_shape=[jax.ShapeDtypeStruct((n, d), dt) for dt in dtypes],
        compiler_params=_params("parallel"), name="hgrn_in",
    )(x2d, gain.reshape(1, d), hg_lb, *weights)


def _hgrn_chunk(q, lf, kg, vv, state_t, reverse):
    chunk = q.shape[0]
    n_sub = chunk // HG_SUB
    incl, _ = _order_masks(chunk, reverse)
    end = 0 if reverse else chunk - 1
    s_idx = lax.broadcasted_iota(jnp.int32, (HG_SUB, 1), 0)

    bcum = _mm_exact_lhs(incl.astype(BF16), lf)
    b_end = bcum[end:end + 1, :]
    carried = _mm_nt(q * jnp.exp(bcum), state_t)

    order = list(range(n_sub - 1, -1, -1)) if reverse else list(range(n_sub))
    outs = [None] * n_sub
    for pos, sb in enumerate(order):
        rows = slice(sb * HG_SUB, (sb + 1) * HG_SUB)
        q_s, k_s, b_s, v_s = q[rows], kg[rows], bcum[rows], vv[rows]
        acc = carried[rows]
        if pos > 0:
            prev_sb = order[pos - 1]
            edge = prev_sb * HG_SUB if reverse else prev_sb * HG_SUB + HG_SUB - 1
            b_edge = bcum[edge:edge + 1, :]
            src = slice((sb + 1) * HG_SUB, chunk) if reverse else slice(0, sb * HG_SUB)
            q_hat = q_s * jnp.exp(b_s - b_edge)
            k_hat = kg[src] * jnp.exp(b_edge - bcum[src])
            acc = acc + _mm(_mm_nt(q_hat, k_hat), vv[src])
        rows_out = []
        for l in range(HG_SUB):
            w = jnp.exp(jnp.minimum(b_s[l:l + 1, :] - b_s, 0.0)) * k_s * q_s[l:l + 1, :]
            att = jnp.sum(w, axis=-1, keepdims=True)
            att = jnp.where((s_idx >= l) if reverse else (s_idx <= l), att, 0.0)
            rows_out.append(jnp.sum(att * v_s, axis=0, keepdims=True))
        outs[sb] = acc + jnp.concatenate(rows_out, axis=0)

    k_tail = kg * jnp.exp(b_end - bcum)
    new_state_t = state_t * jnp.exp(b_end) + _mm_tn(vv, k_tail)
    return jnp.concatenate(outs, axis=0), new_state_t


def _hgrn_scan_kernel(q_ref, lf_ref, kg_ref, i_ref, lb_ref, o_ref, s_ref, hq_ref, hf_ref, hk_ref, hv_ref, ho_ref, *,
                      reverse, layer):
    @pl.when(pl.program_id(1) == 0)
    def _():
        s_ref[...] = jnp.zeros_like(s_ref)

    sub_rows = HG_BLOCK_CHUNKS * HG_CHUNK
    n_sub = q_ref.shape[0] // sub_rows

    def body(i, carry):
        blk = (n_sub - 1 - i) if reverse else i
        _hgrn_scan_block(q_ref, lf_ref, kg_ref, i_ref, lb_ref, o_ref, s_ref, hq_ref, hf_ref, hk_ref, hv_ref, ho_ref,
                         pl.multiple_of(blk * sub_rows, sub_rows), reverse, layer)
        return carry

    lax.fori_loop(0, n_sub, body, 0)


def _hgrn_scan_block(q_ref, lf_ref, kg_ref, i_ref, lb_ref, o_ref, s_ref, hq_ref, hf_ref, hk_ref, hv_ref, ho_ref,
                     base, reverse, layer):
    chunk = HG_CHUNK
    n_chunks = HG_BLOCK_CHUNKS
    incl, _ = _order_masks(chunk, reverse)
    incl_bf = incl.astype(BF16)
    end = 0 if reverse else chunk - 1
    chunk_order = list(range(n_chunks - 1, -1, -1)) if reverse else list(range(n_chunks))

    def head_cols(h):
        return slice(h * HG_KEY_DIM, (h + 1) * HG_KEY_DIM)

    pre = []
    worst = None
    for ci in chunk_order:
        rows = pl.ds(base + ci * chunk, chunk)
        bcum = _mm_exact_lhs(incl_bf, lf_ref[rows, :], terms=2)
        b_end = bcum[end:end + 1, :]
        worst = b_end if worst is None else jnp.minimum(worst, b_end)
        pre.append((ci, rows, bcum, b_end))
    lb_min = jnp.min(_hgrn_lower_bound(lb_ref, layer))
    flag = lax.cond(lb_min >= math.exp(-HG_SAFE_LOG_RANGE / HG_CHUNK),
                    lambda: jnp.int32(1),
                    lambda: (jnp.min(worst) >= -HG_SAFE_LOG_RANGE).astype(jnp.int32))
    safe = flag == 1

    @pl.when(safe)
    def _():
        items = []
        for ci, rows, bcum, b_end in pre:
            kg = kg_ref[rows, :].astype(F32)
            e_neg = jnp.exp(-bcum)
            e_end = jnp.exp(b_end)
            qt_all = q_ref[rows, :].astype(F32) * jnp.exp(bcum)
            kt_all = kg * e_neg
            ktail_all = kt_all * e_end
            v_all = i_ref[rows, :].astype(F32)
            for h in range(HG_HEADS):
                c = head_cols(h)
                items.append(dict(rows=rows, h=h, qt=qt_all[:, c], kt=kt_all[:, c], ktail=ktail_all[:, c],
                                  v=v_all[:, c], e_end=e_end[:, c]))
        for it in items:
            it["attn"] = jnp.where(incl, _mm_nt(it["qt"], it["kt"]), 0.0)
        for it in items:
            it["local"] = _mm(it["attn"], it["v"])
            it["inject"] = _mm_tn(it["v"], it["ktail"])
        states = [s_ref[h] for h in range(HG_HEADS)]
        for it in items:
            h = it["h"]
            o_ref[it["rows"], head_cols(h)] = (it["local"] + _mm_nt(it["qt"], states[h])).astype(o_ref.dtype)
            states[h] = states[h] * it["e_end"] + it["inject"]
        for h in range(HG_HEADS):
            s_ref[h] = states[h]

    @pl.when(jnp.logical_not(safe))
    def _():
        def head_body(h, carry):
            out, new_state = _hgrn_chunk(hq_ref[h], hf_ref[h], hk_ref[h], hv_ref[h], s_ref[h], reverse)
            ho_ref[h] = out
            s_ref[h] = new_state
            return carry

        def chunk_body(i, carry):
            ci = (n_chunks - 1 - i) if reverse else i
            rows = pl.ds(pl.multiple_of(base + ci * chunk, chunk), chunk)
            for h in range(HG_HEADS):
                hq_ref[h] = q_ref[rows, head_cols(h)].astype(F32)
                hf_ref[h] = lf_ref[rows, head_cols(h)]
                hk_ref[h] = kg_ref[rows, head_cols(h)].astype(F32)
                hv_ref[h] = i_ref[rows, head_cols(h)].astype(F32)
            lax.fori_loop(0, HG_HEADS, head_body, 0)
            for h in range(HG_HEADS):
                o_ref[rows, head_cols(h)] = ho_ref[h].astype(o_ref.dtype)
            return carry

        lax.fori_loop(0, n_chunks, chunk_body, 0)


def _hgrn_scan(q, lf, kg, i_val, hg_lb, nb, t_len, reverse, layer):
    n = q.shape[0]
    tb = _scan_rows(t_len, HG_BLOCK_CHUNKS * HG_CHUNK)
    n_outer = t_len // tb

    def idx(b, c):
        return (b * n_outer + (n_outer - 1 - c if reverse else c), 0)

    spec = pl.BlockSpec((tb, HG_KEY_WIDTH), idx)
    head_buf = pltpu.VMEM((HG_HEADS, HG_CHUNK, HG_KEY_DIM), F32)
    return pl.pallas_call(
        functools.partial(_hgrn_scan_kernel, reverse=reverse, layer=layer),
        grid=(nb, n_outer),
        in_specs=[spec, spec, spec, spec, pl.BlockSpec(hg_lb.shape, lambda b, c: (0, 0))],
        out_specs=spec,
        out_shape=jax.ShapeDtypeStruct((n, HG_VAL_WIDTH), ACT),
        scratch_shapes=[pltpu.VMEM((HG_HEADS, HG_VAL_DIM, HG_KEY_DIM), F32)] + [head_buf] * 5,
        compiler_params=_params("parallel", "arbitrary"),
        name="hgrn_scan_bwd" if reverse else "hgrn_scan_fwd",
    )(q, lf, kg, i_val, hg_lb)


def _hgrn_out_kernel(x_ref, of_ref, ob_ref, g_ref, nw_ref, wo_ref, out_ref):
    o = of_ref[...].astype(F32) + ob_ref[...].astype(F32)
    mean_sq = _lane_group_sum(o * o, HG_VAL_DIM) * (1.0 / HG_VAL_DIM)
    o = o * lax.rsqrt(mean_sq + NORM_EPS) * nw_ref[...] * _silu(g_ref[...].astype(F32))
    out_ref[...] = x_ref[...] + _mm(o, wo_ref[...])


def _hgrn_out(x2d, o_f, o_b, g, norm_w, w_out):
    n = x2d.shape[0]
    tm = _tile(n, 512)
    row = pl.BlockSpec((tm, D_MODEL), lambda i: (i, 0))
    return pl.pallas_call(
        _hgrn_out_kernel,
        grid=(n // tm,),
        in_specs=[row] * 4 + [pl.BlockSpec((1, HG_VAL_WIDTH), lambda i: (0, 0)),
                              pl.BlockSpec(w_out.shape, lambda i: (0, 0))],
        out_specs=row,
        out_shape=jax.ShapeDtypeStruct((n, D_MODEL), F32),
        compiler_params=_params("parallel"), name="hgrn_out",
    )(x2d, o_f, o_b, g, norm_w, w_out)


def _xattn_kernel(x_ref, g_ref, wq_ref, kv_ref, wo_ref, out_ref):
    x = x_ref[...]
    q = _mm(_rms_normed(x, g_ref[...]), wq_ref[...])
    scale = XA_HEAD_DIM ** -0.5
    scores = [_mm_nt(q[:, h * XA_HEAD_DIM:(h + 1) * XA_HEAD_DIM], kv_ref[:, h * XA_HEAD_DIM:(h + 1) * XA_HEAD_DIM])
              * scale for h in range(XA_HEADS)]
    probs = []
    for s in scores:
        p = jnp.exp(s - jnp.max(s, axis=-1, keepdims=True))
        probs.append(p * (1.0 / jnp.sum(p, axis=-1, keepdims=True)))
    heads = [_mm(p, kv_ref[:, D_MODEL + h * XA_HEAD_DIM:D_MODEL + (h + 1) * XA_HEAD_DIM])
             for h, p in enumerate(probs)]
    out_ref[...] = x + _mm(jnp.concatenate(heads, axis=-1), wo_ref[...])


def _xattn(x2d, gain, wq, kv, wo, nb, t_len, mem_len):
    n = x2d.shape[0]
    tm = _tile(t_len, 1024)
    per_seq = t_len // tm
    row = pl.BlockSpec((tm, D_MODEL), lambda b, j: (b * per_seq + j, 0))
    return pl.pallas_call(
        _xattn_kernel,
        grid=(nb, per_seq),
        in_specs=[row, pl.BlockSpec((1, D_MODEL), lambda b, j: (0, 0)),
                  pl.BlockSpec(wq.shape, lambda b, j: (0, 0)),
                  pl.BlockSpec((mem_len, 2 * D_MODEL), lambda b, j: (b, 0)),
                  pl.BlockSpec(wo.shape, lambda b, j: (0, 0))],
        out_specs=row,
        out_shape=jax.ShapeDtypeStruct((n, D_MODEL), F32),
        compiler_params=_params("parallel", "parallel"), name="xattn",
    )(x2d, gain.reshape(1, D_MODEL), wq, kv, wo)


def _ffn_kernel(x_ref, g_ref, wi_ref, wo_ref, fg_ref, out_ref, *, final_norm):
    x = x_ref[...]
    gu = _mm(_rms_normed(x, g_ref[...]), wi_ref[...])
    act = _silu(gu[:, :FFN_DIM]) * gu[:, FFN_DIM:]
    y = x + _mm(act, wo_ref[...])
    if final_norm:
        y = _rms_normed(y, fg_ref[...])
    out_ref[...] = y


def _ffn(x2d, gain, w_in, w_out, final_gain, final_norm):
    n = x2d.shape[0]
    tm = _tile(n, 512)
    row = pl.BlockSpec((tm, D_MODEL), lambda i: (i, 0))
    vec = pl.BlockSpec((1, D_MODEL), lambda i: (0, 0))
    resident = pl.Buffered(1)
    return pl.pallas_call(
        functools.partial(_ffn_kernel, final_norm=final_norm),
        grid=(n // tm,),
        in_specs=[row, vec, pl.BlockSpec(w_in.shape, lambda i: (0, 0), pipeline_mode=resident),
                  pl.BlockSpec(w_out.shape, lambda i: (0, 0), pipeline_mode=resident), vec],
        out_specs=row,
        out_shape=jax.ShapeDtypeStruct((n, D_MODEL), F32),
        compiler_params=_params("parallel"), name="ffn",
    )(x2d, gain.reshape(1, D_MODEL), w_in, w_out, final_gain.reshape(1, D_MODEL))


def _pad_lanes(a, width=LANES):
    return jnp.pad(a, [(0, 0)] * (a.ndim - 1) + [(0, width - a.shape[-1])])


def _rwkv_ssd_layer(x2d, nb, t_len, norm_g, w_in, w_out, rw_mu, rw_w0, rw_w2, rw_a0, rw_a2, rw_g2, rw_k_k, rw_k_a,
                    rw_r_k, rw_gn_w, rw_gn_b, conv_w, conv_b, dt_bias, a_log, d_skip, ssd_norm_w):
    o_lora = 3 * RW_DIM
    o_z = RW_PROJ
    o_xbc = o_z + SSD_DIM
    o_dt = o_xbc + SSD_CONV_DIM
    wb = w_in.astype(BF16)
    weights = [wb[:, :o_lora], wb[:, o_lora:o_z], wb[:, o_z:o_xbc], wb[:, o_xbc:o_dt], _pad_lanes(wb[:, o_dt:])]
    head_id = jnp.arange(LANES) // RW_HEAD_DIM
    blockdiag = (head_id[:, None] == head_id[None, :]).astype(BF16)
    row = lambda a: a.reshape(1, -1)
    rwkv_consts = (row(rw_mu[:o_lora]), row(rw_mu[o_lora:]), row(rw_a0), rw_a2.astype(BF16), rw_g2.astype(BF16),
                   row(rw_k_k), row(rw_k_a), rw_w0, rw_w2.astype(BF16), row(rw_r_k), blockdiag)
    ssd_consts = (conv_w, row(conv_b), _pad_lanes(dt_bias), _pad_lanes(a_log))
    (r, k, v, kk, kka, lw_f, lw_b, g, bonus, p_z, xbc, dt_f, dt_b, la_f, la_b) = _mix0_in(
        x2d, nb, t_len, norm_g, weights, rwkv_consts, ssd_consts)
    o_f = _rwkv_scan(r, k, v, kk, kka, lw_f, nb, t_len, reverse=False)
    o_b = _rwkv_scan(r, k, v, kk, kka, lw_b, nb, t_len, reverse=True)
    y_f = _ssd_scan(xbc, dt_f, la_f, nb, t_len, reverse=False)
    y_b = _ssd_scan(xbc, dt_b, la_b, nb, t_len, reverse=True)

    wo = w_out.astype(BF16)
    d_skip_lanes = jnp.repeat(d_skip, SSD_HEAD_DIM).reshape(1, SSD_DIM)
    return _mix0_out(x2d, o_f, o_b, bonus, g, y_f, y_b, xbc, p_z, row(rw_gn_w), row(rw_gn_b), blockdiag,
                     d_skip_lanes, row(ssd_norm_w), wo[:RW_DIM], wo[RW_DIM:])


def _hgrn_layer(x2d, nb, t_len, norm_g, w_in, w_out, norm_w, hg_lb, layer):
    kw, vw = HG_KEY_WIDTH, HG_VAL_WIDTH
    wb = w_in.astype(BF16)
    weights = [wb[:, :kw], wb[:, kw:2 * kw], wb[:, 2 * kw:3 * kw], wb[:, 3 * kw:3 * kw + vw], wb[:, 3 * kw + vw:]]
    q, lf_f, lf_b, kg_f, kg_b, i_val, g = _hgrn_in(x2d, norm_g, hg_lb, weights, layer)
    o_f = _hgrn_scan(q, lf_f, kg_f, i_val, hg_lb, nb, t_len, reverse=False, layer=layer)
    o_b = _hgrn_scan(q, lf_b, kg_b, i_val, hg_lb, nb, t_len, reverse=True, layer=layer)
    return _hgrn_out(x2d, o_f, o_b, g, norm_w.reshape(1, vw), w_out.astype(BF16))


def kernel(x, mem, mix_norm, ab_w_in, ab_w_out, rw_mu, rw_w0, rw_w2, rw_a0, rw_a2, rw_g2, rw_k_k, rw_k_a, rw_r_k, rw_gn_w, rw_gn_b, ssd_conv_w, ssd_conv_b, ssd_dt_bias, ssd_a_log, ssd_d, ssd_norm_w, hg_w_in, hg_w_out, hg_norm_w, hg_lb, xa_norm, mem_norm, xa_wq, xa_wkv, xa_wo, ffn_norm, ffn_w_in, ffn_w_out, final_norm):
    nb, t_len, d = x.shape
    mem_len = mem.shape[1]
    depth = mix_norm.shape[0]
    x2d = x.reshape(nb * t_len, d)
    mem2d = mem.reshape(nb * mem_len, d)
    for layer in range(depth):
        if layer % 2 == 0:
            e = layer // 2
            x2d = _rwkv_ssd_layer(x2d, nb, t_len, mix_norm[layer], ab_w_in[e], ab_w_out[e], rw_mu[e], rw_w0[e],
                                  rw_w2[e], rw_a0[e], rw_a2[e], rw_g2[e], rw_k_k[e], rw_k_a[e],
                                  rw_r_k[e].reshape(-1), rw_gn_w[e], rw_gn_b[e], ssd_conv_w[e], ssd_conv_b[e],
                                  ssd_dt_bias[e], ssd_a_log[e], ssd_d[e], ssd_norm_w[e])
        else:
            o = layer // 2
            x2d = _hgrn_layer(x2d, nb, t_len, mix_norm[layer], hg_w_in[o], hg_w_out[o], hg_norm_w[o], hg_lb, layer)
        (kv,) = _norm_matmul(mem2d, mem_norm[layer], [xa_wkv[layer].astype(BF16)], [BF16])
        x2d = _xattn(x2d, xa_norm[layer], xa_wq[layer].astype(BF16), kv, xa_wo[layer].astype(BF16),
                     nb, t_len, mem_len)
        x2d = _ffn(x2d, ffn_norm[layer], ffn_w_in[layer].astype(BF16), ffn_w_out[layer].astype(BF16),
                   final_norm, final_norm=(layer == depth - 1))
    return x2d.reshape(nb, t_len, d)
```

```python
import functools
import math

import jax
import jax.numpy as jnp
from jax import lax
from jax.experimental import pallas as pl
from jax.experimental.pallas import tpu as pltpu

F32 = jnp.float32
BF16 = jnp.bfloat16
ACT = BF16

D_MODEL = 1024
NORM_EPS = 1e-6

RW_HEAD_DIM = 64
RW_DIM = D_MODEL // 2
RW_HEADS = RW_DIM // RW_HEAD_DIM
RW_DECAY_RANK = 64
RW_A_RANK = 64
RW_GATE_RANK = 128
RW_GN_EPS = 64e-5
RW_LORA = RW_DECAY_RANK + RW_A_RANK + RW_GATE_RANK
RW_PROJ = 3 * RW_DIM + RW_LORA
RW_CHUNK = 64
RW_BLOCK_CHUNKS = 4
RW_HEAD_PAIRS = RW_HEADS // 2
assert RW_CHUNK == RW_HEAD_DIM and 2 * RW_HEAD_DIM == 128

SSD_HEAD_DIM = 64
SSD_DIM = D_MODEL // 2
SSD_HEADS = SSD_DIM // SSD_HEAD_DIM
SSD_GROUPS = 2
SSD_STATE = 128
SSD_CONV_DIM = SSD_DIM + 2 * SSD_GROUPS * SSD_STATE
SSD_CHUNK = 128
SSD_BLOCK_CHUNKS = 4
SSD_GROUP_HEADS = SSD_HEADS // SSD_GROUPS
assert SSD_HEAD_DIM == RW_HEAD_DIM and SSD_GROUP_HEADS % 2 == 0

HG_HEADS = 8
HG_KEY_DIM = 128
HG_VAL_DIM = D_MODEL // HG_HEADS
HG_KEY_WIDTH = HG_HEADS * HG_KEY_DIM
HG_VAL_WIDTH = HG_HEADS * HG_VAL_DIM
HG_CHUNK = 64
HG_SUB = 16
HG_BLOCK_CHUNKS = 4
HG_SAFE_LOG_RANGE = 64.0
assert HG_KEY_DIM == HG_VAL_DIM

XA_HEADS = 4
XA_HEAD_DIM = D_MODEL // XA_HEADS
FFN_DIM = ((8 * D_MODEL + 3 * 256 - 1) // (3 * 256)) * 256

SUBLANES = 8
HALO_ROWS = 16
SHIFT_ROWS = 128
PROJ_COLS = 256
LANES = 128
VMEM_LIMIT_BYTES = 56 * 1024 * 1024
SCAN_STEP_ROWS = 1024
NEG_BIG = -1e30


def _params(*semantics):
    return pltpu.CompilerParams(dimension_semantics=semantics, vmem_limit_bytes=VMEM_LIMIT_BYTES)


def _tile(n, pref):
    t = min(n, pref)
    while n % t or t % SUBLANES:
        t -= 1
    return t


def _scan_rows(t_len, sub_rows):
    assert t_len % sub_rows == 0
    rows = sub_rows
    while rows * 2 <= SCAN_STEP_ROWS and t_len % (rows * 2) == 0:
        rows *= 2
    return rows


def _mm(a, b):
    return jnp.dot(a.astype(BF16), b.astype(BF16), preferred_element_type=F32)


def _mm_nt(a, b):
    return lax.dot_general(a.astype(BF16), b.astype(BF16), (((1,), (1,)), ((), ())),
                           preferred_element_type=F32)


def _mm_tn(a, b):
    return lax.dot_general(a.astype(BF16), b.astype(BF16), (((0,), (0,)), ((), ())),
                           preferred_element_type=F32)


def _split3(x):
    hi = x.astype(BF16)
    r1 = x - hi.astype(F32)
    mid = r1.astype(BF16)
    lo = (r1 - mid.astype(F32)).astype(BF16)
    return hi, mid, lo


def _mm_exact_lhs(m_bf16, x, terms=3):
    dot = functools.partial(jnp.dot, preferred_element_type=F32)
    pieces = _split3(x)[:terms]
    out = dot(m_bf16, pieces[0])
    for piece in pieces[1:]:
        out = out + dot(m_bf16, piece)
    return out


def _mm_exact_rhs_many(xs, m_bf16, terms=3):
    rows = xs[0].shape[0]
    pieces = [piece for x in xs for piece in _split3(x)[:terms]]
    out = jnp.dot(jnp.concatenate(pieces, axis=0), m_bf16, preferred_element_type=F32)
    results = []
    for i in range(len(xs)):
        total = out[i * terms * rows:(i * terms + 1) * rows]
        for t in range(1, terms):
            total = total + out[(i * terms + t) * rows:(i * terms + t + 1) * rows]
        results.append(total)
    return results


def _rms_normed(x, gain):
    ms = jnp.mean(x * x, axis=-1, keepdims=True)
    return x * lax.rsqrt(ms + NORM_EPS) * gain


def _sigmoid(x):
    return 0.5 * jnp.tanh(0.5 * x) + 0.5


def _silu(x):
    h = 0.5 * x
    return h * jnp.tanh(h) + h


def _softplus(x):
    return jnp.maximum(x, 0.0) + jnp.log(1.0 + jnp.exp(-jnp.abs(x)))


def _order_masks(n, reverse):
    row = lax.broadcasted_iota(jnp.int32, (n, n), 0)
    col = lax.broadcasted_iota(jnp.int32, (n, n), 1)
    if reverse:
        return col >= row, col > row
    return col <= row, col < row


def _norm_matmul_kernel(x_ref, g_ref, *refs, n_out):
    xn = _rms_normed(x_ref[...], g_ref[...]).astype(BF16)
    for w_ref, o_ref in zip(refs[:n_out], refs[n_out:]):
        o_ref[...] = jnp.dot(xn, w_ref[...], preferred_element_type=F32).astype(o_ref.dtype)


def _norm_matmul(x2d, gain, weights, out_dtypes, tm_pref=512):
    n, d = x2d.shape
    tm = _tile(n, tm_pref)
    in_specs = [pl.BlockSpec((tm, d), lambda i: (i, 0)), pl.BlockSpec((1, d), lambda i: (0, 0))]
    in_specs += [pl.BlockSpec(w.shape, lambda i: (0, 0)) for w in weights]
    out_specs = [pl.BlockSpec((tm, w.shape[1]), lambda i: (i, 0)) for w in weights]
    out_shape = [jax.ShapeDtypeStruct((n, w.shape[1]), dt) for w, dt in zip(weights, out_dtypes)]
    return pl.pallas_call(
        functools.partial(_norm_matmul_kernel, n_out=len(weights)),
        grid=(n // tm,), in_specs=in_specs, out_specs=out_specs, out_shape=out_shape,
        compiler_params=_params("parallel"), name="norm_matmul",
    )(x2d, gain.reshape(1, d), *weights)


def _neighbours(p, prev_blk, next_blk, first, last):
    tt = p.shape[0]
    ridx = lax.broadcasted_iota(jnp.int32, p.shape, 0)
    prev_row = jnp.where(first, 0.0, prev_blk[HALO_ROWS - 1:HALO_ROWS, :].astype(F32))
    next_row = jnp.where(last, 0.0, next_blk[0:1, :].astype(F32))
    prev = jnp.where(ridx == 0, prev_row, pltpu.roll(p, 1, 0))
    nxt = jnp.where(ridx == tt - 1, next_row, pltpu.roll(p, tt - 1, 0))
    return prev, nxt


def _shift_matrix(tt, offsets, weight):
    t = jnp.arange(tt)[:, None]
    c = jnp.arange(tt + 2 * HALO_ROWS)[None, :]
    hit = functools.reduce(jnp.logical_or, [c == HALO_ROWS + t + d for d in offsets])
    return jnp.where(hit, weight, 0.0).astype(BF16)


def _lane_group_sum(x, width):
    assert width % LANES == 0 and x.shape[1] % width == 0
    ones = jnp.ones((LANES, LANES), BF16)
    groups = []
    for c in range(0, x.shape[1], width):
        total = _mm(x[:, c:c + LANES], ones)
        for c2 in range(c + LANES, c + width, LANES):
            total = total + _mm(x[:, c2:c2 + LANES], ones)
        groups += [total] * (width // LANES)
    return jnp.concatenate(groups, axis=1)


def _head_sum(x, pair_ones_bf16):
    blocks = [_mm(x[:, c:c + LANES], pair_ones_bf16) for c in range(0, x.shape[1], LANES)]
    return jnp.concatenate(blocks, axis=1)


def _mix0_project(x_ref, xp_ref, xn_ref, gain_ref, wrkv_ref, wlora_ref, wz_ref, wxbc_ref, wdt_ref, first, last,
                  prkv_scr, plora_scr, u_scr, z_scr, dt_scr):
    tt = x_ref.shape[0]
    dot = functools.partial(jnp.dot, preferred_element_type=F32)
    x_ext = jnp.concatenate([jnp.where(first, 0.0, xp_ref[...]), x_ref[...], jnp.where(last, 0.0, xn_ref[...])], axis=0)
    xn_ext = _rms_normed(x_ext, gain_ref[...]).astype(BF16)
    xn = xn_ext[HALO_ROWS:HALO_ROWS + tt]

    def column_blocks(lhs, w_ref, scr, block):
        for c in range(0, scr.shape[1], block):
            scr[:, c:c + block] = dot(lhs, w_ref[:, c:c + block]).astype(scr.dtype)

    def project_rwkv():
        column_blocks(xn_ext, wrkv_ref, prkv_scr, RW_DIM)
        column_blocks(xn_ext, wlora_ref, plora_scr, RW_LORA)

    def project_ssd():
        column_blocks(xn_ext, wxbc_ref, u_scr, SSD_DIM)
        column_blocks(xn, wz_ref, z_scr, SSD_DIM)
        column_blocks(xn, wdt_ref, dt_scr, LANES)

    return project_rwkv, project_ssd


def _mix0_operands(project_ssd, prkv_scr, plora_scr, u_scr, z_scr, dt_scr,
                   savg_ref, mu_ref, mul_ref, a0_ref, a2_ref, g2_ref, kk_ref, ka_ref, w0_ref, w2_ref, rk_ref, bd_ref,
                   cw_ref, cb_ref, dtb_ref, alog_ref,
                   r_out, k_out, v_out, kkn_out, kka_out, lwf_out, lwb_out, g_out, bonus_out,
                   z_out, xbc_out, dtf_out, dtb_out, laf_out, lab_out):
    tt = z_scr.shape[0]
    dot = functools.partial(jnp.dot, preferred_element_type=F32)

    def token_shift(p_scr, m_ref, lo_col, hi_col):
        x = p_scr[HALO_ROWS:HALO_ROWS + tt, lo_col:hi_col].astype(F32)
        avg = jnp.concatenate(
            [dot(savg_ref[...], p_scr[t0:t0 + SHIFT_ROWS + 2 * HALO_ROWS, lo_col:hi_col])
             for t0 in range(0, tt, SHIFT_ROWS)], axis=0)
        return x + m_ref[:, lo_col:hi_col] * (avg - x)

    r = token_shift(prkv_scr, mu_ref, 0, RW_DIM)
    k = token_shift(prkv_scr, mu_ref, RW_DIM, 2 * RW_DIM)
    v = token_shift(prkv_scr, mu_ref, 2 * RW_DIM, 3 * RW_DIM)
    lo = token_shift(plora_scr, mul_ref, 0, RW_LORA)
    project_ssd()
    wd = lo[:, :RW_DECAY_RANK]
    ad = lo[:, RW_DECAY_RANK:RW_DECAY_RANK + RW_A_RANK]
    gd = lo[:, RW_DECAY_RANK + RW_A_RANK:]
    bd = bd_ref[...]

    a_logits = _mm(ad, a2_ref[...])
    g = _mm(_sigmoid(gd), g2_ref[...])
    kk = k * kk_ref[...]
    kk_sq = _head_sum(kk * kk, bd)
    w_lora = jnp.tanh(wd)
    decay_logits = [w0_ref[d:d + 1, :] + _mm(w_lora, w2_ref[d]) for d in range(2)]

    z_out[...] = z_scr[...]
    u_ext = u_scr[...]
    u = u_ext[HALO_ROWS:HALO_ROWS + tt].astype(F32)
    prev, nxt = _neighbours(u, u_ext[:HALO_ROWS], u_ext[HALO_ROWS + tt:], False, False)
    conv = prev * cw_ref[0:1, :] + u * cw_ref[1:2, :] + nxt * cw_ref[2:3, :] + cb_ref[...]
    xbc_out[...] = _silu(conv).astype(xbc_out.dtype)
    dt_raw = dt_scr[...]
    dt_f = _softplus(dt_raw + dtb_ref[0:1, :])
    dt_b = _softplus(dt_raw + dtb_ref[1:2, :])
    dtf_out[...] = dt_f
    dtb_out[...] = dt_b
    laf_out[...] = dt_f * (-jnp.exp(alog_ref[0:1, :]))
    lab_out[...] = dt_b * (-jnp.exp(alog_ref[1:2, :]))

    a = _sigmoid(a0_ref[...] + a_logits)
    kk = kk / jnp.maximum(jnp.sqrt(kk_sq), 1e-12)
    k = k * (1.0 + (a - 1.0) * ka_ref[...])
    r_out[...] = r.astype(r_out.dtype)
    k_out[...] = k.astype(k_out.dtype)
    v_out[...] = v.astype(v_out.dtype)
    kkn_out[...] = kk.astype(kkn_out.dtype)
    kka_out[...] = (kk * a).astype(kka_out.dtype)
    lwf_out[...] = -math.exp(-0.5) * _sigmoid(decay_logits[0])
    lwb_out[...] = -math.exp(-0.5) * _sigmoid(decay_logits[1])
    g_out[...] = g.astype(g_out.dtype)
    bonus_out[...] = (_head_sum(r * k * rk_ref[...], bd) * v).astype(bonus_out.dtype)


MIX0_N_PROJ_REFS = 9
MIX0_N_CONSTS = 16
MIX0_N_OUTS = 15
MIX0_N_STAGED = 5


def _mix0_in_kernel(*refs, per_seq):
    proj_refs = refs[:MIX0_N_PROJ_REFS]
    const_refs = refs[MIX0_N_PROJ_REFS:MIX0_N_PROJ_REFS + MIX0_N_CONSTS]
    out_refs = refs[MIX0_N_PROJ_REFS + MIX0_N_CONSTS:MIX0_N_PROJ_REFS + MIX0_N_CONSTS + MIX0_N_OUTS]
    staged = refs[MIX0_N_PROJ_REFS + MIX0_N_CONSTS + MIX0_N_OUTS:]
    pos = lax.rem(pl.program_id(0), per_seq)
    project_rwkv, project_ssd = _mix0_project(*proj_refs, pos == 0, pos == per_seq - 1, *staged)
    project_rwkv()
    _mix0_operands(project_ssd, *staged, *const_refs, *out_refs)


def _mix0_in(x2d, nb, t_len, gain, weights, rwkv_consts, ssd_consts):
    n, d = x2d.shape
    tt = _tile(t_len, 512)
    assert tt % HALO_ROWS == 0
    per_seq = t_len // tt
    n_tiles = n // tt
    blocks_per_tile = tt // HALO_ROWS
    n_blocks = n // HALO_ROWS

    def full(a):
        nd = a.ndim
        return pl.BlockSpec(a.shape, lambda s: (0,) * nd)

    x_specs = [pl.BlockSpec((tt, d), lambda s: (s, 0)),
               pl.BlockSpec((HALO_ROWS, d), lambda s: (jnp.maximum(s * blocks_per_tile - 1, 0), 0)),
               pl.BlockSpec((HALO_ROWS, d), lambda s: (jnp.minimum((s + 1) * blocks_per_tile, n_blocks - 1), 0))]
    proj_consts = (gain.reshape(1, d),) + tuple(weights)
    assert tt % SHIFT_ROWS == 0
    consts = (_shift_matrix(SHIFT_ROWS, (-1, 1), 0.5),) + tuple(rwkv_consts) + tuple(ssd_consts)
    assert 3 + len(proj_consts) == MIX0_N_PROJ_REFS and len(consts) == MIX0_N_CONSTS
    wide = [(RW_DIM, ACT)] * 5 + [(RW_DIM, F32)] * 2 + [(RW_DIM, ACT)] * 2 + [(SSD_DIM, ACT), (SSD_CONV_DIM, ACT)] \
        + [(LANES, F32)] * 4
    assert len(wide) == MIX0_N_OUTS
    ext = tt + 2 * HALO_ROWS
    staged = [pltpu.VMEM((ext, 3 * RW_DIM), ACT), pltpu.VMEM((ext, RW_LORA), ACT), pltpu.VMEM((ext, SSD_CONV_DIM), ACT),
              pltpu.VMEM((tt, SSD_DIM), ACT), pltpu.VMEM((tt, LANES), F32)]
    assert len(staged) == MIX0_N_STAGED
    return pl.pallas_call(
        functools.partial(_mix0_in_kernel, per_seq=per_seq),
        grid=(n_tiles,),
        in_specs=x_specs + [full(a) for a in proj_consts + consts],
        out_specs=[pl.BlockSpec((tt, w), lambda s: (s, 0)) for w, _ in wide],
        out_shape=[jax.ShapeDtypeStruct((n, w), dt) for w, dt in wide],
        scratch_shapes=staged,
        compiler_params=_params("parallel"), name="mix0_in",
    )(x2d, x2d, x2d, *proj_consts, *consts)


def _pair_blockdiag(y):
    lane = lax.broadcasted_iota(jnp.int32, y.shape, 1)
    top = jnp.where(lane < RW_HEAD_DIM, y, 0.0).astype(BF16)
    bottom = jnp.where(lane >= RW_HEAD_DIM, y, 0.0).astype(BF16)
    return jnp.concatenate([top, bottom], axis=0)


def _pair_diag_blocks(full):
    lane = lax.broadcasted_iota(jnp.int32, (RW_HEAD_DIM, LANES), 1)
    return jnp.where(lane < RW_HEAD_DIM, full[:RW_HEAD_DIM], full[RW_HEAD_DIM:])


def _unit_lower_inverses(n_strict_list, eye):
    size = n_strict_list[0].shape[0]
    levels = int(math.log2(size)) - 1
    accs = [eye + n for n in n_strict_list]
    powers = [_mm(n, _pair_blockdiag(n)) for n in n_strict_list]
    for level in range(levels):
        power_bds = [_pair_blockdiag(p) for p in powers]
        if level == levels - 1:
            return [a + _mm(a, b) for a, b in zip(accs, power_bds)]
        both = [_mm(jnp.concatenate([a, p], axis=0), b) for a, p, b in zip(accs, powers, power_bds)]
        accs = [a + ab[:size] for a, ab in zip(accs, both)]
        powers = [ab[size:] for ab in both]


def _rwkv_scan_kernel(r_ref, k_ref, v_ref, kk_ref, kka_ref, lw_ref, o_ref, h_ref, *, reverse):
    @pl.when(pl.program_id(1) == 0)
    def _():
        h_ref[...] = jnp.zeros_like(h_ref)

    sub_rows = RW_BLOCK_CHUNKS * RW_CHUNK
    n_sub = r_ref.shape[0] // sub_rows

    def body(i, carry):
        blk = (n_sub - 1 - i) if reverse else i
        _rwkv_scan_block(r_ref, k_ref, v_ref, kk_ref, kka_ref, lw_ref, o_ref, h_ref,
                         pl.multiple_of(blk * sub_rows, sub_rows), reverse)
        return carry

    lax.fori_loop(0, n_sub, body, 0)


def _rwkv_scan_block(r_ref, k_ref, v_ref, kk_ref, kka_ref, lw_ref, o_ref, h_ref, base, reverse):
    chunk = RW_CHUNK
    n_chunks = RW_BLOCK_CHUNKS
    incl_bf = _order_masks(chunk, reverse)[0].astype(BF16)
    row = lax.broadcasted_iota(jnp.int32, (chunk, LANES), 0)
    pos = lax.broadcasted_iota(jnp.int32, (chunk, LANES), 1) % RW_HEAD_DIM
    incl, strict = (pos >= row, pos > row) if reverse else (pos <= row, pos < row)
    eye = (pos == row).astype(F32)
    end = 0 if reverse else chunk - 1
    chunk_order = list(range(n_chunks - 1, -1, -1)) if reverse else list(range(n_chunks))

    items = []
    for ci in chunk_order:
        rows = pl.ds(base + ci * chunk, chunk)
        lw = lw_ref[rows, :]
        cum = _mm_exact_lhs(incl_bf, lw, terms=2)
        cum_end = cum[end:end + 1, :]
        e_neg = jnp.exp(-cum)
        p_end = jnp.exp(cum_end)
        e_tail = p_end * e_neg
        k = k_ref[rows, :].astype(F32)
        kka = kka_ref[rows, :].astype(F32)
        at_all = -kk_ref[rows, :].astype(F32) * jnp.exp(cum - lw)
        bt_all = kka * e_neg
        kt_all = k * e_neg
        rt_all = r_ref[rows, :].astype(F32) * jnp.exp(cum)
        bh_all = kka * e_tail
        kh_all = k * e_tail
        v = v_ref[rows, :].astype(F32)
        for j in range(RW_HEAD_PAIRS):
            sl = slice(j * LANES, (j + 1) * LANES)
            items.append(dict(ci=ci, j=j, at=at_all[:, sl], bt=bt_all[:, sl], kt=kt_all[:, sl], rt=rt_all[:, sl],
                              bh=bh_all[:, sl], kh=kh_all[:, sl], v=v[:, sl], p_end=p_end[:, sl]))

    for it in items:
        lhs = jnp.concatenate([it["at"], it["rt"]], axis=0)
        both = _mm_nt(lhs, jnp.concatenate([_pair_blockdiag(it["bt"]), _pair_blockdiag(it["kt"])], axis=0))
        it["a_ab"] = jnp.where(strict, both[:chunk, :LANES], 0.0)
        it["a_rb"] = jnp.where(incl, both[chunk:, :LANES], 0.0)
        it["a_ak"] = jnp.where(strict, both[:chunk, LANES:], 0.0)
        it["a_rk"] = jnp.where(incl, both[chunk:, LANES:], 0.0)
    t_invs = _unit_lower_inverses([it["a_ab"] for it in items], eye)
    for it in items:
        akv_rkv = _mm(jnp.concatenate([it["a_ak"], it["a_rk"]], axis=0), _pair_blockdiag(it["v"]))
        it["x1"] = akv_rkv[:chunk]
        it["rk_v"] = akv_rkv[chunk:]
    for it, t_inv in zip(items, t_invs):
        wu_u0 = _mm(t_inv, jnp.concatenate([_pair_blockdiag(it["at"]), _pair_blockdiag(it["x1"])], axis=1))
        it["w_u"] = wu_u0[:, :LANES]
        it["u0"] = wu_u0[:, LANES:]
    for it in items:
        rb_wu_u0 = _mm(it["a_rb"],
                       jnp.concatenate([_pair_blockdiag(it["w_u"]), _pair_blockdiag(it["u0"])], axis=1))
        w_o = it["rt"] + rb_wu_u0[:, :LANES]
        g_mat = eye * it["p_end"] + _pair_diag_blocks(_mm_tn(it["bh"], it["w_u"]))
        it["wo_g"] = jnp.concatenate([w_o, g_mat], axis=0)
        it["o0"] = rb_wu_u0[:, LANES:] + it["rk_v"]
        it["h0"] = _pair_diag_blocks(_mm_tn(jnp.concatenate([it["bh"], it["kh"]], axis=0),
                                            jnp.concatenate([it["u0"], it["v"]], axis=0)))

    states = [h_ref[j] for j in range(RW_HEAD_PAIRS)]
    for it in items:
        j = it["j"]
        res = _mm(it["wo_g"], _pair_blockdiag(states[j]))
        rows = pl.ds(base + it["ci"] * chunk, chunk)
        o_ref[rows, j * LANES:(j + 1) * LANES] = (res[:chunk] + it["o0"]).astype(o_ref.dtype)
        states[j] = res[chunk:] + it["h0"]
    for j in range(RW_HEAD_PAIRS):
        h_ref[j] = states[j]


def _rwkv_scan(r, k, v, kk, kka, lw, nb, t_len, reverse):
    n = r.shape[0]
    tb = _scan_rows(t_len, RW_BLOCK_CHUNKS * RW_CHUNK)
    nc = t_len // tb

    def idx(b, c):
        return (b * nc + (nc - 1 - c if reverse else c), 0)

    spec = pl.BlockSpec((tb, RW_DIM), idx)
    return pl.pallas_call(
        functools.partial(_rwkv_scan_kernel, reverse=reverse),
        grid=(nb, nc), in_specs=[spec] * 6, out_specs=spec,
        out_shape=jax.ShapeDtypeStruct((n, RW_DIM), ACT),
        scratch_shapes=[pltpu.VMEM((RW_HEAD_PAIRS, RW_HEAD_DIM, LANES), F32)],
        compiler_params=_params("parallel", "arbitrary"), name="rwkv_scan_bwd" if reverse else "rwkv_scan_fwd",
    )(r, k, v, kk, kka, lw)


def _ssd_scan_kernel(xbc_ref, dt_ref, la_ref, ex64_ref, y_ref, s_ref, *, reverse):
    @pl.when(pl.program_id(1) == 0)
    def _():
        s_ref[...] = jnp.zeros_like(s_ref)

    sub_rows = SSD_BLOCK_CHUNKS * SSD_CHUNK
    n_sub = xbc_ref.shape[0] // sub_rows

    def body(i, carry):
        blk = (n_sub - 1 - i) if reverse else i
        _ssd_scan_block(xbc_ref, dt_ref, la_ref, ex64_ref, y_ref, s_ref,
                        pl.multiple_of(blk * sub_rows, sub_rows), reverse)
        return carry

    lax.fori_loop(0, n_sub, body, 0)


def _ssd_scan_block(xbc_ref, dt_ref, la_ref, ex64_ref, y_ref, s_ref, base, reverse):
    chunk = SSD_CHUNK
    n_chunks = SSD_BLOCK_CHUNKS
    incl, _ = _order_masks(chunk, reverse)
    incl_bf = incl.astype(BF16)
    end = 0 if reverse else chunk - 1
    gn = SSD_GROUPS * SSD_STATE
    gw = SSD_GROUP_HEADS * SSD_HEAD_DIM
    chunk_order = list(range(n_chunks - 1, -1, -1)) if reverse else list(range(n_chunks))
    ex64 = ex64_ref[...]

    pre = []
    for ci in chunk_order:
        rows = pl.ds(base + ci * chunk, chunk)
        la = la_ref[rows, :]
        cum = _mm_exact_lhs(incl_bf, la, terms=2)
        cum_rows = cum.T
        cum_end = cum[end:end + 1, :]
        ecum64, etail64, dt64 = _mm_exact_rhs_many(
            [jnp.exp(cum), jnp.exp(cum_end - cum), dt_ref[rows, :]], ex64, terms=2)
        xdt = xbc_ref[rows, :SSD_DIM].astype(F32) * dt64
        pre.append(dict(rows=rows, cum=cum, cum_rows=cum_rows, ecum64=ecum64, eend64=ecum64[end:end + 1, :],
                        xdt=xdt, xtail=xdt * etail64,
                        bm=[xbc_ref[rows, SSD_DIM + g * SSD_STATE:SSD_DIM + (g + 1) * SSD_STATE]
                            for g in range(SSD_GROUPS)],
                        cm=[xbc_ref[rows, SSD_DIM + gn + g * SSD_STATE:SSD_DIM + gn + (g + 1) * SSD_STATE]
                            for g in range(SSD_GROUPS)]))
    for p in pre:
        p["scores"] = [_mm_nt(p["cm"][g], p["bm"][g]) for g in range(SSD_GROUPS)]
        p["inject"] = [_mm_tn(p["bm"][g], p["xtail"][:, g * gw:(g + 1) * gw]) for g in range(SSD_GROUPS)]
    for p in pre:
        weights = []
        for h in range(SSD_HEADS):
            diff = p["cum"][:, h:h + 1] - p["cum_rows"][h:h + 1, :]
            decay = jnp.exp(jnp.where(incl, diff, NEG_BIG))
            weights.append(p["scores"][h // SSD_GROUP_HEADS] * decay)
        p["ydiag"] = [_mm(jnp.concatenate([weights[h], weights[h + 1]], axis=1),
                          _pair_blockdiag(p["xdt"][:, h * SSD_HEAD_DIM:(h + 2) * SSD_HEAD_DIM]))
                      for h in range(0, SSD_HEADS, 2)]

    states = [s_ref[g] for g in range(SSD_GROUPS)]
    for p in pre:
        for g in range(SSD_GROUPS):
            y_off = _mm(p["cm"][g], states[g]) * p["ecum64"][:, g * gw:(g + 1) * gw]
            for jp in range(SSD_GROUP_HEADS // 2):
                pair = g * (SSD_GROUP_HEADS // 2) + jp
                y_ref[p["rows"], pair * LANES:(pair + 1) * LANES] = (
                    p["ydiag"][pair] + y_off[:, jp * LANES:(jp + 1) * LANES]).astype(y_ref.dtype)
            states[g] = states[g] * p["eend64"][:, g * gw:(g + 1) * gw] + p["inject"][g]
    for g in range(SSD_GROUPS):
        s_ref[g] = states[g]


def _ssd_scan(xbc, dt, la, nb, t_len, reverse):
    n = xbc.shape[0]
    tb = _scan_rows(t_len, SSD_BLOCK_CHUNKS * SSD_CHUNK)
    assert SSD_CHUNK == LANES
    nc = t_len // tb
    lane_head = jnp.arange(LANES)[:, None]
    ex64 = (lane_head == jnp.arange(SSD_DIM)[None, :] // SSD_HEAD_DIM).astype(BF16)

    def idx(b, c):
        return (b * nc + (nc - 1 - c if reverse else c), 0)

    def full(a):
        return pl.BlockSpec(a.shape, lambda b, c: (0, 0))

    return pl.pallas_call(
        functools.partial(_ssd_scan_kernel, reverse=reverse),
        grid=(nb, nc),
        in_specs=[pl.BlockSpec((tb, SSD_CONV_DIM), idx), pl.BlockSpec((tb, LANES), idx),
                  pl.BlockSpec((tb, LANES), idx), full(ex64)],
        out_specs=pl.BlockSpec((tb, SSD_DIM), idx),
        out_shape=jax.ShapeDtypeStruct((n, SSD_DIM), ACT),
        scratch_shapes=[pltpu.VMEM((SSD_GROUPS, SSD_STATE, SSD_GROUP_HEADS * SSD_HEAD_DIM), F32)],
        compiler_params=_params("parallel", "arbitrary"), name="ssd_scan_bwd" if reverse else "ssd_scan_fwd",
    )(xbc, dt, la, ex64)


def _mix0_out_kernel(x_ref, of_ref, ob_ref, bonus_ref, g_ref, yf_ref, yb_ref, xbc_ref, z_ref,
                     gnw_ref, gnb_ref, bd_ref, dsk_ref, nw_ref, wo_rw_ref, wo_ssd_ref, out_ref):
    bd = bd_ref[...]
    o = of_ref[...].astype(F32) + ob_ref[...].astype(F32)
    inv_n = 1.0 / RW_HEAD_DIM
    mu = _head_sum(o, bd) * inv_n
    oc = o - mu
    var = _head_sum(oc * oc, bd) * inv_n
    o = oc * lax.rsqrt(var + RW_GN_EPS) * gnw_ref[...] + gnb_ref[...]
    o_rw = (o + bonus_ref[...].astype(F32)) * g_ref[...].astype(F32)

    y = yf_ref[...].astype(F32) + yb_ref[...].astype(F32) + dsk_ref[...] * xbc_ref[...].astype(F32)
    y = y * _silu(z_ref[...].astype(F32))
    gw = SSD_DIM // SSD_GROUPS
    parts = []
    for g in range(SSD_GROUPS):
        yg = y[:, g * gw:(g + 1) * gw]
        parts.append(yg * lax.rsqrt(jnp.mean(yg * yg, axis=-1, keepdims=True) + NORM_EPS))
    o_ssd = jnp.concatenate(parts, axis=-1) * nw_ref[...]
    out_ref[...] = x_ref[...] + _mm(o_rw, wo_rw_ref[...]) + _mm(o_ssd, wo_ssd_ref[...])


def _mix0_out(x2d, o_f, o_b, bonus, g, y_f, y_b, xbc, z, gn_w, gn_b, blockdiag, d_skip, norm_w, wo_rw, wo_ssd):
    n = x2d.shape[0]
    tm = _tile(n, 512)

    def row(width):
        return pl.BlockSpec((tm, width), lambda i: (i, 0))

    def full(a):
        return pl.BlockSpec(a.shape, lambda i: (0, 0))

    consts = (gn_w, gn_b, blockdiag, d_skip, norm_w, wo_rw, wo_ssd)
    return pl.pallas_call(
        _mix0_out_kernel,
        grid=(n // tm,),
        in_specs=[row(D_MODEL)] + [row(RW_DIM)] * 6 + [row(SSD_DIM), row(SSD_DIM)] + [full(a) for a in consts],
        out_specs=row(D_MODEL),
        out_shape=jax.ShapeDtypeStruct((n, D_MODEL), F32),
        compiler_params=_params("parallel"), name="mix0_out",
    )(x2d, o_f, o_b, bonus, g, y_f, y_b, xbc, z, *consts)


def _hgrn_lower_bound(lb_ref, layer):
    x = lb_ref[...]
    m = jnp.max(x, axis=0, keepdims=True)
    e = jnp.exp(x - m)
    s = e / jnp.sum(e, axis=0, keepdims=True)
    lb = jnp.zeros_like(s[0:1, :])
    for i in range(1, layer + 1):
        lb = lb + s[i:i + 1, :]
    return lb


def _hgrn_in_kernel(x_ref, g_ref, lb_ref, wq_ref, wff_ref, wfb_ref, wi_ref, wg_ref,
                    q_out, lff_out, lfb_out, kgf_out, kgb_out, i_out, g_out, *, layer):
    xn = _rms_normed(x_ref[...], g_ref[...]).astype(BF16)
    lb = _hgrn_lower_bound(lb_ref, layer)
    dot = functools.partial(jnp.dot, preferred_element_type=F32)
    width = x_ref.shape[1]
    jobs = []
    for c in range(0, width, PROJ_COLS):
        cols = slice(c, c + PROJ_COLS)
        jobs.append((wff_ref, cols, ("gate", lff_out, kgf_out)))
        jobs.append((wq_ref, cols, ("plain", q_out)))
        jobs.append((wfb_ref, cols, ("gate", lfb_out, kgb_out)))
        jobs.append((wi_ref, cols, ("plain", i_out)))
        jobs.append((wg_ref, cols, ("plain", g_out)))

    def finish(acc, cols, how):
        if how[0] == "gate":
            f = lb[:, cols] + (1.0 - lb[:, cols]) * _sigmoid(acc)
            how[1][:, cols] = jnp.log(f)
            how[2][:, cols] = (1.0 - f).astype(how[2].dtype)
        else:
            how[1][:, cols] = acc.astype(how[1].dtype)

    pending = None
    for w_ref, cols, how in jobs:
        acc = dot(xn, w_ref[:, cols])
        if pending is not None:
            finish(*pending)
        pending = (acc, cols, how)
    finish(*pending)


def _hgrn_in(x2d, gain, hg_lb, weights, layer):
    n, d = x2d.shape
    tm = _tile(n, 512)
    row = pl.BlockSpec((tm, d), lambda i: (i, 0))
    in_specs = [row, pl.BlockSpec((1, d), lambda i: (0, 0)), pl.BlockSpec(hg_lb.shape, lambda i: (0, 0))]
    in_specs += [pl.BlockSpec(w.shape, lambda i: (0, 0)) for w in weights]
    dtypes = (ACT, F32, F32, ACT, ACT, ACT, ACT)
    return pl.pallas_call(
        functools.partial(_hgrn_in_kernel, layer=layer),
        grid=(n // tm,), in_specs=in_specs, out_specs=[row] * len(dtypes),
        out_shape=[jax.ShapeDtypeStruct((n, d), dt) for dt in dtypes],
        compiler_params=_params("parallel"), name="hgrn_in",
    )(x2d, gain.reshape(1, d), hg_lb, *weights)


def _hgrn_chunk(q, lf, kg, vv, state_t, reverse):
    chunk = q.shape[0]
    n_sub = chunk // HG_SUB
    incl, _ = _order_masks(chunk, reverse)
    end = 0 if reverse else chunk - 1
    s_idx = lax.broadcasted_iota(jnp.int32, (HG_SUB, 1), 0)

    bcum = _mm_exact_lhs(incl.astype(BF16), lf)
    b_end = bcum[end:end + 1, :]
    carried = _mm_nt(q * jnp.exp(bcum), state_t)

    order = list(range(n_sub - 1, -1, -1)) if reverse else list(range(n_sub))
    outs = [None] * n_sub
    for pos, sb in enumerate(order):
        rows = slice(sb * HG_SUB, (sb + 1) * HG_SUB)
        q_s, k_s, b_s, v_s = q[rows], kg[rows], bcum[rows], vv[rows]
        acc = carried[rows]
        if pos > 0:
            prev_sb = order[pos - 1]
            edge = prev_sb * HG_SUB if reverse else prev_sb * HG_SUB + HG_SUB - 1
            b_edge = bcum[edge:edge + 1, :]
            src = slice((sb + 1) * HG_SUB, chunk) if reverse else slice(0, sb * HG_SUB)
            q_hat = q_s * jnp.exp(b_s - b_edge)
            k_hat = kg[src] * jnp.exp(b_edge - bcum[src])
            acc = acc + _mm(_mm_nt(q_hat, k_hat), vv[src])
        rows_out = []
        for l in range(HG_SUB):
            w = jnp.exp(jnp.minimum(b_s[l:l + 1, :] - b_s, 0.0)) * k_s * q_s[l:l + 1, :]
            att = jnp.sum(w, axis=-1, keepdims=True)
            att = jnp.where((s_idx >= l) if reverse else (s_idx <= l), att, 0.0)
            rows_out.append(jnp.sum(att * v_s, axis=0, keepdims=True))
        outs[sb] = acc + jnp.concatenate(rows_out, axis=0)

    k_tail = kg * jnp.exp(b_end - bcum)
    new_state_t = state_t * jnp.exp(b_end) + _mm_tn(vv, k_tail)
    return jnp.concatenate(outs, axis=0), new_state_t


def _hgrn_scan_kernel(q_ref, lf_ref, kg_ref, i_ref, lb_ref, o_ref, s_ref, hq_ref, hf_ref, hk_ref, hv_ref, ho_ref, *,
                      reverse, layer):
    @pl.when(pl.program_id(1) == 0)
    def _():
        s_ref[...] = jnp.zeros_like(s_ref)

    sub_rows = HG_BLOCK_CHUNKS * HG_CHUNK
    n_sub = q_ref.shape[0] // sub_rows

    def body(i, carry):
        blk = (n_sub - 1 - i) if reverse else i
        _hgrn_scan_block(q_ref, lf_ref, kg_ref, i_ref, lb_ref, o_ref, s_ref, hq_ref, hf_ref, hk_ref, hv_ref, ho_ref,
                         pl.multiple_of(blk * sub_rows, sub_rows), reverse, layer)
        return carry

    lax.fori_loop(0, n_sub, body, 0)


def _hgrn_scan_block(q_ref, lf_ref, kg_ref, i_ref, lb_ref, o_ref, s_ref, hq_ref, hf_ref, hk_ref, hv_ref, ho_ref,
                     base, reverse, layer):
    chunk = HG_CHUNK
    n_chunks = HG_BLOCK_CHUNKS
    incl, _ = _order_masks(chunk, reverse)
    incl_bf = incl.astype(BF16)
    end = 0 if reverse else chunk - 1
    chunk_order = list(range(n_chunks - 1, -1, -1)) if reverse else list(range(n_chunks))

    def head_cols(h):
        return slice(h * HG_KEY_DIM, (h + 1) * HG_KEY_DIM)

    pre = []
    worst = None
    for ci in chunk_order:
        rows = pl.ds(base + ci * chunk, chunk)
        bcum = _mm_exact_lhs(incl_bf, lf_ref[rows, :], terms=2)
        b_end = bcum[end:end + 1, :]
        worst = b_end if worst is None else jnp.minimum(worst, b_end)
        pre.append((ci, rows, bcum, b_end))
    lb_min = jnp.min(_hgrn_lower_bound(lb_ref, layer))
    flag = lax.cond(lb_min >= math.exp(-HG_SAFE_LOG_RANGE / HG_CHUNK),
                    lambda: jnp.int32(1),
                    lambda: (jnp.min(worst) >= -HG_SAFE_LOG_RANGE).astype(jnp.int32))
    safe = flag == 1

    @pl.when(safe)
    def _():
        items = []
        for ci, rows, bcum, b_end in pre:
            kg = kg_ref[rows, :].astype(F32)
            e_neg = jnp.exp(-bcum)
            e_end = jnp.exp(b_end)
            qt_all = q_ref[rows, :].astype(F32) * jnp.exp(bcum)
            kt_all = kg * e_neg
            ktail_all = kt_all * e_end
            v_all = i_ref[rows, :].astype(F32)
            for h in range(HG_HEADS):
                c = head_cols(h)
                items.append(dict(rows=rows, h=h, qt=qt_all[:, c], kt=kt_all[:, c], ktail=ktail_all[:, c],
                                  v=v_all[:, c], e_end=e_end[:, c]))
        for it in items:
            it["attn"] = jnp.where(incl, _mm_nt(it["qt"], it["kt"]), 0.0)
        for it in items:
            it["local"] = _mm(it["attn"], it["v"])
            it["inject"] = _mm_tn(it["v"], it["ktail"])
        states = [s_ref[h] for h in range(HG_HEADS)]
        for it in items:
            h = it["h"]
            o_ref[it["rows"], head_cols(h)] = (it["local"] + _mm_nt(it["qt"], states[h])).astype(o_ref.dtype)
            states[h] = states[h] * it["e_end"] + it["inject"]
        for h in range(HG_HEADS):
            s_ref[h] = states[h]

    @pl.when(jnp.logical_not(safe))
    def _():
        def head_body(h, carry):
            out, new_state = _hgrn_chunk(hq_ref[h], hf_ref[h], hk_ref[h], hv_ref[h], s_ref[h], reverse)
            ho_ref[h] = out
            s_ref[h] = new_state
            return carry

        def chunk_body(i, carry):
            ci = (n_chunks - 1 - i) if reverse else i
            rows = pl.ds(pl.multiple_of(base + ci * chunk, chunk), chunk)
            for h in range(HG_HEADS):
                hq_ref[h] = q_ref[rows, head_cols(h)].astype(F32)
                hf_ref[h] = lf_ref[rows, head_cols(h)]
                hk_ref[h] = kg_ref[rows, head_cols(h)].astype(F32)
                hv_ref[h] = i_ref[rows, head_cols(h)].astype(F32)
            lax.fori_loop(0, HG_HEADS, head_body, 0)
            for h in range(HG_HEADS):
                o_ref[rows, head_cols(h)] = ho_ref[h].astype(o_ref.dtype)
            return carry

        lax.fori_loop(0, n_chunks, chunk_body, 0)


def _hgrn_scan(q, lf, kg, i_val, hg_lb, nb, t_len, reverse, layer):
    n = q.shape[0]
    tb = _scan_rows(t_len, HG_BLOCK_CHUNKS * HG_CHUNK)
    n_outer = t_len // tb

    def idx(b, c):
        return (b * n_outer + (n_outer - 1 - c if reverse else c), 0)

    spec = pl.BlockSpec((tb, HG_KEY_WIDTH), idx)
    head_buf = pltpu.VMEM((HG_HEADS, HG_CHUNK, HG_KEY_DIM), F32)
    return pl.pallas_call(
        functools.partial(_hgrn_scan_kernel, reverse=reverse, layer=layer),
        grid=(nb, n_outer),
        in_specs=[spec, spec, spec, spec, pl.BlockSpec(hg_lb.shape, lambda b, c: (0, 0))],
        out_specs=spec,
        out_shape=jax.ShapeDtypeStruct((n, HG_VAL_WIDTH), ACT),
        scratch_shapes=[pltpu.VMEM((HG_HEADS, HG_VAL_DIM, HG_KEY_DIM), F32)] + [head_buf] * 5,
        compiler_params=_params("parallel", "arbitrary"),
        name="hgrn_scan_bwd" if reverse else "hgrn_scan_fwd",
    )(q, lf, kg, i_val, hg_lb)


def _hgrn_out_kernel(x_ref, of_ref, ob_ref, g_ref, nw_ref, wo_ref, out_ref):
    o = of_ref[...].astype(F32) + ob_ref[...].astype(F32)
    mean_sq = _lane_group_sum(o * o, HG_VAL_DIM) * (1.0 / HG_VAL_DIM)
    o = o * lax.rsqrt(mean_sq + NORM_EPS) * nw_ref[...] * _silu(g_ref[...].astype(F32))
    out_ref[...] = x_ref[...] + _mm(o, wo_ref[...])


def _hgrn_out(x2d, o_f, o_b, g, norm_w, w_out):
    n = x2d.shape[0]
    tm = _tile(n, 512)
    row = pl.BlockSpec((tm, D_MODEL), lambda i: (i, 0))
    return pl.pallas_call(
        _hgrn_out_kernel,
        grid=(n // tm,),
        in_specs=[row] * 4 + [pl.BlockSpec((1, HG_VAL_WIDTH), lambda i: (0, 0)),
                              pl.BlockSpec(w_out.shape, lambda i: (0, 0))],
        out_specs=row,
        out_shape=jax.ShapeDtypeStruct((n, D_MODEL), F32),
        compiler_params=_params("parallel"), name="hgrn_out",
    )(x2d, o_f, o_b, g, norm_w, w_out)


def _xattn_kernel(x_ref, g_ref, wq_ref, kv_ref, wo_ref, out_ref):
    x = x_ref[...]
    q = _mm(_rms_normed(x, g_ref[...]), wq_ref[...])
    scale = XA_HEAD_DIM ** -0.5
    scores = [_mm_nt(q[:, h * XA_HEAD_DIM:(h + 1) * XA_HEAD_DIM], kv_ref[:, h * XA_HEAD_DIM:(h + 1) * XA_HEAD_DIM])
              * scale for h in range(XA_HEADS)]
    probs = []
    for s in scores:
        p = jnp.exp(s - jnp.max(s, axis=-1, keepdims=True))
        probs.append(p * (1.0 / jnp.sum(p, axis=-1, keepdims=True)))
    heads = [_mm(p, kv_ref[:, D_MODEL + h * XA_HEAD_DIM:D_MODEL + (h + 1) * XA_HEAD_DIM])
             for h, p in enumerate(probs)]
    out_ref[...] = x + _mm(jnp.concatenate(heads, axis=-1), wo_ref[...])


def _xattn(x2d, gain, wq, kv, wo, nb, t_len, mem_len):
    n = x2d.shape[0]
    tm = _tile(t_len, 1024)
    per_seq = t_len // tm
    row = pl.BlockSpec((tm, D_MODEL), lambda b, j: (b * per_seq + j, 0))
    return pl.pallas_call(
        _xattn_kernel,
        grid=(nb, per_seq),
        in_specs=[row, pl.BlockSpec((1, D_MODEL), lambda b, j: (0, 0)),
                  pl.BlockSpec(wq.shape, lambda b, j: (0, 0)),
                  pl.BlockSpec((mem_len, 2 * D_MODEL), lambda b, j: (b, 0)),
                  pl.BlockSpec(wo.shape, lambda b, j: (0, 0))],
        out_specs=row,
        out_shape=jax.ShapeDtypeStruct((n, D_MODEL), F32),
        compiler_params=_params("parallel", "parallel"), name="xattn",
    )(x2d, gain.reshape(1, D_MODEL), wq, kv, wo)


def _ffn_kernel(x_ref, g_ref, wi_ref, wo_ref, fg_ref, out_ref, *, final_norm):
    x = x_ref[...]
    gu = _mm(_rms_normed(x, g_ref[...]), wi_ref[...])
    act = _silu(gu[:, :FFN_DIM]) * gu[:, FFN_DIM:]
    y = x + _mm(act, wo_ref[...])
    if final_norm:
        y = _rms_normed(y, fg_ref[...])
    out_ref[...] = y


def _ffn(x2d, gain, w_in, w_out, final_gain, final_norm):
    n = x2d.shape[0]
    tm = _tile(n, 512)
    row = pl.BlockSpec((tm, D_MODEL), lambda i: (i, 0))
    vec = pl.BlockSpec((1, D_MODEL), lambda i: (0, 0))
    resident = pl.Buffered(1)
    return pl.pallas_call(
        functools.partial(_ffn_kernel, final_norm=final_norm),
        grid=(n // tm,),
        in_specs=[row, vec, pl.BlockSpec(w_in.shape, lambda i: (0, 0), pipeline_mode=resident),
                  pl.BlockSpec(w_out.shape, lambda i: (0, 0), pipeline_mode=resident), vec],
        out_specs=row,
        out_shape=jax.ShapeDtypeStruct((n, D_MODEL), F32),
        compiler_params=_params("parallel"), name="ffn",
    )(x2d, gain.reshape(1, D_MODEL), w_in, w_out, final_gain.reshape(1, D_MODEL))


def _pad_lanes(a, width=LANES):
    return jnp.pad(a, [(0, 0)] * (a.ndim - 1) + [(0, width - a.shape[-1])])


def _rwkv_ssd_layer(x2d, nb, t_len, norm_g, w_in, w_out, rw_mu, rw_w0, rw_w2, rw_a0, rw_a2, rw_g2, rw_k_k, rw_k_a,
                    rw_r_k, rw_gn_w, rw_gn_b, conv_w, conv_b, dt_bias, a_log, d_skip, ssd_norm_w):
    o_lora = 3 * RW_DIM
    o_z = RW_PROJ
    o_xbc = o_z + SSD_DIM
    o_dt = o_xbc + SSD_CONV_DIM
    wb = w_in.astype(BF16)
    weights = [wb[:, :o_lora], wb[:, o_lora:o_z], wb[:, o_z:o_xbc], wb[:, o_xbc:o_dt], _pad_lanes(wb[:, o_dt:])]
    head_id = jnp.arange(LANES) // RW_HEAD_DIM
    blockdiag = (head_id[:, None] == head_id[None, :]).astype(BF16)
    row = lambda a: a.reshape(1, -1)
    rwkv_consts = (row(rw_mu[:o_lora]), row(rw_mu[o_lora:]), row(rw_a0), rw_a2.astype(BF16), rw_g2.astype(BF16),
                   row(rw_k_k), row(rw_k_a), rw_w0, rw_w2.astype(BF16), row(rw_r_k), blockdiag)
    ssd_consts = (conv_w, row(conv_b), _pad_lanes(dt_bias), _pad_lanes(a_log))
    (r, k, v, kk, kka, lw_f, lw_b, g, bonus, p_z, xbc, dt_f, dt_b, la_f, la_b) = _mix0_in(
        x2d, nb, t_len, norm_g, weights, rwkv_consts, ssd_consts)
    o_f = _rwkv_scan(r, k, v, kk, kka, lw_f, nb, t_len, reverse=False)
    o_b = _rwkv_scan(r, k, v, kk, kka, lw_b, nb, t_len, reverse=True)
    y_f = _ssd_scan(xbc, dt_f, la_f, nb, t_len, reverse=False)
    y_b = _ssd_scan(xbc, dt_b, la_b, nb, t_len, reverse=True)

    wo = w_out.astype(BF16)
    d_skip_lanes = jnp.repeat(d_skip, SSD_HEAD_DIM).reshape(1, SSD_DIM)
    return _mix0_out(x2d, o_f, o_b, bonus, g, y_f, y_b, xbc, p_z, row(rw_gn_w), row(rw_gn_b), blockdiag,
                     d_skip_lanes, row(ssd_norm_w), wo[:RW_DIM], wo[RW_DIM:])


def _hgrn_layer(x2d, nb, t_len, norm_g, w_in, w_out, norm_w, hg_lb, layer):
    kw, vw = HG_KEY_WIDTH, HG_VAL_WIDTH
    wb = w_in.astype(BF16)
    weights = [wb[:, :kw], wb[:, kw:2 * kw], wb[:, 2 * kw:3 * kw], wb[:, 3 * kw:3 * kw + vw], wb[:, 3 * kw + vw:]]
    q, lf_f, lf_b, kg_f, kg_b, i_val, g = _hgrn_in(x2d, norm_g, hg_lb, weights, layer)
    o_f = _hgrn_scan(q, lf_f, kg_f, i_val, hg_lb, nb, t_len, reverse=False, layer=layer)
    o_b = _hgrn_scan(q, lf_b, kg_b, i_val, hg_lb, nb, t_len, reverse=True, layer=layer)
    return _hgrn_out(x2d, o_f, o_b, g, norm_w.reshape(1, vw), w_out.astype(BF16))


def kernel(x, mem, mix_norm, ab_w_in, ab_w_out, rw_mu, rw_w0, rw_w2, rw_a0, rw_a2, rw_g2, rw_k_k, rw_k_a, rw_r_k, rw_gn_w, rw_gn_b, ssd_conv_w, ssd_conv_b, ssd_dt_bias, ssd_a_log, ssd_d, ssd_norm_w, hg_w_in, hg_w_out, hg_norm_w, hg_lb, xa_norm, mem_norm, xa_wq, xa_wkv, xa_wo, ffn_norm, ffn_w_in, ffn_w_out, final_norm):
    nb, t_len, d = x.shape
    mem_len = mem.shape[1]
    depth = mix_norm.shape[0]
    x2d = x.reshape(nb * t_len, d)
    mem2d = mem.reshape(nb * mem_len, d)
    for layer in range(depth):
        if layer % 2 == 0:
            e = layer // 2
            x2d = _rwkv_ssd_layer(x2d, nb, t_len, mix_norm[layer], ab_w_in[e], ab_w_out[e], rw_mu[e], rw_w0[e],
                                  rw_w2[e], rw_a0[e], rw_a2[e], rw_g2[e], rw_k_k[e], rw_k_a[e],
                                  rw_r_k[e].reshape(-1), rw_gn_w[e], rw_gn_b[e], ssd_conv_w[e], ssd_conv_b[e],
                                  ssd_dt_bias[e], ssd_a_log[e], ssd_d[e], ssd_norm_w[e])
        else:
            o = layer // 2
            x2d = _hgrn_layer(x2d, nb, t_len, mix_norm[layer], hg_w_in[o], hg_w_out[o], hg_norm_w[o], hg_lb, layer)
        (kv,) = _norm_matmul(mem2d, mem_norm[layer], [xa_wkv[layer].astype(BF16)], [BF16])
        x2d = _xattn(x2d, xa_norm[layer], xa_wq[layer].astype(BF16), kv, xa_wo[layer].astype(BF16),
                     nb, t_len, mem_len)
        x2d = _ffn(x2d, ffn_norm[layer], ffn_w_in[layer].astype(BF16), ffn_w_out[layer].astype(BF16),
                   final_norm, final_norm=(layer == depth - 1))
    return x2d.reshape(nb, t_len, d)
```

```python
import functools
import math

import jax
import jax.numpy as jnp
from jax import lax
from jax.experimental import pallas as pl
from jax.experimental.pallas import tpu as pltpu

F32 = jnp.float32
BF16 = jnp.bfloat16
ACT = BF16

D_MODEL = 1024
NORM_EPS = 1e-6

RW_HEAD_DIM = 64
RW_DIM = D_MODEL // 2
RW_HEADS = RW_DIM // RW_HEAD_DIM
RW_DECAY_RANK = 64
RW_A_RANK = 64
RW_GATE_RANK = 128
RW_GN_EPS = 64e-5
RW_LORA = RW_DECAY_RANK + RW_A_RANK + RW_GATE_RANK
RW_PROJ = 3 * RW_DIM + RW_LORA
RW_CHUNK = 64
RW_BLOCK_CHUNKS = 8
RW_HEAD_PAIRS = RW_HEADS // 2
assert RW_CHUNK == RW_HEAD_DIM and 2 * RW_HEAD_DIM == 128

SSD_HEAD_DIM = 64
SSD_DIM = D_MODEL // 2
SSD_HEADS = SSD_DIM // SSD_HEAD_DIM
SSD_GROUPS = 2
SSD_STATE = 128
SSD_CONV_DIM = SSD_DIM + 2 * SSD_GROUPS * SSD_STATE
SSD_CHUNK = 128
SSD_BLOCK_CHUNKS = 8
SSD_GROUP_HEADS = SSD_HEADS // SSD_GROUPS
assert SSD_HEAD_DIM == RW_HEAD_DIM and SSD_GROUP_HEADS % 2 == 0

HG_HEADS = 8
HG_KEY_DIM = 128
HG_VAL_DIM = D_MODEL // HG_HEADS
HG_KEY_WIDTH = HG_HEADS * HG_KEY_DIM
HG_VAL_WIDTH = HG_HEADS * HG_VAL_DIM
HG_CHUNK = 64
HG_SUB = 16
HG_BLOCK_CHUNKS = 8
HG_SAFE_LOG_RANGE = 64.0
assert HG_KEY_DIM == HG_VAL_DIM

XA_HEADS = 4
XA_HEAD_DIM = D_MODEL // XA_HEADS
FFN_DIM = ((8 * D_MODEL + 3 * 256 - 1) // (3 * 256)) * 256

SUBLANES = 8
HALO_ROWS = 16
SHIFT_ROWS = 128
PROJ_COLS = 256
LANES = 128
VMEM_LIMIT_BYTES = 56 * 1024 * 1024
SCAN_STEP_ROWS = 1024
NEG_BIG = -1e30


def _params(*semantics):
    return pltpu.CompilerParams(dimension_semantics=semantics, vmem_limit_bytes=VMEM_LIMIT_BYTES)


def _tile(n, pref):
    t = min(n, pref)
    while n % t or t % SUBLANES:
        t -= 1
    return t


def _scan_rows(t_len, sub_rows):
    assert t_len % sub_rows == 0
    rows = sub_rows
    while rows * 2 <= SCAN_STEP_ROWS and t_len % (rows * 2) == 0:
        rows *= 2
    return rows


def _mm(a, b):
    return jnp.dot(a.astype(BF16), b.astype(BF16), preferred_element_type=F32)


def _mm_nt(a, b):
    return lax.dot_general(a.astype(BF16), b.astype(BF16), (((1,), (1,)), ((), ())),
                           preferred_element_type=F32)


def _mm_tn(a, b):
    return lax.dot_general(a.astype(BF16), b.astype(BF16), (((0,), (0,)), ((), ())),
                           preferred_element_type=F32)


def _split3(x):
    hi = x.astype(BF16)
    r1 = x - hi.astype(F32)
    mid = r1.astype(BF16)
    lo = (r1 - mid.astype(F32)).astype(BF16)
    return hi, mid, lo


def _mm_exact_lhs(m_bf16, x, terms=3):
    dot = functools.partial(jnp.dot, preferred_element_type=F32)
    pieces = _split3(x)[:terms]
    out = dot(m_bf16, pieces[0])
    for piece in pieces[1:]:
        out = out + dot(m_bf16, piece)
    return out


def _mm_exact_rhs_many(xs, m_bf16, terms=3):
    rows = xs[0].shape[0]
    pieces = [piece for x in xs for piece in _split3(x)[:terms]]
    out = jnp.dot(jnp.concatenate(pieces, axis=0), m_bf16, preferred_element_type=F32)
    results = []
    for i in range(len(xs)):
        total = out[i * terms * rows:(i * terms + 1) * rows]
        for t in range(1, terms):
            total = total + out[(i * terms + t) * rows:(i * terms + t + 1) * rows]
        results.append(total)
    return results


def _rms_normed(x, gain):
    ms = jnp.mean(x * x, axis=-1, keepdims=True)
    return x * lax.rsqrt(ms + NORM_EPS) * gain


def _sigmoid(x):
    return 0.5 * jnp.tanh(0.5 * x) + 0.5


def _silu(x):
    h = 0.5 * x
    return h * jnp.tanh(h) + h


def _softplus(x):
    return jnp.maximum(x, 0.0) + jnp.log(1.0 + jnp.exp(-jnp.abs(x)))


def _order_masks(n, reverse):
    row = lax.broadcasted_iota(jnp.int32, (n, n), 0)
    col = lax.broadcasted_iota(jnp.int32, (n, n), 1)
    if reverse:
        return col >= row, col > row
    return col <= row, col < row


def _norm_matmul_kernel(x_ref, g_ref, *refs, n_out):
    xn = _rms_normed(x_ref[...], g_ref[...]).astype(BF16)
    for w_ref, o_ref in zip(refs[:n_out], refs[n_out:]):
        o_ref[...] = jnp.dot(xn, w_ref[...], preferred_element_type=F32).astype(o_ref.dtype)


def _norm_matmul(x2d, gain, weights, out_dtypes, tm_pref=512):
    n, d = x2d.shape
    tm = _tile(n, tm_pref)
    in_specs = [pl.BlockSpec((tm, d), lambda i: (i, 0)), pl.BlockSpec((1, d), lambda i: (0, 0))]
    in_specs += [pl.BlockSpec(w.shape, lambda i: (0, 0)) for w in weights]
    out_specs = [pl.BlockSpec((tm, w.shape[1]), lambda i: (i, 0)) for w in weights]
    out_shape = [jax.ShapeDtypeStruct((n, w.shape[1]), dt) for w, dt in zip(weights, out_dtypes)]
    return pl.pallas_call(
        functools.partial(_norm_matmul_kernel, n_out=len(weights)),
        grid=(n // tm,), in_specs=in_specs, out_specs=out_specs, out_shape=out_shape,
        compiler_params=_params("parallel"), name="norm_matmul",
    )(x2d, gain.reshape(1, d), *weights)


def _neighbours(p, prev_blk, next_blk, first, last):
    tt = p.shape[0]
    ridx = lax.broadcasted_iota(jnp.int32, p.shape, 0)
    prev_row = jnp.where(first, 0.0, prev_blk[HALO_ROWS - 1:HALO_ROWS, :].astype(F32))
    next_row = jnp.where(last, 0.0, next_blk[0:1, :].astype(F32))
    prev = jnp.where(ridx == 0, prev_row, pltpu.roll(p, 1, 0))
    nxt = jnp.where(ridx == tt - 1, next_row, pltpu.roll(p, tt - 1, 0))
    return prev, nxt


def _shift_matrix(tt, offsets, weight):
    t = jnp.arange(tt)[:, None]
    c = jnp.arange(tt + 2 * HALO_ROWS)[None, :]
    hit = functools.reduce(jnp.logical_or, [c == HALO_ROWS + t + d for d in offsets])
    return jnp.where(hit, weight, 0.0).astype(BF16)


def _lane_group_sum(x, width):
    assert width % LANES == 0 and x.shape[1] % width == 0
    ones = jnp.ones((LANES, LANES), BF16)
    groups = []
    for c in range(0, x.shape[1], width):
        total = _mm(x[:, c:c + LANES], ones)
        for c2 in range(c + LANES, c + width, LANES):
            total = total + _mm(x[:, c2:c2 + LANES], ones)
        groups += [total] * (width // LANES)
    return jnp.concatenate(groups, axis=1)


def _head_sum(x, pair_ones_bf16):
    blocks = [_mm(x[:, c:c + LANES], pair_ones_bf16) for c in range(0, x.shape[1], LANES)]
    return jnp.concatenate(blocks, axis=1)


def _mix0_project(x_ref, xp_ref, xn_ref, gain_ref, wrkv_ref, wlora_ref, wz_ref, wxbc_ref, wdt_ref, first, last,
                  prkv_scr, plora_scr, u_scr, z_scr, dt_scr):
    tt = x_ref.shape[0]
    dot = functools.partial(jnp.dot, preferred_element_type=F32)
    x_ext = jnp.concatenate([jnp.where(first, 0.0, xp_ref[...]), x_ref[...], jnp.where(last, 0.0, xn_ref[...])], axis=0)
    xn_ext = _rms_normed(x_ext, gain_ref[...]).astype(BF16)
    xn = xn_ext[HALO_ROWS:HALO_ROWS + tt]

    def column_blocks(lhs, w_ref, scr, block):
        for c in range(0, scr.shape[1], block):
            scr[:, c:c + block] = dot(lhs, w_ref[:, c:c + block]).astype(scr.dtype)

    def project_rwkv():
        column_blocks(xn_ext, wrkv_ref, prkv_scr, RW_DIM)
        column_blocks(xn_ext, wlora_ref, plora_scr, RW_LORA)

    def project_ssd():
        column_blocks(xn_ext, wxbc_ref, u_scr, SSD_DIM)
        column_blocks(xn, wz_ref, z_scr, SSD_DIM)
        column_blocks(xn, wdt_ref, dt_scr, LANES)

    return project_rwkv, project_ssd


def _mix0_operands(project_ssd, prkv_scr, plora_scr, u_scr, z_scr, dt_scr,
                   savg_ref, mu_ref, mul_ref, a0_ref, a2_ref, g2_ref, kk_ref, ka_ref, w0_ref, w2_ref, rk_ref, bd_ref,
                   cw_ref, cb_ref, dtb_ref, alog_ref,
                   r_out, k_out, v_out, kkn_out, kka_out, lwf_out, lwb_out, g_out, bonus_out,
                   z_out, xbc_out, dtf_out, dtb_out, laf_out, lab_out):
    tt = z_scr.shape[0]
    dot = functools.partial(jnp.dot, preferred_element_type=F32)

    def token_shift(p_scr, m_ref, lo_col, hi_col):
        x = p_scr[HALO_ROWS:HALO_ROWS + tt, lo_col:hi_col].astype(F32)
        avg = jnp.concatenate(
            [dot(savg_ref[...], p_scr[t0:t0 + SHIFT_ROWS + 2 * HALO_ROWS, lo_col:hi_col])
             for t0 in range(0, tt, SHIFT_ROWS)], axis=0)
        return x + m_ref[:, lo_col:hi_col] * (avg - x)

    r = token_shift(prkv_scr, mu_ref, 0, RW_DIM)
    k = token_shift(prkv_scr, mu_ref, RW_DIM, 2 * RW_DIM)
    v = token_shift(prkv_scr, mu_ref, 2 * RW_DIM, 3 * RW_DIM)
    lo = token_shift(plora_scr, mul_ref, 0, RW_LORA)
    project_ssd()
    wd = lo[:, :RW_DECAY_RANK]
    ad = lo[:, RW_DECAY_RANK:RW_DECAY_RANK + RW_A_RANK]
    gd = lo[:, RW_DECAY_RANK + RW_A_RANK:]
    bd = bd_ref[...]

    a_logits = _mm(ad, a2_ref[...])
    g = _mm(_sigmoid(gd), g2_ref[...])
    kk = k * kk_ref[...]
    kk_sq = _head_sum(kk * kk, bd)
    w_lora = jnp.tanh(wd)
    decay_logits = [w0_ref[d:d + 1, :] + _mm(w_lora, w2_ref[d]) for d in range(2)]

    z_out[...] = z_scr[...]
    u_ext = u_scr[...]
    u = u_ext[HALO_ROWS:HALO_ROWS + tt].astype(F32)
    prev, nxt = _neighbours(u, u_ext[:HALO_ROWS], u_ext[HALO_ROWS + tt:], False, False)
    conv = prev * cw_ref[0:1, :] + u * cw_ref[1:2, :] + nxt * cw_ref[2:3, :] + cb_ref[...]
    xbc_out[...] = _silu(conv).astype(xbc_out.dtype)
    dt_raw = dt_scr[...]
    dt_f = _softplus(dt_raw + dtb_ref[0:1, :])
    dt_b = _softplus(dt_raw + dtb_ref[1:2, :])
    dtf_out[...] = dt_f
    dtb_out[...] = dt_b
    laf_out[...] = dt_f * (-jnp.exp(alog_ref[0:1, :]))
    lab_out[...] = dt_b * (-jnp.exp(alog_ref[1:2, :]))

    a = _sigmoid(a0_ref[...] + a_logits)
    kk = kk / jnp.maximum(jnp.sqrt(kk_sq), 1e-12)
    k = k * (1.0 + (a - 1.0) * ka_ref[...])
    r_out[...] = r.astype(r_out.dtype)
    k_out[...] = k.astype(k_out.dtype)
    v_out[...] = v.astype(v_out.dtype)
    kkn_out[...] = kk.astype(kkn_out.dtype)
    kka_out[...] = (kk * a).astype(kka_out.dtype)
    lwf_out[...] = -math.exp(-0.5) * _sigmoid(decay_logits[0])
    lwb_out[...] = -math.exp(-0.5) * _sigmoid(decay_logits[1])
    g_out[...] = g.astype(g_out.dtype)
    bonus_out[...] = (_head_sum(r * k * rk_ref[...], bd) * v).astype(bonus_out.dtype)


MIX0_N_PROJ_REFS = 9
MIX0_N_CONSTS = 16
MIX0_N_OUTS = 15
MIX0_N_STAGED = 5


def _mix0_in_kernel(*refs, per_seq):
    proj_refs = refs[:MIX0_N_PROJ_REFS]
    const_refs = refs[MIX0_N_PROJ_REFS:MIX0_N_PROJ_REFS + MIX0_N_CONSTS]
    out_refs = refs[MIX0_N_PROJ_REFS + MIX0_N_CONSTS:MIX0_N_PROJ_REFS + MIX0_N_CONSTS + MIX0_N_OUTS]
    staged = refs[MIX0_N_PROJ_REFS + MIX0_N_CONSTS + MIX0_N_OUTS:]
    pos = lax.rem(pl.program_id(0), per_seq)
    project_rwkv, project_ssd = _mix0_project(*proj_refs, pos == 0, pos == per_seq - 1, *staged)
    project_rwkv()
    _mix0_operands(project_ssd, *staged, *const_refs, *out_refs)


def _mix0_in(x2d, nb, t_len, gain, weights, rwkv_consts, ssd_consts):
    n, d = x2d.shape
    tt = _tile(t_len, 512)
    assert tt % HALO_ROWS == 0
    per_seq = t_len // tt
    n_tiles = n // tt
    blocks_per_tile = tt // HALO_ROWS
    n_blocks = n // HALO_ROWS

    def full(a):
        nd = a.ndim
        return pl.BlockSpec(a.shape, lambda s: (0,) * nd)

    x_specs = [pl.BlockSpec((tt, d), lambda s: (s, 0)),
               pl.BlockSpec((HALO_ROWS, d), lambda s: (jnp.maximum(s * blocks_per_tile - 1, 0), 0)),
               pl.BlockSpec((HALO_ROWS, d), lambda s: (jnp.minimum((s + 1) * blocks_per_tile, n_blocks - 1), 0))]
    proj_consts = (gain.reshape(1, d),) + tuple(weights)
    assert tt % SHIFT_ROWS == 0
    consts = (_shift_matrix(SHIFT_ROWS, (-1, 1), 0.5),) + tuple(rwkv_consts) + tuple(ssd_consts)
    assert 3 + len(proj_consts) == MIX0_N_PROJ_REFS and len(consts) == MIX0_N_CONSTS
    wide = [(RW_DIM, ACT)] * 5 + [(RW_DIM, F32)] * 2 + [(RW_DIM, ACT)] * 2 + [(SSD_DIM, ACT), (SSD_CONV_DIM, ACT)] \
        + [(LANES, F32)] * 4
    assert len(wide) == MIX0_N_OUTS
    ext = tt + 2 * HALO_ROWS
    staged = [pltpu.VMEM((ext, 3 * RW_DIM), ACT), pltpu.VMEM((ext, RW_LORA), ACT), pltpu.VMEM((ext, SSD_CONV_DIM), ACT),
              pltpu.VMEM((tt, SSD_DIM), ACT), pltpu.VMEM((tt, LANES), F32)]
    assert len(staged) == MIX0_N_STAGED
    return pl.pallas_call(
        functools.partial(_mix0_in_kernel, per_seq=per_seq),
        grid=(n_tiles,),
        in_specs=x_specs + [full(a) for a in proj_consts + consts],
        out_specs=[pl.BlockSpec((tt, w), lambda s: (s, 0)) for w, _ in wide],
        out_shape=[jax.ShapeDtypeStruct((n, w), dt) for w, dt in wide],
        scratch_shapes=staged,
        compiler_params=_params("parallel"), name="mix0_in",
    )(x2d, x2d, x2d, *proj_consts, *consts)


def _pair_blockdiag(y):
    lane = lax.broadcasted_iota(jnp.int32, y.shape, 1)
    top = jnp.where(lane < RW_HEAD_DIM, y, 0.0).astype(BF16)
    bottom = jnp.where(lane >= RW_HEAD_DIM, y, 0.0).astype(BF16)
    return jnp.concatenate([top, bottom], axis=0)


def _pair_diag_blocks(full):
    lane = lax.broadcasted_iota(jnp.int32, (RW_HEAD_DIM, LANES), 1)
    return jnp.where(lane < RW_HEAD_DIM, full[:RW_HEAD_DIM], full[RW_HEAD_DIM:])


def _unit_lower_inverses(n_strict_list, eye):
    size = n_strict_list[0].shape[0]
    levels = int(math.log2(size)) - 1
    accs = [eye + n for n in n_strict_list]
    powers = [_mm(n, _pair_blockdiag(n)) for n in n_strict_list]
    for level in range(levels):
        power_bds = [_pair_blockdiag(p) for p in powers]
        if level == levels - 1:
            return [a + _mm(a, b) for a, b in zip(accs, power_bds)]
        both = [_mm(jnp.concatenate([a, p], axis=0), b) for a, p, b in zip(accs, powers, power_bds)]
        accs = [a + ab[:size] for a, ab in zip(accs, both)]
        powers = [ab[size:] for ab in both]


def _rwkv_scan_kernel(r_ref, k_ref, v_ref, kk_ref, kka_ref, lw_ref, o_ref, h_ref, *, reverse):
    @pl.when(pl.program_id(1) == 0)
    def _():
        h_ref[...] = jnp.zeros_like(h_ref)

    sub_rows = RW_BLOCK_CHUNKS * RW_CHUNK
    n_sub = r_ref.shape[0] // sub_rows

    def body(i, carry):
        blk = (n_sub - 1 - i) if reverse else i
        _rwkv_scan_block(r_ref, k_ref, v_ref, kk_ref, kka_ref, lw_ref, o_ref, h_ref,
                         pl.multiple_of(blk * sub_rows, sub_rows), reverse)
        return carry

    lax.fori_loop(0, n_sub, body, 0)


def _rwkv_scan_block(r_ref, k_ref, v_ref, kk_ref, kka_ref, lw_ref, o_ref, h_ref, base, reverse):
    chunk = RW_CHUNK
    n_chunks = RW_BLOCK_CHUNKS
    incl_bf = _order_masks(chunk, reverse)[0].astype(BF16)
    row = lax.broadcasted_iota(jnp.int32, (chunk, LANES), 0)
    pos = lax.broadcasted_iota(jnp.int32, (chunk, LANES), 1) % RW_HEAD_DIM
    incl, strict = (pos >= row, pos > row) if reverse else (pos <= row, pos < row)
    eye = (pos == row).astype(F32)
    end = 0 if reverse else chunk - 1
    chunk_order = list(range(n_chunks - 1, -1, -1)) if reverse else list(range(n_chunks))

    items = []
    for ci in chunk_order:
        rows = pl.ds(base + ci * chunk, chunk)
        lw = lw_ref[rows, :]
        cum = _mm_exact_lhs(incl_bf, lw, terms=2)
        cum_end = cum[end:end + 1, :]
        e_neg = jnp.exp(-cum)
        p_end = jnp.exp(cum_end)
        e_tail = p_end * e_neg
        k = k_ref[rows, :].astype(F32)
        kka = kka_ref[rows, :].astype(F32)
        at_all = -kk_ref[rows, :].astype(F32) * jnp.exp(cum - lw)
        bt_all = kka * e_neg
        kt_all = k * e_neg
        rt_all = r_ref[rows, :].astype(F32) * jnp.exp(cum)
        bh_all = kka * e_tail
        kh_all = k * e_tail
        v = v_ref[rows, :].astype(F32)
        for j in range(RW_HEAD_PAIRS):
            sl = slice(j * LANES, (j + 1) * LANES)
            items.append(dict(ci=ci, j=j, at=at_all[:, sl], bt=bt_all[:, sl], kt=kt_all[:, sl], rt=rt_all[:, sl],
                              bh=bh_all[:, sl], kh=kh_all[:, sl], v=v[:, sl], p_end=p_end[:, sl]))

    for it in items:
        lhs = jnp.concatenate([it["at"], it["rt"]], axis=0)
        both = _mm_nt(lhs, jnp.concatenate([_pair_blockdiag(it["bt"]), _pair_blockdiag(it["kt"])], axis=0))
        it["a_ab"] = jnp.where(strict, both[:chunk, :LANES], 0.0)
        it["a_rb"] = jnp.where(incl, both[chunk:, :LANES], 0.0)
        it["a_ak"] = jnp.where(strict, both[:chunk, LANES:], 0.0)
        it["a_rk"] = jnp.where(incl, both[chunk:, LANES:], 0.0)
    t_invs = _unit_lower_inverses([it["a_ab"] for it in items], eye)
    for it in items:
        akv_rkv = _mm(jnp.concatenate([it["a_ak"], it["a_rk"]], axis=0), _pair_blockdiag(it["v"]))
        it["x1"] = akv_rkv[:chunk]
        it["rk_v"] = akv_rkv[chunk:]
    for it, t_inv in zip(items, t_invs):
        wu_u0 = _mm(t_inv, jnp.concatenate([_pair_blockdiag(it["at"]), _pair_blockdiag(it["x1"])], axis=1))
        it["w_u"] = wu_u0[:, :LANES]
        it["u0"] = wu_u0[:, LANES:]
    for it in items:
        rb_wu_u0 = _mm(it["a_rb"],
                       jnp.concatenate([_pair_blockdiag(it["w_u"]), _pair_blockdiag(it["u0"])], axis=1))
        w_o = it["rt"] + rb_wu_u0[:, :LANES]
        g_mat = eye * it["p_end"] + _pair_diag_blocks(_mm_tn(it["bh"], it["w_u"]))
        it["wo_g"] = jnp.concatenate([w_o, g_mat], axis=0)
        it["o0"] = rb_wu_u0[:, LANES:] + it["rk_v"]
        it["h0"] = _pair_diag_blocks(_mm_tn(jnp.concatenate([it["bh"], it["kh"]], axis=0),
                                            jnp.concatenate([it["u0"], it["v"]], axis=0)))

    states = [h_ref[j] for j in range(RW_HEAD_PAIRS)]
    for it in items:
        j = it["j"]
        res = _mm(it["wo_g"], _pair_blockdiag(states[j]))
        rows = pl.ds(base + it["ci"] * chunk, chunk)
        o_ref[rows, j * LANES:(j + 1) * LANES] = (res[:chunk] + it["o0"]).astype(o_ref.dtype)
        states[j] = res[chunk:] + it["h0"]
    for j in range(RW_HEAD_PAIRS):
        h_ref[j] = states[j]


def _rwkv_scan(r, k, v, kk, kka, lw, nb, t_len, reverse):
    n = r.shape[0]
    tb = _scan_rows(t_len, RW_BLOCK_CHUNKS * RW_CHUNK)
    nc = t_len // tb

    def idx(b, c):
        return (b * nc + (nc - 1 - c if reverse else c), 0)

    spec = pl.BlockSpec((tb, RW_DIM), idx)
    return pl.pallas_call(
        functools.partial(_rwkv_scan_kernel, reverse=reverse),
        grid=(nb, nc), in_specs=[spec] * 6, out_specs=spec,
        out_shape=jax.ShapeDtypeStruct((n, RW_DIM), ACT),
        scratch_shapes=[pltpu.VMEM((RW_HEAD_PAIRS, RW_HEAD_DIM, LANES), F32)],
        compiler_params=_params("parallel", "arbitrary"), name="rwkv_scan_bwd" if reverse else "rwkv_scan_fwd",
    )(r, k, v, kk, kka, lw)


def _ssd_scan_kernel(xbc_ref, dt_ref, la_ref, ex64_ref, y_ref, s_ref, *, reverse):
    @pl.when(pl.program_id(1) == 0)
    def _():
        s_ref[...] = jnp.zeros_like(s_ref)

    sub_rows = SSD_BLOCK_CHUNKS * SSD_CHUNK
    n_sub = xbc_ref.shape[0] // sub_rows

    def body(i, carry):
        blk = (n_sub - 1 - i) if reverse else i
        _ssd_scan_block(xbc_ref, dt_ref, la_ref, ex64_ref, y_ref, s_ref,
                        pl.multiple_of(blk * sub_rows, sub_rows), reverse)
        return carry

    lax.fori_loop(0, n_sub, body, 0)


def _ssd_scan_block(xbc_ref, dt_ref, la_ref, ex64_ref, y_ref, s_ref, base, reverse):
    chunk = SSD_CHUNK
    n_chunks = SSD_BLOCK_CHUNKS
    incl, _ = _order_masks(chunk, reverse)
    incl_bf = incl.astype(BF16)
    end = 0 if reverse else chunk - 1
    gn = SSD_GROUPS * SSD_STATE
    gw = SSD_GROUP_HEADS * SSD_HEAD_DIM
    chunk_order = list(range(n_chunks - 1, -1, -1)) if reverse else list(range(n_chunks))
    ex64 = ex64_ref[...]

    pre = []
    for ci in chunk_order:
        rows = pl.ds(base + ci * chunk, chunk)
        la = la_ref[rows, :]
        cum = _mm_exact_lhs(incl_bf, la, terms=2)
        cum_rows = cum.T
        cum_end = cum[end:end + 1, :]
        ecum64, etail64, dt64 = _mm_exact_rhs_many(
            [jnp.exp(cum), jnp.exp(cum_end - cum), dt_ref[rows, :]], ex64, terms=2)
        xdt = xbc_ref[rows, :SSD_DIM].astype(F32) * dt64
        pre.append(dict(rows=rows, cum=cum, cum_rows=cum_rows, ecum64=ecum64, eend64=ecum64[end:end + 1, :],
                        xdt=xdt, xtail=xdt * etail64,
                        bm=[xbc_ref[rows, SSD_DIM + g * SSD_STATE:SSD_DIM + (g + 1) * SSD_STATE]
                            for g in range(SSD_GROUPS)],
                        cm=[xbc_ref[rows, SSD_DIM + gn + g * SSD_STATE:SSD_DIM + gn + (g + 1) * SSD_STATE]
                            for g in range(SSD_GROUPS)]))
    for p in pre:
        p["scores"] = [_mm_nt(p["cm"][g], p["bm"][g]) for g in range(SSD_GROUPS)]
        p["inject"] = [_mm_tn(p["bm"][g], p["xtail"][:, g * gw:(g + 1) * gw]) for g in range(SSD_GROUPS)]
    for p in pre:
        weights = []
        for h in range(SSD_HEADS):
            diff = p["cum"][:, h:h + 1] - p["cum_rows"][h:h + 1, :]
            decay = jnp.exp(jnp.where(incl, diff, NEG_BIG))
            weights.append(p["scores"][h // SSD_GROUP_HEADS] * decay)
        p["ydiag"] = [_mm(jnp.concatenate([weights[h], weights[h + 1]], axis=1),
                          _pair_blockdiag(p["xdt"][:, h * SSD_HEAD_DIM:(h + 2) * SSD_HEAD_DIM]))
                      for h in range(0, SSD_HEADS, 2)]

    states = [s_ref[g] for g in range(SSD_GROUPS)]
    for p in pre:
        for g in range(SSD_GROUPS):
            y_off = _mm(p["cm"][g], states[g]) * p["ecum64"][:, g * gw:(g + 1) * gw]
            for jp in range(SSD_GROUP_HEADS // 2):
                pair = g * (SSD_GROUP_HEADS // 2) + jp
                y_ref[p["rows"], pair * LANES:(pair + 1) * LANES] = (
                    p["ydiag"][pair] + y_off[:, jp * LANES:(jp + 1) * LANES]).astype(y_ref.dtype)
            states[g] = states[g] * p["eend64"][:, g * gw:(g + 1) * gw] + p["inject"][g]
    for g in range(SSD_GROUPS):
        s_ref[g] = states[g]


def _ssd_scan(xbc, dt, la, nb, t_len, reverse):
    n = xbc.shape[0]
    tb = _scan_rows(t_len, SSD_BLOCK_CHUNKS * SSD_CHUNK)
    assert SSD_CHUNK == LANES
    nc = t_len // tb
    lane_head = jnp.arange(LANES)[:, None]
    ex64 = (lane_head == jnp.arange(SSD_DIM)[None, :] // SSD_HEAD_DIM).astype(BF16)

    def idx(b, c):
        return (b * nc + (nc - 1 - c if reverse else c), 0)

    def full(a):
        return pl.BlockSpec(a.shape, lambda b, c: (0, 0))

    return pl.pallas_call(
        functools.partial(_ssd_scan_kernel, reverse=reverse),
        grid=(nb, nc),
        in_specs=[pl.BlockSpec((tb, SSD_CONV_DIM), idx), pl.BlockSpec((tb, LANES), idx),
                  pl.BlockSpec((tb, LANES), idx), full(ex64)],
        out_specs=pl.BlockSpec((tb, SSD_DIM), idx),
        out_shape=jax.ShapeDtypeStruct((n, SSD_DIM), ACT),
        scratch_shapes=[pltpu.VMEM((SSD_GROUPS, SSD_STATE, SSD_GROUP_HEADS * SSD_HEAD_DIM), F32)],
        compiler_params=_params("parallel", "arbitrary"), name="ssd_scan_bwd" if reverse else "ssd_scan_fwd",
    )(xbc, dt, la, ex64)


def _mix0_out_kernel(x_ref, of_ref, ob_ref, bonus_ref, g_ref, yf_ref, yb_ref, xbc_ref, z_ref,
                     gnw_ref, gnb_ref, bd_ref, dsk_ref, nw_ref, wo_rw_ref, wo_ssd_ref, out_ref):
    bd = bd_ref[...]
    o = of_ref[...].astype(F32) + ob_ref[...].astype(F32)
    inv_n = 1.0 / RW_HEAD_DIM
    mu = _head_sum(o, bd) * inv_n
    oc = o - mu
    var = _head_sum(oc * oc, bd) * inv_n
    o = oc * lax.rsqrt(var + RW_GN_EPS) * gnw_ref[...] + gnb_ref[...]
    o_rw = (o + bonus_ref[...].astype(F32)) * g_ref[...].astype(F32)

    y = yf_ref[...].astype(F32) + yb_ref[...].astype(F32) + dsk_ref[...] * xbc_ref[...].astype(F32)
    y = y * _silu(z_ref[...].astype(F32))
    gw = SSD_DIM // SSD_GROUPS
    parts = []
    for g in range(SSD_GROUPS):
        yg = y[:, g * gw:(g + 1) * gw]
        parts.append(yg * lax.rsqrt(jnp.mean(yg * yg, axis=-1, keepdims=True) + NORM_EPS))
    o_ssd = jnp.concatenate(parts, axis=-1) * nw_ref[...]
    out_ref[...] = x_ref[...] + _mm(o_rw, wo_rw_ref[...]) + _mm(o_ssd, wo_ssd_ref[...])


def _mix0_out(x2d, o_f, o_b, bonus, g, y_f, y_b, xbc, z, gn_w, gn_b, blockdiag, d_skip, norm_w, wo_rw, wo_ssd):
    n = x2d.shape[0]
    tm = _tile(n, 512)

    def row(width):
        return pl.BlockSpec((tm, width), lambda i: (i, 0))

    def full(a):
        return pl.BlockSpec(a.shape, lambda i: (0, 0))

    consts = (gn_w, gn_b, blockdiag, d_skip, norm_w, wo_rw, wo_ssd)
    return pl.pallas_call(
        _mix0_out_kernel,
        grid=(n // tm,),
        in_specs=[row(D_MODEL)] + [row(RW_DIM)] * 6 + [row(SSD_DIM), row(SSD_DIM)] + [full(a) for a in consts],
        out_specs=row(D_MODEL),
        out_shape=jax.ShapeDtypeStruct((n, D_MODEL), F32),
        compiler_params=_params("parallel"), name="mix0_out",
    )(x2d, o_f, o_b, bonus, g, y_f, y_b, xbc, z, *consts)


def _hgrn_lower_bound(lb_ref, layer):
    x = lb_ref[...]
    m = jnp.max(x, axis=0, keepdims=True)
    e = jnp.exp(x - m)
    s = e / jnp.sum(e, axis=0, keepdims=True)
    lb = jnp.zeros_like(s[0:1, :])
    for i in range(1, layer + 1):
        lb = lb + s[i:i + 1, :]
    return lb


def _hgrn_in_kernel(x_ref, g_ref, lb_ref, wq_ref, wff_ref, wfb_ref, wi_ref, wg_ref,
                    q_out, lff_out, lfb_out, kgf_out, kgb_out, i_out, g_out, *, layer):
    xn = _rms_normed(x_ref[...], g_ref[...]).astype(BF16)
    lb = _hgrn_lower_bound(lb_ref, layer)
    dot = functools.partial(jnp.dot, preferred_element_type=F32)
    width = x_ref.shape[1]
    jobs = []
    for c in range(0, width, PROJ_COLS):
        cols = slice(c, c + PROJ_COLS)
        jobs.append((wff_ref, cols, ("gate", lff_out, kgf_out)))
        jobs.append((wq_ref, cols, ("plain", q_out)))
        jobs.append((wfb_ref, cols, ("gate", lfb_out, kgb_out)))
        jobs.append((wi_ref, cols, ("plain", i_out)))
        jobs.append((wg_ref, cols, ("plain", g_out)))

    def finish(acc, cols, how):
        if how[0] == "gate":
            f = lb[:, cols] + (1.0 - lb[:, cols]) * _sigmoid(acc)
            how[1][:, cols] = jnp.log(f)
            how[2][:, cols] = (1.0 - f).astype(how[2].dtype)
        else:
            how[1][:, cols] = acc.astype(how[1].dtype)

    pending = None
    for w_ref, cols, how in jobs:
        acc = dot(xn, w_ref[:, cols])
        if pending is not None:
            finish(*pending)
        pending = (acc, cols, how)
    finish(*pending)


def _hgrn_in(x2d, gain, hg_lb, weights, layer):
    n, d = x2d.shape
    tm = _tile(n, 512)
    row = pl.BlockSpec((tm, d), lambda i: (i, 0))
    in_specs = [row, pl.BlockSpec((1, d), lambda i: (0, 0)), pl.BlockSpec(hg_lb.shape, lambda i: (0, 0))]
    in_specs += [pl.BlockSpec(w.shape, lambda i: (0, 0)) for w in weights]
    dtypes = (ACT, F32, F32, ACT, ACT, ACT, ACT)
    return pl.pallas_call(
        functools.partial(_hgrn_in_kernel, layer=layer),
        grid=(n // tm,), in_specs=in_specs, out_specs=[row] * len(dtypes),
        out_shape=[jax.ShapeDtypeStruct((n, d), dt) for dt in dtypes],
        compiler_params=_params("parallel"), name="hgrn_in",
    )(x2d, gain.reshape(1, d), hg_lb, *weights)


def _hgrn_chunk(q, lf, kg, vv, state_t, reverse):
    chunk = q.shape[0]
    n_sub = chunk // HG_SUB
    incl, _ = _order_masks(chunk, reverse)
    end = 0 if reverse else chunk - 1
    s_idx = lax.broadcasted_iota(jnp.int32, (HG_SUB, 1), 0)

    bcum = _mm_exact_lhs(incl.astype(BF16), lf)
    b_end = bcum[end:end + 1, :]
    carried = _mm_nt(q * jnp.exp(bcum), state_t)

    order = list(range(n_sub - 1, -1, -1)) if reverse else list(range(n_sub))
    outs = [None] * n_sub
    for pos, sb in enumerate(order):
        rows = slice(sb * HG_SUB, (sb + 1) * HG_SUB)
        q_s, k_s, b_s, v_s = q[rows], kg[rows], bcum[rows], vv[rows]
        acc = carried[rows]
        if pos > 0:
            prev_sb = order[pos - 1]
            edge = prev_sb * HG_SUB if reverse else prev_sb * HG_SUB + HG_SUB - 1
            b_edge = bcum[edge:edge + 1, :]
            src = slice((sb + 1) * HG_SUB, chunk) if reverse else slice(0, sb * HG_SUB)
            q_hat = q_s * jnp.exp(b_s - b_edge)
            k_hat = kg[src] * jnp.exp(b_edge - bcum[src])
            acc = acc + _mm(_mm_nt(q_hat, k_hat), vv[src])
        rows_out = []
        for l in range(HG_SUB):
            w = jnp.exp(jnp.minimum(b_s[l:l + 1, :] - b_s, 0.0)) * k_s * q_s[l:l + 1, :]
            att = jnp.sum(w, axis=-1, keepdims=True)
            att = jnp.where((s_idx >= l) if reverse else (s_idx <= l), att, 0.0)
            rows_out.append(jnp.sum(att * v_s, axis=0, keepdims=True))
        outs[sb] = acc + jnp.concatenate(rows_out, axis=0)

    k_tail = kg * jnp.exp(b_end - bcum)
    new_state_t = state_t * jnp.exp(b_end) + _mm_tn(vv, k_tail)
    return jnp.concatenate(outs, axis=0), new_state_t


def _hgrn_scan_kernel(q_ref, lf_ref, kg_ref, i_ref, lb_ref, o_ref, s_ref, hq_ref, hf_ref, hk_ref, hv_ref, ho_ref, *,
                      reverse, layer):
    @pl.when(pl.program_id(1) == 0)
    def _():
        s_ref[...] = jnp.zeros_like(s_ref)

    sub_rows = HG_BLOCK_CHUNKS * HG_CHUNK
    n_sub = q_ref.shape[0] // sub_rows

    def body(i, carry):
        blk = (n_sub - 1 - i) if reverse else i
        _hgrn_scan_block(q_ref, lf_ref, kg_ref, i_ref, lb_ref, o_ref, s_ref, hq_ref, hf_ref, hk_ref, hv_ref, ho_ref,
                         pl.multiple_of(blk * sub_rows, sub_rows), reverse, layer)
        return carry

    lax.fori_loop(0, n_sub, body, 0)


def _hgrn_scan_block(q_ref, lf_ref, kg_ref, i_ref, lb_ref, o_ref, s_ref, hq_ref, hf_ref, hk_ref, hv_ref, ho_ref,
                     base, reverse, layer):
    chunk = HG_CHUNK
    n_chunks = HG_BLOCK_CHUNKS
    incl, _ = _order_masks(chunk, reverse)
    incl_bf = incl.astype(BF16)
    end = 0 if reverse else chunk - 1
    chunk_order = list(range(n_chunks - 1, -1, -1)) if reverse else list(range(n_chunks))

    def head_cols(h):
        return slice(h * HG_KEY_DIM, (h + 1) * HG_KEY_DIM)

    pre = []
    worst = None
    for ci in chunk_order:
        rows = pl.ds(base + ci * chunk, chunk)
        bcum = _mm_exact_lhs(incl_bf, lf_ref[rows, :], terms=2)
        b_end = bcum[end:end + 1, :]
        worst = b_end if worst is None else jnp.minimum(worst, b_end)
        pre.append((ci, rows, bcum, b_end))
    lb_min = jnp.min(_hgrn_lower_bound(lb_ref, layer))
    flag = lax.cond(lb_min >= math.exp(-HG_SAFE_LOG_RANGE / HG_CHUNK),
                    lambda: jnp.int32(1),
                    lambda: (jnp.min(worst) >= -HG_SAFE_LOG_RANGE).astype(jnp.int32))
    safe = flag == 1

    @pl.when(safe)
    def _():
        items = []
        for ci, rows, bcum, b_end in pre:
            kg = kg_ref[rows, :].astype(F32)
            e_neg = jnp.exp(-bcum)
            e_end = jnp.exp(b_end)
            qt_all = q_ref[rows, :].astype(F32) * jnp.exp(bcum)
            kt_all = kg * e_neg
            ktail_all = kt_all * e_end
            v_all = i_ref[rows, :].astype(F32)
            for h in range(HG_HEADS):
                c = head_cols(h)
                items.append(dict(rows=rows, h=h, qt=qt_all[:, c], kt=kt_all[:, c], ktail=ktail_all[:, c],
                                  v=v_all[:, c], e_end=e_end[:, c]))
        for it in items:
            it["attn"] = jnp.where(incl, _mm_nt(it["qt"], it["kt"]), 0.0)
        for it in items:
            it["local"] = _mm(it["attn"], it["v"])
            it["inject"] = _mm_tn(it["v"], it["ktail"])
        states = [s_ref[h] for h in range(HG_HEADS)]
        for it in items:
            h = it["h"]
            o_ref[it["rows"], head_cols(h)] = (it["local"] + _mm_nt(it["qt"], states[h])).astype(o_ref.dtype)
            states[h] = states[h] * it["e_end"] + it["inject"]
        for h in range(HG_HEADS):
            s_ref[h] = states[h]

    @pl.when(jnp.logical_not(safe))
    def _():
        def head_body(h, carry):
            out, new_state = _hgrn_chunk(hq_ref[h], hf_ref[h], hk_ref[h], hv_ref[h], s_ref[h], reverse)
            ho_ref[h] = out
            s_ref[h] = new_state
            return carry

        def chunk_body(i, carry):
            ci = (n_chunks - 1 - i) if reverse else i
            rows = pl.ds(pl.multiple_of(base + ci * chunk, chunk), chunk)
            for h in range(HG_HEADS):
                hq_ref[h] = q_ref[rows, head_cols(h)].astype(F32)
                hf_ref[h] = lf_ref[rows, head_cols(h)]
                hk_ref[h] = kg_ref[rows, head_cols(h)].astype(F32)
                hv_ref[h] = i_ref[rows, head_cols(h)].astype(F32)
            lax.fori_loop(0, HG_HEADS, head_body, 0)
            for h in range(HG_HEADS):
                o_ref[rows, head_cols(h)] = ho_ref[h].astype(o_ref.dtype)
            return carry

        lax.fori_loop(0, n_chunks, chunk_body, 0)


def _hgrn_scan(q, lf, kg, i_val, hg_lb, nb, t_len, reverse, layer):
    n = q.shape[0]
    tb = _scan_rows(t_len, HG_BLOCK_CHUNKS * HG_CHUNK)
    n_outer = t_len // tb

    def idx(b, c):
        return (b * n_outer + (n_outer - 1 - c if reverse else c), 0)

    spec = pl.BlockSpec((tb, HG_KEY_WIDTH), idx)
    head_buf = pltpu.VMEM((HG_HEADS, HG_CHUNK, HG_KEY_DIM), F32)
    return pl.pallas_call(
        functools.partial(_hgrn_scan_kernel, reverse=reverse, layer=layer),
        grid=(nb, n_outer),
        in_specs=[spec, spec, spec, spec, pl.BlockSpec(hg_lb.shape, lambda b, c: (0, 0))],
        out_specs=spec,
        out_shape=jax.ShapeDtypeStruct((n, HG_VAL_WIDTH), ACT),
        scratch_shapes=[pltpu.VMEM((HG_HEADS, HG_VAL_DIM, HG_KEY_DIM), F32)] + [head_buf] * 5,
        compiler_params=_params("parallel", "arbitrary"),
        name="hgrn_scan_bwd" if reverse else "hgrn_scan_fwd",
    )(q, lf, kg, i_val, hg_lb)


def _hgrn_out_kernel(x_ref, of_ref, ob_ref, g_ref, nw_ref, wo_ref, out_ref):
    o = of_ref[...].astype(F32) + ob_ref[...].astype(F32)
    mean_sq = _lane_group_sum(o * o, HG_VAL_DIM) * (1.0 / HG_VAL_DIM)
    o = o * lax.rsqrt(mean_sq + NORM_EPS) * nw_ref[...] * _silu(g_ref[...].astype(F32))
    out_ref[...] = x_ref[...] + _mm(o, wo_ref[...])


def _hgrn_out(x2d, o_f, o_b, g, norm_w, w_out):
    n = x2d.shape[0]
    tm = _tile(n, 512)
    row = pl.BlockSpec((tm, D_MODEL), lambda i: (i, 0))
    return pl.pallas_call(
        _hgrn_out_kernel,
        grid=(n // tm,),
        in_specs=[row] * 4 + [pl.BlockSpec((1, HG_VAL_WIDTH), lambda i: (0, 0)),
                              pl.BlockSpec(w_out.shape, lambda i: (0, 0))],
        out_specs=row,
        out_shape=jax.ShapeDtypeStruct((n, D_MODEL), F32),
        compiler_params=_params("parallel"), name="hgrn_out",
    )(x2d, o_f, o_b, g, norm_w, w_out)


def _xattn_kernel(x_ref, g_ref, wq_ref, kv_ref, wo_ref, out_ref):
    x = x_ref[...]
    q = _mm(_rms_normed(x, g_ref[...]), wq_ref[...])
    scale = XA_HEAD_DIM ** -0.5
    scores = [_mm_nt(q[:, h * XA_HEAD_DIM:(h + 1) * XA_HEAD_DIM], kv_ref[:, h * XA_HEAD_DIM:(h + 1) * XA_HEAD_DIM])
              * scale for h in range(XA_HEADS)]
    probs = []
    for s in scores:
        p = jnp.exp(s - jnp.max(s, axis=-1, keepdims=True))
        probs.append(p * (1.0 / jnp.sum(p, axis=-1, keepdims=True)))
    heads = [_mm(p, kv_ref[:, D_MODEL + h * XA_HEAD_DIM:D_MODEL + (h + 1) * XA_HEAD_DIM])
             for h, p in enumerate(probs)]
    out_ref[...] = x + _mm(jnp.concatenate(heads, axis=-1), wo_ref[...])


def _xattn(x2d, gain, wq, kv, wo, nb, t_len, mem_len):
    n = x2d.shape[0]
    tm = _tile(t_len, 1024)
    per_seq = t_len // tm
    row = pl.BlockSpec((tm, D_MODEL), lambda b, j: (b * per_seq + j, 0))
    return pl.pallas_call(
        _xattn_kernel,
        grid=(nb, per_seq),
        in_specs=[row, pl.BlockSpec((1, D_MODEL), lambda b, j: (0, 0)),
                  pl.BlockSpec(wq.shape, lambda b, j: (0, 0)),
                  pl.BlockSpec((mem_len, 2 * D_MODEL), lambda b, j: (b, 0)),
                  pl.BlockSpec(wo.shape, lambda b, j: (0, 0))],
        out_specs=row,
        out_shape=jax.ShapeDtypeStruct((n, D_MODEL), F32),
        compiler_params=_params("parallel", "parallel"), name="xattn",
    )(x2d, gain.reshape(1, D_MODEL), wq, kv, wo)


def _ffn_kernel(x_ref, g_ref, wi_ref, wo_ref, fg_ref, out_ref, *, final_norm):
    x = x_ref[...]
    gu = _mm(_rms_normed(x, g_ref[...]), wi_ref[...])
    act = _silu(gu[:, :FFN_DIM]) * gu[:, FFN_DIM:]
    y = x + _mm(act, wo_ref[...])
    if final_norm:
        y = _rms_normed(y, fg_ref[...])
    out_ref[...] = y


def _ffn(x2d, gain, w_in, w_out, final_gain, final_norm):
    n = x2d.shape[0]
    tm = _tile(n, 512)
    row = pl.BlockSpec((tm, D_MODEL), lambda i: (i, 0))
    vec = pl.BlockSpec((1, D_MODEL), lambda i: (0, 0))
    resident = pl.Buffered(1)
    return pl.pallas_call(
        functools.partial(_ffn_kernel, final_norm=final_norm),
        grid=(n // tm,),
        in_specs=[row, vec, pl.BlockSpec(w_in.shape, lambda i: (0, 0), pipeline_mode=resident),
                  pl.BlockSpec(w_out.shape, lambda i: (0, 0), pipeline_mode=resident), vec],
        out_specs=row,
        out_shape=jax.ShapeDtypeStruct((n, D_MODEL), F32),
        compiler_params=_params("parallel"), name="ffn",
    )(x2d, gain.reshape(1, D_MODEL), w_in, w_out, final_gain.reshape(1, D_MODEL))


def _pad_lanes(a, width=LANES):
    return jnp.pad(a, [(0, 0)] * (a.ndim - 1) + [(0, width - a.shape[-1])])


def _rwkv_ssd_layer(x2d, nb, t_len, norm_g, w_in, w_out, rw_mu, rw_w0, rw_w2, rw_a0, rw_a2, rw_g2, rw_k_k, rw_k_a,
                    rw_r_k, rw_gn_w, rw_gn_b, conv_w, conv_b, dt_bias, a_log, d_skip, ssd_norm_w):
    o_lora = 3 * RW_DIM
    o_z = RW_PROJ
    o_xbc = o_z + SSD_DIM
    o_dt = o_xbc + SSD_CONV_DIM
    wb = w_in.astype(BF16)
    weights = [wb[:, :o_lora], wb[:, o_lora:o_z], wb[:, o_z:o_xbc], wb[:, o_xbc:o_dt], _pad_lanes(wb[:, o_dt:])]
    head_id = jnp.arange(LANES) // RW_HEAD_DIM
    blockdiag = (head_id[:, None] == head_id[None, :]).astype(BF16)
    row = lambda a: a.reshape(1, -1)
    rwkv_consts = (row(rw_mu[:o_lora]), row(rw_mu[o_lora:]), row(rw_a0), rw_a2.astype(BF16), rw_g2.astype(BF16),
                   row(rw_k_k), row(rw_k_a), rw_w0, rw_w2.astype(BF16), row(rw_r_k), blockdiag)
    ssd_consts = (conv_w, row(conv_b), _pad_lanes(dt_bias), _pad_lanes(a_log))
    (r, k, v, kk, kka, lw_f, lw_b, g, bonus, p_z, xbc, dt_f, dt_b, la_f, la_b) = _mix0_in(
        x2d, nb, t_len, norm_g, weights, rwkv_consts, ssd_consts)
    o_f = _rwkv_scan(r, k, v, kk, kka, lw_f, nb, t_len, reverse=False)
    o_b = _rwkv_scan(r, k, v, kk, kka, lw_b, nb, t_len, reverse=True)
    y_f = _ssd_scan(xbc, dt_f, la_f, nb, t_len, reverse=False)
    y_b = _ssd_scan(xbc, dt_b, la_b, nb, t_len, reverse=True)

    wo = w_out.astype(BF16)
    d_skip_lanes = jnp.repeat(d_skip, SSD_HEAD_DIM).reshape(1, SSD_DIM)
    return _mix0_out(x2d, o_f, o_b, bonus, g, y_f, y_b, xbc, p_z, row(rw_gn_w), row(rw_gn_b), blockdiag,
                     d_skip_lanes, row(ssd_norm_w), wo[:RW_DIM], wo[RW_DIM:])


def _hgrn_layer(x2d, nb, t_len, norm_g, w_in, w_out, norm_w, hg_lb, layer):
    kw, vw = HG_KEY_WIDTH, HG_VAL_WIDTH
    wb = w_in.astype(BF16)
    weights = [wb[:, :kw], wb[:, kw:2 * kw], wb[:, 2 * kw:3 * kw], wb[:, 3 * kw:3 * kw + vw], wb[:, 3 * kw + vw:]]
    q, lf_f, lf_b, kg_f, kg_b, i_val, g = _hgrn_in(x2d, norm_g, hg_lb, weights, layer)
    o_f = _hgrn_scan(q, lf_f, kg_f, i_val, hg_lb, nb, t_len, reverse=False, layer=layer)
    o_b = _hgrn_scan(q, lf_b, kg_b, i_val, hg_lb, nb, t_len, reverse=True, layer=layer)
    return _hgrn_out(x2d, o_f, o_b, g, norm_w.reshape(1, vw), w_out.astype(BF16))


def kernel(x, mem, mix_norm, ab_w_in, ab_w_out, rw_mu, rw_w0, rw_w2, rw_a0, rw_a2, rw_g2, rw_k_k, rw_k_a, rw_r_k, rw_gn_w, rw_gn_b, ssd_conv_w, ssd_conv_b, ssd_dt_bias, ssd_a_log, ssd_d, ssd_norm_w, hg_w_in, hg_w_out, hg_norm_w, hg_lb, xa_norm, mem_norm, xa_wq, xa_wkv, xa_wo, ffn_norm, ffn_w_in, ffn_w_out, final_norm):
    nb, t_len, d = x.shape
    mem_len = mem.shape[1]
    depth = mix_norm.shape[0]
    x2d = x.reshape(nb * t_len, d)
    mem2d = mem.reshape(nb * mem_len, d)
    for layer in range(depth):
        if layer % 2 == 0:
            e = layer // 2
            x2d = _rwkv_ssd_layer(x2d, nb, t_len, mix_norm[layer], ab_w_in[e], ab_w_out[e], rw_mu[e], rw_w0[e],
                                  rw_w2[e], rw_a0[e], rw_a2[e], rw_g2[e], rw_k_k[e], rw_k_a[e],
                                  rw_r_k[e].reshape(-1), rw_gn_w[e], rw_gn_b[e], ssd_conv_w[e], ssd_conv_b[e],
                                  ssd_dt_bias[e], ssd_a_log[e], ssd_d[e], ssd_norm_w[e])
        else:
            o = layer // 2
            x2d = _hgrn_layer(x2d, nb, t_len, mix_norm[layer], hg_w_in[o], hg_w_out[o], hg_norm_w[o], hg_lb, layer)
        (kv,) = _norm_matmul(mem2d, mem_norm[layer], [xa_wkv[layer].astype(BF16)], [BF16])
        x2d = _xattn(x2d, xa_norm[layer], xa_wq[layer].astype(BF16), kv, xa_wo[layer].astype(BF16),
                     nb, t_len, mem_len)
        x2d = _ffn(x2d, ffn_norm[layer], ffn_w_in[layer].astype(BF16), ffn_w_out[layer].astype(BF16),
                   final_norm, final_norm=(layer == depth - 1))
    return x2d.reshape(nb, t_len, d)
```

```python
import functools
import math

import jax
import jax.numpy as jnp
from jax import lax
from jax.experimental import pallas as pl
from jax.experimental.pallas import tpu as pltpu

F32 = jnp.float32
BF16 = jnp.bfloat16
ACT = BF16

D_MODEL = 1024
NORM_EPS = 1e-6

RW_HEAD_DIM = 64
RW_DIM = D_MODEL // 2
RW_HEADS = RW_DIM // RW_HEAD_DIM
RW_DECAY_RANK = 64
RW_A_RANK = 64
RW_GATE_RANK = 128
RW_GN_EPS = 64e-5
RW_LORA = RW_DECAY_RANK + RW_A_RANK + RW_GATE_RANK
RW_PROJ = 3 * RW_DIM + RW_LORA
RW_CHUNK = 64
RW_BLOCK_CHUNKS = 8
RW_HEAD_PAIRS = RW_HEADS // 2
assert RW_CHUNK == RW_HEAD_DIM and 2 * RW_HEAD_DIM == 128

SSD_HEAD_DIM = 64
SSD_DIM = D_MODEL // 2
SSD_HEADS = SSD_DIM // SSD_HEAD_DIM
SSD_GROUPS = 2
SSD_STATE = 128
SSD_CONV_DIM = SSD_DIM + 2 * SSD_GROUPS * SSD_STATE
SSD_CHUNK = 128
SSD_BLOCK_CHUNKS = 8
SSD_GROUP_HEADS = SSD_HEADS // SSD_GROUPS
assert SSD_HEAD_DIM == RW_HEAD_DIM and SSD_GROUP_HEADS % 2 == 0

HG_HEADS = 8
HG_KEY_DIM = 128
HG_VAL_DIM = D_MODEL // HG_HEADS
HG_KEY_WIDTH = HG_HEADS * HG_KEY_DIM
HG_VAL_WIDTH = HG_HEADS * HG_VAL_DIM
HG_CHUNK = 64
HG_SUB = 16
HG_BLOCK_CHUNKS = 8
HG_SAFE_LOG_RANGE = 64.0
assert HG_KEY_DIM == HG_VAL_DIM

XA_HEADS = 4
XA_HEAD_DIM = D_MODEL // XA_HEADS
FFN_DIM = ((8 * D_MODEL + 3 * 256 - 1) // (3 * 256)) * 256

SUBLANES = 8
HALO_ROWS = 16
SHIFT_ROWS = 128
PROJ_COLS = 256
LANES = 128
VMEM_LIMIT_BYTES = 56 * 1024 * 1024
SCAN_STEP_ROWS = 1024
NEG_BIG = -1e30


def _params(*semantics):
    return pltpu.CompilerParams(dimension_semantics=semantics, vmem_limit_bytes=VMEM_LIMIT_BYTES)


def _tile(n, pref):
    t = min(n, pref)
    while n % t or t % SUBLANES:
        t -= 1
    return t


def _scan_rows(t_len, sub_rows):
    assert t_len % sub_rows == 0
    rows = sub_rows
    while rows * 2 <= SCAN_STEP_ROWS and t_len % (rows * 2) == 0:
        rows *= 2
    return rows


def _mm(a, b):
    return jnp.dot(a.astype(BF16), b.astype(BF16), preferred_element_type=F32)


def _mm_nt(a, b):
    return lax.dot_general(a.astype(BF16), b.astype(BF16), (((1,), (1,)), ((), ())),
                           preferred_element_type=F32)


def _mm_tn(a, b):
    return lax.dot_general(a.astype(BF16), b.astype(BF16), (((0,), (0,)), ((), ())),
                           preferred_element_type=F32)


def _split3(x):
    hi = x.astype(BF16)
    r1 = x - hi.astype(F32)
    mid = r1.astype(BF16)
    lo = (r1 - mid.astype(F32)).astype(BF16)
    return hi, mid, lo


def _mm_exact_lhs(m_bf16, x, terms=3):
    dot = functools.partial(jnp.dot, preferred_element_type=F32)
    pieces = _split3(x)[:terms]
    out = dot(m_bf16, pieces[0])
    for piece in pieces[1:]:
        out = out + dot(m_bf16, piece)
    return out


def _mm_exact_rhs_many(xs, m_bf16, terms):
    rows = xs[0].shape[0]
    pieces = [piece for x, n_terms in zip(xs, terms) for piece in _split3(x)[:n_terms]]
    out = jnp.dot(jnp.concatenate(pieces, axis=0), m_bf16, preferred_element_type=F32)
    results, at = [], 0
    for n_terms in terms:
        total = out[at:at + rows]
        for t in range(1, n_terms):
            total = total + out[at + t * rows:at + (t + 1) * rows]
        results.append(total)
        at += n_terms * rows
    return results


def _rms_normed(x, gain):
    ms = jnp.mean(x * x, axis=-1, keepdims=True)
    return x * lax.rsqrt(ms + NORM_EPS) * gain


def _sigmoid(x):
    return 0.5 * jnp.tanh(0.5 * x) + 0.5


def _silu(x):
    h = 0.5 * x
    return h * jnp.tanh(h) + h


def _softplus(x):
    return jnp.maximum(x, 0.0) + jnp.log(1.0 + jnp.exp(-jnp.abs(x)))


def _order_masks(n, reverse):
    row = lax.broadcasted_iota(jnp.int32, (n, n), 0)
    col = lax.broadcasted_iota(jnp.int32, (n, n), 1)
    if reverse:
        return col >= row, col > row
    return col <= row, col < row


def _norm_matmul_kernel(x_ref, g_ref, *refs, n_out):
    xn = _rms_normed(x_ref[...], g_ref[...]).astype(BF16)
    for w_ref, o_ref in zip(refs[:n_out], refs[n_out:]):
        o_ref[...] = jnp.dot(xn, w_ref[...], preferred_element_type=F32).astype(o_ref.dtype)


def _norm_matmul(x2d, gain, weights, out_dtypes, tm_pref=512):
    n, d = x2d.shape
    tm = _tile(n, tm_pref)
    in_specs = [pl.BlockSpec((tm, d), lambda i: (i, 0)), pl.BlockSpec((1, d), lambda i: (0, 0))]
    in_specs += [pl.BlockSpec(w.shape, lambda i: (0, 0)) for w in weights]
    out_specs = [pl.BlockSpec((tm, w.shape[1]), lambda i: (i, 0)) for w in weights]
    out_shape = [jax.ShapeDtypeStruct((n, w.shape[1]), dt) for w, dt in zip(weights, out_dtypes)]
    return pl.pallas_call(
        functools.partial(_norm_matmul_kernel, n_out=len(weights)),
        grid=(n // tm,), in_specs=in_specs, out_specs=out_specs, out_shape=out_shape,
        compiler_params=_params("parallel"), name="norm_matmul",
    )(x2d, gain.reshape(1, d), *weights)


def _neighbours(p, prev_blk, next_blk, first, last):
    tt = p.shape[0]
    ridx = lax.broadcasted_iota(jnp.int32, p.shape, 0)
    prev_row = jnp.where(first, 0.0, prev_blk[HALO_ROWS - 1:HALO_ROWS, :].astype(F32))
    next_row = jnp.where(last, 0.0, next_blk[0:1, :].astype(F32))
    prev = jnp.where(ridx == 0, prev_row, pltpu.roll(p, 1, 0))
    nxt = jnp.where(ridx == tt - 1, next_row, pltpu.roll(p, tt - 1, 0))
    return prev, nxt


def _shift_matrix(tt, offsets, weight):
    t = jnp.arange(tt)[:, None]
    c = jnp.arange(tt + 2 * HALO_ROWS)[None, :]
    hit = functools.reduce(jnp.logical_or, [c == HALO_ROWS + t + d for d in offsets])
    return jnp.where(hit, weight, 0.0).astype(BF16)


def _lane_group_sum(x, width):
    assert width % LANES == 0 and x.shape[1] % width == 0
    ones = jnp.ones((LANES, LANES), BF16)
    groups = []
    for c in range(0, x.shape[1], width):
        total = _mm(x[:, c:c + LANES], ones)
        for c2 in range(c + LANES, c + width, LANES):
            total = total + _mm(x[:, c2:c2 + LANES], ones)
        groups += [total] * (width // LANES)
    return jnp.concatenate(groups, axis=1)


def _head_sum(x, pair_ones_bf16):
    blocks = [_mm(x[:, c:c + LANES], pair_ones_bf16) for c in range(0, x.shape[1], LANES)]
    return jnp.concatenate(blocks, axis=1)


def _mix0_project(x_ref, xp_ref, xn_ref, gain_ref, wrkv_ref, wlora_ref, wz_ref, wxbc_ref, wdt_ref, first, last,
                  prkv_scr, plora_scr, u_scr, z_scr, dt_scr):
    tt = x_ref.shape[0]
    dot = functools.partial(jnp.dot, preferred_element_type=F32)
    x_ext = jnp.concatenate([jnp.where(first, 0.0, xp_ref[...]), x_ref[...], jnp.where(last, 0.0, xn_ref[...])], axis=0)
    xn_ext = _rms_normed(x_ext, gain_ref[...]).astype(BF16)
    xn = xn_ext[HALO_ROWS:HALO_ROWS + tt]

    def column_blocks(lhs, w_ref, scr):
        block = min(PROJ_COLS, scr.shape[1])
        for c in range(0, scr.shape[1], block):
            scr[:, c:c + block] = dot(lhs, w_ref[:, c:c + block]).astype(scr.dtype)

    def project_rwkv():
        column_blocks(xn_ext, wrkv_ref, prkv_scr)
        column_blocks(xn_ext, wlora_ref, plora_scr)

    def project_ssd():
        column_blocks(xn_ext, wxbc_ref, u_scr)
        column_blocks(xn, wz_ref, z_scr)
        column_blocks(xn, wdt_ref, dt_scr)

    return project_rwkv, project_ssd


def _mix0_operands(project_ssd, prkv_scr, plora_scr, u_scr, z_scr, dt_scr,
                   savg_ref, mu_ref, mul_ref, a0_ref, a2_ref, g2_ref, kk_ref, ka_ref, w0_ref, w2_ref, rk_ref, bd_ref,
                   cw_ref, cb_ref, dtb_ref, alog_ref,
                   r_out, k_out, v_out, kkn_out, kka_out, lwf_out, lwb_out, g_out, bonus_out,
                   z_out, xbc_out, dtf_out, dtb_out, laf_out, lab_out):
    tt = z_scr.shape[0]
    dot = functools.partial(jnp.dot, preferred_element_type=F32)

    def token_shift(p_scr, m_ref, lo_col, hi_col):
        x = p_scr[HALO_ROWS:HALO_ROWS + tt, lo_col:hi_col].astype(F32)
        avg = jnp.concatenate(
            [dot(savg_ref[...], p_scr[t0:t0 + SHIFT_ROWS + 2 * HALO_ROWS, lo_col:hi_col])
             for t0 in range(0, tt, SHIFT_ROWS)], axis=0)
        return x + m_ref[:, lo_col:hi_col] * (avg - x)

    r = token_shift(prkv_scr, mu_ref, 0, RW_DIM)
    k = token_shift(prkv_scr, mu_ref, RW_DIM, 2 * RW_DIM)
    v = token_shift(prkv_scr, mu_ref, 2 * RW_DIM, 3 * RW_DIM)
    lo = token_shift(plora_scr, mul_ref, 0, RW_LORA)
    project_ssd()
    wd = lo[:, :RW_DECAY_RANK]
    ad = lo[:, RW_DECAY_RANK:RW_DECAY_RANK + RW_A_RANK]
    gd = lo[:, RW_DECAY_RANK + RW_A_RANK:]
    bd = bd_ref[...]

    a_logits = _mm(ad, a2_ref[...])
    g = _mm(_sigmoid(gd), g2_ref[...])
    kk = k * kk_ref[...]
    kk_sq = _head_sum(kk * kk, bd)
    w_lora = jnp.tanh(wd)
    decay_logits = [w0_ref[d:d + 1, :] + _mm(w_lora, w2_ref[d]) for d in range(2)]

    z_out[...] = z_scr[...]
    u_ext = u_scr[...]
    u = u_ext[HALO_ROWS:HALO_ROWS + tt].astype(F32)
    prev, nxt = _neighbours(u, u_ext[:HALO_ROWS], u_ext[HALO_ROWS + tt:], False, False)
    conv = prev * cw_ref[0:1, :] + u * cw_ref[1:2, :] + nxt * cw_ref[2:3, :] + cb_ref[...]
    xbc_out[...] = _silu(conv).astype(xbc_out.dtype)
    dt_raw = dt_scr[...]
    dt_f = _softplus(dt_raw + dtb_ref[0:1, :])
    dt_b = _softplus(dt_raw + dtb_ref[1:2, :])
    dtf_out[...] = dt_f
    dtb_out[...] = dt_b
    laf_out[...] = dt_f * (-jnp.exp(alog_ref[0:1, :]))
    lab_out[...] = dt_b * (-jnp.exp(alog_ref[1:2, :]))

    a = _sigmoid(a0_ref[...] + a_logits)
    kk = kk / jnp.maximum(jnp.sqrt(kk_sq), 1e-12)
    k = k * (1.0 + (a - 1.0) * ka_ref[...])
    r_out[...] = r.astype(r_out.dtype)
    k_out[...] = k.astype(k_out.dtype)
    v_out[...] = v.astype(v_out.dtype)
    kkn_out[...] = kk.astype(kkn_out.dtype)
    kka_out[...] = (kk * a).astype(kka_out.dtype)
    lwf_out[...] = -math.exp(-0.5) * _sigmoid(decay_logits[0])
    lwb_out[...] = -math.exp(-0.5) * _sigmoid(decay_logits[1])
    g_out[...] = g.astype(g_out.dtype)
    bonus_out[...] = (_head_sum(r * k * rk_ref[...], bd) * v).astype(bonus_out.dtype)


MIX0_N_PROJ_REFS = 9
MIX0_N_CONSTS = 16
MIX0_N_OUTS = 15
MIX0_N_STAGED = 5


def _mix0_in_kernel(*refs, per_seq):
    proj_refs = refs[:MIX0_N_PROJ_REFS]
    const_refs = refs[MIX0_N_PROJ_REFS:MIX0_N_PROJ_REFS + MIX0_N_CONSTS]
    out_refs = refs[MIX0_N_PROJ_REFS + MIX0_N_CONSTS:MIX0_N_PROJ_REFS + MIX0_N_CONSTS + MIX0_N_OUTS]
    staged = refs[MIX0_N_PROJ_REFS + MIX0_N_CONSTS + MIX0_N_OUTS:]
    pos = lax.rem(pl.program_id(0), per_seq)
    project_rwkv, project_ssd = _mix0_project(*proj_refs, pos == 0, pos == per_seq - 1, *staged)
    project_rwkv()
    _mix0_operands(project_ssd, *staged, *const_refs, *out_refs)


def _mix0_in(x2d, nb, t_len, gain, weights, rwkv_consts, ssd_consts):
    n, d = x2d.shape
    tt = _tile(t_len, 512)
    assert tt % HALO_ROWS == 0
    per_seq = t_len // tt
    n_tiles = n // tt
    blocks_per_tile = tt // HALO_ROWS
    n_blocks = n // HALO_ROWS

    def full(a):
        nd = a.ndim
        return pl.BlockSpec(a.shape, lambda s: (0,) * nd)

    x_specs = [pl.BlockSpec((tt, d), lambda s: (s, 0)),
               pl.BlockSpec((HALO_ROWS, d), lambda s: (jnp.maximum(s * blocks_per_tile - 1, 0), 0)),
               pl.BlockSpec((HALO_ROWS, d), lambda s: (jnp.minimum((s + 1) * blocks_per_tile, n_blocks - 1), 0))]
    proj_consts = (gain.reshape(1, d),) + tuple(weights)
    assert tt % SHIFT_ROWS == 0
    consts = (_shift_matrix(SHIFT_ROWS, (-1, 1), 0.5),) + tuple(rwkv_consts) + tuple(ssd_consts)
    assert 3 + len(proj_consts) == MIX0_N_PROJ_REFS and len(consts) == MIX0_N_CONSTS
    wide = [(RW_DIM, ACT)] * 5 + [(RW_DIM, F32)] * 2 + [(RW_DIM, ACT)] * 2 + [(SSD_DIM, ACT), (SSD_CONV_DIM, ACT)] \
        + [(LANES, F32)] * 4
    assert len(wide) == MIX0_N_OUTS
    ext = tt + 2 * HALO_ROWS
    staged = [pltpu.VMEM((ext, 3 * RW_DIM), ACT), pltpu.VMEM((ext, RW_LORA), ACT), pltpu.VMEM((ext, SSD_CONV_DIM), ACT),
              pltpu.VMEM((tt, SSD_DIM), ACT), pltpu.VMEM((tt, LANES), F32)]
    assert len(staged) == MIX0_N_STAGED
    return pl.pallas_call(
        functools.partial(_mix0_in_kernel, per_seq=per_seq),
        grid=(n_tiles,),
        in_specs=x_specs + [full(a) for a in proj_consts + consts],
        out_specs=[pl.BlockSpec((tt, w), lambda s: (s, 0)) for w, _ in wide],
        out_shape=[jax.ShapeDtypeStruct((n, w), dt) for w, dt in wide],
        scratch_shapes=staged,
        compiler_params=_params("parallel"), name="mix0_in",
    )(x2d, x2d, x2d, *proj_consts, *consts)


def _pair_blockdiag(y):
    lane = lax.broadcasted_iota(jnp.int32, y.shape, 1)
    top = jnp.where(lane < RW_HEAD_DIM, y, 0.0).astype(BF16)
    bottom = jnp.where(lane >= RW_HEAD_DIM, y, 0.0).astype(BF16)
    return jnp.concatenate([top, bottom], axis=0)


def _pair_diag_blocks(full):
    lane = lax.broadcasted_iota(jnp.int32, (RW_HEAD_DIM, LANES), 1)
    return jnp.where(lane < RW_HEAD_DIM, full[:RW_HEAD_DIM], full[RW_HEAD_DIM:])


def _unit_lower_inverses(n_strict_list, eye):
    size = n_strict_list[0].shape[0]
    levels = int(math.log2(size)) - 1
    accs = [eye + n for n in n_strict_list]
    powers = [_mm(n, _pair_blockdiag(n)) for n in n_strict_list]
    for level in range(levels):
        power_bds = [_pair_blockdiag(p) for p in powers]
        if level == levels - 1:
            return [a + _mm(a, b) for a, b in zip(accs, power_bds)]
        both = [_mm(jnp.concatenate([a, p], axis=0), b) for a, p, b in zip(accs, powers, power_bds)]
        accs = [a + ab[:size] for a, ab in zip(accs, both)]
        powers = [ab[size:] for ab in both]


def _rwkv_scan_kernel(r_ref, k_ref, v_ref, kk_ref, kka_ref, lw_ref, o_ref, h_ref, *, reverse):
    @pl.when(pl.program_id(1) == 0)
    def _():
        h_ref[...] = jnp.zeros_like(h_ref)

    sub_rows = RW_BLOCK_CHUNKS * RW_CHUNK
    n_sub = r_ref.shape[0] // sub_rows

    def body(i, carry):
        blk = (n_sub - 1 - i) if reverse else i
        _rwkv_scan_block(r_ref, k_ref, v_ref, kk_ref, kka_ref, lw_ref, o_ref, h_ref,
                         pl.multiple_of(blk * sub_rows, sub_rows), reverse)
        return carry

    lax.fori_loop(0, n_sub, body, 0)


def _rwkv_scan_block(r_ref, k_ref, v_ref, kk_ref, kka_ref, lw_ref, o_ref, h_ref, base, reverse):
    chunk = RW_CHUNK
    n_chunks = RW_BLOCK_CHUNKS
    incl_bf = _order_masks(chunk, reverse)[0].astype(BF16)
    row = lax.broadcasted_iota(jnp.int32, (chunk, LANES), 0)
    pos = lax.broadcasted_iota(jnp.int32, (chunk, LANES), 1) % RW_HEAD_DIM
    incl, strict = (pos >= row, pos > row) if reverse else (pos <= row, pos < row)
    eye = (pos == row).astype(F32)
    end = 0 if reverse else chunk - 1
    chunk_order = list(range(n_chunks - 1, -1, -1)) if reverse else list(range(n_chunks))

    items = []
    for ci in chunk_order:
        rows = pl.ds(base + ci * chunk, chunk)
        lw = lw_ref[rows, :]
        cum = _mm_exact_lhs(incl_bf, lw, terms=2)
        cum_end = cum[end:end + 1, :]
        e_neg = jnp.exp(-cum)
        p_end = jnp.exp(cum_end)
        e_tail = p_end * e_neg
        k = k_ref[rows, :].astype(F32)
        kka = kka_ref[rows, :].astype(F32)
        at_all = -kk_ref[rows, :].astype(F32) * jnp.exp(cum - lw)
        bt_all = kka * e_neg
        kt_all = k * e_neg
        rt_all = r_ref[rows, :].astype(F32) * jnp.exp(cum)
        bh_all = kka * e_tail
        kh_all = k * e_tail
        v = v_ref[rows, :].astype(F32)
        for j in range(RW_HEAD_PAIRS):
            sl = slice(j * LANES, (j + 1) * LANES)
            items.append(dict(ci=ci, j=j, at=at_all[:, sl], bt=bt_all[:, sl], kt=kt_all[:, sl], rt=rt_all[:, sl],
                              bh=bh_all[:, sl], kh=kh_all[:, sl], v=v[:, sl], p_end=p_end[:, sl]))

    for it in items:
        lhs = jnp.concatenate([it["at"], it["rt"]], axis=0)
        both = _mm_nt(lhs, jnp.concatenate([_pair_blockdiag(it["bt"]), _pair_blockdiag(it["kt"])], axis=0))
        it["a_ab"] = jnp.where(strict, both[:chunk, :LANES], 0.0)
        it["a_rb"] = jnp.where(incl, both[chunk:, :LANES], 0.0)
        it["a_ak"] = jnp.where(strict, both[:chunk, LANES:], 0.0)
        it["a_rk"] = jnp.where(incl, both[chunk:, LANES:], 0.0)
    t_invs = _unit_lower_inverses([it["a_ab"] for it in items], eye)
    for it in items:
        akv_rkv = _mm(jnp.concatenate([it["a_ak"], it["a_rk"]], axis=0), _pair_blockdiag(it["v"]))
        it["x1"] = akv_rkv[:chunk]
        it["rk_v"] = akv_rkv[chunk:]
    for it, t_inv in zip(items, t_invs):
        wu_u0 = _mm(t_inv, jnp.concatenate([_pair_blockdiag(it["at"]), _pair_blockdiag(it["x1"])], axis=1))
        it["w_u"] = wu_u0[:, :LANES]
        it["u0"] = wu_u0[:, LANES:]
    for it in items:
        rb_wu_u0 = _mm(it["a_rb"],
                       jnp.concatenate([_pair_blockdiag(it["w_u"]), _pair_blockdiag(it["u0"])], axis=1))
        w_o = it["rt"] + rb_wu_u0[:, :LANES]
        g_mat = eye * it["p_end"] + _pair_diag_blocks(_mm_tn(it["bh"], it["w_u"]))
        it["wo_g"] = jnp.concatenate([w_o, g_mat], axis=0)
        it["o0"] = rb_wu_u0[:, LANES:] + it["rk_v"]
        it["h0"] = _pair_diag_blocks(_mm_tn(jnp.concatenate([it["bh"], it["kh"]], axis=0),
                                            jnp.concatenate([it["u0"], it["v"]], axis=0)))

    states = [h_ref[j] for j in range(RW_HEAD_PAIRS)]
    for it in items:
        j = it["j"]
        res = _mm(it["wo_g"], _pair_blockdiag(states[j]))
        rows = pl.ds(base + it["ci"] * chunk, chunk)
        o_ref[rows, j * LANES:(j + 1) * LANES] = (res[:chunk] + it["o0"]).astype(o_ref.dtype)
        states[j] = res[chunk:] + it["h0"]
    for j in range(RW_HEAD_PAIRS):
        h_ref[j] = states[j]


def _rwkv_scan(r, k, v, kk, kka, lw, nb, t_len, reverse):
    n = r.shape[0]
    tb = _scan_rows(t_len, RW_BLOCK_CHUNKS * RW_CHUNK)
    nc = t_len // tb

    def idx(b, c):
        return (b * nc + (nc - 1 - c if reverse else c), 0)

    spec = pl.BlockSpec((tb, RW_DIM), idx)
    return pl.pallas_call(
        functools.partial(_rwkv_scan_kernel, reverse=reverse),
        grid=(nb, nc), in_specs=[spec] * 6, out_specs=spec,
        out_shape=jax.ShapeDtypeStruct((n, RW_DIM), ACT),
        scratch_shapes=[pltpu.VMEM((RW_HEAD_PAIRS, RW_HEAD_DIM, LANES), F32)],
        compiler_params=_params("parallel", "arbitrary"), name="rwkv_scan_bwd" if reverse else "rwkv_scan_fwd",
    )(r, k, v, kk, kka, lw)


def _ssd_scan_kernel(xbc_ref, dt_ref, la_ref, ex64_ref, y_ref, s_ref, *, reverse):
    @pl.when(pl.program_id(1) == 0)
    def _():
        s_ref[...] = jnp.zeros_like(s_ref)

    sub_rows = SSD_BLOCK_CHUNKS * SSD_CHUNK
    n_sub = xbc_ref.shape[0] // sub_rows

    def body(i, carry):
        blk = (n_sub - 1 - i) if reverse else i
        _ssd_scan_block(xbc_ref, dt_ref, la_ref, ex64_ref, y_ref, s_ref,
                        pl.multiple_of(blk * sub_rows, sub_rows), reverse)
        return carry

    lax.fori_loop(0, n_sub, body, 0)


def _ssd_scan_block(xbc_ref, dt_ref, la_ref, ex64_ref, y_ref, s_ref, base, reverse):
    chunk = SSD_CHUNK
    n_chunks = SSD_BLOCK_CHUNKS
    incl, _ = _order_masks(chunk, reverse)
    incl_bf = incl.astype(BF16)
    end = 0 if reverse else chunk - 1
    gn = SSD_GROUPS * SSD_STATE
    gw = SSD_GROUP_HEADS * SSD_HEAD_DIM
    chunk_order = list(range(n_chunks - 1, -1, -1)) if reverse else list(range(n_chunks))
    ex64 = ex64_ref[...]

    pre = []
    for ci in chunk_order:
        rows = pl.ds(base + ci * chunk, chunk)
        la = la_ref[rows, :]
        cum = _mm_exact_lhs(incl_bf, la, terms=2)
        cum_rows = cum.T
        cum_end = cum[end:end + 1, :]
        ecum64, etail64, dt64 = _mm_exact_rhs_many(
            [jnp.exp(cum), jnp.exp(cum_end - cum), dt_ref[rows, :]], ex64, terms=(2, 1, 1))
        xdt = xbc_ref[rows, :SSD_DIM].astype(F32) * dt64
        pre.append(dict(rows=rows, cum=cum, cum_rows=cum_rows, ecum64=ecum64, eend64=ecum64[end:end + 1, :],
                        xdt=xdt, xtail=xdt * etail64,
                        bm=[xbc_ref[rows, SSD_DIM + g * SSD_STATE:SSD_DIM + (g + 1) * SSD_STATE]
                            for g in range(SSD_GROUPS)],
                        cm=[xbc_ref[rows, SSD_DIM + gn + g * SSD_STATE:SSD_DIM + gn + (g + 1) * SSD_STATE]
                            for g in range(SSD_GROUPS)]))
    for p in pre:
        p["scores"] = [_mm_nt(p["cm"][g], p["bm"][g]) for g in range(SSD_GROUPS)]
        p["inject"] = [_mm_tn(p["bm"][g], p["xtail"][:, g * gw:(g + 1) * gw]) for g in range(SSD_GROUPS)]
    for p in pre:
        weights = []
        for h in range(SSD_HEADS):
            diff = p["cum"][:, h:h + 1] - p["cum_rows"][h:h + 1, :]
            decay = jnp.exp(jnp.where(incl, diff, NEG_BIG))
            weights.append(p["scores"][h // SSD_GROUP_HEADS] * decay)
        p["ydiag"] = [_mm(jnp.concatenate([weights[h], weights[h + 1]], axis=1),
                          _pair_blockdiag(p["xdt"][:, h * SSD_HEAD_DIM:(h + 2) * SSD_HEAD_DIM]))
                      for h in range(0, SSD_HEADS, 2)]

    states = [s_ref[g] for g in range(SSD_GROUPS)]
    for p in pre:
        for g in range(SSD_GROUPS):
            y_off = _mm(p["cm"][g], states[g]) * p["ecum64"][:, g * gw:(g + 1) * gw]
            for jp in range(SSD_GROUP_HEADS // 2):
                pair = g * (SSD_GROUP_HEADS // 2) + jp
                y_ref[p["rows"], pair * LANES:(pair + 1) * LANES] = (
                    p["ydiag"][pair] + y_off[:, jp * LANES:(jp + 1) * LANES]).astype(y_ref.dtype)
            states[g] = states[g] * p["eend64"][:, g * gw:(g + 1) * gw] + p["inject"][g]
    for g in range(SSD_GROUPS):
        s_ref[g] = states[g]


def _ssd_scan(xbc, dt, la, nb, t_len, reverse):
    n = xbc.shape[0]
    tb = _scan_rows(t_len, SSD_BLOCK_CHUNKS * SSD_CHUNK)
    assert SSD_CHUNK == LANES
    nc = t_len // tb
    lane_head = jnp.arange(LANES)[:, None]
    ex64 = (lane_head == jnp.arange(SSD_DIM)[None, :] // SSD_HEAD_DIM).astype(BF16)

    def idx(b, c):
        return (b * nc + (nc - 1 - c if reverse else c), 0)

    def full(a):
        return pl.BlockSpec(a.shape, lambda b, c: (0, 0))

    return pl.pallas_call(
        functools.partial(_ssd_scan_kernel, reverse=reverse),
        grid=(nb, nc),
        in_specs=[pl.BlockSpec((tb, SSD_CONV_DIM), idx), pl.BlockSpec((tb, LANES), idx),
                  pl.BlockSpec((tb, LANES), idx), full(ex64)],
        out_specs=pl.BlockSpec((tb, SSD_DIM), idx),
        out_shape=jax.ShapeDtypeStruct((n, SSD_DIM), ACT),
        scratch_shapes=[pltpu.VMEM((SSD_GROUPS, SSD_STATE, SSD_GROUP_HEADS * SSD_HEAD_DIM), F32)],
        compiler_params=_params("parallel", "arbitrary"), name="ssd_scan_bwd" if reverse else "ssd_scan_fwd",
    )(xbc, dt, la, ex64)


def _mix0_out_kernel(x_ref, of_ref, ob_ref, bonus_ref, g_ref, yf_ref, yb_ref, xbc_ref, z_ref,
                     gnw_ref, gnb_ref, bd_ref, dsk_ref, nw_ref, wo_rw_ref, wo_ssd_ref, out_ref):
    bd = bd_ref[...]
    o = of_ref[...].astype(F32) + ob_ref[...].astype(F32)
    inv_n = 1.0 / RW_HEAD_DIM
    mu = _head_sum(o, bd) * inv_n
    oc = o - mu
    var = _head_sum(oc * oc, bd) * inv_n
    o = oc * lax.rsqrt(var + RW_GN_EPS) * gnw_ref[...] + gnb_ref[...]
    o_rw = (o + bonus_ref[...].astype(F32)) * g_ref[...].astype(F32)

    y = yf_ref[...].astype(F32) + yb_ref[...].astype(F32) + dsk_ref[...] * xbc_ref[...].astype(F32)
    y = y * _silu(z_ref[...].astype(F32))
    gw = SSD_DIM // SSD_GROUPS
    parts = []
    for g in range(SSD_GROUPS):
        yg = y[:, g * gw:(g + 1) * gw]
        parts.append(yg * lax.rsqrt(jnp.mean(yg * yg, axis=-1, keepdims=True) + NORM_EPS))
    o_ssd = jnp.concatenate(parts, axis=-1) * nw_ref[...]
    out_ref[...] = x_ref[...] + _mm(o_rw, wo_rw_ref[...]) + _mm(o_ssd, wo_ssd_ref[...])


def _mix0_out(x2d, o_f, o_b, bonus, g, y_f, y_b, xbc, z, gn_w, gn_b, blockdiag, d_skip, norm_w, wo_rw, wo_ssd):
    n = x2d.shape[0]
    tm = _tile(n, 512)

    def row(width):
        return pl.BlockSpec((tm, width), lambda i: (i, 0))

    def full(a):
        return pl.BlockSpec(a.shape, lambda i: (0, 0))

    consts = (gn_w, gn_b, blockdiag, d_skip, norm_w, wo_rw, wo_ssd)
    return pl.pallas_call(
        _mix0_out_kernel,
        grid=(n // tm,),
        in_specs=[row(D_MODEL)] + [row(RW_DIM)] * 6 + [row(SSD_DIM), row(SSD_DIM)] + [full(a) for a in consts],
        out_specs=row(D_MODEL),
        out_shape=jax.ShapeDtypeStruct((n, D_MODEL), F32),
        compiler_params=_params("parallel"), name="mix0_out",
    )(x2d, o_f, o_b, bonus, g, y_f, y_b, xbc, z, *consts)


def _hgrn_lower_bound(lb_ref, layer):
    x = lb_ref[...]
    m = jnp.max(x, axis=0, keepdims=True)
    e = jnp.exp(x - m)
    s = e / jnp.sum(e, axis=0, keepdims=True)
    lb = jnp.zeros_like(s[0:1, :])
    for i in range(1, layer + 1):
        lb = lb + s[i:i + 1, :]
    return lb


def _hgrn_in_kernel(x_ref, g_ref, lb_ref, wq_ref, wff_ref, wfb_ref, wi_ref, wg_ref,
                    q_out, lff_out, lfb_out, kgf_out, kgb_out, i_out, g_out, *, layer):
    xn = _rms_normed(x_ref[...], g_ref[...]).astype(BF16)
    lb = _hgrn_lower_bound(lb_ref, layer)
    dot = functools.partial(jnp.dot, preferred_element_type=F32)
    width = x_ref.shape[1]
    jobs = []
    for c in range(0, width, PROJ_COLS):
        cols = slice(c, c + PROJ_COLS)
        jobs.append((wff_ref, cols, ("gate", lff_out, kgf_out)))
        jobs.append((wq_ref, cols, ("plain", q_out)))
        jobs.append((wfb_ref, cols, ("gate", lfb_out, kgb_out)))
        jobs.append((wi_ref, cols, ("plain", i_out)))
        jobs.append((wg_ref, cols, ("plain", g_out)))

    def finish(acc, cols, how):
        if how[0] == "gate":
            f = lb[:, cols] + (1.0 - lb[:, cols]) * _sigmoid(acc)
            how[1][:, cols] = jnp.log(f)
            how[2][:, cols] = (1.0 - f).astype(how[2].dtype)
        else:
            how[1][:, cols] = acc.astype(how[1].dtype)

    pending = None
    for w_ref, cols, how in jobs:
        acc = dot(xn, w_ref[:, cols])
        if pending is not None:
            finish(*pending)
        pending = (acc, cols, how)
    finish(*pending)


def _hgrn_in(x2d, gain, hg_lb, weights, layer):
    n, d = x2d.shape
    tm = _tile(n, 512)
    row = pl.BlockSpec((tm, d), lambda i: (i, 0))
    in_specs = [row, pl.BlockSpec((1, d), lambda i: (0, 0)), pl.BlockSpec(hg_lb.shape, lambda i: (0, 0))]
    in_specs += [pl.BlockSpec(w.shape, lambda i: (0, 0)) for w in weights]
    dtypes = (ACT, F32, F32, ACT, ACT, ACT, ACT)
    return pl.pallas_call(
        functools.partial(_hgrn_in_kernel, layer=layer),
        grid=(n // tm,), in_specs=in_specs, out_specs=[row] * len(dtypes),
        out_shape=[jax.ShapeDtypeStruct((n, d), dt) for dt in dtypes],
        compiler_params=_params("parallel"), name="hgrn_in",
    )(x2d, gain.reshape(1, d), hg_lb, *weights)


def _hgrn_chunk(q, lf, kg, vv, state_t, reverse):
    chunk = q.shape[0]
    n_sub = chunk // HG_SUB
    incl, _ = _order_masks(chunk, reverse)
    end = 0 if reverse else chunk - 1
    s_idx = lax.broadcasted_iota(jnp.int32, (HG_SUB, 1), 0)

    bcum = _mm_exact_lhs(incl.astype(BF16), lf)
    b_end = bcum[end:end + 1, :]
    carried = _mm_nt(q * jnp.exp(bcum), state_t)

    order = list(range(n_sub - 1, -1, -1)) if reverse else list(range(n_sub))
    outs = [None] * n_sub
    for pos, sb in enumerate(order):
        rows = slice(sb * HG_SUB, (sb + 1) * HG_SUB)
        q_s, k_s, b_s, v_s = q[rows], kg[rows], bcum[rows], vv[rows]
        acc = carried[rows]
        if pos > 0:
            prev_sb = order[pos - 1]
            edge = prev_sb * HG_SUB if reverse else prev_sb * HG_SUB + HG_SUB - 1
            b_edge = bcum[edge:edge + 1, :]
            src = slice((sb + 1) * HG_SUB, chunk) if reverse else slice(0, sb * HG_SUB)
            q_hat = q_s * jnp.exp(b_s - b_edge)
            k_hat = kg[src] * jnp.exp(b_edge - bcum[src])
            acc = acc + _mm(_mm_nt(q_hat, k_hat), vv[src])
        rows_out = []
        for l in range(HG_SUB):
            w = jnp.exp(jnp.minimum(b_s[l:l + 1, :] - b_s, 0.0)) * k_s * q_s[l:l + 1, :]
            att = jnp.sum(w, axis=-1, keepdims=True)
            att = jnp.where((s_idx >= l) if reverse else (s_idx <= l), att, 0.0)
            rows_out.append(jnp.sum(att * v_s, axis=0, keepdims=True))
        outs[sb] = acc + jnp.concatenate(rows_out, axis=0)

    k_tail = kg * jnp.exp(b_end - bcum)
    new_state_t = state_t * jnp.exp(b_end) + _mm_tn(vv, k_tail)
    return jnp.concatenate(outs, axis=0), new_state_t


def _hgrn_scan_kernel(q_ref, lf_ref, kg_ref, i_ref, lb_ref, o_ref, s_ref, hq_ref, hf_ref, hk_ref, hv_ref, ho_ref, *,
                      reverse, layer):
    @pl.when(pl.program_id(1) == 0)
    def _():
        s_ref[...] = jnp.zeros_like(s_ref)

    sub_rows = HG_BLOCK_CHUNKS * HG_CHUNK
    n_sub = q_ref.shape[0] // sub_rows

    def body(i, carry):
        blk = (n_sub - 1 - i) if reverse else i
        _hgrn_scan_block(q_ref, lf_ref, kg_ref, i_ref, lb_ref, o_ref, s_ref, hq_ref, hf_ref, hk_ref, hv_ref, ho_ref,
                         pl.multiple_of(blk * sub_rows, sub_rows), reverse, layer)
        return carry

    lax.fori_loop(0, n_sub, body, 0)


def _hgrn_scan_block(q_ref, lf_ref, kg_ref, i_ref, lb_ref, o_ref, s_ref, hq_ref, hf_ref, hk_ref, hv_ref, ho_ref,
                     base, reverse, layer):
    chunk = HG_CHUNK
    n_chunks = HG_BLOCK_CHUNKS
    incl, _ = _order_masks(chunk, reverse)
    incl_bf = incl.astype(BF16)
    end = 0 if reverse else chunk - 1
    chunk_order = list(range(n_chunks - 1, -1, -1)) if reverse else list(range(n_chunks))

    def head_cols(h):
        return slice(h * HG_KEY_DIM, (h + 1) * HG_KEY_DIM)

    pre = []
    worst = None
    for ci in chunk_order:
        rows = pl.ds(base + ci * chunk, chunk)
        bcum = _mm_exact_lhs(incl_bf, lf_ref[rows, :], terms=2)
        b_end = bcum[end:end + 1, :]
        worst = b_end if worst is None else jnp.minimum(worst, b_end)
        pre.append((ci, rows, bcum, b_end))
    lb_min = jnp.min(_hgrn_lower_bound(lb_ref, layer))
    flag = lax.cond(lb_min >= math.exp(-HG_SAFE_LOG_RANGE / HG_CHUNK),
                    lambda: jnp.int32(1),
                    lambda: (jnp.min(worst) >= -HG_SAFE_LOG_RANGE).astype(jnp.int32))
    safe = flag == 1

    @pl.when(safe)
    def _():
        items = []
        for ci, rows, bcum, b_end in pre:
            kg = kg_ref[rows, :].astype(F32)
            e_neg = jnp.exp(-bcum)
            e_end = jnp.exp(b_end)
            qt_all = q_ref[rows, :].astype(F32) * jnp.exp(bcum)
            kt_all = kg * e_neg
            ktail_all = kt_all * e_end
            v_all = i_ref[rows, :].astype(F32)
            for h in range(HG_HEADS):
                c = head_cols(h)
                items.append(dict(rows=rows, h=h, qt=qt_all[:, c], kt=kt_all[:, c], ktail=ktail_all[:, c],
                                  v=v_all[:, c], e_end=e_end[:, c]))
        for it in items:
            it["attn"] = jnp.where(incl, _mm_nt(it["qt"], it["kt"]), 0.0)
        for it in items:
            it["local"] = _mm(it["attn"], it["v"])
            it["inject"] = _mm_tn(it["v"], it["ktail"])
        states = [s_ref[h] for h in range(HG_HEADS)]
        for it in items:
            h = it["h"]
            o_ref[it["rows"], head_cols(h)] = (it["local"] + _mm_nt(it["qt"], states[h])).astype(o_ref.dtype)
            states[h] = states[h] * it["e_end"] + it["inject"]
        for h in range(HG_HEADS):
            s_ref[h] = states[h]

    @pl.when(jnp.logical_not(safe))
    def _():
        def head_body(h, carry):
            out, new_state = _hgrn_chunk(hq_ref[h], hf_ref[h], hk_ref[h], hv_ref[h], s_ref[h], reverse)
            ho_ref[h] = out
            s_ref[h] = new_state
            return carry

        def chunk_body(i, carry):
            ci = (n_chunks - 1 - i) if reverse else i
            rows = pl.ds(pl.multiple_of(base + ci * chunk, chunk), chunk)
            for h in range(HG_HEADS):
                hq_ref[h] = q_ref[rows, head_cols(h)].astype(F32)
                hf_ref[h] = lf_ref[rows, head_cols(h)]
                hk_ref[h] = kg_ref[rows, head_cols(h)].astype(F32)
                hv_ref[h] = i_ref[rows, head_cols(h)].astype(F32)
            lax.fori_loop(0, HG_HEADS, head_body, 0)
            for h in range(HG_HEADS):
                o_ref[rows, head_cols(h)] = ho_ref[h].astype(o_ref.dtype)
            return carry

        lax.fori_loop(0, n_chunks, chunk_body, 0)


def _hgrn_scan(q, lf, kg, i_val, hg_lb, nb, t_len, reverse, layer):
    n = q.shape[0]
    tb = _scan_rows(t_len, HG_BLOCK_CHUNKS * HG_CHUNK)
    n_outer = t_len // tb

    def idx(b, c):
        return (b * n_outer + (n_outer - 1 - c if reverse else c), 0)

    spec = pl.BlockSpec((tb, HG_KEY_WIDTH), idx)
    head_buf = pltpu.VMEM((HG_HEADS, HG_CHUNK, HG_KEY_DIM), F32)
    return pl.pallas_call(
        functools.partial(_hgrn_scan_kernel, reverse=reverse, layer=layer),
        grid=(nb, n_outer),
        in_specs=[spec, spec, spec, spec, pl.BlockSpec(hg_lb.shape, lambda b, c: (0, 0))],
        out_specs=spec,
        out_shape=jax.ShapeDtypeStruct((n, HG_VAL_WIDTH), ACT),
        scratch_shapes=[pltpu.VMEM((HG_HEADS, HG_VAL_DIM, HG_KEY_DIM), F32)] + [head_buf] * 5,
        compiler_params=_params("parallel", "arbitrary"),
        name="hgrn_scan_bwd" if reverse else "hgrn_scan_fwd",
    )(q, lf, kg, i_val, hg_lb)


def _hgrn_out_kernel(x_ref, of_ref, ob_ref, g_ref, nw_ref, wo_ref, out_ref):
    o = of_ref[...].astype(F32) + ob_ref[...].astype(F32)
    mean_sq = _lane_group_sum(o * o, HG_VAL_DIM) * (1.0 / HG_VAL_DIM)
    o = o * lax.rsqrt(mean_sq + NORM_EPS) * nw_ref[...] * _silu(g_ref[...].astype(F32))
    out_ref[...] = x_ref[...] + _mm(o, wo_ref[...])


def _hgrn_out(x2d, o_f, o_b, g, norm_w, w_out):
    n = x2d.shape[0]
    tm = _tile(n, 512)
    row = pl.BlockSpec((tm, D_MODEL), lambda i: (i, 0))
    return pl.pallas_call(
        _hgrn_out_kernel,
        grid=(n // tm,),
        in_specs=[row] * 4 + [pl.BlockSpec((1, HG_VAL_WIDTH), lambda i: (0, 0)),
                              pl.BlockSpec(w_out.shape, lambda i: (0, 0))],
        out_specs=row,
        out_shape=jax.ShapeDtypeStruct((n, D_MODEL), F32),
        compiler_params=_params("parallel"), name="hgrn_out",
    )(x2d, o_f, o_b, g, norm_w, w_out)


def _xattn_kernel(x_ref, g_ref, wq_ref, kv_ref, wo_ref, out_ref):
    x = x_ref[...]
    q = _mm(_rms_normed(x, g_ref[...]), wq_ref[...])
    scale = XA_HEAD_DIM ** -0.5
    scores = [_mm_nt(q[:, h * XA_HEAD_DIM:(h + 1) * XA_HEAD_DIM], kv_ref[:, h * XA_HEAD_DIM:(h + 1) * XA_HEAD_DIM])
              * scale for h in range(XA_HEADS)]
    probs = []
    for s in scores:
        p = jnp.exp(s - jnp.max(s, axis=-1, keepdims=True))
        probs.append(p * (1.0 / jnp.sum(p, axis=-1, keepdims=True)))
    heads = [_mm(p, kv_ref[:, D_MODEL + h * XA_HEAD_DIM:D_MODEL + (h + 1) * XA_HEAD_DIM])
             for h, p in enumerate(probs)]
    out_ref[...] = x + _mm(jnp.concatenate(heads, axis=-1), wo_ref[...])


def _xattn(x2d, gain, wq, kv, wo, nb, t_len, mem_len):
    n = x2d.shape[0]
    tm = _tile(t_len, 1024)
    per_seq = t_len // tm
    row = pl.BlockSpec((tm, D_MODEL), lambda b, j: (b * per_seq + j, 0))
    return pl.pallas_call(
        _xattn_kernel,
        grid=(nb, per_seq),
        in_specs=[row, pl.BlockSpec((1, D_MODEL), lambda b, j: (0, 0)),
                  pl.BlockSpec(wq.shape, lambda b, j: (0, 0)),
                  pl.BlockSpec((mem_len, 2 * D_MODEL), lambda b, j: (b, 0)),
                  pl.BlockSpec(wo.shape, lambda b, j: (0, 0))],
        out_specs=row,
        out_shape=jax.ShapeDtypeStruct((n, D_MODEL), F32),
        compiler_params=_params("parallel", "parallel"), name="xattn",
    )(x2d, gain.reshape(1, D_MODEL), wq, kv, wo)


def _ffn_kernel(x_ref, g_ref, wi_ref, wo_ref, fg_ref, out_ref, *, final_norm):
    x = x_ref[...]
    gu = _mm(_rms_normed(x, g_ref[...]), wi_ref[...])
    act = _silu(gu[:, :FFN_DIM]) * gu[:, FFN_DIM:]
    y = x + _mm(act, wo_ref[...])
    if final_norm:
        y = _rms_normed(y, fg_ref[...])
    out_ref[...] = y


def _ffn(x2d, gain, w_in, w_out, final_gain, final_norm):
    n = x2d.shape[0]
    tm = _tile(n, 512)
    row = pl.BlockSpec((tm, D_MODEL), lambda i: (i, 0))
    vec = pl.BlockSpec((1, D_MODEL), lambda i: (0, 0))
    resident = pl.Buffered(1)
    return pl.pallas_call(
        functools.partial(_ffn_kernel, final_norm=final_norm),
        grid=(n // tm,),
        in_specs=[row, vec, pl.BlockSpec(w_in.shape, lambda i: (0, 0), pipeline_mode=resident),
                  pl.BlockSpec(w_out.shape, lambda i: (0, 0), pipeline_mode=resident), vec],
        out_specs=row,
        out_shape=jax.ShapeDtypeStruct((n, D_MODEL), F32),
        compiler_params=_params("parallel"), name="ffn",
    )(x2d, gain.reshape(1, D_MODEL), w_in, w_out, final_gain.reshape(1, D_MODEL))


def _pad_lanes(a, width=LANES):
    return jnp.pad(a, [(0, 0)] * (a.ndim - 1) + [(0, width - a.shape[-1])])


def _rwkv_ssd_layer(x2d, nb, t_len, norm_g, w_in, w_out, rw_mu, rw_w0, rw_w2, rw_a0, rw_a2, rw_g2, rw_k_k, rw_k_a,
                    rw_r_k, rw_gn_w, rw_gn_b, conv_w, conv_b, dt_bias, a_log, d_skip, ssd_norm_w):
    o_lora = 3 * RW_DIM
    o_z = RW_PROJ
    o_xbc = o_z + SSD_DIM
    o_dt = o_xbc + SSD_CONV_DIM
    wb = w_in.astype(BF16)
    weights = [wb[:, :o_lora], wb[:, o_lora:o_z], wb[:, o_z:o_xbc], wb[:, o_xbc:o_dt], _pad_lanes(wb[:, o_dt:])]
    head_id = jnp.arange(LANES) // RW_HEAD_DIM
    blockdiag = (head_id[:, None] == head_id[None, :]).astype(BF16)
    row = lambda a: a.reshape(1, -1)
    rwkv_consts = (row(rw_mu[:o_lora]), row(rw_mu[o_lora:]), row(rw_a0), rw_a2.astype(BF16), rw_g2.astype(BF16),
                   row(rw_k_k), row(rw_k_a), rw_w0, rw_w2.astype(BF16), row(rw_r_k), blockdiag)
    ssd_consts = (conv_w, row(conv_b), _pad_lanes(dt_bias), _pad_lanes(a_log))
    (r, k, v, kk, kka, lw_f, lw_b, g, bonus, p_z, xbc, dt_f, dt_b, la_f, la_b) = _mix0_in(
        x2d, nb, t_len, norm_g, weights, rwkv_consts, ssd_consts)
    o_f = _rwkv_scan(r, k, v, kk, kka, lw_f, nb, t_len, reverse=False)
    o_b = _rwkv_scan(r, k, v, kk, kka, lw_b, nb, t_len, reverse=True)
    y_f = _ssd_scan(xbc, dt_f, la_f, nb, t_len, reverse=False)
    y_b = _ssd_scan(xbc, dt_b, la_b, nb, t_len, reverse=True)

    wo = w_out.astype(BF16)
    d_skip_lanes = jnp.repeat(d_skip, SSD_HEAD_DIM).reshape(1, SSD_DIM)
    return _mix0_out(x2d, o_f, o_b, bonus, g, y_f, y_b, xbc, p_z, row(rw_gn_w), row(rw_gn_b), blockdiag,
                     d_skip_lanes, row(ssd_norm_w), wo[:RW_DIM], wo[RW_DIM:])


def _hgrn_layer(x2d, nb, t_len, norm_g, w_in, w_out, norm_w, hg_lb, layer):
    kw, vw = HG_KEY_WIDTH, HG_VAL_WIDTH
    wb = w_in.astype(BF16)
    weights = [wb[:, :kw], wb[:, kw:2 * kw], wb[:, 2 * kw:3 * kw], wb[:, 3 * kw:3 * kw + vw], wb[:, 3 * kw + vw:]]
    q, lf_f, lf_b, kg_f, kg_b, i_val, g = _hgrn_in(x2d, norm_g, hg_lb, weights, layer)
    o_f = _hgrn_scan(q, lf_f, kg_f, i_val, hg_lb, nb, t_len, reverse=False, layer=layer)
    o_b = _hgrn_scan(q, lf_b, kg_b, i_val, hg_lb, nb, t_len, reverse=True, layer=layer)
    return _hgrn_out(x2d, o_f, o_b, g, norm_w.reshape(1, vw), w_out.astype(BF16))


def kernel(x, mem, mix_norm, ab_w_in, ab_w_out, rw_mu, rw_w0, rw_w2, rw_a0, rw_a2, rw_g2, rw_k_k, rw_k_a, rw_r_k, rw_gn_w, rw_gn_b, ssd_conv_w, ssd_conv_b, ssd_dt_bias, ssd_a_log, ssd_d, ssd_norm_w, hg_w_in, hg_w_out, hg_norm_w, hg_lb, xa_norm, mem_norm, xa_wq, xa_wkv, xa_wo, ffn_norm, ffn_w_in, ffn_w_out, final_norm):
    nb, t_len, d = x.shape
    mem_len = mem.shape[1]
    depth = mix_norm.shape[0]
    x2d = x.reshape(nb * t_len, d)
    mem2d = mem.reshape(nb * mem_len, d)
    for layer in range(depth):
        if layer % 2 == 0:
            e = layer // 2
            x2d = _rwkv_ssd_layer(x2d, nb, t_len, mix_norm[layer], ab_w_in[e], ab_w_out[e], rw_mu[e], rw_w0[e],
                                  rw_w2[e], rw_a0[e], rw_a2[e], rw_g2[e], rw_k_k[e], rw_k_a[e],
                                  rw_r_k[e].reshape(-1), rw_gn_w[e], rw_gn_b[e], ssd_conv_w[e], ssd_conv_b[e],
                                  ssd_dt_bias[e], ssd_a_log[e], ssd_d[e], ssd_norm_w[e])
        else:
            o = layer // 2
            x2d = _hgrn_layer(x2d, nb, t_len, mix_norm[layer], hg_w_in[o], hg_w_out[o], hg_norm_w[o], hg_lb, layer)
        (kv,) = _norm_matmul(mem2d, mem_norm[layer], [xa_wkv[layer].astype(BF16)], [BF16])
        x2d = _xattn(x2d, xa_norm[layer], xa_wq[layer].astype(BF16), kv, xa_wo[layer].astype(BF16),
                     nb, t_len, mem_len)
        x2d = _ffn(x2d, ffn_norm[layer], ffn_w_in[layer].astype(BF16), ffn_w_out[layer].astype(BF16),
                   final_norm, final_norm=(layer == depth - 1))
    return x2d.reshape(nb, t_len, d)
```

```python
import functools
import math

import jax
import jax.numpy as jnp
from jax import lax
from jax.experimental import pallas as pl
from jax.experimental.pallas import tpu as pltpu

F32 = jnp.float32
BF16 = jnp.bfloat16
ACT = BF16

D_MODEL = 1024
NORM_EPS = 1e-6

RW_HEAD_DIM = 64
RW_DIM = D_MODEL // 2
RW_HEADS = RW_DIM // RW_HEAD_DIM
RW_DECAY_RANK = 64
RW_A_RANK = 64
RW_GATE_RANK = 128
RW_GN_EPS = 64e-5
RW_LORA = RW_DECAY_RANK + RW_A_RANK + RW_GATE_RANK
RW_PROJ = 3 * RW_DIM + RW_LORA
RW_CHUNK = 64
RW_BLOCK_CHUNKS = 8
RW_HEAD_PAIRS = RW_HEADS // 2
assert RW_CHUNK == RW_HEAD_DIM and 2 * RW_HEAD_DIM == 128

SSD_HEAD_DIM = 64
SSD_DIM = D_MODEL // 2
SSD_HEADS = SSD_DIM // SSD_HEAD_DIM
SSD_GROUPS = 2
SSD_STATE = 128
SSD_CONV_DIM = SSD_DIM + 2 * SSD_GROUPS * SSD_STATE
SSD_CHUNK = 128
SSD_BLOCK_CHUNKS = 8
SSD_GROUP_HEADS = SSD_HEADS // SSD_GROUPS
assert SSD_HEAD_DIM == RW_HEAD_DIM and SSD_GROUP_HEADS % 2 == 0

HG_HEADS = 8
HG_KEY_DIM = 128
HG_VAL_DIM = D_MODEL // HG_HEADS
HG_KEY_WIDTH = HG_HEADS * HG_KEY_DIM
HG_VAL_WIDTH = HG_HEADS * HG_VAL_DIM
HG_CHUNK = 64
HG_SUB = 16
HG_BLOCK_CHUNKS = 8
HG_SAFE_LOG_RANGE = 64.0
assert HG_KEY_DIM == HG_VAL_DIM

XA_HEADS = 4
XA_HEAD_DIM = D_MODEL // XA_HEADS
FFN_DIM = ((8 * D_MODEL + 3 * 256 - 1) // (3 * 256)) * 256

SUBLANES = 8
HALO_ROWS = 16
SHIFT_ROWS = 128
PROJ_COLS = 256
LANES = 128
VMEM_LIMIT_BYTES = 56 * 1024 * 1024
SCAN_STEP_ROWS = 1024
NEG_BIG = -1e30


def _params(*semantics):
    return pltpu.CompilerParams(dimension_semantics=semantics, vmem_limit_bytes=VMEM_LIMIT_BYTES)


def _tile(n, pref):
    t = min(n, pref)
    while n % t or t % SUBLANES:
        t -= 1
    return t


def _scan_rows(t_len, sub_rows):
    assert t_len % sub_rows == 0
    rows = sub_rows
    while rows * 2 <= SCAN_STEP_ROWS and t_len % (rows * 2) == 0:
        rows *= 2
    return rows


def _mm(a, b):
    return jnp.dot(a.astype(BF16), b.astype(BF16), preferred_element_type=F32)


def _mm_nt(a, b):
    return lax.dot_general(a.astype(BF16), b.astype(BF16), (((1,), (1,)), ((), ())),
                           preferred_element_type=F32)


def _mm_tn(a, b):
    return lax.dot_general(a.astype(BF16), b.astype(BF16), (((0,), (0,)), ((), ())),
                           preferred_element_type=F32)


def _split3(x):
    hi = x.astype(BF16)
    r1 = x - hi.astype(F32)
    mid = r1.astype(BF16)
    lo = (r1 - mid.astype(F32)).astype(BF16)
    return hi, mid, lo


def _mm_exact_lhs(m_bf16, x, terms=3):
    dot = functools.partial(jnp.dot, preferred_element_type=F32)
    pieces = _split3(x)[:terms]
    out = dot(m_bf16, pieces[0])
    for piece in pieces[1:]:
        out = out + dot(m_bf16, piece)
    return out


def _mm_exact_rhs_many(xs, m_bf16, terms):
    rows = xs[0].shape[0]
    pieces = [piece for x, n_terms in zip(xs, terms) for piece in _split3(x)[:n_terms]]
    out = jnp.dot(jnp.concatenate(pieces, axis=0), m_bf16, preferred_element_type=F32)
    results, at = [], 0
    for n_terms in terms:
        total = out[at:at + rows]
        for t in range(1, n_terms):
            total = total + out[at + t * rows:at + (t + 1) * rows]
        results.append(total)
        at += n_terms * rows
    return results


def _rms_normed(x, gain):
    ms = jnp.mean(x * x, axis=-1, keepdims=True)
    return x * lax.rsqrt(ms + NORM_EPS) * gain


def _sigmoid(x):
    return 0.5 * jnp.tanh(0.5 * x) + 0.5


def _silu(x):
    h = 0.5 * x
    return h * jnp.tanh(h) + h


def _softplus(x):
    return jnp.maximum(x, 0.0) + jnp.log(1.0 + jnp.exp(-jnp.abs(x)))


def _order_masks(n, reverse):
    row = lax.broadcasted_iota(jnp.int32, (n, n), 0)
    col = lax.broadcasted_iota(jnp.int32, (n, n), 1)
    if reverse:
        return col >= row, col > row
    return col <= row, col < row


def _norm_matmul_kernel(x_ref, g_ref, *refs, n_out):
    xn = _rms_normed(x_ref[...], g_ref[...]).astype(BF16)
    for w_ref, o_ref in zip(refs[:n_out], refs[n_out:]):
        o_ref[...] = jnp.dot(xn, w_ref[...], preferred_element_type=F32).astype(o_ref.dtype)


def _norm_matmul(x2d, gain, weights, out_dtypes, tm_pref=512):
    n, d = x2d.shape
    tm = _tile(n, tm_pref)
    in_specs = [pl.BlockSpec((tm, d), lambda i: (i, 0)), pl.BlockSpec((1, d), lambda i: (0, 0))]
    in_specs += [pl.BlockSpec(w.shape, lambda i: (0, 0)) for w in weights]
    out_specs = [pl.BlockSpec((tm, w.shape[1]), lambda i: (i, 0)) for w in weights]
    out_shape = [jax.ShapeDtypeStruct((n, w.shape[1]), dt) for w, dt in zip(weights, out_dtypes)]
    return pl.pallas_call(
        functools.partial(_norm_matmul_kernel, n_out=len(weights)),
        grid=(n // tm,), in_specs=in_specs, out_specs=out_specs, out_shape=out_shape,
        compiler_params=_params("parallel"), name="norm_matmul",
    )(x2d, gain.reshape(1, d), *weights)


def _neighbours(p, prev_blk, next_blk, first, last):
    tt = p.shape[0]
    ridx = lax.broadcasted_iota(jnp.int32, p.shape, 0)
    prev_row = jnp.where(first, 0.0, prev_blk[HALO_ROWS - 1:HALO_ROWS, :].astype(F32))
    next_row = jnp.where(last, 0.0, next_blk[0:1, :].astype(F32))
    prev = jnp.where(ridx == 0, prev_row, pltpu.roll(p, 1, 0))
    nxt = jnp.where(ridx == tt - 1, next_row, pltpu.roll(p, tt - 1, 0))
    return prev, nxt


def _shift_matrix(tt, offsets, weight):
    t = jnp.arange(tt)[:, None]
    c = jnp.arange(tt + 2 * HALO_ROWS)[None, :]
    hit = functools.reduce(jnp.logical_or, [c == HALO_ROWS + t + d for d in offsets])
    return jnp.where(hit, weight, 0.0).astype(BF16)


def _lane_group_sum(x, width):
    assert width % LANES == 0 and x.shape[1] % width == 0
    ones = jnp.ones((LANES, LANES), BF16)
    groups = []
    for c in range(0, x.shape[1], width):
        total = _mm(x[:, c:c + LANES], ones)
        for c2 in range(c + LANES, c + width, LANES):
            total = total + _mm(x[:, c2:c2 + LANES], ones)
        groups += [total] * (width // LANES)
    return jnp.concatenate(groups, axis=1)


def _head_sum(x, pair_ones_bf16):
    blocks = [_mm(x[:, c:c + LANES], pair_ones_bf16) for c in range(0, x.shape[1], LANES)]
    return jnp.concatenate(blocks, axis=1)


def _mix0_project(x_ref, xp_ref, xn_ref, gain_ref, wrkv_ref, wlora_ref, wz_ref, wxbc_ref, wdt_ref, first, last,
                  prkv_scr, plora_scr, u_scr, z_scr, dt_scr):
    tt = x_ref.shape[0]
    dot = functools.partial(jnp.dot, preferred_element_type=F32)
    x_ext = jnp.concatenate([jnp.where(first, 0.0, xp_ref[...]), x_ref[...], jnp.where(last, 0.0, xn_ref[...])], axis=0)
    xn_ext = _rms_normed(x_ext, gain_ref[...]).astype(BF16)
    xn = xn_ext[HALO_ROWS:HALO_ROWS + tt]

    def column_blocks(lhs, w_ref, scr):
        block = min(PROJ_COLS, scr.shape[1])
        for c in range(0, scr.shape[1], block):
            scr[:, c:c + block] = dot(lhs, w_ref[:, c:c + block]).astype(scr.dtype)

    def project_rwkv():
        column_blocks(xn_ext, wrkv_ref, prkv_scr)
        column_blocks(xn_ext, wlora_ref, plora_scr)

    def project_ssd():
        column_blocks(xn_ext, wxbc_ref, u_scr)
        column_blocks(xn, wz_ref, z_scr)
        column_blocks(xn, wdt_ref, dt_scr)

    return project_rwkv, project_ssd


def _mix0_operands(project_ssd, prkv_scr, plora_scr, u_scr, z_scr, dt_scr,
                   savg_ref, mu_ref, mul_ref, a0_ref, a2_ref, g2_ref, kk_ref, ka_ref, w0_ref, w2_ref, rk_ref, bd_ref,
                   cw_ref, cb_ref, dtb_ref, alog_ref,
                   r_out, k_out, v_out, kkn_out, kka_out, lwf_out, lwb_out, g_out, bonus_out,
                   z_out, xbc_out, dtf_out, dtb_out, laf_out, lab_out):
    tt = z_scr.shape[0]
    dot = functools.partial(jnp.dot, preferred_element_type=F32)

    def token_shift(p_scr, m_ref, lo_col, hi_col):
        x = p_scr[HALO_ROWS:HALO_ROWS + tt, lo_col:hi_col].astype(F32)
        avg = jnp.concatenate(
            [dot(savg_ref[...], p_scr[t0:t0 + SHIFT_ROWS + 2 * HALO_ROWS, lo_col:hi_col])
             for t0 in range(0, tt, SHIFT_ROWS)], axis=0)
        return x + m_ref[:, lo_col:hi_col] * (avg - x)

    r = token_shift(prkv_scr, mu_ref, 0, RW_DIM)
    k = token_shift(prkv_scr, mu_ref, RW_DIM, 2 * RW_DIM)
    v = token_shift(prkv_scr, mu_ref, 2 * RW_DIM, 3 * RW_DIM)
    lo = token_shift(plora_scr, mul_ref, 0, RW_LORA)
    project_ssd()
    wd = lo[:, :RW_DECAY_RANK]
    ad = lo[:, RW_DECAY_RANK:RW_DECAY_RANK + RW_A_RANK]
    gd = lo[:, RW_DECAY_RANK + RW_A_RANK:]
    bd = bd_ref[...]

    a_logits = _mm(ad, a2_ref[...])
    g = _mm(_sigmoid(gd), g2_ref[...])
    kk = k * kk_ref[...]
    kk_sq = _head_sum(kk * kk, bd)
    w_lora = jnp.tanh(wd)
    decay_logits = [w0_ref[d:d + 1, :] + _mm(w_lora, w2_ref[d]) for d in range(2)]

    z_out[...] = z_scr[...]
    u_ext = u_scr[...]
    u = u_ext[HALO_ROWS:HALO_ROWS + tt].astype(F32)
    prev, nxt = _neighbours(u, u_ext[:HALO_ROWS], u_ext[HALO_ROWS + tt:], False, False)
    conv = prev * cw_ref[0:1, :] + u * cw_ref[1:2, :] + nxt * cw_ref[2:3, :] + cb_ref[...]
    xbc_out[...] = _silu(conv).astype(xbc_out.dtype)
    dt_raw = dt_scr[...]
    dt_f = _softplus(dt_raw + dtb_ref[0:1, :])
    dt_b = _softplus(dt_raw + dtb_ref[1:2, :])
    dtf_out[...] = dt_f
    dtb_out[...] = dt_b
    laf_out[...] = dt_f * (-jnp.exp(alog_ref[0:1, :]))
    lab_out[...] = dt_b * (-jnp.exp(alog_ref[1:2, :]))

    a = _sigmoid(a0_ref[...] + a_logits)
    kk = kk / jnp.maximum(jnp.sqrt(kk_sq), 1e-12)
    k = k * (1.0 + (a - 1.0) * ka_ref[...])
    r_out[...] = r.astype(r_out.dtype)
    k_out[...] = k.astype(k_out.dtype)
    v_out[...] = v.astype(v_out.dtype)
    kkn_out[...] = kk.astype(kkn_out.dtype)
    kka_out[...] = (kk * a).astype(kka_out.dtype)
    lwf_out[...] = -math.exp(-0.5) * _sigmoid(decay_logits[0])
    lwb_out[...] = -math.exp(-0.5) * _sigmoid(decay_logits[1])
    g_out[...] = g.astype(g_out.dtype)
    bonus_out[...] = (_head_sum(r * k * rk_ref[...], bd) * v).astype(bonus_out.dtype)


MIX0_N_PROJ_REFS = 9
MIX0_N_CONSTS = 16
MIX0_N_OUTS = 15
MIX0_N_STAGED = 5


def _mix0_in_kernel(*refs, per_seq):
    proj_refs = refs[:MIX0_N_PROJ_REFS]
    const_refs = refs[MIX0_N_PROJ_REFS:MIX0_N_PROJ_REFS + MIX0_N_CONSTS]
    out_refs = refs[MIX0_N_PROJ_REFS + MIX0_N_CONSTS:MIX0_N_PROJ_REFS + MIX0_N_CONSTS + MIX0_N_OUTS]
    staged = refs[MIX0_N_PROJ_REFS + MIX0_N_CONSTS + MIX0_N_OUTS:]
    pos = lax.rem(pl.program_id(0), per_seq)
    project_rwkv, project_ssd = _mix0_project(*proj_refs, pos == 0, pos == per_seq - 1, *staged)
    project_rwkv()
    _mix0_operands(project_ssd, *staged, *const_refs, *out_refs)


def _mix0_in(x2d, nb, t_len, gain, weights, rwkv_consts, ssd_consts):
    n, d = x2d.shape
    tt = _tile(t_len, 512)
    assert tt % HALO_ROWS == 0
    per_seq = t_len // tt
    n_tiles = n // tt
    blocks_per_tile = tt // HALO_ROWS
    n_blocks = n // HALO_ROWS

    def full(a):
        nd = a.ndim
        return pl.BlockSpec(a.shape, lambda s: (0,) * nd)

    x_specs = [pl.BlockSpec((tt, d), lambda s: (s, 0)),
               pl.BlockSpec((HALO_ROWS, d), lambda s: (jnp.maximum(s * blocks_per_tile - 1, 0), 0)),
               pl.BlockSpec((HALO_ROWS, d), lambda s: (jnp.minimum((s + 1) * blocks_per_tile, n_blocks - 1), 0))]
    proj_consts = (gain.reshape(1, d),) + tuple(weights)
    assert tt % SHIFT_ROWS == 0
    consts = (_shift_matrix(SHIFT_ROWS, (-1, 1), 0.5),) + tuple(rwkv_consts) + tuple(ssd_consts)
    assert 3 + len(proj_consts) == MIX0_N_PROJ_REFS and len(consts) == MIX0_N_CONSTS
    wide = [(RW_DIM, ACT)] * 5 + [(RW_DIM, F32)] * 2 + [(RW_DIM, ACT)] * 2 + [(SSD_DIM, ACT), (SSD_CONV_DIM, ACT)] \
        + [(LANES, F32)] * 4
    assert len(wide) == MIX0_N_OUTS
    ext = tt + 2 * HALO_ROWS
    staged = [pltpu.VMEM((ext, 3 * RW_DIM), ACT), pltpu.VMEM((ext, RW_LORA), ACT), pltpu.VMEM((ext, SSD_CONV_DIM), ACT),
              pltpu.VMEM((tt, SSD_DIM), ACT), pltpu.VMEM((tt, LANES), F32)]
    assert len(staged) == MIX0_N_STAGED
    return pl.pallas_call(
        functools.partial(_mix0_in_kernel, per_seq=per_seq),
        grid=(n_tiles,),
        in_specs=x_specs + [full(a) for a in proj_consts + consts],
        out_specs=[pl.BlockSpec((tt, w), lambda s: (s, 0)) for w, _ in wide],
        out_shape=[jax.ShapeDtypeStruct((n, w), dt) for w, dt in wide],
        scratch_shapes=staged,
        compiler_params=_params("parallel"), name="mix0_in",
    )(x2d, x2d, x2d, *proj_consts, *consts)


def _pair_blockdiag(y):
    lane = lax.broadcasted_iota(jnp.int32, y.shape, 1)
    top = jnp.where(lane < RW_HEAD_DIM, y, 0.0).astype(BF16)
    bottom = jnp.where(lane >= RW_HEAD_DIM, y, 0.0).astype(BF16)
    return jnp.concatenate([top, bottom], axis=0)


def _pair_diag_blocks(full):
    lane = lax.broadcasted_iota(jnp.int32, (RW_HEAD_DIM, LANES), 1)
    return jnp.where(lane < RW_HEAD_DIM, full[:RW_HEAD_DIM], full[RW_HEAD_DIM:])


def _unit_lower_inverses(n_strict_list, eye):
    size = n_strict_list[0].shape[0]
    levels = int(math.log2(size)) - 1
    accs = [eye + n for n in n_strict_list]
    powers = [_mm(n, _pair_blockdiag(n)) for n in n_strict_list]
    for level in range(levels):
        power_bds = [_pair_blockdiag(p) for p in powers]
        if level == levels - 1:
            return [a + _mm(a, b) for a, b in zip(accs, power_bds)]
        both = [_mm(jnp.concatenate([a, p], axis=0), b) for a, p, b in zip(accs, powers, power_bds)]
        accs = [a + ab[:size] for a, ab in zip(accs, both)]
        powers = [ab[size:] for ab in both]


def _rwkv_scan_kernel(r_ref, k_ref, v_ref, kk_ref, kka_ref, lw_ref, o_ref, h_ref, *, reverse):
    @pl.when(pl.program_id(1) == 0)
    def _():
        h_ref[...] = jnp.zeros_like(h_ref)

    sub_rows = RW_BLOCK_CHUNKS * RW_CHUNK
    n_sub = r_ref.shape[0] // sub_rows

    def body(i, carry):
        blk = (n_sub - 1 - i) if reverse else i
        _rwkv_scan_block(r_ref, k_ref, v_ref, kk_ref, kka_ref, lw_ref, o_ref, h_ref,
                         pl.multiple_of(blk * sub_rows, sub_rows), reverse)
        return carry

    lax.fori_loop(0, n_sub, body, 0)


def _rwkv_scan_block(r_ref, k_ref, v_ref, kk_ref, kka_ref, lw_ref, o_ref, h_ref, base, reverse):
    chunk = RW_CHUNK
    n_chunks = RW_BLOCK_CHUNKS
    incl_bf = _order_masks(chunk, reverse)[0].astype(BF16)
    row = lax.broadcasted_iota(jnp.int32, (chunk, LANES), 0)
    pos = lax.broadcasted_iota(jnp.int32, (chunk, LANES), 1) % RW_HEAD_DIM
    incl, strict = (pos >= row, pos > row) if reverse else (pos <= row, pos < row)
    eye = (pos == row).astype(F32)
    end = 0 if reverse else chunk - 1
    chunk_order = list(range(n_chunks - 1, -1, -1)) if reverse else list(range(n_chunks))

    items = []
    for ci in chunk_order:
        rows = pl.ds(base + ci * chunk, chunk)
        lw = lw_ref[rows, :]
        cum = _mm_exact_lhs(incl_bf, lw, terms=2)
        cum_end = cum[end:end + 1, :]
        e_neg = jnp.exp(-cum)
        p_end = jnp.exp(cum_end)
        e_tail = p_end * e_neg
        k = k_ref[rows, :].astype(F32)
        kka = kka_ref[rows, :].astype(F32)
        at_all = -kk_ref[rows, :].astype(F32) * jnp.exp(cum - lw)
        bt_all = kka * e_neg
        kt_all = k * e_neg
        rt_all = r_ref[rows, :].astype(F32) * jnp.exp(cum)
        bh_all = kka * e_tail
        kh_all = k * e_tail
        v = v_ref[rows, :].astype(F32)
        for j in range(RW_HEAD_PAIRS):
            sl = slice(j * LANES, (j + 1) * LANES)
            items.append(dict(ci=ci, j=j, at=at_all[:, sl], bt=bt_all[:, sl], kt=kt_all[:, sl], rt=rt_all[:, sl],
                              bh=bh_all[:, sl], kh=kh_all[:, sl], v=v[:, sl], p_end=p_end[:, sl]))

    for it in items:
        lhs = jnp.concatenate([it["at"], it["rt"]], axis=0)
        both = _mm_nt(lhs, jnp.concatenate([_pair_blockdiag(it["bt"]), _pair_blockdiag(it["kt"])], axis=0))
        it["a_ab"] = jnp.where(strict, both[:chunk, :LANES], 0.0)
        it["a_rb"] = jnp.where(incl, both[chunk:, :LANES], 0.0)
        it["a_ak"] = jnp.where(strict, both[:chunk, LANES:], 0.0)
        it["a_rk"] = jnp.where(incl, both[chunk:, LANES:], 0.0)
    t_invs = _unit_lower_inverses([it["a_ab"] for it in items], eye)
    for it in items:
        akv_rkv = _mm(jnp.concatenate([it["a_ak"], it["a_rk"]], axis=0), _pair_blockdiag(it["v"]))
        it["x1"] = akv_rkv[:chunk]
        it["rk_v"] = akv_rkv[chunk:]
    for it, t_inv in zip(items, t_invs):
        wu_u0 = _mm(t_inv, jnp.concatenate([_pair_blockdiag(it["at"]), _pair_blockdiag(it["x1"])], axis=1))
        it["w_u"] = wu_u0[:, :LANES]
        it["u0"] = wu_u0[:, LANES:]
    for it in items:
        rb_wu_u0 = _mm(it["a_rb"],
                       jnp.concatenate([_pair_blockdiag(it["w_u"]), _pair_blockdiag(it["u0"])], axis=1))
        w_o = it["rt"] + rb_wu_u0[:, :LANES]
        g_mat = eye * it["p_end"] + _pair_diag_blocks(_mm_tn(it["bh"], it["w_u"]))
        it["wo_g"] = jnp.concatenate([w_o, g_mat], axis=0)
        it["o0"] = rb_wu_u0[:, LANES:] + it["rk_v"]
        it["h0"] = _pair_diag_blocks(_mm_tn(jnp.concatenate([it["bh"], it["kh"]], axis=0),
                                            jnp.concatenate([it["u0"], it["v"]], axis=0)))

    states = [h_ref[j] for j in range(RW_HEAD_PAIRS)]
    for it in items:
        j = it["j"]
        res = _mm(it["wo_g"], _pair_blockdiag(states[j]))
        rows = pl.ds(base + it["ci"] * chunk, chunk)
        o_ref[rows, j * LANES:(j + 1) * LANES] = (res[:chunk] + it["o0"]).astype(o_ref.dtype)
        states[j] = res[chunk:] + it["h0"]
    for j in range(RW_HEAD_PAIRS):
        h_ref[j] = states[j]


def _rwkv_scan(r, k, v, kk, kka, lw, nb, t_len, reverse):
    n = r.shape[0]
    tb = _scan_rows(t_len, RW_BLOCK_CHUNKS * RW_CHUNK)
    nc = t_len // tb

    def idx(b, c):
        return (b * nc + (nc - 1 - c if reverse else c), 0)

    spec = pl.BlockSpec((tb, RW_DIM), idx)
    return pl.pallas_call(
        functools.partial(_rwkv_scan_kernel, reverse=reverse),
        grid=(nb, nc), in_specs=[spec] * 6, out_specs=spec,
        out_shape=jax.ShapeDtypeStruct((n, RW_DIM), ACT),
        scratch_shapes=[pltpu.VMEM((RW_HEAD_PAIRS, RW_HEAD_DIM, LANES), F32)],
        compiler_params=_params("parallel", "arbitrary"), name="rwkv_scan_bwd" if reverse else "rwkv_scan_fwd",
    )(r, k, v, kk, kka, lw)


def _ssd_scan_kernel(xbc_ref, dt_ref, la_ref, ex64_ref, y_ref, s_ref, *, reverse):
    @pl.when(pl.program_id(1) == 0)
    def _():
        s_ref[...] = jnp.zeros_like(s_ref)

    sub_rows = SSD_BLOCK_CHUNKS * SSD_CHUNK
    n_sub = xbc_ref.shape[0] // sub_rows

    def body(i, carry):
        blk = (n_sub - 1 - i) if reverse else i
        _ssd_scan_block(xbc_ref, dt_ref, la_ref, ex64_ref, y_ref, s_ref,
                        pl.multiple_of(blk * sub_rows, sub_rows), reverse)
        return carry

    lax.fori_loop(0, n_sub, body, 0)


def _ssd_scan_block(xbc_ref, dt_ref, la_ref, ex64_ref, y_ref, s_ref, base, reverse):
    chunk = SSD_CHUNK
    n_chunks = SSD_BLOCK_CHUNKS
    incl, _ = _order_masks(chunk, reverse)
    incl_bf = incl.astype(BF16)
    end = 0 if reverse else chunk - 1
    gn = SSD_GROUPS * SSD_STATE
    gw = SSD_GROUP_HEADS * SSD_HEAD_DIM
    chunk_order = list(range(n_chunks - 1, -1, -1)) if reverse else list(range(n_chunks))
    ex64 = ex64_ref[...]

    pre = []
    for ci in chunk_order:
        rows = pl.ds(base + ci * chunk, chunk)
        la = la_ref[rows, :]
        cum = _mm_exact_lhs(incl_bf, la, terms=2)
        cum_rows = cum.T
        cum_end = cum[end:end + 1, :]
        ecum64, etail64, dt64 = _mm_exact_rhs_many(
            [jnp.exp(cum), jnp.exp(cum_end - cum), dt_ref[rows, :]], ex64, terms=(2, 1, 1))
        xdt = xbc_ref[rows, :SSD_DIM].astype(F32) * dt64
        pre.append(dict(rows=rows, cum=cum, cum_rows=cum_rows, ecum64=ecum64, eend64=ecum64[end:end + 1, :],
                        xdt=xdt, xtail=xdt * etail64,
                        bm=[xbc_ref[rows, SSD_DIM + g * SSD_STATE:SSD_DIM + (g + 1) * SSD_STATE]
                            for g in range(SSD_GROUPS)],
                        cm=[xbc_ref[rows, SSD_DIM + gn + g * SSD_STATE:SSD_DIM + gn + (g + 1) * SSD_STATE]
                            for g in range(SSD_GROUPS)]))
    for p in pre:
        p["scores"] = [_mm_nt(p["cm"][g], p["bm"][g]) for g in range(SSD_GROUPS)]
        p["inject"] = [_mm_tn(p["bm"][g], p["xtail"][:, g * gw:(g + 1) * gw]) for g in range(SSD_GROUPS)]
    for p in pre:
        weights = []
        for h in range(SSD_HEADS):
            diff = p["cum"][:, h:h + 1] - p["cum_rows"][h:h + 1, :]
            decay = jnp.exp(jnp.where(incl, diff, NEG_BIG))
            weights.append(p["scores"][h // SSD_GROUP_HEADS] * decay)
        p["ydiag"] = [_mm(jnp.concatenate([weights[h], weights[h + 1]], axis=1),
                          _pair_blockdiag(p["xdt"][:, h * SSD_HEAD_DIM:(h + 2) * SSD_HEAD_DIM]))
                      for h in range(0, SSD_HEADS, 2)]

    states = [s_ref[g] for g in range(SSD_GROUPS)]
    for p in pre:
        for g in range(SSD_GROUPS):
            y_off = _mm(p["cm"][g], states[g]) * p["ecum64"][:, g * gw:(g + 1) * gw]
            for jp in range(SSD_GROUP_HEADS // 2):
                pair = g * (SSD_GROUP_HEADS // 2) + jp
                y_ref[p["rows"], pair * LANES:(pair + 1) * LANES] = (
                    p["ydiag"][pair] + y_off[:, jp * LANES:(jp + 1) * LANES]).astype(y_ref.dtype)
            states[g] = states[g] * p["eend64"][:, g * gw:(g + 1) * gw] + p["inject"][g]
    for g in range(SSD_GROUPS):
        s_ref[g] = states[g]


def _ssd_scan(xbc, dt, la, nb, t_len, reverse):
    n = xbc.shape[0]
    tb = _scan_rows(t_len, SSD_BLOCK_CHUNKS * SSD_CHUNK)
    assert SSD_CHUNK == LANES
    nc = t_len // tb
    lane_head = jnp.arange(LANES)[:, None]
    ex64 = (lane_head == jnp.arange(SSD_DIM)[None, :] // SSD_HEAD_DIM).astype(BF16)

    def idx(b, c):
        return (b * nc + (nc - 1 - c if reverse else c), 0)

    def full(a):
        return pl.BlockSpec(a.shape, lambda b, c: (0, 0))

    return pl.pallas_call(
        functools.partial(_ssd_scan_kernel, reverse=reverse),
        grid=(nb, nc),
        in_specs=[pl.BlockSpec((tb, SSD_CONV_DIM), idx), pl.BlockSpec((tb, LANES), idx),
                  pl.BlockSpec((tb, LANES), idx), full(ex64)],
        out_specs=pl.BlockSpec((tb, SSD_DIM), idx),
        out_shape=jax.ShapeDtypeStruct((n, SSD_DIM), ACT),
        scratch_shapes=[pltpu.VMEM((SSD_GROUPS, SSD_STATE, SSD_GROUP_HEADS * SSD_HEAD_DIM), F32)],
        compiler_params=_params("parallel", "arbitrary"), name="ssd_scan_bwd" if reverse else "ssd_scan_fwd",
    )(xbc, dt, la, ex64)


def _mix0_out_kernel(x_ref, of_ref, ob_ref, bonus_ref, g_ref, yf_ref, yb_ref, xbc_ref, z_ref,
                     gnw_ref, gnb_ref, bd_ref, dsk_ref, nw_ref, wo_rw_ref, wo_ssd_ref, out_ref):
    bd = bd_ref[...]
    o = of_ref[...].astype(F32) + ob_ref[...].astype(F32)
    inv_n = 1.0 / RW_HEAD_DIM
    mu = _head_sum(o, bd) * inv_n
    oc = o - mu
    var = _head_sum(oc * oc, bd) * inv_n
    o = oc * lax.rsqrt(var + RW_GN_EPS) * gnw_ref[...] + gnb_ref[...]
    o_rw = (o + bonus_ref[...].astype(F32)) * g_ref[...].astype(F32)

    y = yf_ref[...].astype(F32) + yb_ref[...].astype(F32) + dsk_ref[...] * xbc_ref[...].astype(F32)
    y = y * _silu(z_ref[...].astype(F32))
    gw = SSD_DIM // SSD_GROUPS
    parts = []
    for g in range(SSD_GROUPS):
        yg = y[:, g * gw:(g + 1) * gw]
        parts.append(yg * lax.rsqrt(jnp.mean(yg * yg, axis=-1, keepdims=True) + NORM_EPS))
    o_ssd = jnp.concatenate(parts, axis=-1) * nw_ref[...]
    out_ref[...] = x_ref[...] + _mm(o_rw, wo_rw_ref[...]) + _mm(o_ssd, wo_ssd_ref[...])


def _mix0_out(x2d, o_f, o_b, bonus, g, y_f, y_b, xbc, z, gn_w, gn_b, blockdiag, d_skip, norm_w, wo_rw, wo_ssd):
    n = x2d.shape[0]
    tm = _tile(n, 1024)

    def row(width):
        return pl.BlockSpec((tm, width), lambda i: (i, 0))

    def full(a):
        return pl.BlockSpec(a.shape, lambda i: (0, 0))

    consts = (gn_w, gn_b, blockdiag, d_skip, norm_w, wo_rw, wo_ssd)
    return pl.pallas_call(
        _mix0_out_kernel,
        grid=(n // tm,),
        in_specs=[row(D_MODEL)] + [row(RW_DIM)] * 6 + [row(SSD_DIM), row(SSD_DIM)] + [full(a) for a in consts],
        out_specs=row(D_MODEL),
        out_shape=jax.ShapeDtypeStruct((n, D_MODEL), F32),
        compiler_params=_params("parallel"), name="mix0_out",
    )(x2d, o_f, o_b, bonus, g, y_f, y_b, xbc, z, *consts)


def _hgrn_lower_bound(lb_ref, layer):
    x = lb_ref[...]
    m = jnp.max(x, axis=0, keepdims=True)
    e = jnp.exp(x - m)
    s = e / jnp.sum(e, axis=0, keepdims=True)
    lb = jnp.zeros_like(s[0:1, :])
    for i in range(1, layer + 1):
        lb = lb + s[i:i + 1, :]
    return lb


def _hgrn_in_kernel(x_ref, g_ref, lb_ref, wq_ref, wff_ref, wfb_ref, wi_ref, wg_ref,
                    q_out, lff_out, lfb_out, kgf_out, kgb_out, i_out, g_out, *, layer):
    xn = _rms_normed(x_ref[...], g_ref[...]).astype(BF16)
    lb = _hgrn_lower_bound(lb_ref, layer)
    dot = functools.partial(jnp.dot, preferred_element_type=F32)
    width = x_ref.shape[1]
    jobs = []
    for c in range(0, width, PROJ_COLS):
        cols = slice(c, c + PROJ_COLS)
        jobs.append((wff_ref, cols, ("gate", lff_out, kgf_out)))
        jobs.append((wq_ref, cols, ("plain", q_out)))
        jobs.append((wfb_ref, cols, ("gate", lfb_out, kgb_out)))
        jobs.append((wi_ref, cols, ("plain", i_out)))
        jobs.append((wg_ref, cols, ("plain", g_out)))

    def finish(acc, cols, how):
        if how[0] == "gate":
            f = lb[:, cols] + (1.0 - lb[:, cols]) * _sigmoid(acc)
            how[1][:, cols] = jnp.log(f)
            how[2][:, cols] = (1.0 - f).astype(how[2].dtype)
        else:
            how[1][:, cols] = acc.astype(how[1].dtype)

    pending = None
    for w_ref, cols, how in jobs:
        acc = dot(xn, w_ref[:, cols])
        if pending is not None:
            finish(*pending)
        pending = (acc, cols, how)
    finish(*pending)


def _hgrn_in(x2d, gain, hg_lb, weights, layer):
    n, d = x2d.shape
    tm = _tile(n, 512)
    row = pl.BlockSpec((tm, d), lambda i: (i, 0))
    in_specs = [row, pl.BlockSpec((1, d), lambda i: (0, 0)), pl.BlockSpec(hg_lb.shape, lambda i: (0, 0))]
    in_specs += [pl.BlockSpec(w.shape, lambda i: (0, 0)) for w in weights]
    dtypes = (ACT, F32, F32, ACT, ACT, ACT, ACT)
    return pl.pallas_call(
        functools.partial(_hgrn_in_kernel, layer=layer),
        grid=(n // tm,), in_specs=in_specs, out_specs=[row] * len(dtypes),
        out_shape=[jax.ShapeDtypeStruct((n, d), dt) for dt in dtypes],
        compiler_params=_params("parallel"), name="hgrn_in",
    )(x2d, gain.reshape(1, d), hg_lb, *weights)


def _hgrn_chunk(q, lf, kg, vv, state_t, reverse):
    chunk = q.shape[0]
    n_sub = chunk // HG_SUB
    incl, _ = _order_masks(chunk, reverse)
    end = 0 if reverse else chunk - 1
    s_idx = lax.broadcasted_iota(jnp.int32, (HG_SUB, 1), 0)

    bcum = _mm_exact_lhs(incl.astype(BF16), lf)
    b_end = bcum[end:end + 1, :]
    carried = _mm_nt(q * jnp.exp(bcum), state_t)

    order = list(range(n_sub - 1, -1, -1)) if reverse else list(range(n_sub))
    outs = [None] * n_sub
    for pos, sb in enumerate(order):
        rows = slice(sb * HG_SUB, (sb + 1) * HG_SUB)
        q_s, k_s, b_s, v_s = q[rows], kg[rows], bcum[rows], vv[rows]
        acc = carried[rows]
        if pos > 0:
            prev_sb = order[pos - 1]
            edge = prev_sb * HG_SUB if reverse else prev_sb * HG_SUB + HG_SUB - 1
            b_edge = bcum[edge:edge + 1, :]
            src = slice((sb + 1) * HG_SUB, chunk) if reverse else slice(0, sb * HG_SUB)
            q_hat = q_s * jnp.exp(b_s - b_edge)
            k_hat = kg[src] * jnp.exp(b_edge - bcum[src])
            acc = acc + _mm(_mm_nt(q_hat, k_hat), vv[src])
        rows_out = []
        for l in range(HG_SUB):
            w = jnp.exp(jnp.minimum(b_s[l:l + 1, :] - b_s, 0.0)) * k_s * q_s[l:l + 1, :]
            att = jnp.sum(w, axis=-1, keepdims=True)
            att = jnp.where((s_idx >= l) if reverse else (s_idx <= l), att, 0.0)
            rows_out.append(jnp.sum(att * v_s, axis=0, keepdims=True))
        outs[sb] = acc + jnp.concatenate(rows_out, axis=0)

    k_tail = kg * jnp.exp(b_end - bcum)
    new_state_t = state_t * jnp.exp(b_end) + _mm_tn(vv, k_tail)
    return jnp.concatenate(outs, axis=0), new_state_t


def _hgrn_scan_kernel(q_ref, lf_ref, kg_ref, i_ref, lb_ref, o_ref, s_ref, hq_ref, hf_ref, hk_ref, hv_ref, ho_ref, *,
                      reverse, layer):
    @pl.when(pl.program_id(1) == 0)
    def _():
        s_ref[...] = jnp.zeros_like(s_ref)

    sub_rows = HG_BLOCK_CHUNKS * HG_CHUNK
    n_sub = q_ref.shape[0] // sub_rows

    def body(i, carry):
        blk = (n_sub - 1 - i) if reverse else i
        _hgrn_scan_block(q_ref, lf_ref, kg_ref, i_ref, lb_ref, o_ref, s_ref, hq_ref, hf_ref, hk_ref, hv_ref, ho_ref,
                         pl.multiple_of(blk * sub_rows, sub_rows), reverse, layer)
        return carry

    lax.fori_loop(0, n_sub, body, 0)


def _hgrn_scan_block(q_ref, lf_ref, kg_ref, i_ref, lb_ref, o_ref, s_ref, hq_ref, hf_ref, hk_ref, hv_ref, ho_ref,
                     base, reverse, layer):
    chunk = HG_CHUNK
    n_chunks = HG_BLOCK_CHUNKS
    incl, _ = _order_masks(chunk, reverse)
    incl_bf = incl.astype(BF16)
    end = 0 if reverse else chunk - 1
    chunk_order = list(range(n_chunks - 1, -1, -1)) if reverse else list(range(n_chunks))

    def head_cols(h):
        return slice(h * HG_KEY_DIM, (h + 1) * HG_KEY_DIM)

    pre = []
    worst = None
    for ci in chunk_order:
        rows = pl.ds(base + ci * chunk, chunk)
        bcum = _mm_exact_lhs(incl_bf, lf_ref[rows, :], terms=2)
        b_end = bcum[end:end + 1, :]
        worst = b_end if worst is None else jnp.minimum(worst, b_end)
        pre.append((ci, rows, bcum, b_end))
    lb_min = jnp.min(_hgrn_lower_bound(lb_ref, layer))
    flag = lax.cond(lb_min >= math.exp(-HG_SAFE_LOG_RANGE / HG_CHUNK),
                    lambda: jnp.int32(1),
                    lambda: (jnp.min(worst) >= -HG_SAFE_LOG_RANGE).astype(jnp.int32))
    safe = flag == 1

    @pl.when(safe)
    def _():
        items = []
        for ci, rows, bcum, b_end in pre:
            kg = kg_ref[rows, :].astype(F32)
            e_neg = jnp.exp(-bcum)
            e_end = jnp.exp(b_end)
            qt_all = q_ref[rows, :].astype(F32) * jnp.exp(bcum)
            kt_all = kg * e_neg
            ktail_all = kt_all * e_end
            v_all = i_ref[rows, :].astype(F32)
            for h in range(HG_HEADS):
                c = head_cols(h)
                items.append(dict(rows=rows, h=h, qt=qt_all[:, c], kt=kt_all[:, c], ktail=ktail_all[:, c],
                                  v=v_all[:, c], e_end=e_end[:, c]))
        for it in items:
            it["attn"] = jnp.where(incl, _mm_nt(it["qt"], it["kt"]), 0.0)
        for it in items:
            it["local"] = _mm(it["attn"], it["v"])
            it["inject"] = _mm_tn(it["v"], it["ktail"])
        states = [s_ref[h] for h in range(HG_HEADS)]
        for it in items:
            h = it["h"]
            o_ref[it["rows"], head_cols(h)] = (it["local"] + _mm_nt(it["qt"], states[h])).astype(o_ref.dtype)
            states[h] = states[h] * it["e_end"] + it["inject"]
        for h in range(HG_HEADS):
            s_ref[h] = states[h]

    @pl.when(jnp.logical_not(safe))
    def _():
        def head_body(h, carry):
            out, new_state = _hgrn_chunk(hq_ref[h], hf_ref[h], hk_ref[h], hv_ref[h], s_ref[h], reverse)
            ho_ref[h] = out
            s_ref[h] = new_state
            return carry

        def chunk_body(i, carry):
            ci = (n_chunks - 1 - i) if reverse else i
            rows = pl.ds(pl.multiple_of(base + ci * chunk, chunk), chunk)
            for h in range(HG_HEADS):
                hq_ref[h] = q_ref[rows, head_cols(h)].astype(F32)
                hf_ref[h] = lf_ref[rows, head_cols(h)]
                hk_ref[h] = kg_ref[rows, head_cols(h)].astype(F32)
                hv_ref[h] = i_ref[rows, head_cols(h)].astype(F32)
            lax.fori_loop(0, HG_HEADS, head_body, 0)
            for h in range(HG_HEADS):
                o_ref[rows, head_cols(h)] = ho_ref[h].astype(o_ref.dtype)
            return carry

        lax.fori_loop(0, n_chunks, chunk_body, 0)


def _hgrn_scan(q, lf, kg, i_val, hg_lb, nb, t_len, reverse, layer):
    n = q.shape[0]
    tb = _scan_rows(t_len, HG_BLOCK_CHUNKS * HG_CHUNK)
    n_outer = t_len // tb

    def idx(b, c):
        return (b * n_outer + (n_outer - 1 - c if reverse else c), 0)

    spec = pl.BlockSpec((tb, HG_KEY_WIDTH), idx)
    head_buf = pltpu.VMEM((HG_HEADS, HG_CHUNK, HG_KEY_DIM), F32)
    return pl.pallas_call(
        functools.partial(_hgrn_scan_kernel, reverse=reverse, layer=layer),
        grid=(nb, n_outer),
        in_specs=[spec, spec, spec, spec, pl.BlockSpec(hg_lb.shape, lambda b, c: (0, 0))],
        out_specs=spec,
        out_shape=jax.ShapeDtypeStruct((n, HG_VAL_WIDTH), ACT),
        scratch_shapes=[pltpu.VMEM((HG_HEADS, HG_VAL_DIM, HG_KEY_DIM), F32)] + [head_buf] * 5,
        compiler_params=_params("parallel", "arbitrary"),
        name="hgrn_scan_bwd" if reverse else "hgrn_scan_fwd",
    )(q, lf, kg, i_val, hg_lb)


def _hgrn_out_kernel(x_ref, of_ref, ob_ref, g_ref, nw_ref, wo_ref, out_ref):
    o = of_ref[...].astype(F32) + ob_ref[...].astype(F32)
    mean_sq = _lane_group_sum(o * o, HG_VAL_DIM) * (1.0 / HG_VAL_DIM)
    o = o * lax.rsqrt(mean_sq + NORM_EPS) * nw_ref[...] * _silu(g_ref[...].astype(F32))
    out_ref[...] = x_ref[...] + _mm(o, wo_ref[...])


def _hgrn_out(x2d, o_f, o_b, g, norm_w, w_out):
    n = x2d.shape[0]
    tm = _tile(n, 1024)
    row = pl.BlockSpec((tm, D_MODEL), lambda i: (i, 0))
    return pl.pallas_call(
        _hgrn_out_kernel,
        grid=(n // tm,),
        in_specs=[row] * 4 + [pl.BlockSpec((1, HG_VAL_WIDTH), lambda i: (0, 0)),
                              pl.BlockSpec(w_out.shape, lambda i: (0, 0))],
        out_specs=row,
        out_shape=jax.ShapeDtypeStruct((n, D_MODEL), F32),
        compiler_params=_params("parallel"), name="hgrn_out",
    )(x2d, o_f, o_b, g, norm_w, w_out)


def _xattn_kernel(x_ref, g_ref, wq_ref, kv_ref, wo_ref, out_ref):
    x = x_ref[...]
    q = _mm(_rms_normed(x, g_ref[...]), wq_ref[...])
    scale = XA_HEAD_DIM ** -0.5
    scores = [_mm_nt(q[:, h * XA_HEAD_DIM:(h + 1) * XA_HEAD_DIM], kv_ref[:, h * XA_HEAD_DIM:(h + 1) * XA_HEAD_DIM])
              * scale for h in range(XA_HEADS)]
    probs = []
    for s in scores:
        p = jnp.exp(s - jnp.max(s, axis=-1, keepdims=True))
        probs.append(p * (1.0 / jnp.sum(p, axis=-1, keepdims=True)))
    heads = [_mm(p, kv_ref[:, D_MODEL + h * XA_HEAD_DIM:D_MODEL + (h + 1) * XA_HEAD_DIM])
             for h, p in enumerate(probs)]
    out_ref[...] = x + _mm(jnp.concatenate(heads, axis=-1), wo_ref[...])


def _xattn(x2d, gain, wq, kv, wo, nb, t_len, mem_len):
    n = x2d.shape[0]
    tm = _tile(t_len, 1024)
    per_seq = t_len // tm
    row = pl.BlockSpec((tm, D_MODEL), lambda b, j: (b * per_seq + j, 0))
    return pl.pallas_call(
        _xattn_kernel,
        grid=(nb, per_seq),
        in_specs=[row, pl.BlockSpec((1, D_MODEL), lambda b, j: (0, 0)),
                  pl.BlockSpec(wq.shape, lambda b, j: (0, 0)),
                  pl.BlockSpec((mem_len, 2 * D_MODEL), lambda b, j: (b, 0)),
                  pl.BlockSpec(wo.shape, lambda b, j: (0, 0))],
        out_specs=row,
        out_shape=jax.ShapeDtypeStruct((n, D_MODEL), F32),
        compiler_params=_params("parallel", "parallel"), name="xattn",
    )(x2d, gain.reshape(1, D_MODEL), wq, kv, wo)


def _ffn_kernel(x_ref, g_ref, wi_ref, wo_ref, fg_ref, out_ref, *, final_norm):
    x = x_ref[...]
    gu = _mm(_rms_normed(x, g_ref[...]), wi_ref[...])
    act = _silu(gu[:, :FFN_DIM]) * gu[:, FFN_DIM:]
    y = x + _mm(act, wo_ref[...])
    if final_norm:
        y = _rms_normed(y, fg_ref[...])
    out_ref[...] = y


def _ffn(x2d, gain, w_in, w_out, final_gain, final_norm):
    n = x2d.shape[0]
    tm = _tile(n, 512)
    row = pl.BlockSpec((tm, D_MODEL), lambda i: (i, 0))
    vec = pl.BlockSpec((1, D_MODEL), lambda i: (0, 0))
    resident = pl.Buffered(1)
    return pl.pallas_call(
        functools.partial(_ffn_kernel, final_norm=final_norm),
        grid=(n // tm,),
        in_specs=[row, vec, pl.BlockSpec(w_in.shape, lambda i: (0, 0), pipeline_mode=resident),
                  pl.BlockSpec(w_out.shape, lambda i: (0, 0), pipeline_mode=resident), vec],
        out_specs=row,
        out_shape=jax.ShapeDtypeStruct((n, D_MODEL), F32),
        compiler_params=_params("parallel"), name="ffn",
    )(x2d, gain.reshape(1, D_MODEL), w_in, w_out, final_gain.reshape(1, D_MODEL))


def _pad_lanes(a, width=LANES):
    return jnp.pad(a, [(0, 0)] * (a.ndim - 1) + [(0, width - a.shape[-1])])


def _rwkv_ssd_layer(x2d, nb, t_len, norm_g, w_in, w_out, rw_mu, rw_w0, rw_w2, rw_a0, rw_a2, rw_g2, rw_k_k, rw_k_a,
                    rw_r_k, rw_gn_w, rw_gn_b, conv_w, conv_b, dt_bias, a_log, d_skip, ssd_norm_w):
    o_lora = 3 * RW_DIM
    o_z = RW_PROJ
    o_xbc = o_z + SSD_DIM
    o_dt = o_xbc + SSD_CONV_DIM
    wb = w_in.astype(BF16)
    weights = [wb[:, :o_lora], wb[:, o_lora:o_z], wb[:, o_z:o_xbc], wb[:, o_xbc:o_dt], _pad_lanes(wb[:, o_dt:])]
    head_id = jnp.arange(LANES) // RW_HEAD_DIM
    blockdiag = (head_id[:, None] == head_id[None, :]).astype(BF16)
    row = lambda a: a.reshape(1, -1)
    rwkv_consts = (row(rw_mu[:o_lora]), row(rw_mu[o_lora:]), row(rw_a0), rw_a2.astype(BF16), rw_g2.astype(BF16),
                   row(rw_k_k), row(rw_k_a), rw_w0, rw_w2.astype(BF16), row(rw_r_k), blockdiag)
    ssd_consts = (conv_w, row(conv_b), _pad_lanes(dt_bias), _pad_lanes(a_log))
    (r, k, v, kk, kka, lw_f, lw_b, g, bonus, p_z, xbc, dt_f, dt_b, la_f, la_b) = _mix0_in(
        x2d, nb, t_len, norm_g, weights, rwkv_consts, ssd_consts)
    o_f = _rwkv_scan(r, k, v, kk, kka, lw_f, nb, t_len, reverse=False)
    o_b = _rwkv_scan(r, k, v, kk, kka, lw_b, nb, t_len, reverse=True)
    y_f = _ssd_scan(xbc, dt_f, la_f, nb, t_len, reverse=False)
    y_b = _ssd_scan(xbc, dt_b, la_b, nb, t_len, reverse=True)

    wo = w_out.astype(BF16)
    d_skip_lanes = jnp.repeat(d_skip, SSD_HEAD_DIM).reshape(1, SSD_DIM)
    return _mix0_out(x2d, o_f, o_b, bonus, g, y_f, y_b, xbc, p_z, row(rw_gn_w), row(rw_gn_b), blockdiag,
                     d_skip_lanes, row(ssd_norm_w), wo[:RW_DIM], wo[RW_DIM:])


def _hgrn_layer(x2d, nb, t_len, norm_g, w_in, w_out, norm_w, hg_lb, layer):
    kw, vw = HG_KEY_WIDTH, HG_VAL_WIDTH
    wb = w_in.astype(BF16)
    weights = [wb[:, :kw], wb[:, kw:2 * kw], wb[:, 2 * kw:3 * kw], wb[:, 3 * kw:3 * kw + vw], wb[:, 3 * kw + vw:]]
    q, lf_f, lf_b, kg_f, kg_b, i_val, g = _hgrn_in(x2d, norm_g, hg_lb, weights, layer)
    o_f = _hgrn_scan(q, lf_f, kg_f, i_val, hg_lb, nb, t_len, reverse=False, layer=layer)
    o_b = _hgrn_scan(q, lf_b, kg_b, i_val, hg_lb, nb, t_len, reverse=True, layer=layer)
    return _hgrn_out(x2d, o_f, o_b, g, norm_w.reshape(1, vw), w_out.astype(BF16))


def kernel(x, mem, mix_norm, ab_w_in, ab_w_out, rw_mu, rw_w0, rw_w2, rw_a0, rw_a2, rw_g2, rw_k_k, rw_k_a, rw_r_k, rw_gn_w, rw_gn_b, ssd_conv_w, ssd_conv_b, ssd_dt_bias, ssd_a_log, ssd_d, ssd_norm_w, hg_w_in, hg_w_out, hg_norm_w, hg_lb, xa_norm, mem_norm, xa_wq, xa_wkv, xa_wo, ffn_norm, ffn_w_in, ffn_w_out, final_norm):
    nb, t_len, d = x.shape
    mem_len = mem.shape[1]
    depth = mix_norm.shape[0]
    x2d = x.reshape(nb * t_len, d)
    mem2d = mem.reshape(nb * mem_len, d)
    for layer in range(depth):
        if layer % 2 == 0:
            e = layer // 2
            x2d = _rwkv_ssd_layer(x2d, nb, t_len, mix_norm[layer], ab_w_in[e], ab_w_out[e], rw_mu[e], rw_w0[e],
                                  rw_w2[e], rw_a0[e], rw_a2[e], rw_g2[e], rw_k_k[e], rw_k_a[e],
                                  rw_r_k[e].reshape(-1), rw_gn_w[e], rw_gn_b[e], ssd_conv_w[e], ssd_conv_b[e],
                                  ssd_dt_bias[e], ssd_a_log[e], ssd_d[e], ssd_norm_w[e])
        else:
            o = layer // 2
            x2d = _hgrn_layer(x2d, nb, t_len, mix_norm[layer], hg_w_in[o], hg_w_out[o], hg_norm_w[o], hg_lb, layer)
        (kv,) = _norm_matmul(mem2d, mem_norm[layer], [xa_wkv[layer].astype(BF16)], [BF16])
        x2d = _xattn(x2d, xa_norm[layer], xa_wq[layer].astype(BF16), kv, xa_wo[layer].astype(BF16),
                     nb, t_len, mem_len)
        x2d = _ffn(x2d, ffn_norm[layer], ffn_w_in[layer].astype(BF16), ffn_w_out[layer].astype(BF16),
                   final_norm, final_norm=(layer == depth - 1))
    return x2d.reshape(nb, t_len, d)
```

```python
import functools
import math

import jax
import jax.numpy as jnp
from jax import lax
from jax.experimental import pallas as pl
from jax.experimental.pallas import tpu as pltpu

F32 = jnp.float32
BF16 = jnp.bfloat16
ACT = BF16

D_MODEL = 1024
NORM_EPS = 1e-6

RW_HEAD_DIM = 64
RW_DIM = D_MODEL // 2
RW_HEADS = RW_DIM // RW_HEAD_DIM
RW_DECAY_RANK = 64
RW_A_RANK = 64
RW_GATE_RANK = 128
RW_GN_EPS = 64e-5
RW_LORA = RW_DECAY_RANK + RW_A_RANK + RW_GATE_RANK
RW_PROJ = 3 * RW_DIM + RW_LORA
RW_CHUNK = 64
RW_BLOCK_CHUNKS = 8
RW_HEAD_PAIRS = RW_HEADS // 2
assert RW_CHUNK == RW_HEAD_DIM and 2 * RW_HEAD_DIM == 128

SSD_HEAD_DIM = 64
SSD_DIM = D_MODEL // 2
SSD_HEADS = SSD_DIM // SSD_HEAD_DIM
SSD_GROUPS = 2
SSD_STATE = 128
SSD_CONV_DIM = SSD_DIM + 2 * SSD_GROUPS * SSD_STATE
SSD_CHUNK = 128
SSD_BLOCK_CHUNKS = 8
SSD_GROUP_HEADS = SSD_HEADS // SSD_GROUPS
assert SSD_HEAD_DIM == RW_HEAD_DIM and SSD_GROUP_HEADS % 2 == 0

HG_HEADS = 8
HG_KEY_DIM = 128
HG_VAL_DIM = D_MODEL // HG_HEADS
HG_KEY_WIDTH = HG_HEADS * HG_KEY_DIM
HG_VAL_WIDTH = HG_HEADS * HG_VAL_DIM
HG_CHUNK = 64
HG_SUB = 16
HG_BLOCK_CHUNKS = 8
HG_SAFE_LOG_RANGE = 64.0
assert HG_KEY_DIM == HG_VAL_DIM

XA_HEADS = 4
XA_HEAD_DIM = D_MODEL // XA_HEADS
FFN_DIM = ((8 * D_MODEL + 3 * 256 - 1) // (3 * 256)) * 256

SUBLANES = 8
HALO_ROWS = 16
SHIFT_ROWS = 128
PROJ_COLS = 256
LANES = 128
VMEM_LIMIT_BYTES = 56 * 1024 * 1024
SCAN_STEP_ROWS = 1024
NEG_BIG = -1e30


def _params(*semantics):
    return pltpu.CompilerParams(dimension_semantics=semantics, vmem_limit_bytes=VMEM_LIMIT_BYTES)


def _tile(n, pref):
    t = min(n, pref)
    while n % t or t % SUBLANES:
        t -= 1
    return t


def _scan_rows(t_len, sub_rows):
    assert t_len % sub_rows == 0
    rows = sub_rows
    while rows * 2 <= SCAN_STEP_ROWS and t_len % (rows * 2) == 0:
        rows *= 2
    return rows


def _mm(a, b):
    return jnp.dot(a.astype(BF16), b.astype(BF16), preferred_element_type=F32)


def _mm_nt(a, b):
    return lax.dot_general(a.astype(BF16), b.astype(BF16), (((1,), (1,)), ((), ())),
                           preferred_element_type=F32)


def _mm_tn(a, b):
    return lax.dot_general(a.astype(BF16), b.astype(BF16), (((0,), (0,)), ((), ())),
                           preferred_element_type=F32)


def _split3(x):
    hi = x.astype(BF16)
    r1 = x - hi.astype(F32)
    mid = r1.astype(BF16)
    lo = (r1 - mid.astype(F32)).astype(BF16)
    return hi, mid, lo


def _mm_exact_lhs(m_bf16, x, terms=3):
    dot = functools.partial(jnp.dot, preferred_element_type=F32)
    pieces = _split3(x)[:terms]
    out = dot(m_bf16, pieces[0])
    for piece in pieces[1:]:
        out = out + dot(m_bf16, piece)
    return out


def _mm_exact_rhs_many(xs, m_bf16, terms):
    rows = xs[0].shape[0]
    pieces = [piece for x, n_terms in zip(xs, terms) for piece in _split3(x)[:n_terms]]
    out = jnp.dot(jnp.concatenate(pieces, axis=0), m_bf16, preferred_element_type=F32)
    results, at = [], 0
    for n_terms in terms:
        total = out[at:at + rows]
        for t in range(1, n_terms):
            total = total + out[at + t * rows:at + (t + 1) * rows]
        results.append(total)
        at += n_terms * rows
    return results


def _rms_normed(x, gain):
    ms = jnp.mean(x * x, axis=-1, keepdims=True)
    return x * lax.rsqrt(ms + NORM_EPS) * gain


def _sigmoid(x):
    return 0.5 * jnp.tanh(0.5 * x) + 0.5


def _silu(x):
    h = 0.5 * x
    return h * jnp.tanh(h) + h


def _softplus(x):
    return jnp.maximum(x, 0.0) + jnp.log(1.0 + jnp.exp(-jnp.abs(x)))


def _order_masks(n, reverse):
    row = lax.broadcasted_iota(jnp.int32, (n, n), 0)
    col = lax.broadcasted_iota(jnp.int32, (n, n), 1)
    if reverse:
        return col >= row, col > row
    return col <= row, col < row


def _norm_matmul_kernel(x_ref, g_ref, *refs, n_out):
    xn = _rms_normed(x_ref[...], g_ref[...]).astype(BF16)
    for w_ref, o_ref in zip(refs[:n_out], refs[n_out:]):
        o_ref[...] = jnp.dot(xn, w_ref[...], preferred_element_type=F32).astype(o_ref.dtype)


def _norm_matmul(x2d, gain, weights, out_dtypes, tm_pref=512):
    n, d = x2d.shape
    tm = _tile(n, tm_pref)
    in_specs = [pl.BlockSpec((tm, d), lambda i: (i, 0)), pl.BlockSpec((1, d), lambda i: (0, 0))]
    in_specs += [pl.BlockSpec(w.shape, lambda i: (0, 0)) for w in weights]
    out_specs = [pl.BlockSpec((tm, w.shape[1]), lambda i: (i, 0)) for w in weights]
    out_shape = [jax.ShapeDtypeStruct((n, w.shape[1]), dt) for w, dt in zip(weights, out_dtypes)]
    return pl.pallas_call(
        functools.partial(_norm_matmul_kernel, n_out=len(weights)),
        grid=(n // tm,), in_specs=in_specs, out_specs=out_specs, out_shape=out_shape,
        compiler_params=_params("parallel"), name="norm_matmul",
    )(x2d, gain.reshape(1, d), *weights)


def _neighbours(p, prev_blk, next_blk, first, last):
    tt = p.shape[0]
    ridx = lax.broadcasted_iota(jnp.int32, p.shape, 0)
    prev_row = jnp.where(first, 0.0, prev_blk[HALO_ROWS - 1:HALO_ROWS, :].astype(F32))
    next_row = jnp.where(last, 0.0, next_blk[0:1, :].astype(F32))
    prev = jnp.where(ridx == 0, prev_row, pltpu.roll(p, 1, 0))
    nxt = jnp.where(ridx == tt - 1, next_row, pltpu.roll(p, tt - 1, 0))
    return prev, nxt


def _shift_matrix(tt, offsets, weight):
    t = jnp.arange(tt)[:, None]
    c = jnp.arange(tt + 2 * HALO_ROWS)[None, :]
    hit = functools.reduce(jnp.logical_or, [c == HALO_ROWS + t + d for d in offsets])
    return jnp.where(hit, weight, 0.0).astype(BF16)


def _lane_group_sum(x, width):
    assert width % LANES == 0 and x.shape[1] % width == 0
    ones = jnp.ones((LANES, LANES), BF16)
    groups = []
    for c in range(0, x.shape[1], width):
        total = _mm(x[:, c:c + LANES], ones)
        for c2 in range(c + LANES, c + width, LANES):
            total = total + _mm(x[:, c2:c2 + LANES], ones)
        groups += [total] * (width // LANES)
    return jnp.concatenate(groups, axis=1)


def _head_sum(x, pair_ones_bf16):
    blocks = [_mm(x[:, c:c + LANES], pair_ones_bf16) for c in range(0, x.shape[1], LANES)]
    return jnp.concatenate(blocks, axis=1)


def _mix0_project(x_ref, xp_ref, xn_ref, gain_ref, wrkv_ref, wlora_ref, wz_ref, wxbc_ref, wdt_ref, first, last,
                  prkv_scr, plora_scr, u_scr, z_scr, dt_scr):
    tt = x_ref.shape[0]
    dot = functools.partial(jnp.dot, preferred_element_type=F32)
    x_ext = jnp.concatenate([jnp.where(first, 0.0, xp_ref[...]), x_ref[...], jnp.where(last, 0.0, xn_ref[...])], axis=0)
    xn_ext = _rms_normed(x_ext, gain_ref[...]).astype(BF16)
    xn = xn_ext[HALO_ROWS:HALO_ROWS + tt]

    def column_blocks(lhs, w_ref, scr):
        block = min(PROJ_COLS, scr.shape[1])
        for c in range(0, scr.shape[1], block):
            scr[:, c:c + block] = dot(lhs, w_ref[:, c:c + block]).astype(scr.dtype)

    def project_rwkv():
        column_blocks(xn_ext, wrkv_ref, prkv_scr)
        column_blocks(xn_ext, wlora_ref, plora_scr)

    def project_ssd():
        column_blocks(xn_ext, wxbc_ref, u_scr)
        column_blocks(xn, wz_ref, z_scr)
        column_blocks(xn, wdt_ref, dt_scr)

    return project_rwkv, project_ssd


def _mix0_operands(project_ssd, prkv_scr, plora_scr, u_scr, z_scr, dt_scr,
                   savg_ref, mu_ref, mul_ref, a0_ref, a2_ref, g2_ref, kk_ref, ka_ref, w0_ref, w2_ref, rk_ref, bd_ref,
                   cw_ref, cb_ref, dtb_ref, alog_ref,
                   r_out, k_out, v_out, kkn_out, kka_out, lwf_out, lwb_out, g_out, bonus_out,
                   z_out, xbc_out, dtf_out, dtb_out, laf_out, lab_out):
    tt = z_scr.shape[0]
    dot = functools.partial(jnp.dot, preferred_element_type=F32)

    def token_shift(p_scr, m_ref, lo_col, hi_col):
        x = p_scr[HALO_ROWS:HALO_ROWS + tt, lo_col:hi_col].astype(F32)
        avg = jnp.concatenate(
            [dot(savg_ref[...], p_scr[t0:t0 + SHIFT_ROWS + 2 * HALO_ROWS, lo_col:hi_col])
             for t0 in range(0, tt, SHIFT_ROWS)], axis=0)
        return x + m_ref[:, lo_col:hi_col] * (avg - x)

    r = token_shift(prkv_scr, mu_ref, 0, RW_DIM)
    k = token_shift(prkv_scr, mu_ref, RW_DIM, 2 * RW_DIM)
    v = token_shift(prkv_scr, mu_ref, 2 * RW_DIM, 3 * RW_DIM)
    lo = token_shift(plora_scr, mul_ref, 0, RW_LORA)
    project_ssd()
    wd = lo[:, :RW_DECAY_RANK]
    ad = lo[:, RW_DECAY_RANK:RW_DECAY_RANK + RW_A_RANK]
    gd = lo[:, RW_DECAY_RANK + RW_A_RANK:]
    bd = bd_ref[...]

    a_logits = _mm(ad, a2_ref[...])
    g = _mm(_sigmoid(gd), g2_ref[...])
    kk = k * kk_ref[...]
    kk_sq = _head_sum(kk * kk, bd)
    w_lora = jnp.tanh(wd)
    decay_logits = [w0_ref[d:d + 1, :] + _mm(w_lora, w2_ref[d]) for d in range(2)]

    z_out[...] = z_scr[...]
    u_ext = u_scr[...]
    u = u_ext[HALO_ROWS:HALO_ROWS + tt].astype(F32)
    prev, nxt = _neighbours(u, u_ext[:HALO_ROWS], u_ext[HALO_ROWS + tt:], False, False)
    conv = prev * cw_ref[0:1, :] + u * cw_ref[1:2, :] + nxt * cw_ref[2:3, :] + cb_ref[...]
    xbc_out[...] = _silu(conv).astype(xbc_out.dtype)
    dt_raw = dt_scr[...]
    dt_f = _softplus(dt_raw + dtb_ref[0:1, :])
    dt_b = _softplus(dt_raw + dtb_ref[1:2, :])
    dtf_out[...] = dt_f
    dtb_out[...] = dt_b
    laf_out[...] = dt_f * (-jnp.exp(alog_ref[0:1, :]))
    lab_out[...] = dt_b * (-jnp.exp(alog_ref[1:2, :]))

    a = _sigmoid(a0_ref[...] + a_logits)
    kk = kk / jnp.maximum(jnp.sqrt(kk_sq), 1e-12)
    k = k * (1.0 + (a - 1.0) * ka_ref[...])
    r_out[...] = r.astype(r_out.dtype)
    k_out[...] = k.astype(k_out.dtype)
    v_out[...] = v.astype(v_out.dtype)
    kkn_out[...] = kk.astype(kkn_out.dtype)
    kka_out[...] = (kk * a).astype(kka_out.dtype)
    lwf_out[...] = -math.exp(-0.5) * _sigmoid(decay_logits[0])
    lwb_out[...] = -math.exp(-0.5) * _sigmoid(decay_logits[1])
    g_out[...] = g.astype(g_out.dtype)
    bonus_out[...] = (_head_sum(r * k * rk_ref[...], bd) * v).astype(bonus_out.dtype)


MIX0_N_PROJ_REFS = 9
MIX0_N_CONSTS = 16
MIX0_N_OUTS = 15
MIX0_N_STAGED = 5


def _mix0_in_kernel(*refs, per_seq):
    proj_refs = refs[:MIX0_N_PROJ_REFS]
    const_refs = refs[MIX0_N_PROJ_REFS:MIX0_N_PROJ_REFS + MIX0_N_CONSTS]
    out_refs = refs[MIX0_N_PROJ_REFS + MIX0_N_CONSTS:MIX0_N_PROJ_REFS + MIX0_N_CONSTS + MIX0_N_OUTS]
    staged = refs[MIX0_N_PROJ_REFS + MIX0_N_CONSTS + MIX0_N_OUTS:]
    pos = lax.rem(pl.program_id(0), per_seq)
    project_rwkv, project_ssd = _mix0_project(*proj_refs, pos == 0, pos == per_seq - 1, *staged)
    project_rwkv()
    _mix0_operands(project_ssd, *staged, *const_refs, *out_refs)


def _mix0_in(x2d, nb, t_len, gain, weights, rwkv_consts, ssd_consts):
    n, d = x2d.shape
    tt = _tile(t_len, 512)
    assert tt % HALO_ROWS == 0
    per_seq = t_len // tt
    n_tiles = n // tt
    blocks_per_tile = tt // HALO_ROWS
    n_blocks = n // HALO_ROWS

    def full(a):
        nd = a.ndim
        return pl.BlockSpec(a.shape, lambda s: (0,) * nd)

    x_specs = [pl.BlockSpec((tt, d), lambda s: (s, 0)),
               pl.BlockSpec((HALO_ROWS, d), lambda s: (jnp.maximum(s * blocks_per_tile - 1, 0), 0)),
               pl.BlockSpec((HALO_ROWS, d), lambda s: (jnp.minimum((s + 1) * blocks_per_tile, n_blocks - 1), 0))]
    proj_consts = (gain.reshape(1, d),) + tuple(weights)
    assert tt % SHIFT_ROWS == 0
    consts = (_shift_matrix(SHIFT_ROWS, (-1, 1), 0.5),) + tuple(rwkv_consts) + tuple(ssd_consts)
    assert 3 + len(proj_consts) == MIX0_N_PROJ_REFS and len(consts) == MIX0_N_CONSTS
    wide = [(RW_DIM, ACT)] * 5 + [(RW_DIM, F32)] * 2 + [(RW_DIM, ACT)] * 2 + [(SSD_DIM, ACT), (SSD_CONV_DIM, ACT)] \
        + [(LANES, F32)] * 4
    assert len(wide) == MIX0_N_OUTS
    ext = tt + 2 * HALO_ROWS
    staged = [pltpu.VMEM((ext, 3 * RW_DIM), ACT), pltpu.VMEM((ext, RW_LORA), ACT), pltpu.VMEM((ext, SSD_CONV_DIM), ACT),
              pltpu.VMEM((tt, SSD_DIM), ACT), pltpu.VMEM((tt, LANES), F32)]
    assert len(staged) == MIX0_N_STAGED
    return pl.pallas_call(
        functools.partial(_mix0_in_kernel, per_seq=per_seq),
        grid=(n_tiles,),
        in_specs=x_specs + [full(a) for a in proj_consts + consts],
        out_specs=[pl.BlockSpec((tt, w), lambda s: (s, 0)) for w, _ in wide],
        out_shape=[jax.ShapeDtypeStruct((n, w), dt) for w, dt in wide],
        scratch_shapes=staged,
        compiler_params=_params("parallel"), name="mix0_in",
    )(x2d, x2d, x2d, *proj_consts, *consts)


def _pair_blockdiag(y):
    lane = lax.broadcasted_iota(jnp.int32, y.shape, 1)
    top = jnp.where(lane < RW_HEAD_DIM, y, 0.0).astype(BF16)
    bottom = jnp.where(lane >= RW_HEAD_DIM, y, 0.0).astype(BF16)
    return jnp.concatenate([top, bottom], axis=0)


def _pair_diag_blocks(full):
    lane = lax.broadcasted_iota(jnp.int32, (RW_HEAD_DIM, LANES), 1)
    return jnp.where(lane < RW_HEAD_DIM, full[:RW_HEAD_DIM], full[RW_HEAD_DIM:])


def _unit_lower_inverses(n_strict_list, eye):
    size = n_strict_list[0].shape[0]
    levels = int(math.log2(size)) - 1
    accs = [eye + n for n in n_strict_list]
    powers = [_mm(n, _pair_blockdiag(n)) for n in n_strict_list]
    for level in range(levels):
        power_bds = [_pair_blockdiag(p) for p in powers]
        if level == levels - 1:
            return [a + _mm(a, b) for a, b in zip(accs, power_bds)]
        both = [_mm(jnp.concatenate([a, p], axis=0), b) for a, p, b in zip(accs, powers, power_bds)]
        accs = [a + ab[:size] for a, ab in zip(accs, both)]
        powers = [ab[size:] for ab in both]


def _rwkv_scan_kernel(r_ref, k_ref, v_ref, kk_ref, kka_ref, lw_ref, o_ref, h_ref, *, reverse):
    @pl.when(pl.program_id(1) == 0)
    def _():
        h_ref[...] = jnp.zeros_like(h_ref)

    sub_rows = RW_BLOCK_CHUNKS * RW_CHUNK
    n_sub = r_ref.shape[0] // sub_rows

    def body(i, carry):
        blk = (n_sub - 1 - i) if reverse else i
        _rwkv_scan_block(r_ref, k_ref, v_ref, kk_ref, kka_ref, lw_ref, o_ref, h_ref,
                         pl.multiple_of(blk * sub_rows, sub_rows), reverse)
        return carry

    lax.fori_loop(0, n_sub, body, 0)


def _rwkv_scan_block(r_ref, k_ref, v_ref, kk_ref, kka_ref, lw_ref, o_ref, h_ref, base, reverse):
    chunk = RW_CHUNK
    n_chunks = RW_BLOCK_CHUNKS
    incl_bf = _order_masks(chunk, reverse)[0].astype(BF16)
    row = lax.broadcasted_iota(jnp.int32, (chunk, LANES), 0)
    pos = lax.broadcasted_iota(jnp.int32, (chunk, LANES), 1) % RW_HEAD_DIM
    incl, strict = (pos >= row, pos > row) if reverse else (pos <= row, pos < row)
    eye = (pos == row).astype(F32)
    end = 0 if reverse else chunk - 1
    chunk_order = list(range(n_chunks - 1, -1, -1)) if reverse else list(range(n_chunks))

    items = []
    for ci in chunk_order:
        rows = pl.ds(base + ci * chunk, chunk)
        lw = lw_ref[rows, :]
        cum = _mm_exact_lhs(incl_bf, lw, terms=2)
        cum_end = cum[end:end + 1, :]
        e_neg = jnp.exp(-cum)
        p_end = jnp.exp(cum_end)
        e_tail = p_end * e_neg
        k = k_ref[rows, :].astype(F32)
        kka = kka_ref[rows, :].astype(F32)
        at_all = -kk_ref[rows, :].astype(F32) * jnp.exp(cum - lw)
        bt_all = kka * e_neg
        kt_all = k * e_neg
        rt_all = r_ref[rows, :].astype(F32) * jnp.exp(cum)
        bh_all = kka * e_tail
        kh_all = k * e_tail
        v = v_ref[rows, :].astype(F32)
        for j in range(RW_HEAD_PAIRS):
            sl = slice(j * LANES, (j + 1) * LANES)
            items.append(dict(ci=ci, j=j, at=at_all[:, sl], bt=bt_all[:, sl], kt=kt_all[:, sl], rt=rt_all[:, sl],
                              bh=bh_all[:, sl], kh=kh_all[:, sl], v=v[:, sl], p_end=p_end[:, sl]))

    for it in items:
        lhs = jnp.concatenate([it["at"], it["rt"]], axis=0)
        both = _mm_nt(lhs, jnp.concatenate([_pair_blockdiag(it["bt"]), _pair_blockdiag(it["kt"])], axis=0))
        it["a_ab"] = jnp.where(strict, both[:chunk, :LANES], 0.0)
        it["a_rb"] = jnp.where(incl, both[chunk:, :LANES], 0.0)
        it["a_ak"] = jnp.where(strict, both[:chunk, LANES:], 0.0)
        it["a_rk"] = jnp.where(incl, both[chunk:, LANES:], 0.0)
    t_invs = _unit_lower_inverses([it["a_ab"] for it in items], eye)
    for it in items:
        akv_rkv = _mm(jnp.concatenate([it["a_ak"], it["a_rk"]], axis=0), _pair_blockdiag(it["v"]))
        it["x1"] = akv_rkv[:chunk]
        it["rk_v"] = akv_rkv[chunk:]
    for it, t_inv in zip(items, t_invs):
        wu_u0 = _mm(t_inv, jnp.concatenate([_pair_blockdiag(it["at"]), _pair_blockdiag(it["x1"])], axis=1))
        it["w_u"] = wu_u0[:, :LANES]
        it["u0"] = wu_u0[:, LANES:]
    for it in items:
        rb_wu_u0 = _mm(it["a_rb"],
                       jnp.concatenate([_pair_blockdiag(it["w_u"]), _pair_blockdiag(it["u0"])], axis=1))
        w_o = it["rt"] + rb_wu_u0[:, :LANES]
        g_mat = eye * it["p_end"] + _pair_diag_blocks(_mm_tn(it["bh"], it["w_u"]))
        it["wo_g"] = jnp.concatenate([w_o, g_mat], axis=0)
        it["o0"] = rb_wu_u0[:, LANES:] + it["rk_v"]
        it["h0"] = _pair_diag_blocks(_mm_tn(jnp.concatenate([it["bh"], it["kh"]], axis=0),
                                            jnp.concatenate([it["u0"], it["v"]], axis=0)))

    states = [h_ref[j] for j in range(RW_HEAD_PAIRS)]
    for it in items:
        j = it["j"]
        res = _mm(it["wo_g"], _pair_blockdiag(states[j]))
        rows = pl.ds(base + it["ci"] * chunk, chunk)
        o_ref[rows, j * LANES:(j + 1) * LANES] = (res[:chunk] + it["o0"]).astype(o_ref.dtype)
        states[j] = res[chunk:] + it["h0"]
    for j in range(RW_HEAD_PAIRS):
        h_ref[j] = states[j]


def _rwkv_scan(r, k, v, kk, kka, lw, nb, t_len, reverse):
    n = r.shape[0]
    tb = _scan_rows(t_len, RW_BLOCK_CHUNKS * RW_CHUNK)
    nc = t_len // tb

    def idx(b, c):
        return (b * nc + (nc - 1 - c if reverse else c), 0)

    spec = pl.BlockSpec((tb, RW_DIM), idx)
    return pl.pallas_call(
        functools.partial(_rwkv_scan_kernel, reverse=reverse),
        grid=(nb, nc), in_specs=[spec] * 6, out_specs=spec,
        out_shape=jax.ShapeDtypeStruct((n, RW_DIM), ACT),
        scratch_shapes=[pltpu.VMEM((RW_HEAD_PAIRS, RW_HEAD_DIM, LANES), F32)],
        compiler_params=_params("parallel", "arbitrary"), name="rwkv_scan_bwd" if reverse else "rwkv_scan_fwd",
    )(r, k, v, kk, kka, lw)


def _rwkv_scan_both_kernel(rf, kf, vf, kkf, kkaf, lwf, rb, kb, vb, kkb, kkab, lwb, of_ref, ob_ref, hf_ref, hb_ref):
    @pl.when(pl.program_id(1) == 0)
    def _():
        hf_ref[...] = jnp.zeros_like(hf_ref)
        hb_ref[...] = jnp.zeros_like(hb_ref)

    sub_rows = RW_BLOCK_CHUNKS * RW_CHUNK
    n_sub = rf.shape[0] // sub_rows

    def body(i, carry):
        _rwkv_scan_block(rf, kf, vf, kkf, kkaf, lwf, of_ref, hf_ref, pl.multiple_of(i * sub_rows, sub_rows), False)
        _rwkv_scan_block(rb, kb, vb, kkb, kkab, lwb, ob_ref, hb_ref,
                         pl.multiple_of((n_sub - 1 - i) * sub_rows, sub_rows), True)
        return carry

    lax.fori_loop(0, n_sub, body, 0)


def _rwkv_scan_both(r, k, v, kk, kka, lw_f, lw_b, nb, t_len):
    n = r.shape[0]
    tb = _scan_rows(t_len, RW_BLOCK_CHUNKS * RW_CHUNK)
    nc = t_len // tb
    fwd = pl.BlockSpec((tb, RW_DIM), lambda b, c: (b * nc + c, 0))
    bwd = pl.BlockSpec((tb, RW_DIM), lambda b, c: (b * nc + nc - 1 - c, 0))
    state = pltpu.VMEM((RW_HEAD_PAIRS, RW_HEAD_DIM, LANES), F32)
    return pl.pallas_call(
        _rwkv_scan_both_kernel,
        grid=(nb, nc), in_specs=[fwd] * 6 + [bwd] * 6, out_specs=[fwd, bwd],
        out_shape=[jax.ShapeDtypeStruct((n, RW_DIM), ACT)] * 2,
        scratch_shapes=[state, state],
        compiler_params=_params("parallel", "arbitrary"), name="rwkv_scan_both",
    )(r, k, v, kk, kka, lw_f, r, k, v, kk, kka, lw_b)


def _ssd_scan_kernel(xbc_ref, dt_ref, la_ref, ex64_ref, y_ref, s_ref, *, reverse):
    @pl.when(pl.program_id(1) == 0)
    def _():
        s_ref[...] = jnp.zeros_like(s_ref)

    sub_rows = SSD_BLOCK_CHUNKS * SSD_CHUNK
    n_sub = xbc_ref.shape[0] // sub_rows

    def body(i, carry):
        blk = (n_sub - 1 - i) if reverse else i
        _ssd_scan_block(xbc_ref, dt_ref, la_ref, ex64_ref, y_ref, s_ref,
                        pl.multiple_of(blk * sub_rows, sub_rows), reverse)
        return carry

    lax.fori_loop(0, n_sub, body, 0)


def _ssd_scan_block(xbc_ref, dt_ref, la_ref, ex64_ref, y_ref, s_ref, base, reverse):
    chunk = SSD_CHUNK
    n_chunks = SSD_BLOCK_CHUNKS
    incl, _ = _order_masks(chunk, reverse)
    incl_bf = incl.astype(BF16)
    end = 0 if reverse else chunk - 1
    gn = SSD_GROUPS * SSD_STATE
    gw = SSD_GROUP_HEADS * SSD_HEAD_DIM
    chunk_order = list(range(n_chunks - 1, -1, -1)) if reverse else list(range(n_chunks))
    ex64 = ex64_ref[...]

    pre = []
    for ci in chunk_order:
        rows = pl.ds(base + ci * chunk, chunk)
        la = la_ref[rows, :]
        cum = _mm_exact_lhs(incl_bf, la, terms=2)
        cum_rows = cum.T
        cum_end = cum[end:end + 1, :]
        ecum64, etail64, dt64 = _mm_exact_rhs_many(
            [jnp.exp(cum), jnp.exp(cum_end - cum), dt_ref[rows, :]], ex64, terms=(2, 1, 1))
        xdt = xbc_ref[rows, :SSD_DIM].astype(F32) * dt64
        pre.append(dict(rows=rows, cum=cum, cum_rows=cum_rows, ecum64=ecum64, eend64=ecum64[end:end + 1, :],
                        xdt=xdt, xtail=xdt * etail64,
                        bm=[xbc_ref[rows, SSD_DIM + g * SSD_STATE:SSD_DIM + (g + 1) * SSD_STATE]
                            for g in range(SSD_GROUPS)],
                        cm=[xbc_ref[rows, SSD_DIM + gn + g * SSD_STATE:SSD_DIM + gn + (g + 1) * SSD_STATE]
                            for g in range(SSD_GROUPS)]))
    for p in pre:
        p["scores"] = [_mm_nt(p["cm"][g], p["bm"][g]) for g in range(SSD_GROUPS)]
        p["inject"] = [_mm_tn(p["bm"][g], p["xtail"][:, g * gw:(g + 1) * gw]) for g in range(SSD_GROUPS)]
    for p in pre:
        weights = []
        for h in range(SSD_HEADS):
            diff = p["cum"][:, h:h + 1] - p["cum_rows"][h:h + 1, :]
            decay = jnp.exp(jnp.where(incl, diff, NEG_BIG))
            weights.append(p["scores"][h // SSD_GROUP_HEADS] * decay)
        p["ydiag"] = [_mm(jnp.concatenate([weights[h], weights[h + 1]], axis=1),
                          _pair_blockdiag(p["xdt"][:, h * SSD_HEAD_DIM:(h + 2) * SSD_HEAD_DIM]))
                      for h in range(0, SSD_HEADS, 2)]

    states = [s_ref[g] for g in range(SSD_GROUPS)]
    for p in pre:
        for g in range(SSD_GROUPS):
            y_off = _mm(p["cm"][g], states[g]) * p["ecum64"][:, g * gw:(g + 1) * gw]
            for jp in range(SSD_GROUP_HEADS // 2):
                pair = g * (SSD_GROUP_HEADS // 2) + jp
                y_ref[p["rows"], pair * LANES:(pair + 1) * LANES] = (
                    p["ydiag"][pair] + y_off[:, jp * LANES:(jp + 1) * LANES]).astype(y_ref.dtype)
            states[g] = states[g] * p["eend64"][:, g * gw:(g + 1) * gw] + p["inject"][g]
    for g in range(SSD_GROUPS):
        s_ref[g] = states[g]


def _ssd_scan(xbc, dt, la, nb, t_len, reverse):
    n = xbc.shape[0]
    tb = _scan_rows(t_len, SSD_BLOCK_CHUNKS * SSD_CHUNK)
    assert SSD_CHUNK == LANES
    nc = t_len // tb
    lane_head = jnp.arange(LANES)[:, None]
    ex64 = (lane_head == jnp.arange(SSD_DIM)[None, :] // SSD_HEAD_DIM).astype(BF16)

    def idx(b, c):
        return (b * nc + (nc - 1 - c if reverse else c), 0)

    def full(a):
        return pl.BlockSpec(a.shape, lambda b, c: (0, 0))

    return pl.pallas_call(
        functools.partial(_ssd_scan_kernel, reverse=reverse),
        grid=(nb, nc),
        in_specs=[pl.BlockSpec((tb, SSD_CONV_DIM), idx), pl.BlockSpec((tb, LANES), idx),
                  pl.BlockSpec((tb, LANES), idx), full(ex64)],
        out_specs=pl.BlockSpec((tb, SSD_DIM), idx),
        out_shape=jax.ShapeDtypeStruct((n, SSD_DIM), ACT),
        scratch_shapes=[pltpu.VMEM((SSD_GROUPS, SSD_STATE, SSD_GROUP_HEADS * SSD_HEAD_DIM), F32)],
        compiler_params=_params("parallel", "arbitrary"), name="ssd_scan_bwd" if reverse else "ssd_scan_fwd",
    )(xbc, dt, la, ex64)


def _mix0_out_kernel(x_ref, of_ref, ob_ref, bonus_ref, g_ref, yf_ref, yb_ref, xbc_ref, z_ref,
                     gnw_ref, gnb_ref, bd_ref, dsk_ref, nw_ref, wo_rw_ref, wo_ssd_ref, out_ref):
    bd = bd_ref[...]
    o = of_ref[...].astype(F32) + ob_ref[...].astype(F32)
    inv_n = 1.0 / RW_HEAD_DIM
    mu = _head_sum(o, bd) * inv_n
    oc = o - mu
    var = _head_sum(oc * oc, bd) * inv_n
    o = oc * lax.rsqrt(var + RW_GN_EPS) * gnw_ref[...] + gnb_ref[...]
    o_rw = (o + bonus_ref[...].astype(F32)) * g_ref[...].astype(F32)

    y = yf_ref[...].astype(F32) + yb_ref[...].astype(F32) + dsk_ref[...] * xbc_ref[...].astype(F32)
    y = y * _silu(z_ref[...].astype(F32))
    gw = SSD_DIM // SSD_GROUPS
    parts = []
    for g in range(SSD_GROUPS):
        yg = y[:, g * gw:(g + 1) * gw]
        parts.append(yg * lax.rsqrt(jnp.mean(yg * yg, axis=-1, keepdims=True) + NORM_EPS))
    o_ssd = jnp.concatenate(parts, axis=-1) * nw_ref[...]
    out_ref[...] = x_ref[...] + _mm(o_rw, wo_rw_ref[...]) + _mm(o_ssd, wo_ssd_ref[...])


def _mix0_out(x2d, o_f, o_b, bonus, g, y_f, y_b, xbc, z, gn_w, gn_b, blockdiag, d_skip, norm_w, wo_rw, wo_ssd):
    n = x2d.shape[0]
    tm = _tile(n, 1024)

    def row(width):
        return pl.BlockSpec((tm, width), lambda i: (i, 0))

    def full(a):
        return pl.BlockSpec(a.shape, lambda i: (0, 0))

    consts = (gn_w, gn_b, blockdiag, d_skip, norm_w, wo_rw, wo_ssd)
    return pl.pallas_call(
        _mix0_out_kernel,
        grid=(n // tm,),
        in_specs=[row(D_MODEL)] + [row(RW_DIM)] * 6 + [row(SSD_DIM), row(SSD_DIM)] + [full(a) for a in consts],
        out_specs=row(D_MODEL),
        out_shape=jax.ShapeDtypeStruct((n, D_MODEL), F32),
        compiler_params=_params("parallel"), name="mix0_out",
    )(x2d, o_f, o_b, bonus, g, y_f, y_b, xbc, z, *consts)


def _hgrn_lower_bound(lb_ref, layer):
    x = lb_ref[...]
    m = jnp.max(x, axis=0, keepdims=True)
    e = jnp.exp(x - m)
    s = e / jnp.sum(e, axis=0, keepdims=True)
    lb = jnp.zeros_like(s[0:1, :])
    for i in range(1, layer + 1):
        lb = lb + s[i:i + 1, :]
    return lb


def _hgrn_in_kernel(x_ref, g_ref, lb_ref, wq_ref, wff_ref, wfb_ref, wi_ref, wg_ref,
                    q_out, lff_out, lfb_out, kgf_out, kgb_out, i_out, g_out, *, layer):
    xn = _rms_normed(x_ref[...], g_ref[...]).astype(BF16)
    lb = _hgrn_lower_bound(lb_ref, layer)
    dot = functools.partial(jnp.dot, preferred_element_type=F32)
    width = x_ref.shape[1]
    jobs = []
    for c in range(0, width, PROJ_COLS):
        cols = slice(c, c + PROJ_COLS)
        jobs.append((wff_ref, cols, ("gate", lff_out, kgf_out)))
        jobs.append((wq_ref, cols, ("plain", q_out)))
        jobs.append((wfb_ref, cols, ("gate", lfb_out, kgb_out)))
        jobs.append((wi_ref, cols, ("plain", i_out)))
        jobs.append((wg_ref, cols, ("plain", g_out)))

    def finish(acc, cols, how):
        if how[0] == "gate":
            f = lb[:, cols] + (1.0 - lb[:, cols]) * _sigmoid(acc)
            how[1][:, cols] = jnp.log(f)
            how[2][:, cols] = (1.0 - f).astype(how[2].dtype)
        else:
            how[1][:, cols] = acc.astype(how[1].dtype)

    pending = None
    for w_ref, cols, how in jobs:
        acc = dot(xn, w_ref[:, cols])
        if pending is not None:
            finish(*pending)
        pending = (acc, cols, how)
    finish(*pending)


def _hgrn_in(x2d, gain, hg_lb, weights, layer):
    n, d = x2d.shape
    tm = _tile(n, 512)
    row = pl.BlockSpec((tm, d), lambda i: (i, 0))
    in_specs = [row, pl.BlockSpec((1, d), lambda i: (0, 0)), pl.BlockSpec(hg_lb.shape, lambda i: (0, 0))]
    in_specs += [pl.BlockSpec(w.shape, lambda i: (0, 0)) for w in weights]
    dtypes = (ACT, F32, F32, ACT, ACT, ACT, ACT)
    return pl.pallas_call(
        functools.partial(_hgrn_in_kernel, layer=layer),
        grid=(n // tm,), in_specs=in_specs, out_specs=[row] * len(dtypes),
        out_shape=[jax.ShapeDtypeStruct((n, d), dt) for dt in dtypes],
        compiler_params=_params("parallel"), name="hgrn_in",
    )(x2d, gain.reshape(1, d), hg_lb, *weights)


def _hgrn_chunk(q, lf, kg, vv, state_t, reverse):
    chunk = q.shape[0]
    n_sub = chunk // HG_SUB
    incl, _ = _order_masks(chunk, reverse)
    end = 0 if reverse else chunk - 1
    s_idx = lax.broadcasted_iota(jnp.int32, (HG_SUB, 1), 0)

    bcum = _mm_exact_lhs(incl.astype(BF16), lf)
    b_end = bcum[end:end + 1, :]
    carried = _mm_nt(q * jnp.exp(bcum), state_t)

    order = list(range(n_sub - 1, -1, -1)) if reverse else list(range(n_sub))
    outs = [None] * n_sub
    for pos, sb in enumerate(order):
        rows = slice(sb * HG_SUB, (sb + 1) * HG_SUB)
        q_s, k_s, b_s, v_s = q[rows], kg[rows], bcum[rows], vv[rows]
        acc = carried[rows]
        if pos > 0:
            prev_sb = order[pos - 1]
            edge = prev_sb * HG_SUB if reverse else prev_sb * HG_SUB + HG_SUB - 1
            b_edge = bcum[edge:edge + 1, :]
            src = slice((sb + 1) * HG_SUB, chunk) if reverse else slice(0, sb * HG_SUB)
            q_hat = q_s * jnp.exp(b_s - b_edge)
            k_hat = kg[src] * jnp.exp(b_edge - bcum[src])
            acc = acc + _mm(_mm_nt(q_hat, k_hat), vv[src])
        rows_out = []
        for l in range(HG_SUB):
            w = jnp.exp(jnp.minimum(b_s[l:l + 1, :] - b_s, 0.0)) * k_s * q_s[l:l + 1, :]
            att = jnp.sum(w, axis=-1, keepdims=True)
            att = jnp.where((s_idx >= l) if reverse else (s_idx <= l), att, 0.0)
            rows_out.append(jnp.sum(att * v_s, axis=0, keepdims=True))
        outs[sb] = acc + jnp.concatenate(rows_out, axis=0)

    k_tail = kg * jnp.exp(b_end - bcum)
    new_state_t = state_t * jnp.exp(b_end) + _mm_tn(vv, k_tail)
    return jnp.concatenate(outs, axis=0), new_state_t


def _hgrn_scan_kernel(q_ref, lf_ref, kg_ref, i_ref, lb_ref, o_ref, s_ref, hq_ref, hf_ref, hk_ref, hv_ref, ho_ref, *,
                      reverse, layer):
    @pl.when(pl.program_id(1) == 0)
    def _():
        s_ref[...] = jnp.zeros_like(s_ref)

    sub_rows = HG_BLOCK_CHUNKS * HG_CHUNK
    n_sub = q_ref.shape[0] // sub_rows

    def body(i, carry):
        blk = (n_sub - 1 - i) if reverse else i
        _hgrn_scan_block(q_ref, lf_ref, kg_ref, i_ref, lb_ref, o_ref, s_ref, hq_ref, hf_ref, hk_ref, hv_ref, ho_ref,
                         pl.multiple_of(blk * sub_rows, sub_rows), reverse, layer)
        return carry

    lax.fori_loop(0, n_sub, body, 0)


def _hgrn_scan_block(q_ref, lf_ref, kg_ref, i_ref, lb_ref, o_ref, s_ref, hq_ref, hf_ref, hk_ref, hv_ref, ho_ref,
                     base, reverse, layer):
    chunk = HG_CHUNK
    n_chunks = HG_BLOCK_CHUNKS
    incl, _ = _order_masks(chunk, reverse)
    incl_bf = incl.astype(BF16)
    end = 0 if reverse else chunk - 1
    chunk_order = list(range(n_chunks - 1, -1, -1)) if reverse else list(range(n_chunks))

    def head_cols(h):
        return slice(h * HG_KEY_DIM, (h + 1) * HG_KEY_DIM)

    pre = []
    worst = None
    for ci in chunk_order:
        rows = pl.ds(base + ci * chunk, chunk)
        bcum = _mm_exact_lhs(incl_bf, lf_ref[rows, :], terms=2)
        b_end = bcum[end:end + 1, :]
        worst = b_end if worst is None else jnp.minimum(worst, b_end)
        pre.append((ci, rows, bcum, b_end))
    lb_min = jnp.min(_hgrn_lower_bound(lb_ref, layer))
    flag = lax.cond(lb_min >= math.exp(-HG_SAFE_LOG_RANGE / HG_CHUNK),
                    lambda: jnp.int32(1),
                    lambda: (jnp.min(worst) >= -HG_SAFE_LOG_RANGE).astype(jnp.int32))
    safe = flag == 1

    @pl.when(safe)
    def _():
        items = []
        for ci, rows, bcum, b_end in pre:
            kg = kg_ref[rows, :].astype(F32)
            e_neg = jnp.exp(-bcum)
            e_end = jnp.exp(b_end)
            qt_all = q_ref[rows, :].astype(F32) * jnp.exp(bcum)
            kt_all = kg * e_neg
            ktail_all = kt_all * e_end
            v_all = i_ref[rows, :].astype(F32)
            for h in range(HG_HEADS):
                c = head_cols(h)
                items.append(dict(rows=rows, h=h, qt=qt_all[:, c], kt=kt_all[:, c], ktail=ktail_all[:, c],
                                  v=v_all[:, c], e_end=e_end[:, c]))
        for it in items:
            it["attn"] = jnp.where(incl, _mm_nt(it["qt"], it["kt"]), 0.0)
        for it in items:
            it["local"] = _mm(it["attn"], it["v"])
            it["inject"] = _mm_tn(it["v"], it["ktail"])
        states = [s_ref[h] for h in range(HG_HEADS)]
        for it in items:
            h = it["h"]
            o_ref[it["rows"], head_cols(h)] = (it["local"] + _mm_nt(it["qt"], states[h])).astype(o_ref.dtype)
            states[h] = states[h] * it["e_end"] + it["inject"]
        for h in range(HG_HEADS):
            s_ref[h] = states[h]

    @pl.when(jnp.logical_not(safe))
    def _():
        def head_body(h, carry):
            out, new_state = _hgrn_chunk(hq_ref[h], hf_ref[h], hk_ref[h], hv_ref[h], s_ref[h], reverse)
            ho_ref[h] = out
            s_ref[h] = new_state
            return carry

        def chunk_body(i, carry):
            ci = (n_chunks - 1 - i) if reverse else i
            rows = pl.ds(pl.multiple_of(base + ci * chunk, chunk), chunk)
            for h in range(HG_HEADS):
                hq_ref[h] = q_ref[rows, head_cols(h)].astype(F32)
                hf_ref[h] = lf_ref[rows, head_cols(h)]
                hk_ref[h] = kg_ref[rows, head_cols(h)].astype(F32)
                hv_ref[h] = i_ref[rows, head_cols(h)].astype(F32)
            lax.fori_loop(0, HG_HEADS, head_body, 0)
            for h in range(HG_HEADS):
                o_ref[rows, head_cols(h)] = ho_ref[h].astype(o_ref.dtype)
            return carry

        lax.fori_loop(0, n_chunks, chunk_body, 0)


def _hgrn_scan(q, lf, kg, i_val, hg_lb, nb, t_len, reverse, layer):
    n = q.shape[0]
    tb = _scan_rows(t_len, HG_BLOCK_CHUNKS * HG_CHUNK)
    n_outer = t_len // tb

    def idx(b, c):
        return (b * n_outer + (n_outer - 1 - c if reverse else c), 0)

    spec = pl.BlockSpec((tb, HG_KEY_WIDTH), idx)
    head_buf = pltpu.VMEM((HG_HEADS, HG_CHUNK, HG_KEY_DIM), F32)
    return pl.pallas_call(
        functools.partial(_hgrn_scan_kernel, reverse=reverse, layer=layer),
        grid=(nb, n_outer),
        in_specs=[spec, spec, spec, spec, pl.BlockSpec(hg_lb.shape, lambda b, c: (0, 0))],
        out_specs=spec,
        out_shape=jax.ShapeDtypeStruct((n, HG_VAL_WIDTH), ACT),
        scratch_shapes=[pltpu.VMEM((HG_HEADS, HG_VAL_DIM, HG_KEY_DIM), F32)] + [head_buf] * 5,
        compiler_params=_params("parallel", "arbitrary"),
        name="hgrn_scan_bwd" if reverse else "hgrn_scan_fwd",
    )(q, lf, kg, i_val, hg_lb)


def _hgrn_out_kernel(x_ref, of_ref, ob_ref, g_ref, nw_ref, wo_ref, out_ref):
    o = of_ref[...].astype(F32) + ob_ref[...].astype(F32)
    mean_sq = _lane_group_sum(o * o, HG_VAL_DIM) * (1.0 / HG_VAL_DIM)
    o = o * lax.rsqrt(mean_sq + NORM_EPS) * nw_ref[...] * _silu(g_ref[...].astype(F32))
    out_ref[...] = x_ref[...] + _mm(o, wo_ref[...])


def _hgrn_out(x2d, o_f, o_b, g, norm_w, w_out):
    n = x2d.shape[0]
    tm = _tile(n, 1024)
    row = pl.BlockSpec((tm, D_MODEL), lambda i: (i, 0))
    return pl.pallas_call(
        _hgrn_out_kernel,
        grid=(n // tm,),
        in_specs=[row] * 4 + [pl.BlockSpec((1, HG_VAL_WIDTH), lambda i: (0, 0)),
                              pl.BlockSpec(w_out.shape, lambda i: (0, 0))],
        out_specs=row,
        out_shape=jax.ShapeDtypeStruct((n, D_MODEL), F32),
        compiler_params=_params("parallel"), name="hgrn_out",
    )(x2d, o_f, o_b, g, norm_w, w_out)


def _xattn_kernel(x_ref, g_ref, wq_ref, kv_ref, wo_ref, out_ref):
    x = x_ref[...]
    q = _mm(_rms_normed(x, g_ref[...]), wq_ref[...])
    scale = XA_HEAD_DIM ** -0.5
    scores = [_mm_nt(q[:, h * XA_HEAD_DIM:(h + 1) * XA_HEAD_DIM], kv_ref[:, h * XA_HEAD_DIM:(h + 1) * XA_HEAD_DIM])
              * scale for h in range(XA_HEADS)]
    probs = []
    for s in scores:
        p = jnp.exp(s - jnp.max(s, axis=-1, keepdims=True))
        probs.append(p * (1.0 / jnp.sum(p, axis=-1, keepdims=True)))
    heads = [_mm(p, kv_ref[:, D_MODEL + h * XA_HEAD_DIM:D_MODEL + (h + 1) * XA_HEAD_DIM])
             for h, p in enumerate(probs)]
    out_ref[...] = x + _mm(jnp.concatenate(heads, axis=-1), wo_ref[...])


def _xattn(x2d, gain, wq, kv, wo, nb, t_len, mem_len):
    n = x2d.shape[0]
    tm = _tile(t_len, 1024)
    per_seq = t_len // tm
    row = pl.BlockSpec((tm, D_MODEL), lambda b, j: (b * per_seq + j, 0))
    return pl.pallas_call(
        _xattn_kernel,
        grid=(nb, per_seq),
        in_specs=[row, pl.BlockSpec((1, D_MODEL), lambda b, j: (0, 0)),
                  pl.BlockSpec(wq.shape, lambda b, j: (0, 0)),
                  pl.BlockSpec((mem_len, 2 * D_MODEL), lambda b, j: (b, 0)),
                  pl.BlockSpec(wo.shape, lambda b, j: (0, 0))],
        out_specs=row,
        out_shape=jax.ShapeDtypeStruct((n, D_MODEL), F32),
        compiler_params=_params("parallel", "parallel"), name="xattn",
    )(x2d, gain.reshape(1, D_MODEL), wq, kv, wo)


def _ffn_kernel(x_ref, g_ref, wi_ref, wo_ref, fg_ref, out_ref, *, final_norm):
    x = x_ref[...]
    gu = _mm(_rms_normed(x, g_ref[...]), wi_ref[...])
    act = _silu(gu[:, :FFN_DIM]) * gu[:, FFN_DIM:]
    y = x + _mm(act, wo_ref[...])
    if final_norm:
        y = _rms_normed(y, fg_ref[...])
    out_ref[...] = y


def _ffn(x2d, gain, w_in, w_out, final_gain, final_norm):
    n = x2d.shape[0]
    tm = _tile(n, 512)
    row = pl.BlockSpec((tm, D_MODEL), lambda i: (i, 0))
    vec = pl.BlockSpec((1, D_MODEL), lambda i: (0, 0))
    resident = pl.Buffered(1)
    return pl.pallas_call(
        functools.partial(_ffn_kernel, final_norm=final_norm),
        grid=(n // tm,),
        in_specs=[row, vec, pl.BlockSpec(w_in.shape, lambda i: (0, 0), pipeline_mode=resident),
                  pl.BlockSpec(w_out.shape, lambda i: (0, 0), pipeline_mode=resident), vec],
        out_specs=row,
        out_shape=jax.ShapeDtypeStruct((n, D_MODEL), F32),
        compiler_params=_params("parallel"), name="ffn",
    )(x2d, gain.reshape(1, D_MODEL), w_in, w_out, final_gain.reshape(1, D_MODEL))


def _pad_lanes(a, width=LANES):
    return jnp.pad(a, [(0, 0)] * (a.ndim - 1) + [(0, width - a.shape[-1])])


def _rwkv_ssd_layer(x2d, nb, t_len, norm_g, w_in, w_out, rw_mu, rw_w0, rw_w2, rw_a0, rw_a2, rw_g2, rw_k_k, rw_k_a,
                    rw_r_k, rw_gn_w, rw_gn_b, conv_w, conv_b, dt_bias, a_log, d_skip, ssd_norm_w):
    o_lora = 3 * RW_DIM
    o_z = RW_PROJ
    o_xbc = o_z + SSD_DIM
    o_dt = o_xbc + SSD_CONV_DIM
    wb = w_in.astype(BF16)
    weights = [wb[:, :o_lora], wb[:, o_lora:o_z], wb[:, o_z:o_xbc], wb[:, o_xbc:o_dt], _pad_lanes(wb[:, o_dt:])]
    head_id = jnp.arange(LANES) // RW_HEAD_DIM
    blockdiag = (head_id[:, None] == head_id[None, :]).astype(BF16)
    row = lambda a: a.reshape(1, -1)
    rwkv_consts = (row(rw_mu[:o_lora]), row(rw_mu[o_lora:]), row(rw_a0), rw_a2.astype(BF16), rw_g2.astype(BF16),
                   row(rw_k_k), row(rw_k_a), rw_w0, rw_w2.astype(BF16), row(rw_r_k), blockdiag)
    ssd_consts = (conv_w, row(conv_b), _pad_lanes(dt_bias), _pad_lanes(a_log))
    (r, k, v, kk, kka, lw_f, lw_b, g, bonus, p_z, xbc, dt_f, dt_b, la_f, la_b) = _mix0_in(
        x2d, nb, t_len, norm_g, weights, rwkv_consts, ssd_consts)
    o_f, o_b = _rwkv_scan_both(r, k, v, kk, kka, lw_f, lw_b, nb, t_len)
    y_f = _ssd_scan(xbc, dt_f, la_f, nb, t_len, reverse=False)
    y_b = _ssd_scan(xbc, dt_b, la_b, nb, t_len, reverse=True)

    wo = w_out.astype(BF16)
    d_skip_lanes = jnp.repeat(d_skip, SSD_HEAD_DIM).reshape(1, SSD_DIM)
    return _mix0_out(x2d, o_f, o_b, bonus, g, y_f, y_b, xbc, p_z, row(rw_gn_w), row(rw_gn_b), blockdiag,
                     d_skip_lanes, row(ssd_norm_w), wo[:RW_DIM], wo[RW_DIM:])


def _hgrn_layer(x2d, nb, t_len, norm_g, w_in, w_out, norm_w, hg_lb, layer):
    kw, vw = HG_KEY_WIDTH, HG_VAL_WIDTH
    wb = w_in.astype(BF16)
    weights = [wb[:, :kw], wb[:, kw:2 * kw], wb[:, 2 * kw:3 * kw], wb[:, 3 * kw:3 * kw + vw], wb[:, 3 * kw + vw:]]
    q, lf_f, lf_b, kg_f, kg_b, i_val, g = _hgrn_in(x2d, norm_g, hg_lb, weights, layer)
    o_f = _hgrn_scan(q, lf_f, kg_f, i_val, hg_lb, nb, t_len, reverse=False, layer=layer)
    o_b = _hgrn_scan(q, lf_b, kg_b, i_val, hg_lb, nb, t_len, reverse=True, layer=layer)
    return _hgrn_out(x2d, o_f, o_b, g, norm_w.reshape(1, vw), w_out.astype(BF16))


def kernel(x, mem, mix_norm, ab_w_in, ab_w_out, rw_mu, rw_w0, rw_w2, rw_a0, rw_a2, rw_g2, rw_k_k, rw_k_a, rw_r_k, rw_gn_w, rw_gn_b, ssd_conv_w, ssd_conv_b, ssd_dt_bias, ssd_a_log, ssd_d, ssd_norm_w, hg_w_in, hg_w_out, hg_norm_w, hg_lb, xa_norm, mem_norm, xa_wq, xa_wkv, xa_wo, ffn_norm, ffn_w_in, ffn_w_out, final_norm):
    nb, t_len, d = x.shape
    mem_len = mem.shape[1]
    depth = mix_norm.shape[0]
    x2d = x.reshape(nb * t_len, d)
    mem2d = mem.reshape(nb * mem_len, d)
    for layer in range(depth):
        if layer % 2 == 0:
            e = layer // 2
            x2d = _rwkv_ssd_layer(x2d, nb, t_len, mix_norm[layer], ab_w_in[e], ab_w_out[e], rw_mu[e], rw_w0[e],
                                  rw_w2[e], rw_a0[e], rw_a2[e], rw_g2[e], rw_k_k[e], rw_k_a[e],
                                  rw_r_k[e].reshape(-1), rw_gn_w[e], rw_gn_b[e], ssd_conv_w[e], ssd_conv_b[e],
                                  ssd_dt_bias[e], ssd_a_log[e], ssd_d[e], ssd_norm_w[e])
        else:
            o = layer // 2
            x2d = _hgrn_layer(x2d, nb, t_len, mix_norm[layer], hg_w_in[o], hg_w_out[o], hg_norm_w[o], hg_lb, layer)
        (kv,) = _norm_matmul(mem2d, mem_norm[layer], [xa_wkv[layer].astype(BF16)], [BF16])
        x2d = _xattn(x2d, xa_norm[layer], xa_wq[layer].astype(BF16), kv, xa_wo[layer].astype(BF16),
                     nb, t_len, mem_len)
        x2d = _ffn(x2d, ffn_norm[layer], ffn_w_in[layer].astype(BF16), ffn_w_out[layer].astype(BF16),
                   final_norm, final_norm=(layer == depth - 1))
    return x2d.reshape(nb, t_len, d)
```
